```python
import math
import jax, jax.numpy as jnp
from jax import lax
import numpy as np

D_MODEL = 1024
BATCH = 8
SEQ = 2048
DEPTH = 2
DEC_BATCH = 32
DEC_SEQ = 1
PAST_LEN = 8192
PAGE_SIZE = 128

PLE_DIM = 256
SSD_EXPAND = 2
SSD_D_INNER = SSD_EXPAND * D_MODEL
SSD_HEADDIM = 64
SSD_HEADS = SSD_D_INNER // SSD_HEADDIM
SSD_GROUPS = 4
SSD_HPG = SSD_HEADS // SSD_GROUPS
SSD_STATE = 128
SSD_CONV = 4
SSD_CHUNK = 128
SSD_CONV_DIM = SSD_D_INNER + 2 * SSD_GROUPS * SSD_STATE
SSD_IN_DIM = SSD_D_INNER + SSD_CONV_DIM + SSD_HEADS
NSA_HEADS = 16
NSA_KV_HEADS = 4
NSA_REP = NSA_HEADS // NSA_KV_HEADS
NSA_HEAD_DIM = D_MODEL // NSA_HEADS
NSA_BLOCK = 64
NSA_TOP_BLOCKS = 16
NSA_WINDOW = 512
NSA_QBLK = 16
NSA_KV_SLOTS = 4
NSA_IN_DIM = NSA_HEADS * NSA_HEAD_DIM + 6 * NSA_KV_HEADS * NSA_HEAD_DIM + 3 * NSA_HEADS
N_EXPERTS = 16
N_EXPERT_GROUPS = 4
EXPERTS_PER_GROUP = N_EXPERTS // N_EXPERT_GROUPS
MOE_TOP_K = 2
D_FF_EXPERT = 512
N_SSD_LAYERS = (DEPTH + 1) // 2
N_NSA_LAYERS = DEPTH // 2
DEEPNORM_ALPHA = (2.0 * DEPTH) ** 0.25
DEEPNORM_BETA = (8.0 * DEPTH) ** -0.25
LN_EPS = 1e-5
NEG_INF = -1e30

kernel_name = "ssd_nsa_hybrid_decode_step"


def layer_norm(x, g, b):
    xf = x.astype(jnp.float32)
    mu = xf.mean(-1, keepdims=True)
    var = jnp.square(xf - mu).mean(-1, keepdims=True)
    return ((xf - mu) * lax.rsqrt(var + LN_EPS) * g + b).astype(x.dtype)


def masked_softmax(s, mask):
    s = jnp.where(mask, s.astype(jnp.float32), NEG_INF)
    m = s.max(-1, keepdims=True)
    e = jnp.exp(s - m) * mask
    return e / jnp.maximum(e.sum(-1, keepdims=True), 1e-30)


def alibi_slopes():
    return 2.0 ** (-8.0 * jnp.arange(1, NSA_HEADS + 1, dtype=jnp.float32) / NSA_HEADS)


def ssd_chunk_step(state, xs, a_heads):
    x, dt, bm, cm = xs
    q = x.shape[1]
    cum = jnp.cumsum(dt * a_heads, axis=1)
    causal = jnp.tril(jnp.ones((q, q), dtype=bool))[None, :, :, None, None]
    seg = cum[:, :, None] - cum[:, None, :]
    decay = jnp.where(causal, jnp.exp(jnp.where(causal, seg, 0.0)), 0.0)
    cb = jnp.einsum('btgn,bsgn->btsg', cm, bm)
    xdt = x * dt[..., None]
    y = jnp.einsum('btsgr,bsgrp->btgrp', cb[..., None] * decay, xdt)
    y = y + jnp.einsum('btgn,bgrpn->btgrp', cm, state) * jnp.exp(cum)[..., None]
    to_end = jnp.exp(cum[:, -1:] - cum)
    new_state = state * jnp.exp(cum[:, -1])[..., None, None] + jnp.einsum('bsgrp,bsgn->bgrpn', xdt * to_end[..., None], bm)
    return new_state, y


def ssd_mixer(h, conv_state, ssm_state, w_in, conv_w, conv_b, dt_bias, a_log, d_skip, norm_g, w_out):
    bsz, t, _ = h.shape
    f32 = jnp.float32
    proj = h @ w_in
    z = proj[..., :SSD_D_INNER]
    xbc = proj[..., SSD_D_INNER:SSD_D_INNER + SSD_CONV_DIM]
    dt_raw = proj[..., SSD_D_INNER + SSD_CONV_DIM:]
    xpad = jnp.concatenate([conv_state.astype(xbc.dtype), xbc], axis=1)
    conv = conv_b + xpad[:, 0:t] * conv_w[0]
    for k in range(1, SSD_CONV):
        conv = conv + xpad[:, k:k + t] * conv_w[k]
    conv = jax.nn.silu(conv)
    new_conv_state = xpad[:, t:]
    nbc = SSD_GROUPS * SSD_STATE
    xs = conv[..., :SSD_D_INNER].reshape(bsz, t, SSD_GROUPS, SSD_HPG, SSD_HEADDIM).astype(f32)
    bm = conv[..., SSD_D_INNER:SSD_D_INNER + nbc].reshape(bsz, t, SSD_GROUPS, SSD_STATE).astype(f32)
    cm = conv[..., SSD_D_INNER + nbc:].reshape(bsz, t, SSD_GROUPS, SSD_STATE).astype(f32)
    dt = jax.nn.softplus(dt_raw.astype(f32) + dt_bias).reshape(bsz, t, SSD_GROUPS, SSD_HPG)
    a_heads = -jnp.exp(a_log.astype(f32)).reshape(SSD_GROUPS, SSD_HPG)
    q = SSD_CHUNK if t % SSD_CHUNK == 0 else t
    nc = t // q

    def to_chunks(a):
        return jnp.moveaxis(a.reshape(bsz, nc, q, *a.shape[2:]), 1, 0)

    state0 = ssm_state.astype(f32).reshape(bsz, SSD_GROUPS, SSD_HPG, SSD_HEADDIM, SSD_STATE)
    state, y = lax.scan(lambda s, c: ssd_chunk_step(s, c, a_heads), state0,
                        (to_chunks(xs), to_chunks(dt), to_chunks(bm), to_chunks(cm)))
    y = jnp.moveaxis(y, 0, 1).reshape(bsz, t, SSD_GROUPS, SSD_HPG, SSD_HEADDIM)
    y = y + d_skip.reshape(SSD_GROUPS, SSD_HPG)[..., None] * xs
    y = y.reshape(bsz, t, SSD_GROUPS, SSD_HPG * SSD_HEADDIM) * jax.nn.silu(z.astype(f32)).reshape(bsz, t, SSD_GROUPS, SSD_HPG * SSD_HEADDIM)
    y = y * lax.rsqrt(jnp.mean(y * y, -1, keepdims=True) + 1e-5)
    y = (y.reshape(bsz, t, SSD_D_INNER) * norm_g).astype(h.dtype)
    new_state = state.reshape(bsz, SSD_HEADS, SSD_HEADDIM, SSD_STATE).astype(h.dtype)
    return y @ w_out, new_conv_state, new_state


def nsa_project(h, w_in):
    bsz, t, _ = h.shape
    proj = h @ w_in
    qd = NSA_HEADS * NSA_HEAD_DIM
    kvd = NSA_KV_HEADS * NSA_HEAD_DIM
    q = proj[..., :qd].reshape(bsz, t, NSA_KV_HEADS, NSA_REP, NSA_HEAD_DIM) * (NSA_HEAD_DIM ** -0.5)
    kv = proj[..., qd:qd + 6 * kvd].reshape(bsz, t, 6, NSA_KV_HEADS, NSA_HEAD_DIM)
    gates = jax.nn.sigmoid(proj[..., qd + 6 * kvd:].astype(jnp.float32)).reshape(bsz, t, NSA_KV_HEADS, NSA_REP, 3)
    return q, kv[:, :, 0:4], kv[:, :, 4:6], gates


def nsa_compress(kv_cmp, pos_w):
    bsz, s = kv_cmp.shape[:2]
    blocks = kv_cmp.reshape(bsz, s // NSA_BLOCK, NSA_BLOCK, 2, NSA_KV_HEADS, NSA_HEAD_DIM)
    w = jax.nn.softmax(pos_w.astype(jnp.float32), axis=1).astype(kv_cmp.dtype)
    kvc = jnp.einsum('bnlckd,clk->bnckd', blocks, w)
    return kvc[:, :, 0], kvc[:, :, 1]


def nsa_attend(q, gates, q_pos, kc, vc, gather_sel, kw, vw, w_pos, slopes):
    bsz, tq = q.shape[:2]
    nb = kc.shape[1]
    slope = slopes.reshape(NSA_KV_HEADS, NSA_REP)
    blk = jnp.arange(nb)
    dist_c = (q_pos[:, None] - ((blk + 1) * NSA_BLOCK - 1)[None, :]).astype(jnp.float32)
    s_c = jnp.einsum('btkrd,bnkd->btkrn', q, kc).astype(jnp.float32) - slope[None, None, :, :, None] * dist_c[None, :, None, None, :]
    p_c = masked_softmax(s_c, (dist_c >= 0)[None, :, None, None, :])
    o_c = jnp.einsum('btkrn,bnkd->btkrd', p_c.astype(vc.dtype), vc)
    cur = q_pos // NSA_BLOCK
    imp = p_c.sum(axis=3)
    forced = (blk[None, :] == cur[:, None]) | (blk[None, :] == 0)
    imp = jnp.where(forced[None, :, None, :], 1e4, imp)
    imp = jnp.where((blk[None, :] > cur[:, None])[None, :, None, :], -1.0, imp)
    n_sel = min(NSA_TOP_BLOCKS, nb)
    _, idx = lax.top_k(imp, n_sel)
    ks, vs = gather_sel(idx)
    s_pos = idx[..., None] * NSA_BLOCK + jnp.arange(NSA_BLOCK)
    dist_s = (q_pos[None, :, None, None, None] - s_pos).astype(jnp.float32)
    s_s = jnp.einsum('btkrd,btknld->btkrnl', q, ks).astype(jnp.float32) - slope[None, None, :, :, None, None] * dist_s[:, :, :, None]
    s_s = s_s.reshape(bsz, tq, NSA_KV_HEADS, NSA_REP, n_sel * NSA_BLOCK)
    mask_s = (dist_s >= 0).reshape(bsz, tq, NSA_KV_HEADS, 1, n_sel * NSA_BLOCK)
    p_s = masked_softmax(s_s, mask_s).reshape(bsz, tq, NSA_KV_HEADS, NSA_REP, n_sel, NSA_BLOCK)
    o_s = jnp.einsum('btkrnl,btknld->btkrd', p_s.astype(vs.dtype), vs)
    dist_w = q_pos[:, None] - w_pos[None, :]
    mask_w = (dist_w >= 0) & (dist_w < NSA_WINDOW) & (w_pos >= 0)[None, :]
    s_w = jnp.einsum('btkrd,bskd->btkrs', q, kw).astype(jnp.float32) - slope[None, None, :, :, None] * dist_w.astype(jnp.float32)[None, :, None, None, :]
    p_w = masked_softmax(s_w, mask_w[None, :, None, None, :])
    o_w = jnp.einsum('btkrs,bskd->btkrd', p_w.astype(vw.dtype), vw)
    o = gates[..., 0:1] * o_c + gates[..., 1:2] * o_s + gates[..., 2:3] * o_w
    return o.astype(q.dtype)


def nsa_prompt(h, w_in, pos_w, w_out, slopes):
    bsz, t, _ = h.shape
    q, kv4, kvw, gates = nsa_project(h, w_in)
    s_pad = -(-t // NSA_BLOCK) * NSA_BLOCK
    kv4p = jnp.pad(kv4, ((0, 0), (0, s_pad - t), (0, 0), (0, 0), (0, 0)))
    kc, vc = nsa_compress(kv4p[:, :, 0:2], pos_w)
    sel_blocks = kv4p.reshape(bsz, s_pad // NSA_BLOCK, NSA_BLOCK, NSA_KV_SLOTS, NSA_KV_HEADS, NSA_HEAD_DIM)
    bi = jnp.arange(bsz)[:, None, None, None]
    ki = jnp.arange(NSA_KV_HEADS)[None, None, :, None]

    def gather_sel(idx):
        g = sel_blocks[bi, idx, :, 2:4, ki]
        return g[..., 0, :], g[..., 1, :]

    kwp = jnp.pad(kvw, ((0, 0), (NSA_WINDOW, 0), (0, 0), (0, 0), (0, 0)))
    nq = t // NSA_QBLK

    def block(args):
        qb, gb, start = args
        q_pos = start + jnp.arange(NSA_QBLK)
        win = lax.dynamic_slice_in_dim(kwp, start, NSA_WINDOW + NSA_QBLK, axis=1)
        w_pos = start - NSA_WINDOW + jnp.arange(NSA_WINDOW + NSA_QBLK)
        return nsa_attend(qb, gb, q_pos, kc, vc, gather_sel, win[:, :, 0], win[:, :, 1], w_pos, slopes)

    qb = jnp.moveaxis(q.reshape(bsz, nq, NSA_QBLK, NSA_KV_HEADS, NSA_REP, NSA_HEAD_DIM), 1, 0)
    gb = jnp.moveaxis(gates.reshape(bsz, nq, NSA_QBLK, NSA_KV_HEADS, NSA_REP, 3), 1, 0)
    starts = jnp.arange(nq, dtype=jnp.int32) * NSA_QBLK
    o = lax.map(block, (qb, gb, starts))
    o = jnp.moveaxis(o, 0, 1).reshape(bsz, t, NSA_HEADS * NSA_HEAD_DIM)
    w_keep = min(NSA_WINDOW, t)
    return o @ w_out, kv4, kvw[:, t - w_keep:]


def nsa_sample(h, kv_pool, page_table, win_cache, w_in, pos_w, w_out, slopes):
    bsz, t, _ = h.shape
    q, kv4, kvw, gates = nsa_project(h, w_in)
    total = PAST_LEN + t
    s_pad = -(-total // NSA_BLOCK) * NSA_BLOCK
    past_cmp = kv_pool[page_table, :, 0:2].reshape(bsz, PAST_LEN, 2, NSA_KV_HEADS, NSA_HEAD_DIM)
    cmp_rows = jnp.concatenate([past_cmp.astype(kv4.dtype), kv4[:, :, 0:2]], axis=1)
    cmp_rows = jnp.pad(cmp_rows, ((0, 0), (0, s_pad - total), (0, 0), (0, 0), (0, 0)))
    kc, vc = nsa_compress(cmp_rows, pos_w)
    nb_past = PAST_LEN // NSA_BLOCK
    bpp = PAGE_SIZE // NSA_BLOCK
    pool_blocks = kv_pool.reshape(kv_pool.shape[0], bpp, NSA_BLOCK, NSA_KV_SLOTS, NSA_KV_HEADS, NSA_HEAD_DIM)
    new_len = s_pad - PAST_LEN
    new_blocks = jnp.pad(kv4, ((0, 0), (0, new_len - t), (0, 0), (0, 0), (0, 0))).reshape(
        bsz, new_len // NSA_BLOCK, NSA_BLOCK, NSA_KV_SLOTS, NSA_KV_HEADS, NSA_HEAD_DIM)
    bi = jnp.arange(bsz)[:, None, None, None]
    ki = jnp.arange(NSA_KV_HEADS)[None, None, :, None]

    def gather_sel(idx):
        past_idx = jnp.minimum(idx, nb_past - 1)
        phys = page_table[bi, past_idx // bpp]
        g_past = pool_blocks[phys, past_idx % bpp, :, 2:4, ki]
        g_new = new_blocks[bi, jnp.clip(idx - nb_past, 0, new_blocks.shape[1] - 1), :, 2:4, ki]
        g = jnp.where((idx >= nb_past)[..., None, None, None], g_new, g_past.astype(g_new.dtype))
        return g[..., 0, :], g[..., 1, :]

    win_rows = jnp.concatenate([win_cache.astype(kvw.dtype), kvw], axis=1)
    w_buf = win_cache.shape[1]
    w_pos = PAST_LEN - w_buf + jnp.arange(w_buf + t)
    q_pos = PAST_LEN + jnp.arange(t)
    o = nsa_attend(q, gates, q_pos, kc, vc, gather_sel, win_rows[:, :, 0], win_rows[:, :, 1], w_pos, slopes)
    o = o.reshape(bsz, t, NSA_HEADS * NSA_HEAD_DIM)
    return o @ w_out, kv4, win_rows[:, t:]


def moe(h, router_w, router_bias, w_gate, w_up, w_down):
    score = jax.nn.sigmoid((h @ router_w).astype(jnp.float32))
    sel = score + router_bias
    grp = sel.reshape(*sel.shape[:-1], N_EXPERT_GROUPS, EXPERTS_PER_GROUP)
    grp_score = lax.top_k(grp, 2)[0].sum(-1)
    best = jnp.argmax(grp_score, axis=-1)
    in_group = (jnp.arange(N_EXPERTS) // EXPERTS_PER_GROUP) == best[..., None]
    _, idx = lax.top_k(jnp.where(in_group, sel, NEG_INF), MOE_TOP_K)
    w = jnp.take_along_axis(score, idx, axis=-1)
    w = w / w.sum(-1, keepdims=True)
    gate = (jax.nn.one_hot(idx, N_EXPERTS, dtype=jnp.float32) * w[..., None]).sum(-2).astype(h.dtype)
    out = jnp.zeros_like(h)
    for e in range(N_EXPERTS):
        hid = jax.nn.silu(h @ w_gate[e]) * (h @ w_up[e])
        out = out + gate[..., e:e + 1] * (hid @ w_down[e])
    return out


def layer_tail(x, mix, p_i, ln1_g, ln1_b, ln2_g, ln2_b, router_w, router_bias, w_gate, w_up, w_down, ple_proj, ple_gate):
    h = layer_norm(DEEPNORM_ALPHA * x + mix, ln1_g, ln1_b)
    h = layer_norm(DEEPNORM_ALPHA * h + moe(h, router_w, router_bias, w_gate, w_up, w_down), ln2_g, ln2_b)
    return h + jax.nn.sigmoid(h @ ple_gate) * (p_i @ ple_proj)


def setup_inputs(seed: int = 0) -> dict:
    key = jax.random.key(seed)
    ks = iter(jax.random.split(key, 48))
    f32 = jnp.float32

    def nrm(shape, scale):
        return scale * jax.random.normal(next(ks), shape, f32)

    n_pages = PAST_LEN // PAGE_SIZE
    n_pool = (DEC_BATCH * n_pages * 5 + 3) // 4
    w_buf = min(NSA_WINDOW, PAST_LEN)
    beta = DEEPNORM_BETA
    x_prompt = nrm((BATCH, SEQ, D_MODEL), 1.0)
    x_sample = nrm((DEC_BATCH, DEC_SEQ, D_MODEL), 1.0)
    state_ssm = nrm((N_SSD_LAYERS, DEC_BATCH, SSD_HEADS, SSD_HEADDIM, SSD_STATE), 0.5)
    state_conv = nrm((N_SSD_LAYERS, DEC_BATCH, SSD_CONV - 1, SSD_CONV_DIM), 1.0)
    cache_kv = nrm((N_NSA_LAYERS, n_pool, PAGE_SIZE, NSA_KV_SLOTS, NSA_KV_HEADS, NSA_HEAD_DIM), 1.0)
    cache_win = nrm((N_NSA_LAYERS, DEC_BATCH, w_buf, 2, NSA_KV_HEADS, NSA_HEAD_DIM), 1.0)
    page_table = jax.random.permutation(next(ks), n_pool)[:DEC_BATCH * n_pages].reshape(DEC_BATCH, n_pages).astype(jnp.int32)
    p_prompt = nrm((DEPTH, BATCH, SEQ, PLE_DIM), 1.0)
    p_sample = nrm((DEPTH, DEC_BATCH, DEC_SEQ, PLE_DIM), 1.0)
    ssd_w_in = nrm((N_SSD_LAYERS, D_MODEL, SSD_IN_DIM), D_MODEL ** -0.5)
    ssd_conv_w = nrm((N_SSD_LAYERS, SSD_CONV, SSD_CONV_DIM), SSD_CONV ** -0.5)
    ssd_conv_b = nrm((N_SSD_LAYERS, SSD_CONV_DIM), 0.02)
    dt0 = jnp.exp(jax.random.uniform(next(ks), (N_SSD_LAYERS, SSD_HEADS), f32, math.log(1e-3), math.log(1e-1)))
    ssd_dt_bias = dt0 + jnp.log(-jnp.expm1(-dt0))
    ssd_a_log = jnp.log(jax.random.uniform(next(ks), (N_SSD_LAYERS, SSD_HEADS), f32, 1.0, 16.0))
    ssd_d = 1.0 + nrm((N_SSD_LAYERS, SSD_HEADS), 0.02)
    ssd_norm_g = 1.0 + nrm((N_SSD_LAYERS, SSD_D_INNER), 0.02)
    ssd_w_out = nrm((N_SSD_LAYERS, SSD_D_INNER, D_MODEL), SSD_D_INNER ** -0.5 * beta)
    kvd = NSA_KV_HEADS * NSA_HEAD_DIM
    slot_scale = jnp.repeat(jnp.array([1.0, beta, 1.0, beta, 1.0, beta], f32), kvd)
    col_scale = jnp.concatenate([jnp.ones((NSA_HEADS * NSA_HEAD_DIM,), f32), slot_scale, jnp.ones((3 * NSA_HEADS,), f32)])
    nsa_w_in = nrm((N_NSA_LAYERS, D_MODEL, NSA_IN_DIM), D_MODEL ** -0.5) * col_scale
    nsa_pos_w = nrm((N_NSA_LAYERS, 2, NSA_BLOCK, NSA_KV_HEADS), 0.5)
    nsa_w_out = nrm((N_NSA_LAYERS, NSA_HEADS * NSA_HEAD_DIM, D_MODEL), (NSA_HEADS * NSA_HEAD_DIM) ** -0.5 * beta)
    ln1_g = 1.0 + nrm((DEPTH, D_MODEL), 0.02)
    ln1_b = nrm((DEPTH, D_MODEL), 0.02)
    ln2_g = 1.0 + nrm((DEPTH, D_MODEL), 0.02)
    ln2_b = nrm((DEPTH, D_MODEL), 0.02)
    router_w = nrm((D_MODEL, N_EXPERTS), D_MODEL ** -0.5)
    router_bias = nrm((N_EXPERTS,), 0.01)
    moe_w_gate = nrm((DEPTH, N_EXPERTS, D_MODEL, D_FF_EXPERT), D_MODEL ** -0.5)
    moe_w_up = nrm((DEPTH, N_EXPERTS, D_MODEL, D_FF_EXPERT), D_MODEL ** -0.5 * beta)
    moe_w_down = nrm((DEPTH, N_EXPERTS, D_FF_EXPERT, D_MODEL), D_FF_EXPERT ** -0.5 * beta)
    ple_proj = nrm((DEPTH, PLE_DIM, D_MODEL), PLE_DIM ** -0.5 * beta)
    ple_gate = nrm((DEPTH, D_MODEL, D_MODEL), D_MODEL ** -0.5)
    return {"x_prompt": x_prompt, "x_sample": x_sample, "state_ssm": state_ssm, "state_conv": state_conv,
            "cache_kv": cache_kv, "cache_win": cache_win, "page_table": page_table,
            "p_prompt": p_prompt, "p_sample": p_sample,
            "ssd_w_in": ssd_w_in, "ssd_conv_w": ssd_conv_w, "ssd_conv_b": ssd_conv_b, "ssd_dt_bias": ssd_dt_bias,
            "ssd_a_log": ssd_a_log, "ssd_d": ssd_d, "ssd_norm_g": ssd_norm_g, "ssd_w_out": ssd_w_out,
            "nsa_w_in": nsa_w_in, "nsa_pos_w": nsa_pos_w, "nsa_w_out": nsa_w_out,
            "ln1_g": ln1_g, "ln1_b": ln1_b, "ln2_g": ln2_g, "ln2_b": ln2_b,
            "router_w": router_w, "router_bias": router_bias,
            "moe_w_gate": moe_w_gate, "moe_w_up": moe_w_up, "moe_w_down": moe_w_down,
            "ple_proj": ple_proj, "ple_gate": ple_gate}


def reference(x_prompt, x_sample, state_ssm, state_conv, cache_kv, cache_win, page_table, p_prompt, p_sample,
              ssd_w_in, ssd_conv_w, ssd_conv_b, ssd_dt_bias, ssd_a_log, ssd_d, ssd_norm_g, ssd_w_out,
              nsa_w_in, nsa_pos_w, nsa_w_out, ln1_g, ln1_b, ln2_g, ln2_b, router_w, router_bias,
              moe_w_gate, moe_w_up, moe_w_down, ple_proj, ple_gate):
    slopes = alibi_slopes()
    xp, xs = x_prompt, x_sample
    bp = xp.shape[0]
    ssm_p, conv_p, kv_p, win_p = [], [], [], []
    ssm_s, conv_s, kv_s, win_s = [], [], [], []
    for i in range(DEPTH):
        j = i // 2
        if i % 2 == 0:
            ssd_w = (ssd_w_in[j], ssd_conv_w[j], ssd_conv_b[j], ssd_dt_bias[j], ssd_a_log[j], ssd_d[j], ssd_norm_g[j], ssd_w_out[j])
            zero_conv = jnp.zeros((bp, SSD_CONV - 1, SSD_CONV_DIM), xp.dtype)
            zero_ssm = jnp.zeros((bp, SSD_HEADS, SSD_HEADDIM, SSD_STATE), xp.dtype)
            mix_p, c_new, s_new = ssd_mixer(xp, zero_conv, zero_ssm, *ssd_w)
            conv_p.append(c_new)
            ssm_p.append(s_new)
            mix_s, c_new, s_new = ssd_mixer(xs, state_conv[j], state_ssm[j], *ssd_w)
            conv_s.append(c_new)
            ssm_s.append(s_new)
        else:
            mix_p, r_new, w_new = nsa_prompt(xp, nsa_w_in[j], nsa_pos_w[j], nsa_w_out[j], slopes)
            kv_p.append(r_new)
            win_p.append(w_new)
            mix_s, r_new, w_new = nsa_sample(xs, cache_kv[j], page_table, cache_win[j], nsa_w_in[j], nsa_pos_w[j], nsa_w_out[j], slopes)
            kv_s.append(r_new)
            win_s.append(w_new)
        tail_w = (ln1_g[i], ln1_b[i], ln2_g[i], ln2_b[i], router_w, router_bias,
                  moe_w_gate[i], moe_w_up[i], moe_w_down[i], ple_proj[i], ple_gate[i])
        xp = layer_tail(xp, mix_p, p_prompt[i], *tail_w)
        xs = layer_tail(xs, mix_s, p_sample[i], *tail_w)
    y_prompt = xp
    y_sample = xs
    new_ssm_prompt = jnp.stack(ssm_p)
    new_conv_prompt = jnp.stack(conv_p)
    new_kv_prompt = jnp.stack(kv_p)
    new_win_prompt = jnp.stack(win_p)
    new_ssm_sample = jnp.stack(ssm_s)
    new_conv_sample = jnp.stack(conv_s)
    new_kv_sample = jnp.stack(kv_s)
    new_win_sample = jnp.stack(win_s)
    return (y_prompt, y_sample, new_ssm_prompt, new_conv_prompt, new_kv_prompt, new_win_prompt,
            new_ssm_sample, new_conv_sample, new_kv_sample, new_win_sample)
```

```python
import functools

import jax
import jax.numpy as jnp
from jax import lax
from jax.experimental import pallas as pl
from jax.experimental.pallas import tpu as pltpu

f32, bf16, i32 = jnp.float32, jnp.bfloat16, jnp.int32
HIGHEST = lax.Precision.HIGHEST

D_MODEL = 1024
DEPTH = 2
PLE_DIM = 256
SSD_D_INNER = 2048
SSD_HEADDIM = 64
SSD_HEADS = 32
SSD_GROUPS = 4
SSD_HPG = 8
SSD_STATE = 128
SSD_CONV = 4
SSD_CHUNK = 128
SSD_GW = SSD_HPG * SSD_HEADDIM
SSD_CONV_DIM = SSD_D_INNER + 2 * SSD_GROUPS * SSD_STATE
NSA_HEADS = 16
NSA_KV = 4
NSA_REP = 4
NSA_DH = 64
NSA_BLOCK = 64
NSA_TOP = 16
NSA_WINDOW = 512
NSA_QD = NSA_HEADS * NSA_DH
NSA_KVD = NSA_KV * NSA_DH
PAGE_SIZE = 128
N_EXPERTS = 16
N_EGROUPS = 4
EPG = 4
D_FF = 512
ALPHA = (2.0 * DEPTH) ** 0.25
LN_EPS = 1e-5
NEG = -1e30
LANES = 128
VMEM_LIMIT = 56 * 1024 * 1024


def _cparams(sem):
    return pltpu.CompilerParams(dimension_semantics=sem, vmem_limit_bytes=VMEM_LIMIT)


def _ln(v, g, b):
    mu = jnp.mean(v, axis=-1, keepdims=True)
    d = v - mu
    var = jnp.mean(d * d, axis=-1, keepdims=True)
    return d * lax.rsqrt(var + LN_EPS) * g + b


def _dot(a, b):
    return jnp.dot(a, b, preferred_element_type=f32)


def _mxu(a, b, precise):
    if precise:
        return jnp.dot(a.astype(f32), b.astype(f32), precision=HIGHEST, preferred_element_type=f32)
    return jnp.dot(a.astype(bf16), b.astype(bf16), preferred_element_type=f32)


def _mxu_nt(a, b, precise):
    if precise:
        return _dot_nt(a.astype(f32), b.astype(f32), precision=HIGHEST)
    return _dot_nt(a.astype(bf16), b.astype(bf16))


def _dot_nt(a, b, precision=None):
    return lax.dot_general(a, b, (((1,), (1,)), ((), ())), precision=precision, preferred_element_type=f32)


def _dot_tn(a, b, precision=None):
    return lax.dot_general(a, b, (((0,), (0,)), ((), ())), precision=precision, preferred_element_type=f32)


def _is_f32(ref):
    return ref.dtype == jnp.float32


def _mm_body(x_ref, w_ref, *o_refs, act, scale):
    acc = _mxu(x_ref[...], w_ref[...], _is_f32(w_ref))
    if scale != 1.0:
        acc = acc * scale
    if act == "sigmoid":
        acc = jax.nn.sigmoid(acc)
    for o_ref in o_refs:
        o_ref[...] = acc.astype(o_ref.dtype)


def _mm(x, w, *, tm, tn, out_dtypes, name, act=None, scale=1.0):
    m, k = x.shape
    n = w.shape[1]
    tm, tn = min(tm, m), min(tn, n)
    assert m % tm == 0 and n % tn == 0
    outs = pl.pallas_call(
        functools.partial(_mm_body, act=act, scale=scale),
        name=f"{name}_r{m}",
        grid=(m // tm, n // tn),
        in_specs=[pl.BlockSpec((tm, k), lambda i, j: (i, 0)), pl.BlockSpec((k, tn), lambda i, j: (0, j))],
        out_specs=[pl.BlockSpec((tm, tn), lambda i, j: (i, j)) for _ in out_dtypes],
        out_shape=[jax.ShapeDtypeStruct((m, n), dt) for dt in out_dtypes],
        compiler_params=_cparams(("parallel", "arbitrary")),
    )(x, w)
    return outs


def _mm_ln_body(x_ref, w_ref, res_ref, g_ref, b_ref, o_ref):
    acc = _mxu(x_ref[...], w_ref[...], _is_f32(w_ref))
    o_ref[...] = _ln(ALPHA * res_ref[...] + acc, g_ref[...], b_ref[...])


def _mm_ln(x, w, res, g, b, *, tm, name):
    m, k = x.shape
    n = w.shape[1]
    tm = min(tm, m)
    assert m % tm == 0
    return pl.pallas_call(
        _mm_ln_body,
        name=f"{name}_r{m}",
        grid=(m // tm,),
        in_specs=[pl.BlockSpec((tm, k), lambda i: (i, 0)), pl.BlockSpec((k, n), lambda i: (0, 0)),
                  pl.BlockSpec((tm, n), lambda i: (i, 0)), pl.BlockSpec((1, n), lambda i: (0, 0)),
                  pl.BlockSpec((1, n), lambda i: (0, 0))],
        out_specs=pl.BlockSpec((tm, n), lambda i: (i, 0)),
        out_shape=jax.ShapeDtypeStruct((m, n), f32),
        compiler_params=_cparams(("parallel",)),
    )(x, w, res, g.reshape(1, n), b.reshape(1, n))


def _router_body(h_ref, rwt_ref, rb_ref, gate_ref):
    logits = _mxu_nt(rwt_ref[...], h_ref[...], _is_f32(rwt_ref))
    score = jax.nn.sigmoid(logits)
    sel = score + rb_ref[...]
    gsum = []
    for g in range(N_EGROUPS):
        a, b, c, d = (sel[EPG * g + i:EPG * g + i + 1] for i in range(EPG))
        hi1, lo1, hi2, lo2 = jnp.maximum(a, b), jnp.minimum(a, b), jnp.maximum(c, d), jnp.minimum(c, d)
        gsum.append(jnp.maximum(hi1, hi2) + jnp.maximum(jnp.minimum(hi1, hi2), jnp.maximum(lo1, lo2)))
    best = jnp.zeros_like(gsum[0], dtype=i32)
    top = gsum[0]
    for g in range(1, N_EGROUPS):
        upd = gsum[g] > top
        best = jnp.where(upd, g, best)
        top = jnp.where(upd, gsum[g], top)
    selg = sel[0:EPG]
    scg = score[0:EPG]
    for g in range(1, N_EGROUPS):
        selg = jnp.where(best == g, sel[EPG * g:EPG * (g + 1)], selg)
        scg = jnp.where(best == g, score[EPG * g:EPG * (g + 1)], scg)
    rows = [selg[i:i + 1] for i in range(EPG)]
    chosen = []
    for i in range(EPG):
        rank = jnp.zeros_like(best)
        for j in range(EPG):
            if j == i:
                continue
            ahead = (rows[j] > rows[i]) | ((rows[j] == rows[i]) if j < i else False)
            rank = rank + ahead.astype(i32)
        chosen.append(rank < 2)
    wsum = sum(jnp.where(chosen[i], scg[i:i + 1], 0.0) for i in range(EPG))
    gates = [jnp.where(chosen[i], scg[i:i + 1] / wsum, 0.0) for i in range(EPG)]
    out_rows = []
    for g in range(N_EGROUPS):
        for i in range(EPG):
            out_rows.append(jnp.where(best == g, gates[i], 0.0))
    gate_ref[...] = jnp.concatenate(out_rows, axis=0).T


def _router(h, rwt, rb, *, tm):
    m, d = h.shape
    tm = min(tm, m)
    return pl.pallas_call(
        _router_body,
        name=f"router_r{m}",
        grid=(m // tm,),
        in_specs=[pl.BlockSpec((tm, d), lambda i: (i, 0)), pl.BlockSpec((N_EXPERTS, d), lambda i: (0, 0)),
                  pl.BlockSpec((N_EXPERTS, 1), lambda i: (0, 0))],
        out_specs=pl.BlockSpec((tm, N_EXPERTS), lambda i: (i, 0)),
        out_shape=jax.ShapeDtypeStruct((m, N_EXPERTS), f32),
        compiler_params=_cparams(("parallel",)),
    )(h, rwt, rb)


def _moe_body(h_ref, gate_ref, wg_ref, wu_ref, wd_ref, g2_ref, b2_ref, o_ref, acc_ref):
    e = pl.program_id(1)

    @pl.when(e == 0)
    def _():
        acc_ref[...] = jnp.zeros_like(acc_ref)

    precise = _is_f32(wg_ref)
    h = h_ref[...]
    hid = jax.nn.silu(_mxu(h, wg_ref[0], precise)) * _mxu(h, wu_ref[0], precise)
    out = _mxu(hid, wd_ref[0], precise)
    gate = gate_ref[...]
    lane = lax.broadcasted_iota(i32, gate.shape, 1)
    gcol = jnp.sum(jnp.where(lane == e, gate, 0.0), axis=1, keepdims=True)
    acc_ref[...] += gcol * out

    @pl.when(e == N_EXPERTS - 1)
    def _():
        o_ref[...] = _ln(ALPHA * h_ref[...] + acc_ref[...], g2_ref[...], b2_ref[...])


def _moe_ln(h, gate, wg, wu, wd, g2, b2, *, tm):
    m, d = h.shape
    tm = min(tm, m)
    return pl.pallas_call(
        _moe_body,
        name=f"moe_ln_r{m}",
        grid=(m // tm, N_EXPERTS),
        in_specs=[pl.BlockSpec((tm, d), lambda i, e: (i, 0)), pl.BlockSpec((tm, N_EXPERTS), lambda i, e: (i, 0)),
                  pl.BlockSpec((1, d, D_FF), lambda i, e: (e, 0, 0)), pl.BlockSpec((1, d, D_FF), lambda i, e: (e, 0, 0)),
                  pl.BlockSpec((1, D_FF, d), lambda i, e: (e, 0, 0)),
                  pl.BlockSpec((1, d), lambda i, e: (0, 0)), pl.BlockSpec((1, d), lambda i, e: (0, 0))],
        out_specs=pl.BlockSpec((tm, d), lambda i, e: (i, 0)),
        out_shape=jax.ShapeDtypeStruct((m, d), f32),
        scratch_shapes=[pltpu.VMEM((tm, d), f32)],
        compiler_params=_cparams(("parallel", "arbitrary")),
    )(h, gate, wg, wu, wd, g2.reshape(1, d), b2.reshape(1, d))


def _ple_body(h_ref, p_ref, wg_ref, wp_ref, o_ref):
    h = h_ref[...]
    precise = _is_f32(wg_ref)
    gate = jax.nn.sigmoid(_mxu(h, wg_ref[...], precise))
    o_ref[...] = h + gate * _mxu(p_ref[...], wp_ref[...], precise)


def _ple(h, p, wg, wp, *, tm):
    m, d = h.shape
    tm = min(tm, m)
    return pl.pallas_call(
        _ple_body,
        name=f"ple_r{m}",
        grid=(m // tm,),
        in_specs=[pl.BlockSpec((tm, d), lambda i: (i, 0)), pl.BlockSpec((tm, PLE_DIM), lambda i: (i, 0)),
                  pl.BlockSpec((d, d), lambda i: (0, 0)), pl.BlockSpec((PLE_DIM, d), lambda i: (0, 0))],
        out_specs=pl.BlockSpec((tm, d), lambda i: (i, 0)),
        out_shape=jax.ShapeDtypeStruct((m, d), f32),
        compiler_params=_cparams(("parallel",)),
    )(h, p, wg, wp)


def _layer_tail(h1, p, tw, *, tm):
    gate = _router(h1, tw["rwt"], tw["rb"], tm=tm)
    h2 = _moe_ln(h1, gate, tw["wg"], tw["wu"], tw["wd"], tw["ln2_g"], tw["ln2_b"], tm=tm)
    return _ple(h2, p, tw["ple_gate"], tw["ple_proj"], tm=tm)


def _ssd_conv(x_ref, xp_ref, w_ref, b_ref):
    q = x_ref.shape[0]
    xp_ref[8:8 + q, :] = x_ref[...]
    w = w_ref[...]
    acc = b_ref[...] + xp_ref[5:5 + q, :] * w[0:1]
    for k in range(1, SSD_CONV):
        acc = acc + xp_ref[5 + k:5 + k + q, :] * w[k:k + 1]
    xp_ref[0:8, :] = xp_ref[q:q + 8, :]
    return jax.nn.silu(acc)


def _head_expand(width):
    r = lax.broadcasted_iota(i32, (SSD_HPG, SSD_HPG * width), 0)
    c = lax.broadcasted_iota(i32, (SSD_HPG, SSD_HPG * width), 1)
    return (c // width == r).astype(f32)


def _ssd_body(xs_ref, bm_ref, cm_ref, z_ref, dt_ref, dtt_ref, wx_ref, wb_ref, wc_ref, bx_ref, bb_ref, bc_ref,
              dtb_ref, dtbt_ref, alog_ref, alogt_ref, dsk_ref, ng_ref, y_ref, st_ref,
              stt_ref, xpx_ref, xpb_ref, xpc_ref):
    c = pl.program_id(2)
    q = SSD_CHUNK

    @pl.when(c == 0)
    def _():
        stt_ref[...] = jnp.zeros_like(stt_ref)
        xpx_ref[0:8, :] = jnp.zeros((8, xpx_ref.shape[1]), f32)
        xpb_ref[0:8, :] = jnp.zeros((8, xpb_ref.shape[1]), f32)
        xpc_ref[0:8, :] = jnp.zeros((8, xpc_ref.shape[1]), f32)

    xs = _ssd_conv(xs_ref, xpx_ref, wx_ref, bx_ref)
    bm = _ssd_conv(bm_ref, xpb_ref, wb_ref, bb_ref)
    cm = _ssd_conv(cm_ref, xpc_ref, wc_ref, bc_ref)
    dt = jax.nn.softplus(dt_ref[0] + dtb_ref[0])
    dtt = jax.nn.softplus(dtt_ref[0] + dtbt_ref[0])
    dta = dt * (-jnp.exp(alog_ref[0]))
    dtat = dtt * (-jnp.exp(alogt_ref[0]))
    row = lax.broadcasted_iota(i32, (q, q), 0)
    col = lax.broadcasted_iota(i32, (q, q), 1)
    causal = row >= col
    cum = jnp.dot(causal.astype(f32), dta, precision=HIGHEST, preferred_element_type=f32)
    cumt = jnp.dot(dtat, (row <= col).astype(f32), precision=HIGHEST, preferred_element_type=f32)
    e64 = _head_expand(SSD_HEADDIM)
    dtx = jnp.dot(dt, e64, precision=HIGHEST, preferred_element_type=f32)
    cumx = jnp.dot(cum, e64, precision=HIGHEST, preferred_element_type=f32)
    cum128 = jnp.dot(cum, _head_expand(q), precision=HIGHEST, preferred_element_type=f32)
    cum_last = cumx[q - 1:q, :]
    xdt = xs * dtx
    xdt_b = xdt.astype(bf16)
    bm_b = bm.astype(bf16)
    cm_b = cm.astype(bf16)
    cb = _dot_nt(cm_b, bm_b)
    head = lax.broadcasted_iota(i32, (q, SSD_GW), 1) // SSD_HEADDIM
    y = jnp.zeros((q, SSD_GW), f32)
    for r in range(SSD_HPG):
        seg = cum128[:, r * q:(r + 1) * q] - cumt[r:r + 1, :]
        decay = jnp.where(causal, jnp.exp(jnp.where(causal, seg, 0.0)), 0.0)
        yr = _dot((cb * decay).astype(bf16), xdt_b)
        y = jnp.where(head == r, yr, y)
    stt = stt_ref[...]
    y = y + _dot(cm_b, stt.astype(bf16)) * jnp.exp(cumx) + dsk_ref[0] * xs
    to_end = jnp.exp(cum_last - cumx)
    stt_new = stt * jnp.exp(cum_last) + _dot_tn(bm_b, (xdt * to_end).astype(bf16))
    stt_ref[...] = stt_new
    yz = y * jax.nn.silu(z_ref[...])
    yn = yz * lax.rsqrt(jnp.mean(yz * yz, axis=-1, keepdims=True) + 1e-5) * ng_ref[...]
    y_ref[...] = yn.astype(y_ref.dtype)

    @pl.when(c == pl.num_programs(2) - 1)
    def _():
        st_ref[0] = stt_new.T.reshape(SSD_HPG, SSD_HEADDIM, SSD_STATE)


def _ssd_prompt_scan(xbc, z, dt_raw, sw, bsz, t):
    m = bsz * t
    nc = t // SSD_CHUNK
    q, gw, n, g_, hpg = SSD_CHUNK, SSD_GW, SSD_STATE, SSD_GROUPS, SSD_HPG
    dt_g = dt_raw.reshape(m, g_, hpg).transpose(1, 0, 2)
    dtt_g = dt_g.transpose(0, 2, 1)
    nxb = SSD_D_INNER // n
    row = lambda b, g, c: b * nc + c
    in_specs = [
        pl.BlockSpec((q, gw), lambda b, g, c: (row(b, g, c), g)),
        pl.BlockSpec((q, n), lambda b, g, c: (row(b, g, c), nxb + g)),
        pl.BlockSpec((q, n), lambda b, g, c: (row(b, g, c), nxb + g_ + g)),
        pl.BlockSpec((q, gw), lambda b, g, c: (row(b, g, c), g)),
        pl.BlockSpec((1, q, hpg), lambda b, g, c: (g, row(b, g, c), 0)),
        pl.BlockSpec((1, hpg, q), lambda b, g, c: (g, 0, row(b, g, c))),
        pl.BlockSpec((SSD_CONV, gw), lambda b, g, c: (0, g)),
        pl.BlockSpec((SSD_CONV, n), lambda b, g, c: (0, nxb + g)),
        pl.BlockSpec((SSD_CONV, n), lambda b, g, c: (0, nxb + g_ + g)),
        pl.BlockSpec((1, gw), lambda b, g, c: (0, g)),
        pl.BlockSpec((1, n), lambda b, g, c: (0, nxb + g)),
        pl.BlockSpec((1, n), lambda b, g, c: (0, nxb + g_ + g)),
        pl.BlockSpec((1, 1, hpg), lambda b, g, c: (g, 0, 0)),
        pl.BlockSpec((1, hpg, 1), lambda b, g, c: (g, 0, 0)),
        pl.BlockSpec((1, 1, hpg), lambda b, g, c: (g, 0, 0)),
        pl.BlockSpec((1, hpg, 1), lambda b, g, c: (g, 0, 0)),
        pl.BlockSpec((1, 1, gw), lambda b, g, c: (g, 0, 0)),
        pl.BlockSpec((1, gw), lambda b, g, c: (0, g)),
    ]
    y, st = pl.pallas_call(
        _ssd_body,
        name="ssd_scan",
        grid=(bsz, g_, nc),
        in_specs=in_specs,
        out_specs=[pl.BlockSpec((q, gw), lambda b, g, c: (row(b, g, c), g)),
                   pl.BlockSpec((1, hpg, SSD_HEADDIM, n), lambda b, g, c: (b, g, 0, 0))],
        out_shape=[jax.ShapeDtypeStruct((m, SSD_D_INNER), bf16),
                   jax.ShapeDtypeStruct((bsz, SSD_HEADS, SSD_HEADDIM, n), f32)],
        scratch_shapes=[pltpu.VMEM((n, gw), f32), pltpu.VMEM((q + 8, gw), f32),
                        pltpu.VMEM((q + 8, n), f32), pltpu.VMEM((q + 8, n), f32)],
        compiler_params=_cparams(("arbitrary", "arbitrary", "arbitrary")),
    )(xbc, xbc, xbc, z, dt_g, dtt_g, sw["conv_w"], sw["conv_w"], sw["conv_w"], sw["conv_b"], sw["conv_b"], sw["conv_b"],
      sw["dtb"], sw["dtbt"], sw["alog"], sw["alogt"], sw["dskx"], sw["norm_g"])
    return y, st


def _ssd_weights(w_in, conv_w, conv_b, dt_bias, a_log, d_skip, norm_g, w_out, wdt):
    g_, hpg = SSD_GROUPS, SSD_HPG
    return dict(
        w_z=w_in[:, :SSD_D_INNER].astype(wdt),
        w_xbc=w_in[:, SSD_D_INNER:SSD_D_INNER + SSD_CONV_DIM].astype(wdt),
        w_dt=w_in[:, SSD_D_INNER + SSD_CONV_DIM:].astype(wdt),
        conv_w=conv_w, conv_b=conv_b.reshape(1, SSD_CONV_DIM),
        dtb=dt_bias.reshape(g_, 1, hpg), dtbt=dt_bias.reshape(g_, hpg, 1),
        alog=a_log.reshape(g_, 1, hpg), alogt=a_log.reshape(g_, hpg, 1),
        dskx=jnp.repeat(d_skip, SSD_HEADDIM).reshape(g_, 1, SSD_GW),
        dsk=d_skip, dt_bias=dt_bias, a_log=a_log,
        norm_g=norm_g.reshape(1, SSD_D_INNER), w_out=w_out.astype(wdt))


def _ssd_prompt(x2d, sw, bsz, t, ln_g, ln_b):
    (z,) = _mm(x2d, sw["w_z"], tm=512, tn=1024, out_dtypes=(f32,), name="ssd_in_z")
    (xbc,) = _mm(x2d, sw["w_xbc"], tm=512, tn=1024, out_dtypes=(f32,), name="ssd_in_xbc")
    (dt_raw,) = _mm(x2d, sw["w_dt"], tm=512, tn=SSD_HEADS, out_dtypes=(f32,), name="ssd_in_dt")
    y, st = _ssd_prompt_scan(xbc, z, dt_raw, sw, bsz, t)
    h1 = _mm_ln(y, sw["w_out"], x2d, ln_g, ln_b, tm=512, name="ssd_out_ln")
    new_conv = xbc.reshape(bsz, t, SSD_CONV_DIM)[:, t - (SSD_CONV - 1):]
    return h1, new_conv, st


def _pad_rows(x, rows=8):
    return jnp.concatenate([x, jnp.zeros((rows - x.shape[0], x.shape[1]), x.dtype)], axis=0)


def _ssd_step_body(z_ref, xbc_ref, dt_ref, cs_ref, st_ref, cw_ref, cb_ref, dtb_ref, alog_ref, dsk_ref, ng_ref,
                   y_ref, nc_ref, ns_ref):
    n, gw, hd = SSD_STATE, SSD_GW, SSD_HEADDIM
    xbc = xbc_ref[0]
    cs = cs_ref[0]
    w = cw_ref[...]
    conv = cb_ref[...] + cs[0:1] * w[0:1]
    for k in range(1, SSD_CONV - 1):
        conv = conv + cs[k:k + 1] * w[k:k + 1]
    conv = jax.nn.silu(conv + xbc * w[SSD_CONV - 1:SSD_CONV])
    nc_ref[0] = jnp.concatenate([cs[1:], xbc], axis=0)
    xs = conv[:, :SSD_D_INNER]
    b_g = _pad_rows(jnp.concatenate(
        [conv[:, SSD_D_INNER + g * n:SSD_D_INNER + (g + 1) * n] for g in range(SSD_GROUPS)], axis=0))
    c_g = _pad_rows(jnp.concatenate(
        [conv[:, SSD_D_INNER + (SSD_GROUPS + g) * n:SSD_D_INNER + (SSD_GROUPS + g + 1) * n] for g in range(SSD_GROUPS)],
        axis=0))
    dt = jax.nn.softplus(dt_ref[0] + dtb_ref[...])
    decay = jnp.exp(dt * (-jnp.exp(alog_ref[...])))
    er = lax.broadcasted_iota(i32, (SSD_HEADS, SSD_D_INNER), 0)
    ec = lax.broadcasted_iota(i32, (SSD_HEADS, SSD_D_INNER), 1)
    per_head = _pad_rows(jnp.concatenate([dt, decay, dsk_ref[...]], axis=0))
    hx = jnp.dot(per_head, (ec // hd == er).astype(f32), precision=HIGHEST, preferred_element_type=f32)
    dtx, decx, dskx = hx[0:1], hx[1:2], hx[2:3]
    xdt = xs * dtx
    gr = lax.broadcasted_iota(i32, (8, SSD_D_INNER), 0)
    gc = lax.broadcasted_iota(i32, (8, SSD_D_INNER), 1)
    gmask = (gc // gw == gr).astype(f32)
    row0 = (gr == 0).astype(f32)
    st = st_ref[0].reshape(SSD_D_INNER, n)
    upd = _dot_tn(gmask * xdt, b_g, precision=HIGHEST)
    dec_full = _dot_tn(row0 * decx, jnp.ones((8, n), f32), precision=HIGHEST)
    ns_ref[0] = (st * dec_full + upd).reshape(SSD_HEADS, hd, n)
    cst = _dot_nt(c_g, st, precision=HIGHEST)
    y_state = jnp.sum(gmask * cst, axis=0, keepdims=True)
    cbx = jnp.sum(gmask * jnp.sum(c_g * b_g, axis=1, keepdims=True), axis=0, keepdims=True)
    y = cbx * xdt + decx * y_state + dskx * xs
    yz = y * jax.nn.silu(z_ref[0])
    parts = []
    for g in range(SSD_GROUPS):
        seg = yz[:, g * gw:(g + 1) * gw]
        parts.append(seg * lax.rsqrt(jnp.mean(seg * seg, axis=-1, keepdims=True) + 1e-5))
    y_ref[0] = jnp.concatenate(parts, axis=1) * ng_ref[...]


def _ssd_step(x2d, conv_state, ssm_state, sw, ln_g, ln_b):
    bsz = x2d.shape[0]
    (z,) = _mm(x2d, sw["w_z"], tm=bsz, tn=1024, out_dtypes=(f32,), name="ssd_in_z")
    (xbc,) = _mm(x2d, sw["w_xbc"], tm=bsz, tn=1024, out_dtypes=(f32,), name="ssd_in_xbc")
    (dt_raw,) = _mm(x2d, sw["w_dt"], tm=bsz, tn=SSD_HEADS, out_dtypes=(f32,), name="ssd_in_dt")
    rowspec = lambda width: pl.BlockSpec((1, 1, width), lambda b: (b, 0, 0))
    full = lambda r, c: pl.BlockSpec((r, c), lambda b: (0, 0))
    y, new_conv, new_state = pl.pallas_call(
        _ssd_step_body,
        name="ssd_step",
        grid=(bsz,),
        in_specs=[rowspec(SSD_D_INNER), rowspec(SSD_CONV_DIM), rowspec(SSD_HEADS),
                  pl.BlockSpec((1, SSD_CONV - 1, SSD_CONV_DIM), lambda b: (b, 0, 0)),
                  pl.BlockSpec((1, SSD_HEADS, SSD_HEADDIM, SSD_STATE), lambda b: (b, 0, 0, 0)),
                  full(SSD_CONV, SSD_CONV_DIM), full(1, SSD_CONV_DIM), full(1, SSD_HEADS), full(1, SSD_HEADS),
                  full(1, SSD_HEADS), full(1, SSD_D_INNER)],
        out_specs=[rowspec(SSD_D_INNER),
                   pl.BlockSpec((1, SSD_CONV - 1, SSD_CONV_DIM), lambda b: (b, 0, 0)),
                   pl.BlockSpec((1, SSD_HEADS, SSD_HEADDIM, SSD_STATE), lambda b: (b, 0, 0, 0))],
        out_shape=[jax.ShapeDtypeStruct((bsz, 1, SSD_D_INNER), f32),
                   jax.ShapeDtypeStruct((bsz, SSD_CONV - 1, SSD_CONV_DIM), f32),
                   jax.ShapeDtypeStruct((bsz, SSD_HEADS, SSD_HEADDIM, SSD_STATE), f32)],
        compiler_params=_cparams(("parallel",)),
    )(z.reshape(bsz, 1, -1), xbc.reshape(bsz, 1, -1), dt_raw.reshape(bsz, 1, -1), conv_state, ssm_state,
      sw["conv_w"], sw["conv_b"], sw["dt_bias"].reshape(1, -1), sw["a_log"].reshape(1, -1), sw["dsk"].reshape(1, -1),
      sw["norm_g"])
    h1 = _mm_ln(y.reshape(bsz, SSD_D_INNER), sw["w_out"], x2d, ln_g, ln_b, tm=bsz, name="ssd_out_ln")
    return h1, new_conv, new_state


def _slope(head):
    return 2.0 ** (-8.0 * (head + 1) / NSA_HEADS)


def _masked_softmax(s, mask):
    s = jnp.where(mask, s, NEG)
    m = jnp.max(s, axis=-1, keepdims=True)
    e = jnp.exp(s - m) * mask.astype(f32)
    return e / jnp.maximum(jnp.sum(e, axis=-1, keepdims=True), 1e-30)


def _tile_rows(x, n):
    return jnp.concatenate([x] * n, axis=0)


def _slope_col(kv_head, rows):
    return jnp.concatenate([jnp.full((rows, 1), _slope(kv_head * NSA_REP + r), f32) for r in range(NSA_REP)], axis=0)


def _topk_rank(imp):
    nb = imp.shape[1]
    lane = lax.broadcasted_iota(i32, imp.shape, 1)
    rank = jnp.zeros(imp.shape, i32)
    for j in range(nb):
        cj = imp[:, j:j + 1]
        rank = rank + ((cj > imp) | ((cj == imp) & (lane > j))).astype(i32)
    return rank


def _topk_mask(imp, n_sel):
    return _topk_rank(imp) < n_sel


def _pos_weights(pw_ref):
    w = jax.nn.softmax(pw_ref[...], axis=0)
    r = lax.broadcasted_iota(i32, (2 * NSA_KV, 2 * NSA_KVD), 0)
    c = lax.broadcasted_iota(i32, (2 * NSA_KV, 2 * NSA_KVD), 1)
    return jnp.dot(w, (c // NSA_DH == r).astype(f32), precision=HIGHEST, preferred_element_type=f32)


def _compress_body(x_ref, pw_ref, o_ref):
    wx = _pos_weights(pw_ref)
    x = x_ref[...]
    nb = x.shape[0] // NSA_BLOCK
    o_ref[0] = jnp.sum(x.reshape(nb, NSA_BLOCK, 2 * NSA_KVD) * wx[None], axis=1)


def _nsa_compress(kv4, pw, bsz, t):
    nb = t // NSA_BLOCK
    return pl.pallas_call(
        _compress_body,
        name="nsa_compress",
        grid=(bsz,),
        in_specs=[pl.BlockSpec((t, 2 * NSA_KVD), lambda b: (b, 0)),
                  pl.BlockSpec((NSA_BLOCK, 2 * NSA_KV), lambda b: (0, 0))],
        out_specs=pl.BlockSpec((1, nb, 2 * NSA_KVD), lambda b: (b, 0, 0)),
        out_shape=jax.ShapeDtypeStruct((bsz, nb, 2 * NSA_KVD), f32),
        compiler_params=_cparams(("parallel",)),
    )(kv4, pw)


NSA_TQ = 128
NSA_CK = 512


def _nsa_prompt_body(q_ref, g_ref, kcvc_ref, kvs_ref, kvw_ref, o_ref, *, t_len):
    i = pl.program_id(1)
    tq, ck, rep = NSA_TQ, NSA_CK, NSA_REP
    ck = min(ck, t_len)
    nb = t_len // NSA_BLOCK
    n_sel = min(NSA_TOP, nb)
    wlen = min(NSA_WINDOW + tq, t_len)
    t0 = i * tq
    t_col = t0 + lax.broadcasted_iota(i32, (tq, 1), 0)
    lane128 = lax.broadcasted_iota(i32, (1, LANES), 1)
    gates = g_ref[...]
    kcvc = kcvc_ref[0]
    blk = lax.broadcasted_iota(i32, (1, nb), 1)
    dist_c = t_col - ((blk + 1) * NSA_BLOCK - 1)
    dist_c4 = _tile_rows(dist_c, rep)
    cur = t_col // NSA_BLOCK
    n_chunks = (t0 + tq + ck - 1) // ck
    w_start = jnp.clip(t0 - NSA_WINDOW, 0, t_len - wlen)
    w_start = pl.multiple_of(w_start, tq)
    dist_w = t_col - (w_start + lax.broadcasted_iota(i32, (1, wlen), 1))
    dist_w4 = _tile_rows(dist_w, rep)
    mask_w4 = (dist_w4 >= 0) & (dist_w4 < NSA_WINDOW)

    for j in range(NSA_KV // 2):
        cols = slice(j * LANES, (j + 1) * LANES)
        vcols = slice(NSA_KVD + j * LANES, NSA_KVD + (j + 1) * LANES)
        q_tiles = [q_ref[:, (j * rep + r) * LANES:(j * rep + r + 1) * LANES] for r in range(rep)]
        q_rows = jnp.concatenate(q_tiles, axis=0)
        kc_b = kcvc[:, cols].astype(bf16)
        vc_b = kcvc[:, vcols].astype(bf16)
        out_tiles = [jnp.zeros((tq, LANES), f32) for _ in range(rep)]
        for h in range(2):
            k = 2 * j + h
            half = (lane128 // NSA_DH) == h
            q_pad = jnp.where(half, q_rows, jnp.zeros_like(q_rows))
            slope = _slope_col(k, tq)
            s_c = _dot_nt(q_pad, kc_b) - slope * dist_c4.astype(f32)
            p_c = _masked_softmax(s_c, dist_c4 >= 0)
            o_c = _dot(p_c.astype(bf16), vc_b)
            imp = sum(p_c[r * tq:(r + 1) * tq] for r in range(rep))
            imp = jnp.where((blk == cur) | (blk == 0), 1e4, imp)
            imp = jnp.where(blk > cur, -1.0, imp)
            sel_b = _topk_mask(imp, n_sel).astype(bf16)

            def sel_step(c, carry, q_pad=q_pad, slope=slope, sel_b=sel_b, cols=cols, vcols=vcols):
                m, l, acc = carry
                k0 = pl.multiple_of(c * ck, ck)
                ks = kvs_ref[pl.ds(k0, ck), cols]
                vs = kvs_ref[pl.ds(k0, ck), vcols]
                kpos = k0 + lax.broadcasted_iota(i32, (1, ck), 1)
                brow = lax.broadcasted_iota(i32, (nb, ck), 0)
                expand = ((k0 + lax.broadcasted_iota(i32, (nb, ck), 1)) // NSA_BLOCK == brow).astype(bf16)
                selx = _dot(sel_b, expand)
                dist = t_col - kpos
                mask = _tile_rows((selx > 0.5) & (dist >= 0), rep)
                s = _dot_nt(q_pad, ks) - slope * _tile_rows(dist, rep).astype(f32)
                s = jnp.where(mask, s, NEG)
                m_new = jnp.maximum(m, jnp.max(s, axis=-1, keepdims=True))
                a = jnp.exp(m - m_new)
                p = jnp.exp(s - m_new) * mask.astype(f32)
                l = a * l + jnp.sum(p, axis=-1, keepdims=True)
                acc = a * acc + _dot(p.astype(bf16), vs)
                return m_new, l, acc

            init = (jnp.full((rep * tq, 1), NEG, f32), jnp.zeros((rep * tq, 1), f32), jnp.zeros((rep * tq, LANES), f32))
            _, l_s, acc_s = lax.fori_loop(0, n_chunks, sel_step, init)
            o_s = acc_s / jnp.maximum(l_s, 1e-30)
            kw = kvw_ref[pl.ds(w_start, wlen), cols]
            vw = kvw_ref[pl.ds(w_start, wlen), vcols]
            s_w = _dot_nt(q_pad, kw) - slope * dist_w4.astype(f32)
            p_w = _masked_softmax(s_w, mask_w4)
            o_w = _dot(p_w.astype(bf16), vw)
            for r in range(rep):
                gc = (k * rep + r) * 3
                rows = slice(r * tq, (r + 1) * tq)
                o_r = gates[:, gc:gc + 1] * o_c[rows] + gates[:, gc + 1:gc + 2] * o_s[rows] + gates[:, gc + 2:gc + 3] * o_w[rows]
                out_tiles[r] = jnp.where(half, o_r, out_tiles[r])
        for r in range(rep):
            o_ref[:, (j * rep + r) * LANES:(j * rep + r + 1) * LANES] = out_tiles[r].astype(o_ref.dtype)


def _nsa_prompt_attn(q, gates, kcvc, kv4_b, kvw_b, bsz, t):
    nq = t // NSA_TQ
    nb = t // NSA_BLOCK
    return pl.pallas_call(
        functools.partial(_nsa_prompt_body, t_len=t),
        name="nsa_attn",
        grid=(bsz, nq),
        in_specs=[pl.BlockSpec((NSA_TQ, NSA_QD), lambda b, i: (b * nq + i, 0)),
                  pl.BlockSpec((NSA_TQ, LANES), lambda b, i: (b * nq + i, 0)),
                  pl.BlockSpec((1, nb, 2 * NSA_KVD), lambda b, i: (b, 0, 0)),
                  pl.BlockSpec((t, 2 * NSA_KVD), lambda b, i: (b, 1)),
                  pl.BlockSpec((t, 2 * NSA_KVD), lambda b, i: (b, 0))],
        out_specs=pl.BlockSpec((NSA_TQ, NSA_QD), lambda b, i: (b * nq + i, 0)),
        out_shape=jax.ShapeDtypeStruct((bsz * t, NSA_QD), bf16),
        compiler_params=_cparams(("parallel", "arbitrary")),
    )(q, gates, kcvc, kv4_b, kvw_b)


def _pair_layout_cols(w):
    lead = w.shape[:-1]
    w = w.reshape(*lead, NSA_KV // 2, 2, NSA_REP, NSA_DH)
    return jnp.swapaxes(w, -3, -2).reshape(*lead, NSA_QD)


def _nsa_weights(w_in, pos_w, w_out, wdt):
    kv0 = NSA_QD
    g0 = NSA_QD + 6 * NSA_KVD
    wg = jnp.zeros((D_MODEL, LANES), f32).at[:, :3 * NSA_HEADS].set(w_in[:, g0:])
    return dict(
        w_q=_pair_layout_cols(w_in[:, :NSA_QD]).astype(wdt),
        w_kv4=w_in[:, kv0:kv0 + 4 * NSA_KVD].astype(wdt),
        w_kvw=w_in[:, kv0 + 4 * NSA_KVD:g0].astype(wdt),
        w_g=wg.astype(wdt),
        pw=pos_w.transpose(1, 0, 2).reshape(NSA_BLOCK, 2 * NSA_KV),
        w_out=_pair_layout_cols(w_out.T).T.astype(wdt))


def _nsa_prompt(x2d, nw, bsz, t, ln_g, ln_b):
    (q,) = _mm(x2d, nw["w_q"], tm=512, tn=1024, out_dtypes=(bf16,), scale=NSA_DH ** -0.5, name="nsa_in_q")
    kv4, kv4_b = _mm(x2d, nw["w_kv4"], tm=512, tn=1024, out_dtypes=(f32, bf16), name="nsa_in_kv4")
    kvw, kvw_b = _mm(x2d, nw["w_kvw"], tm=512, tn=512, out_dtypes=(f32, bf16), name="nsa_in_kvw")
    (gates,) = _mm(x2d, nw["w_g"], tm=512, tn=LANES, out_dtypes=(f32,), act="sigmoid", name="nsa_in_gates")
    kcvc = _nsa_compress(kv4, nw["pw"], bsz, t)
    o = _nsa_prompt_attn(q, gates, kcvc, kv4_b, kvw_b, bsz, t)
    h1 = _mm_ln(o, nw["w_out"], x2d, ln_g, ln_b, tm=512, name="nsa_out_ln")
    w_keep = min(NSA_WINDOW, t)
    new_kv = kv4.reshape(bsz, t, 4, NSA_KV, NSA_DH)
    new_win = kvw.reshape(bsz, t, 2, NSA_KV, NSA_DH)[:, t - w_keep:]
    return h1, new_kv, new_win


PAGES_PER_STEP = 16


def _page_compress_body(pt_ref, *refs):
    del pt_ref
    page_refs, pw_ref, o_ref = refs[:PAGES_PER_STEP], refs[PAGES_PER_STEP], refs[PAGES_PER_STEP + 1]
    wx = _pos_weights(pw_ref)
    bpp = PAGE_SIZE // NSA_BLOCK
    for i, p_ref in enumerate(page_refs):
        o_ref[0, i] = jnp.sum(p_ref[0].reshape(bpp, NSA_BLOCK, 2 * NSA_KVD) * wx[None], axis=1)


def _page_compress(pool, page_table, pw):
    bsz, n_pages = page_table.shape
    bpp = PAGE_SIZE // NSA_BLOCK
    steps = n_pages // PAGES_PER_STEP
    assert n_pages % PAGES_PER_STEP == 0
    page_specs = [pl.BlockSpec((1, PAGE_SIZE, 2 * NSA_KVD),
                               lambda b, s, pt, i=i: (pt[b, s * PAGES_PER_STEP + i], 0, 0))
                  for i in range(PAGES_PER_STEP)]
    out = pl.pallas_call(
        _page_compress_body,
        name="nsa_page_compress",
        grid_spec=pltpu.PrefetchScalarGridSpec(
            num_scalar_prefetch=1, grid=(bsz, steps),
            in_specs=page_specs + [pl.BlockSpec((NSA_BLOCK, 2 * NSA_KV), lambda b, s, pt: (0, 0))],
            out_specs=pl.BlockSpec((1, PAGES_PER_STEP, bpp, 2 * NSA_KVD), lambda b, s, pt: (b, s, 0, 0))),
        out_shape=jax.ShapeDtypeStruct((bsz, n_pages, bpp, 2 * NSA_KVD), f32),
        compiler_params=_cparams(("parallel", "arbitrary")),
    )(page_table, *([pool] * PAGES_PER_STEP), pw)
    return out.reshape(bsz, n_pages * bpp, 2 * NSA_KVD)


def _step_slopes(kv_head):
    r = lax.broadcasted_iota(i32, (8, 1), 0)
    return jnp.exp2(-0.5 * (kv_head * NSA_REP + r + 1).astype(f32))


def _nsa_step_cw_body(q_ref, g_ref, kcvc_ref, kvn_ref, pw_ref, win_ref, kvwn_ref, o_ref, idx_ref, *, past_len):
    rep = NSA_REP
    nbp = past_len // NSA_BLOCK
    nbt = nbp + 8
    w_buf = win_ref.shape[1]
    wx0 = _pos_weights(pw_ref)[0:1]
    new_c = wx0 * kvn_ref[0][:, :2 * NSA_KVD]
    lane128 = lax.broadcasted_iota(i32, (1, LANES), 1)
    blk = lax.broadcasted_iota(i32, (1, nbt), 1)
    dist_c = past_len - ((blk + 1) * NSA_BLOCK - 1)
    cur = past_len // NSA_BLOCK
    wi = lax.broadcasted_iota(i32, (1, w_buf + 8), 1)
    dist_w = jnp.where(wi <= w_buf, w_buf - wi, -1)
    mask_w = (dist_w >= 0) & (dist_w < NSA_WINDOW)
    gates = g_ref[0]
    zpad = jnp.zeros((7, LANES), f32)
    for j in range(NSA_KV // 2):
        cols = slice(j * LANES, (j + 1) * LANES)
        vcols = slice(NSA_KVD + j * LANES, NSA_KVD + (j + 1) * LANES)
        q_rows = _pad_rows(jnp.concatenate(
            [q_ref[0][:, (j * rep + r) * LANES:(j * rep + r + 1) * LANES] for r in range(rep)], axis=0))
        kc_b = jnp.concatenate([kcvc_ref[0][:, cols], new_c[:, cols], zpad], axis=0)
        vc_b = jnp.concatenate([kcvc_ref[0][:, vcols], new_c[:, vcols], zpad], axis=0)
        kw_b = jnp.concatenate([win_ref[0][:, cols], kvwn_ref[0][:, cols], zpad], axis=0)
        vw_b = jnp.concatenate([win_ref[0][:, vcols], kvwn_ref[0][:, vcols], zpad], axis=0)
        out_tiles = [jnp.zeros((1, LANES), f32) for _ in range(rep)]
        for h in range(2):
            k = 2 * j + h
            half = (lane128 // NSA_DH) == h
            q_pad = jnp.where(half, q_rows, jnp.zeros_like(q_rows))
            slope = _step_slopes(k)
            s_c = _mxu_nt(q_pad, kc_b, True) - slope * dist_c.astype(f32)
            p_c = _masked_softmax(s_c, jnp.broadcast_to(dist_c >= 0, s_c.shape))
            o_c = _mxu(p_c, vc_b, True)
            imp = jnp.sum(p_c[0:rep], axis=0, keepdims=True)
            imp = jnp.where((blk == cur) | (blk == 0), 1e4, imp)
            imp = jnp.where(blk > cur, -2.0, imp)
            rank = _topk_rank(imp)
            pick = lax.broadcasted_iota(i32, (NSA_TOP, nbt), 0) == rank
            idx_ref[0, k] = jnp.sum(jnp.where(pick, blk.astype(f32), 0.0), axis=1, keepdims=True).astype(i32)
            s_w = _mxu_nt(q_pad, kw_b, True) - slope * dist_w.astype(f32)
            p_w = _masked_softmax(s_w, jnp.broadcast_to(mask_w, s_w.shape))
            o_w = _mxu(p_w, vw_b, True)
            for r in range(rep):
                gc = (k * rep + r) * 3
                o_r = gates[:, gc:gc + 1] * o_c[r:r + 1] + gates[:, gc + 2:gc + 3] * o_w[r:r + 1]
                out_tiles[r] = jnp.where(half, o_r, out_tiles[r])
        for r in range(rep):
            o_ref[0, j, r:r + 1, :] = out_tiles[r]


def _nsa_step_sel_body(idx_ref, pt_ref, *refs, past_len):
    del pt_ref
    n_in = 2 * NSA_TOP
    blk_refs = refs[:n_in]
    q_ref, g_ref, kn_ref, vn_ref, part_ref, o_ref = refs[n_in:]
    b, k = pl.program_id(0), pl.program_id(1)
    h = k % 2
    nbp = past_len // NSA_BLOCK
    rep, lb = NSA_REP, NSA_BLOCK
    lane128 = lax.broadcasted_iota(i32, (1, LANES), 1)
    half = (lane128 // NSA_DH) == h
    q_rows = _pad_rows(q_ref[0, 0])
    q_pad = jnp.where(half, q_rows, jnp.zeros_like(q_rows))
    row0 = lax.broadcasted_iota(i32, (lb, LANES), 0) == 0
    new_k = jnp.where(row0, kn_ref[0, 0], 0.0)
    new_v = jnp.where(row0, vn_ref[0, 0], 0.0)
    lane_k = lax.broadcasted_iota(i32, (1, NSA_TOP * lb), 1)
    spos = lane_k % lb
    ks, vs = [], []
    for i in range(NSA_TOP):
        n = idx_ref[(b * NSA_KV + k) * NSA_TOP + i]
        is_new = n >= nbp
        ks.append(jnp.where(is_new, new_k, blk_refs[2 * i][0, 0]))
        vs.append(jnp.where(is_new, new_v, blk_refs[2 * i + 1][0, 0]))
        spos = spos + jnp.where(lane_k // lb == i, n * lb, 0)
    k_all = jnp.concatenate(ks, axis=0)
    v_all = jnp.concatenate(vs, axis=0)
    dist = past_len - spos
    s = _mxu_nt(q_pad, k_all, True) - _step_slopes(k) * dist.astype(f32)
    p = _masked_softmax(s, jnp.broadcast_to(dist >= 0, s.shape))
    o_s = _mxu(p, v_all, True)
    g = g_ref[0, 0]
    rows = [g[:, 3 * r + 1:3 * r + 2] * o_s[r:r + 1] for r in range(rep)]
    add = jnp.where(half, jnp.concatenate(rows, axis=0), 0.0)

    @pl.when(h == 0)
    def _():
        o_ref[0, 0] = part_ref[0, 0] + add

    @pl.when(h != 0)
    def _():
        o_ref[0, 0] = o_ref[0, 0] + add


def _nsa_step(x2d, pool, page_table, win_cache, nw, ln_g, ln_b):
    bsz = x2d.shape[0]
    n_pages = page_table.shape[1]
    past_len = n_pages * PAGE_SIZE
    w_buf = win_cache.shape[1]
    rep = NSA_REP
    (q,) = _mm(x2d, nw["w_q"], tm=bsz, tn=1024, out_dtypes=(f32,), scale=NSA_DH ** -0.5, name="nsa_in_q")
    (kv4,) = _mm(x2d, nw["w_kv4"], tm=bsz, tn=1024, out_dtypes=(f32,), name="nsa_in_kv4")
    (kvw,) = _mm(x2d, nw["w_kvw"], tm=bsz, tn=512, out_dtypes=(f32,), name="nsa_in_kvw")
    (gates,) = _mm(x2d, nw["w_g"], tm=bsz, tn=LANES, out_dtypes=(f32,), act="sigmoid", name="nsa_in_gates")
    pool2 = pool.reshape(pool.shape[0], PAGE_SIZE, 4 * NSA_KVD)
    kcvc = _page_compress(pool2, page_table, nw["pw"])
    nbp = past_len // NSA_BLOCK
    row3 = lambda width: pl.BlockSpec((1, 1, width), lambda b: (b, 0, 0))
    part, idx = pl.pallas_call(
        functools.partial(_nsa_step_cw_body, past_len=past_len),
        name="nsa_step_cw",
        grid=(bsz,),
        in_specs=[row3(NSA_QD), row3(LANES), pl.BlockSpec((1, nbp, 2 * NSA_KVD), lambda b: (b, 0, 0)),
                  row3(4 * NSA_KVD), pl.BlockSpec((NSA_BLOCK, 2 * NSA_KV), lambda b: (0, 0)),
                  pl.BlockSpec((1, w_buf, 2 * NSA_KVD), lambda b: (b, 0, 0)), row3(2 * NSA_KVD)],
        out_specs=[pl.BlockSpec((1, NSA_KV // 2, rep, LANES), lambda b: (b, 0, 0, 0)),
                   pl.BlockSpec((1, NSA_KV, NSA_TOP, 1), lambda b: (b, 0, 0, 0))],
        out_shape=[jax.ShapeDtypeStruct((bsz, NSA_KV // 2, rep, LANES), f32),
                   jax.ShapeDtypeStruct((bsz, NSA_KV, NSA_TOP, 1), i32)],
        compiler_params=_cparams(("parallel",)),
    )(q.reshape(bsz, 1, -1), gates.reshape(bsz, 1, -1), kcvc, kv4.reshape(bsz, 1, -1), nw["pw"],
      win_cache.reshape(bsz, w_buf, 2 * NSA_KVD), kvw.reshape(bsz, 1, -1))

    bpp = PAGE_SIZE // NSA_BLOCK
    pool4 = pool.reshape(pool.shape[0], bpp, NSA_BLOCK, 4 * NSA_KVD)
    k_col0 = 2 * NSA_KVD // LANES
    v_col0 = 3 * NSA_KVD // LANES

    def blk_spec(i, col0):
        def imap(b, k, idx_r, pt_r):
            n = jnp.minimum(idx_r[(b * NSA_KV + k) * NSA_TOP + i], nbp - 1)
            return (pt_r[b * n_pages + n // bpp], n % bpp, 0, col0 + k // 2)
        return pl.BlockSpec((1, 1, NSA_BLOCK, LANES), imap)

    blk_specs = []
    for i in range(NSA_TOP):
        blk_specs += [blk_spec(i, k_col0), blk_spec(i, v_col0)]
    kv_new = kv4.reshape(bsz, 4 * NSA_KVD // LANES, 1, LANES)
    o = pl.pallas_call(
        functools.partial(_nsa_step_sel_body, past_len=past_len),
        name="nsa_step_sel",
        grid_spec=pltpu.PrefetchScalarGridSpec(
            num_scalar_prefetch=2, grid=(bsz, NSA_KV),
            in_specs=blk_specs + [
                pl.BlockSpec((1, 1, rep, LANES), lambda b, k, *_: (b, k // 2, 0, 0)),
                pl.BlockSpec((1, 1, 1, 3 * rep), lambda b, k, *_: (b, k, 0, 0)),
                pl.BlockSpec((1, 1, 1, LANES), lambda b, k, *_: (b, k_col0 + k // 2, 0, 0)),
                pl.BlockSpec((1, 1, 1, LANES), lambda b, k, *_: (b, v_col0 + k // 2, 0, 0)),
                pl.BlockSpec((1, 1, rep, LANES), lambda b, k, *_: (b, k // 2, 0, 0))],
            out_specs=pl.BlockSpec((1, 1, rep, LANES), lambda b, k, *_: (b, k // 2, 0, 0))),
        out_shape=jax.ShapeDtypeStruct((bsz, NSA_KV // 2, rep, LANES), f32),
        compiler_params=_cparams(("arbitrary", "arbitrary")),
    )(idx.reshape(-1), page_table.reshape(-1), *([pool4] * (2 * NSA_TOP)),
      q.reshape(bsz, NSA_KV // 2, rep, LANES), gates[:, :3 * NSA_HEADS].reshape(bsz, NSA_KV, 1, 3 * rep),
      kv_new, kv_new, part)
    h1 = _mm_ln(o.reshape(bsz, NSA_QD), nw["w_out"], x2d, ln_g, ln_b, tm=bsz, name="nsa_out_ln")
    new_kv = kv4.reshape(bsz, 1, 4, NSA_KV, NSA_DH)
    new_win = jnp.concatenate([win_cache[:, 1:], kvw.reshape(bsz, 1, 2, NSA_KV, NSA_DH)], axis=1)
    return h1, new_kv, new_win


def kernel(x_prompt, x_sample, state_ssm, state_conv, cache_kv, cache_win, page_table, p_prompt, p_sample,
           ssd_w_in, ssd_conv_w, ssd_conv_b, ssd_dt_bias, ssd_a_log, ssd_d, ssd_norm_g, ssd_w_out,
           nsa_w_in, nsa_pos_w, nsa_w_out, ln1_g, ln1_b, ln2_g, ln2_b, router_w, router_bias,
           moe_w_gate, moe_w_up, moe_w_down, ple_proj, ple_gate):
    bp, t, d = x_prompt.shape
    bs = x_sample.shape[0]
    xp = x_prompt.reshape(bp * t, d)
    xs = x_sample.reshape(bs, d)
    rb = router_bias.reshape(N_EXPERTS, 1)
    ssm_p, conv_p, kv_p, win_p, ssm_s, conv_s, kv_s, win_s = [], [], [], [], [], [], [], []
    for i in range(DEPTH):
        j = i // 2
        if i % 2 == 0:
            ssd = (ssd_w_in[j], ssd_conv_w[j], ssd_conv_b[j], ssd_dt_bias[j], ssd_a_log[j], ssd_d[j], ssd_norm_g[j],
                   ssd_w_out[j])
            h1p, c_new, s_new = _ssd_prompt(xp, _ssd_weights(*ssd, bf16), bp, t, ln1_g[i], ln1_b[i])
            conv_p.append(c_new)
            ssm_p.append(s_new)
            h1s, c_new, s_new = _ssd_step(xs, state_conv[j], state_ssm[j], _ssd_weights(*ssd, f32), ln1_g[i], ln1_b[i])
            conv_s.append(c_new)
            ssm_s.append(s_new)
        else:
            nsa = (nsa_w_in[j], nsa_pos_w[j], nsa_w_out[j])
            h1p, r_new, w_new = _nsa_prompt(xp, _nsa_weights(*nsa, bf16), bp, t, ln1_g[i], ln1_b[i])
            kv_p.append(r_new)
            win_p.append(w_new)
            h1s, r_new, w_new = _nsa_step(xs, cache_kv[j], page_table, cache_win[j], _nsa_weights(*nsa, f32),
                                          ln1_g[i], ln1_b[i])
            kv_s.append(r_new)
            win_s.append(w_new)

        def tail_weights(wdt, i=i):
            return dict(rwt=router_w.T.astype(wdt), rb=rb, wg=moe_w_gate[i].astype(wdt), wu=moe_w_up[i].astype(wdt),
                        wd=moe_w_down[i].astype(wdt), ln2_g=ln2_g[i], ln2_b=ln2_b[i],
                        ple_gate=ple_gate[i].astype(wdt), ple_proj=ple_proj[i].astype(wdt))

        xp = _layer_tail(h1p, p_prompt[i].reshape(bp * t, PLE_DIM), tail_weights(bf16), tm=512)
        xs = _layer_tail(h1s, p_sample[i].reshape(bs, PLE_DIM), tail_weights(f32), tm=bs)
    return (xp.reshape(bp, t, d), xs.reshape(bs, 1, d), jnp.stack(ssm_p), jnp.stack(conv_p), jnp.stack(kv_p),
            jnp.stack(win_p), jnp.stack(ssm_s), jnp.stack(conv_s), jnp.stack(kv_s), jnp.stack(win_s))
```

```python
import functools

import jax
import jax.numpy as jnp
from jax import lax
from jax.experimental import pallas as pl
from jax.experimental.pallas import tpu as pltpu

f32, bf16, i32 = jnp.float32, jnp.bfloat16, jnp.int32
HIGHEST = lax.Precision.HIGHEST

D_MODEL = 1024
DEPTH = 2
PLE_DIM = 256
SSD_D_INNER = 2048
SSD_HEADDIM = 64
SSD_HEADS = 32
SSD_GROUPS = 4
SSD_HPG = 8
SSD_STATE = 128
SSD_CONV = 4
SSD_CHUNK = 128
SSD_GW = SSD_HPG * SSD_HEADDIM
SSD_CONV_DIM = SSD_D_INNER + 2 * SSD_GROUPS * SSD_STATE
NSA_HEADS = 16
NSA_KV = 4
NSA_REP = 4
NSA_DH = 64
NSA_BLOCK = 64
NSA_TOP = 16
NSA_WINDOW = 512
NSA_QD = NSA_HEADS * NSA_DH
NSA_KVD = NSA_KV * NSA_DH
PAGE_SIZE = 128
N_EXPERTS = 16
N_EGROUPS = 4
EPG = 4
D_FF = 512
ALPHA = (2.0 * DEPTH) ** 0.25
LN_EPS = 1e-5
NEG = -1e30
LANES = 128
VMEM_LIMIT = 56 * 1024 * 1024


def _cparams(sem):
    return pltpu.CompilerParams(dimension_semantics=sem, vmem_limit_bytes=VMEM_LIMIT)


def _ln(v, g, b):
    mu = jnp.mean(v, axis=-1, keepdims=True)
    d = v - mu
    var = jnp.mean(d * d, axis=-1, keepdims=True)
    return d * lax.rsqrt(var + LN_EPS) * g + b


def _dot(a, b):
    return jnp.dot(a, b, preferred_element_type=f32)


def _mxu(a, b, precise):
    if precise:
        return jnp.dot(a.astype(f32), b.astype(f32), precision=HIGHEST, preferred_element_type=f32)
    return jnp.dot(a.astype(bf16), b.astype(bf16), preferred_element_type=f32)


def _mxu_nt(a, b, precise):
    if precise:
        return _dot_nt(a.astype(f32), b.astype(f32), precision=HIGHEST)
    return _dot_nt(a.astype(bf16), b.astype(bf16))


def _dot_nt(a, b, precision=None):
    return lax.dot_general(a, b, (((1,), (1,)), ((), ())), precision=precision, preferred_element_type=f32)


def _dot_tn(a, b, precision=None):
    return lax.dot_general(a, b, (((0,), (0,)), ((), ())), precision=precision, preferred_element_type=f32)


def _is_f32(ref):
    return ref.dtype == jnp.float32


def _mm_body(x_ref, w_ref, *o_refs, act, scale):
    acc = _mxu(x_ref[...], w_ref[...], _is_f32(w_ref))
    if scale != 1.0:
        acc = acc * scale
    if act == "sigmoid":
        acc = jax.nn.sigmoid(acc)
    for o_ref in o_refs:
        o_ref[...] = acc.astype(o_ref.dtype)


def _mm(x, w, *, tm, tn, out_dtypes, name, act=None, scale=1.0):
    m, k = x.shape
    n = w.shape[1]
    tm, tn = min(tm, m), min(tn, n)
    assert m % tm == 0 and n % tn == 0
    outs = pl.pallas_call(
        functools.partial(_mm_body, act=act, scale=scale),
        name=f"{name}_r{m}",
        grid=(m // tm, n // tn),
        in_specs=[pl.BlockSpec((tm, k), lambda i, j: (i, 0)), pl.BlockSpec((k, tn), lambda i, j: (0, j))],
        out_specs=[pl.BlockSpec((tm, tn), lambda i, j: (i, j)) for _ in out_dtypes],
        out_shape=[jax.ShapeDtypeStruct((m, n), dt) for dt in out_dtypes],
        compiler_params=_cparams(("parallel", "arbitrary")),
    )(x, w)
    return outs


def _mm_ln_body(x_ref, w_ref, res_ref, g_ref, b_ref, o_ref):
    acc = _mxu(x_ref[...], w_ref[...], _is_f32(w_ref))
    o_ref[...] = _ln(ALPHA * res_ref[...] + acc, g_ref[...], b_ref[...])


def _mm_ln(x, w, res, g, b, *, tm, name):
    m, k = x.shape
    n = w.shape[1]
    tm = min(tm, m)
    assert m % tm == 0
    return pl.pallas_call(
        _mm_ln_body,
        name=f"{name}_r{m}",
        grid=(m // tm,),
        in_specs=[pl.BlockSpec((tm, k), lambda i: (i, 0)), pl.BlockSpec((k, n), lambda i: (0, 0)),
                  pl.BlockSpec((tm, n), lambda i: (i, 0)), pl.BlockSpec((1, n), lambda i: (0, 0)),
                  pl.BlockSpec((1, n), lambda i: (0, 0))],
        out_specs=pl.BlockSpec((tm, n), lambda i: (i, 0)),
        out_shape=jax.ShapeDtypeStruct((m, n), f32),
        compiler_params=_cparams(("parallel",)),
    )(x, w, res, g.reshape(1, n), b.reshape(1, n))


def _router_body(h_ref, rwt_ref, rb_ref, gate_ref):
    logits = _mxu_nt(rwt_ref[...], h_ref[...], _is_f32(rwt_ref))
    score = jax.nn.sigmoid(logits)
    sel = score + rb_ref[...]
    gsum = []
    for g in range(N_EGROUPS):
        a, b, c, d = (sel[EPG * g + i:EPG * g + i + 1] for i in range(EPG))
        hi1, lo1, hi2, lo2 = jnp.maximum(a, b), jnp.minimum(a, b), jnp.maximum(c, d), jnp.minimum(c, d)
        gsum.append(jnp.maximum(hi1, hi2) + jnp.maximum(jnp.minimum(hi1, hi2), jnp.maximum(lo1, lo2)))
    best = jnp.zeros_like(gsum[0], dtype=i32)
    top = gsum[0]
    for g in range(1, N_EGROUPS):
        upd = gsum[g] > top
        best = jnp.where(upd, g, best)
        top = jnp.where(upd, gsum[g], top)
    selg = sel[0:EPG]
    scg = score[0:EPG]
    for g in range(1, N_EGROUPS):
        selg = jnp.where(best == g, sel[EPG * g:EPG * (g + 1)], selg)
        scg = jnp.where(best == g, score[EPG * g:EPG * (g + 1)], scg)
    rows = [selg[i:i + 1] for i in range(EPG)]
    chosen = []
    for i in range(EPG):
        rank = jnp.zeros_like(best)
        for j in range(EPG):
            if j == i:
                continue
            ahead = (rows[j] > rows[i]) | ((rows[j] == rows[i]) if j < i else False)
            rank = rank + ahead.astype(i32)
        chosen.append(rank < 2)
    wsum = sum(jnp.where(chosen[i], scg[i:i + 1], 0.0) for i in range(EPG))
    gates = [jnp.where(chosen[i], scg[i:i + 1] / wsum, 0.0) for i in range(EPG)]
    out_rows = []
    for g in range(N_EGROUPS):
        for i in range(EPG):
            out_rows.append(jnp.where(best == g, gates[i], 0.0))
    gate_ref[...] = jnp.concatenate(out_rows, axis=0).T


def _router(h, rwt, rb, *, tm):
    m, d = h.shape
    tm = min(tm, m)
    return pl.pallas_call(
        _router_body,
        name=f"router_r{m}",
        grid=(m // tm,),
        in_specs=[pl.BlockSpec((tm, d), lambda i: (i, 0)), pl.BlockSpec((N_EXPERTS, d), lambda i: (0, 0)),
                  pl.BlockSpec((N_EXPERTS, 1), lambda i: (0, 0))],
        out_specs=pl.BlockSpec((tm, N_EXPERTS), lambda i: (i, 0)),
        out_shape=jax.ShapeDtypeStruct((m, N_EXPERTS), f32),
        compiler_params=_cparams(("parallel",)),
    )(h, rwt, rb)


def _moe_body(h_ref, gate_ref, wg_ref, wu_ref, wd_ref, g2_ref, b2_ref, o_ref, acc_ref):
    e = pl.program_id(1)

    @pl.when(e == 0)
    def _():
        acc_ref[...] = jnp.zeros_like(acc_ref)

    precise = _is_f32(wg_ref)
    h = h_ref[...]
    hid = jax.nn.silu(_mxu(h, wg_ref[0], precise)) * _mxu(h, wu_ref[0], precise)
    out = _mxu(hid, wd_ref[0], precise)
    gate = gate_ref[...]
    lane = lax.broadcasted_iota(i32, gate.shape, 1)
    gcol = jnp.sum(jnp.where(lane == e, gate, 0.0), axis=1, keepdims=True)
    acc_ref[...] += gcol * out

    @pl.when(e == N_EXPERTS - 1)
    def _():
        o_ref[...] = _ln(ALPHA * h_ref[...] + acc_ref[...], g2_ref[...], b2_ref[...])


def _moe_ln(h, gate, wg, wu, wd, g2, b2, *, tm):
    m, d = h.shape
    tm = min(tm, m)
    return pl.pallas_call(
        _moe_body,
        name=f"moe_ln_r{m}",
        grid=(m // tm, N_EXPERTS),
        in_specs=[pl.BlockSpec((tm, d), lambda i, e: (i, 0)), pl.BlockSpec((tm, N_EXPERTS), lambda i, e: (i, 0)),
                  pl.BlockSpec((1, d, D_FF), lambda i, e: (e, 0, 0)), pl.BlockSpec((1, d, D_FF), lambda i, e: (e, 0, 0)),
                  pl.BlockSpec((1, D_FF, d), lambda i, e: (e, 0, 0)),
                  pl.BlockSpec((1, d), lambda i, e: (0, 0)), pl.BlockSpec((1, d), lambda i, e: (0, 0))],
        out_specs=pl.BlockSpec((tm, d), lambda i, e: (i, 0)),
        out_shape=jax.ShapeDtypeStruct((m, d), f32),
        scratch_shapes=[pltpu.VMEM((tm, d), f32)],
        compiler_params=_cparams(("parallel", "arbitrary")),
    )(h, gate, wg, wu, wd, g2.reshape(1, d), b2.reshape(1, d))


def _ple_body(h_ref, p_ref, wg_ref, wp_ref, o_ref):
    h = h_ref[...]
    precise = _is_f32(wg_ref)
    gate = jax.nn.sigmoid(_mxu(h, wg_ref[...], precise))
    o_ref[...] = h + gate * _mxu(p_ref[...], wp_ref[...], precise)


def _ple(h, p, wg, wp, *, tm):
    m, d = h.shape
    tm = min(tm, m)
    return pl.pallas_call(
        _ple_body,
        name=f"ple_r{m}",
        grid=(m // tm,),
        in_specs=[pl.BlockSpec((tm, d), lambda i: (i, 0)), pl.BlockSpec((tm, PLE_DIM), lambda i: (i, 0)),
                  pl.BlockSpec((d, d), lambda i: (0, 0)), pl.BlockSpec((PLE_DIM, d), lambda i: (0, 0))],
        out_specs=pl.BlockSpec((tm, d), lambda i: (i, 0)),
        out_shape=jax.ShapeDtypeStruct((m, d), f32),
        compiler_params=_cparams(("parallel",)),
    )(h, p, wg, wp)


def _layer_tail(h1, p, tw, *, tm):
    gate = _router(h1, tw["rwt"], tw["rb"], tm=tm)
    h2 = _moe_ln(h1, gate, tw["wg"], tw["wu"], tw["wd"], tw["ln2_g"], tw["ln2_b"], tm=tm)
    return _ple(h2, p, tw["ple_gate"], tw["ple_proj"], tm=tm)


def _ssd_conv(x_ref, xp_ref, w_ref, b_ref):
    q = x_ref.shape[0]
    xp_ref[8:8 + q, :] = x_ref[...]
    w = w_ref[...]
    acc = b_ref[...] + xp_ref[5:5 + q, :] * w[0:1]
    for k in range(1, SSD_CONV):
        acc = acc + xp_ref[5 + k:5 + k + q, :] * w[k:k + 1]
    xp_ref[0:8, :] = xp_ref[q:q + 8, :]
    return jax.nn.silu(acc)


def _head_expand(width):
    r = lax.broadcasted_iota(i32, (SSD_HPG, SSD_HPG * width), 0)
    c = lax.broadcasted_iota(i32, (SSD_HPG, SSD_HPG * width), 1)
    return (c // width == r).astype(f32)


def _ssd_body(xs_ref, bm_ref, cm_ref, z_ref, dt_ref, dtt_ref, wx_ref, wb_ref, wc_ref, bx_ref, bb_ref, bc_ref,
              dtb_ref, dtbt_ref, alog_ref, alogt_ref, dsk_ref, ng_ref, y_ref, st_ref,
              stt_ref, xpx_ref, xpb_ref, xpc_ref):
    c = pl.program_id(2)
    q = SSD_CHUNK

    @pl.when(c == 0)
    def _():
        stt_ref[...] = jnp.zeros_like(stt_ref)
        xpx_ref[0:8, :] = jnp.zeros((8, xpx_ref.shape[1]), f32)
        xpb_ref[0:8, :] = jnp.zeros((8, xpb_ref.shape[1]), f32)
        xpc_ref[0:8, :] = jnp.zeros((8, xpc_ref.shape[1]), f32)

    xs = _ssd_conv(xs_ref, xpx_ref, wx_ref, bx_ref)
    bm = _ssd_conv(bm_ref, xpb_ref, wb_ref, bb_ref)
    cm = _ssd_conv(cm_ref, xpc_ref, wc_ref, bc_ref)
    dt = jax.nn.softplus(dt_ref[0] + dtb_ref[0])
    dtt = jax.nn.softplus(dtt_ref[0] + dtbt_ref[0])
    dta = dt * (-jnp.exp(alog_ref[0]))
    dtat = dtt * (-jnp.exp(alogt_ref[0]))
    row = lax.broadcasted_iota(i32, (q, q), 0)
    col = lax.broadcasted_iota(i32, (q, q), 1)
    causal = row >= col
    cum = jnp.dot(causal.astype(f32), dta, precision=HIGHEST, preferred_element_type=f32)
    cumt = jnp.dot(dtat, (row <= col).astype(f32), precision=HIGHEST, preferred_element_type=f32)
    e64 = _head_expand(SSD_HEADDIM)
    dtx = jnp.dot(dt, e64, precision=HIGHEST, preferred_element_type=f32)
    cumx = jnp.dot(cum, e64, precision=HIGHEST, preferred_element_type=f32)
    cum128 = jnp.dot(cum, _head_expand(q), precision=HIGHEST, preferred_element_type=f32)
    cum_last = cumx[q - 1:q, :]
    xdt = xs * dtx
    xdt_b = xdt.astype(bf16)
    bm_b = bm.astype(bf16)
    cm_b = cm.astype(bf16)
    cb = _dot_nt(cm_b, bm_b)
    head = lax.broadcasted_iota(i32, (q, SSD_GW), 1) // SSD_HEADDIM
    y = jnp.zeros((q, SSD_GW), f32)
    for r in range(SSD_HPG):
        seg = cum128[:, r * q:(r + 1) * q] - cumt[r:r + 1, :]
        decay = jnp.where(causal, jnp.exp(jnp.where(causal, seg, 0.0)), 0.0)
        yr = _dot((cb * decay).astype(bf16), xdt_b)
        y = jnp.where(head == r, yr, y)
    stt = stt_ref[...]
    y = y + _dot(cm_b, stt.astype(bf16)) * jnp.exp(cumx) + dsk_ref[0] * xs
    to_end = jnp.exp(cum_last - cumx)
    stt_new = stt * jnp.exp(cum_last) + _dot_tn(bm_b, (xdt * to_end).astype(bf16))
    stt_ref[...] = stt_new
    yz = y * jax.nn.silu(z_ref[...])
    yn = yz * lax.rsqrt(jnp.mean(yz * yz, axis=-1, keepdims=True) + 1e-5) * ng_ref[...]
    y_ref[...] = yn.astype(y_ref.dtype)

    @pl.when(c == pl.num_programs(2) - 1)
    def _():
        st_ref[0] = stt_new.T.reshape(SSD_HPG, SSD_HEADDIM, SSD_STATE)


def _ssd_prompt_scan(xbc, z, dt_raw, sw, bsz, t):
    m = bsz * t
    nc = t // SSD_CHUNK
    q, gw, n, g_, hpg = SSD_CHUNK, SSD_GW, SSD_STATE, SSD_GROUPS, SSD_HPG
    dt_g = dt_raw.reshape(m, g_, hpg).transpose(1, 0, 2)
    dtt_g = dt_g.transpose(0, 2, 1)
    nxb = SSD_D_INNER // n
    row = lambda b, g, c: b * nc + c
    in_specs = [
        pl.BlockSpec((q, gw), lambda b, g, c: (row(b, g, c), g)),
        pl.BlockSpec((q, n), lambda b, g, c: (row(b, g, c), nxb + g)),
        pl.BlockSpec((q, n), lambda b, g, c: (row(b, g, c), nxb + g_ + g)),
        pl.BlockSpec((q, gw), lambda b, g, c: (row(b, g, c), g)),
        pl.BlockSpec((1, q, hpg), lambda b, g, c: (g, row(b, g, c), 0)),
        pl.BlockSpec((1, hpg, q), lambda b, g, c: (g, 0, row(b, g, c))),
        pl.BlockSpec((SSD_CONV, gw), lambda b, g, c: (0, g)),
        pl.BlockSpec((SSD_CONV, n), lambda b, g, c: (0, nxb + g)),
        pl.BlockSpec((SSD_CONV, n), lambda b, g, c: (0, nxb + g_ + g)),
        pl.BlockSpec((1, gw), lambda b, g, c: (0, g)),
        pl.BlockSpec((1, n), lambda b, g, c: (0, nxb + g)),
        pl.BlockSpec((1, n), lambda b, g, c: (0, nxb + g_ + g)),
        pl.BlockSpec((1, 1, hpg), lambda b, g, c: (g, 0, 0)),
        pl.BlockSpec((1, hpg, 1), lambda b, g, c: (g, 0, 0)),
        pl.BlockSpec((1, 1, hpg), lambda b, g, c: (g, 0, 0)),
        pl.BlockSpec((1, hpg, 1), lambda b, g, c: (g, 0, 0)),
        pl.BlockSpec((1, 1, gw), lambda b, g, c: (g, 0, 0)),
        pl.BlockSpec((1, gw), lambda b, g, c: (0, g)),
    ]
    y, st = pl.pallas_call(
        _ssd_body,
        name="ssd_scan",
        grid=(bsz, g_, nc),
        in_specs=in_specs,
        out_specs=[pl.BlockSpec((q, gw), lambda b, g, c: (row(b, g, c), g)),
                   pl.BlockSpec((1, hpg, SSD_HEADDIM, n), lambda b, g, c: (b, g, 0, 0))],
        out_shape=[jax.ShapeDtypeStruct((m, SSD_D_INNER), bf16),
                   jax.ShapeDtypeStruct((bsz, SSD_HEADS, SSD_HEADDIM, n), f32)],
        scratch_shapes=[pltpu.VMEM((n, gw), f32), pltpu.VMEM((q + 8, gw), f32),
                        pltpu.VMEM((q + 8, n), f32), pltpu.VMEM((q + 8, n), f32)],
        compiler_params=_cparams(("arbitrary", "arbitrary", "arbitrary")),
    )(xbc, xbc, xbc, z, dt_g, dtt_g, sw["conv_w"], sw["conv_w"], sw["conv_w"], sw["conv_b"], sw["conv_b"], sw["conv_b"],
      sw["dtb"], sw["dtbt"], sw["alog"], sw["alogt"], sw["dskx"], sw["norm_g"])
    return y, st


def _ssd_weights(w_in, conv_w, conv_b, dt_bias, a_log, d_skip, norm_g, w_out, wdt):
    g_, hpg = SSD_GROUPS, SSD_HPG
    return dict(
        w_z=w_in[:, :SSD_D_INNER].astype(wdt),
        w_xbc=w_in[:, SSD_D_INNER:SSD_D_INNER + SSD_CONV_DIM].astype(wdt),
        w_dt=w_in[:, SSD_D_INNER + SSD_CONV_DIM:].astype(wdt),
        conv_w=conv_w, conv_b=conv_b.reshape(1, SSD_CONV_DIM),
        dtb=dt_bias.reshape(g_, 1, hpg), dtbt=dt_bias.reshape(g_, hpg, 1),
        alog=a_log.reshape(g_, 1, hpg), alogt=a_log.reshape(g_, hpg, 1),
        dskx=jnp.repeat(d_skip, SSD_HEADDIM).reshape(g_, 1, SSD_GW),
        dsk=d_skip, dt_bias=dt_bias, a_log=a_log,
        norm_g=norm_g.reshape(1, SSD_D_INNER), w_out=w_out.astype(wdt))


def _ssd_prompt(x2d, sw, bsz, t, ln_g, ln_b):
    (z,) = _mm(x2d, sw["w_z"], tm=512, tn=1024, out_dtypes=(f32,), name="ssd_in_z")
    (xbc,) = _mm(x2d, sw["w_xbc"], tm=512, tn=1024, out_dtypes=(f32,), name="ssd_in_xbc")
    (dt_raw,) = _mm(x2d, sw["w_dt"], tm=512, tn=SSD_HEADS, out_dtypes=(f32,), name="ssd_in_dt")
    y, st = _ssd_prompt_scan(xbc, z, dt_raw, sw, bsz, t)
    h1 = _mm_ln(y, sw["w_out"], x2d, ln_g, ln_b, tm=512, name="ssd_out_ln")
    new_conv = xbc.reshape(bsz, t, SSD_CONV_DIM)[:, t - (SSD_CONV - 1):]
    return h1, new_conv, st


def _pad_rows(x, rows=8):
    return jnp.concatenate([x, jnp.zeros((rows - x.shape[0], x.shape[1]), x.dtype)], axis=0)


def _ssd_step_body(z_ref, xbc_ref, dt_ref, cs_ref, st_ref, cw_ref, cb_ref, dtb_ref, alog_ref, dsk_ref, ng_ref,
                   y_ref, nc_ref, ns_ref):
    n, gw, hd = SSD_STATE, SSD_GW, SSD_HEADDIM
    xbc = xbc_ref[0]
    cs = cs_ref[0]
    w = cw_ref[...]
    conv = cb_ref[...] + cs[0:1] * w[0:1]
    for k in range(1, SSD_CONV - 1):
        conv = conv + cs[k:k + 1] * w[k:k + 1]
    conv = jax.nn.silu(conv + xbc * w[SSD_CONV - 1:SSD_CONV])
    nc_ref[0] = jnp.concatenate([cs[1:], xbc], axis=0)
    xs = conv[:, :SSD_D_INNER]
    b_g = _pad_rows(jnp.concatenate(
        [conv[:, SSD_D_INNER + g * n:SSD_D_INNER + (g + 1) * n] for g in range(SSD_GROUPS)], axis=0))
    c_g = _pad_rows(jnp.concatenate(
        [conv[:, SSD_D_INNER + (SSD_GROUPS + g) * n:SSD_D_INNER + (SSD_GROUPS + g + 1) * n] for g in range(SSD_GROUPS)],
        axis=0))
    dt = jax.nn.softplus(dt_ref[0] + dtb_ref[...])
    decay = jnp.exp(dt * (-jnp.exp(alog_ref[...])))
    er = lax.broadcasted_iota(i32, (SSD_HEADS, SSD_D_INNER), 0)
    ec = lax.broadcasted_iota(i32, (SSD_HEADS, SSD_D_INNER), 1)
    per_head = _pad_rows(jnp.concatenate([dt, decay, dsk_ref[...]], axis=0))
    hx = jnp.dot(per_head, (ec // hd == er).astype(f32), precision=HIGHEST, preferred_element_type=f32)
    dtx, decx, dskx = hx[0:1], hx[1:2], hx[2:3]
    xdt = xs * dtx
    gr = lax.broadcasted_iota(i32, (8, SSD_D_INNER), 0)
    gc = lax.broadcasted_iota(i32, (8, SSD_D_INNER), 1)
    gmask = (gc // gw == gr).astype(f32)
    row0 = (gr == 0).astype(f32)
    st = st_ref[0].reshape(SSD_D_INNER, n)
    upd = _dot_tn(gmask * xdt, b_g, precision=HIGHEST)
    dec_full = _dot_tn(row0 * decx, jnp.ones((8, n), f32), precision=HIGHEST)
    ns_ref[0] = (st * dec_full + upd).reshape(SSD_HEADS, hd, n)
    cst = _dot_nt(c_g, st, precision=HIGHEST)
    y_state = jnp.sum(gmask * cst, axis=0, keepdims=True)
    cbx = jnp.sum(gmask * jnp.sum(c_g * b_g, axis=1, keepdims=True), axis=0, keepdims=True)
    y = cbx * xdt + decx * y_state + dskx * xs
    yz = y * jax.nn.silu(z_ref[0])
    parts = []
    for g in range(SSD_GROUPS):
        seg = yz[:, g * gw:(g + 1) * gw]
        parts.append(seg * lax.rsqrt(jnp.mean(seg * seg, axis=-1, keepdims=True) + 1e-5))
    y_ref[0] = jnp.concatenate(parts, axis=1) * ng_ref[...]


def _ssd_step(x2d, conv_state, ssm_state, sw, ln_g, ln_b):
    bsz = x2d.shape[0]
    (z,) = _mm(x2d, sw["w_z"], tm=bsz, tn=1024, out_dtypes=(f32,), name="ssd_in_z")
    (xbc,) = _mm(x2d, sw["w_xbc"], tm=bsz, tn=1024, out_dtypes=(f32,), name="ssd_in_xbc")
    (dt_raw,) = _mm(x2d, sw["w_dt"], tm=bsz, tn=SSD_HEADS, out_dtypes=(f32,), name="ssd_in_dt")
    rowspec = lambda width: pl.BlockSpec((1, 1, width), lambda b: (b, 0, 0))
    full = lambda r, c: pl.BlockSpec((r, c), lambda b: (0, 0))
    y, new_conv, new_state = pl.pallas_call(
        _ssd_step_body,
        name="ssd_step",
        grid=(bsz,),
        in_specs=[rowspec(SSD_D_INNER), rowspec(SSD_CONV_DIM), rowspec(SSD_HEADS),
                  pl.BlockSpec((1, SSD_CONV - 1, SSD_CONV_DIM), lambda b: (b, 0, 0)),
                  pl.BlockSpec((1, SSD_HEADS, SSD_HEADDIM, SSD_STATE), lambda b: (b, 0, 0, 0)),
                  full(SSD_CONV, SSD_CONV_DIM), full(1, SSD_CONV_DIM), full(1, SSD_HEADS), full(1, SSD_HEADS),
                  full(1, SSD_HEADS), full(1, SSD_D_INNER)],
        out_specs=[rowspec(SSD_D_INNER),
                   pl.BlockSpec((1, SSD_CONV - 1, SSD_CONV_DIM), lambda b: (b, 0, 0)),
                   pl.BlockSpec((1, SSD_HEADS, SSD_HEADDIM, SSD_STATE), lambda b: (b, 0, 0, 0))],
        out_shape=[jax.ShapeDtypeStruct((bsz, 1, SSD_D_INNER), f32),
                   jax.ShapeDtypeStruct((bsz, SSD_CONV - 1, SSD_CONV_DIM), f32),
                   jax.ShapeDtypeStruct((bsz, SSD_HEADS, SSD_HEADDIM, SSD_STATE), f32)],
        compiler_params=_cparams(("parallel",)),
    )(z.reshape(bsz, 1, -1), xbc.reshape(bsz, 1, -1), dt_raw.reshape(bsz, 1, -1), conv_state, ssm_state,
      sw["conv_w"], sw["conv_b"], sw["dt_bias"].reshape(1, -1), sw["a_log"].reshape(1, -1), sw["dsk"].reshape(1, -1),
      sw["norm_g"])
    h1 = _mm_ln(y.reshape(bsz, SSD_D_INNER), sw["w_out"], x2d, ln_g, ln_b, tm=bsz, name="ssd_out_ln")
    return h1, new_conv, new_state


def _slope(head):
    return 2.0 ** (-8.0 * (head + 1) / NSA_HEADS)


def _masked_softmax(s, mask):
    s = jnp.where(mask, s, NEG)
    m = jnp.max(s, axis=-1, keepdims=True)
    e = jnp.exp(s - m) * mask.astype(f32)
    return e / jnp.maximum(jnp.sum(e, axis=-1, keepdims=True), 1e-30)


def _tile_rows(x, n):
    return jnp.concatenate([x] * n, axis=0)


def _slope_col(kv_head, rows):
    return jnp.concatenate([jnp.full((rows, 1), _slope(kv_head * NSA_REP + r), f32) for r in range(NSA_REP)], axis=0)


def _topk_rank(imp):
    nb = imp.shape[1]
    lane = lax.broadcasted_iota(i32, imp.shape, 1)
    rank = jnp.zeros(imp.shape, i32)
    for j in range(nb):
        cj = imp[:, j:j + 1]
        rank = rank + ((cj > imp) | ((cj == imp) & (lane > j))).astype(i32)
    return rank


def _topk_mask(imp, n_sel):
    return _topk_rank(imp) < n_sel


def _pos_weights(pw_ref):
    w = jax.nn.softmax(pw_ref[...], axis=0)
    r = lax.broadcasted_iota(i32, (2 * NSA_KV, 2 * NSA_KVD), 0)
    c = lax.broadcasted_iota(i32, (2 * NSA_KV, 2 * NSA_KVD), 1)
    return jnp.dot(w, (c // NSA_DH == r).astype(f32), precision=HIGHEST, preferred_element_type=f32)


def _compress_body(x_ref, pw_ref, o_ref):
    wx = _pos_weights(pw_ref)
    x = x_ref[...]
    nb = x.shape[0] // NSA_BLOCK
    o_ref[0] = jnp.sum(x.reshape(nb, NSA_BLOCK, 2 * NSA_KVD) * wx[None], axis=1)


def _nsa_compress(kv4, pw, bsz, t):
    nb = t // NSA_BLOCK
    return pl.pallas_call(
        _compress_body,
        name="nsa_compress",
        grid=(bsz,),
        in_specs=[pl.BlockSpec((t, 2 * NSA_KVD), lambda b: (b, 0)),
                  pl.BlockSpec((NSA_BLOCK, 2 * NSA_KV), lambda b: (0, 0))],
        out_specs=pl.BlockSpec((1, nb, 2 * NSA_KVD), lambda b: (b, 0, 0)),
        out_shape=jax.ShapeDtypeStruct((bsz, nb, 2 * NSA_KVD), f32),
        compiler_params=_cparams(("parallel",)),
    )(kv4, pw)


NSA_TQ = 128
NSA_CK = 512


def _nsa_prompt_body(q_ref, g_ref, kcvc_ref, kvs_ref, kvw_ref, o_ref, *, t_len):
    i = pl.program_id(1)
    tq, ck, rep = NSA_TQ, NSA_CK, NSA_REP
    ck = min(ck, t_len)
    nb = t_len // NSA_BLOCK
    n_sel = min(NSA_TOP, nb)
    wlen = min(NSA_WINDOW + tq, t_len)
    t0 = i * tq
    t_col = t0 + lax.broadcasted_iota(i32, (tq, 1), 0)
    lane128 = lax.broadcasted_iota(i32, (1, LANES), 1)
    gates = g_ref[...]
    kcvc = kcvc_ref[0]
    blk = lax.broadcasted_iota(i32, (1, nb), 1)
    dist_c = t_col - ((blk + 1) * NSA_BLOCK - 1)
    dist_c4 = _tile_rows(dist_c, rep)
    cur = t_col // NSA_BLOCK
    n_chunks = (t0 + tq + ck - 1) // ck
    w_start = jnp.clip(t0 - NSA_WINDOW, 0, t_len - wlen)
    w_start = pl.multiple_of(w_start, tq)
    dist_w = t_col - (w_start + lax.broadcasted_iota(i32, (1, wlen), 1))
    dist_w4 = _tile_rows(dist_w, rep)
    mask_w4 = (dist_w4 >= 0) & (dist_w4 < NSA_WINDOW)

    for j in range(NSA_KV // 2):
        cols = slice(j * LANES, (j + 1) * LANES)
        vcols = slice(NSA_KVD + j * LANES, NSA_KVD + (j + 1) * LANES)
        q_tiles = [q_ref[:, (j * rep + r) * LANES:(j * rep + r + 1) * LANES] for r in range(rep)]
        q_rows = jnp.concatenate(q_tiles, axis=0)
        kc_b = kcvc[:, cols].astype(bf16)
        vc_b = kcvc[:, vcols].astype(bf16)
        out_tiles = [jnp.zeros((tq, LANES), f32) for _ in range(rep)]
        for h in range(2):
            k = 2 * j + h
            half = (lane128 // NSA_DH) == h
            q_pad = jnp.where(half, q_rows, jnp.zeros_like(q_rows))
            slope = _slope_col(k, tq)
            s_c = _dot_nt(q_pad, kc_b) - slope * dist_c4.astype(f32)
            p_c = _masked_softmax(s_c, dist_c4 >= 0)
            o_c = _dot(p_c.astype(bf16), vc_b)
            imp = sum(p_c[r * tq:(r + 1) * tq] for r in range(rep))
            imp = jnp.where((blk == cur) | (blk == 0), 1e4, imp)
            imp = jnp.where(blk > cur, -1.0, imp)
            sel_b = _topk_mask(imp, n_sel).astype(bf16)

            def sel_step(c, carry, q_pad=q_pad, slope=slope, sel_b=sel_b, cols=cols, vcols=vcols):
                m, l, acc = carry
                k0 = pl.multiple_of(c * ck, ck)
                ks = kvs_ref[pl.ds(k0, ck), cols]
                vs = kvs_ref[pl.ds(k0, ck), vcols]
                kpos = k0 + lax.broadcasted_iota(i32, (1, ck), 1)
                brow = lax.broadcasted_iota(i32, (nb, ck), 0)
                expand = ((k0 + lax.broadcasted_iota(i32, (nb, ck), 1)) // NSA_BLOCK == brow).astype(bf16)
                selx = _dot(sel_b, expand)
                dist = t_col - kpos
                mask = _tile_rows((selx > 0.5) & (dist >= 0), rep)
                s = _dot_nt(q_pad, ks) - slope * _tile_rows(dist, rep).astype(f32)
                s = jnp.where(mask, s, NEG)
                m_new = jnp.maximum(m, jnp.max(s, axis=-1, keepdims=True))
                a = jnp.exp(m - m_new)
                p = jnp.exp(s - m_new) * mask.astype(f32)
                l = a * l + jnp.sum(p, axis=-1, keepdims=True)
                acc = a * acc + _dot(p.astype(bf16), vs)
                return m_new, l, acc

            init = (jnp.full((rep * tq, 1), NEG, f32), jnp.zeros((rep * tq, 1), f32), jnp.zeros((rep * tq, LANES), f32))
            _, l_s, acc_s = lax.fori_loop(0, n_chunks, sel_step, init)
            o_s = acc_s / jnp.maximum(l_s, 1e-30)
            kw = kvw_ref[pl.ds(w_start, wlen), cols]
            vw = kvw_ref[pl.ds(w_start, wlen), vcols]
            s_w = _dot_nt(q_pad, kw) - slope * dist_w4.astype(f32)
            p_w = _masked_softmax(s_w, mask_w4)
            o_w = _dot(p_w.astype(bf16), vw)
            for r in range(rep):
                gc = (k * rep + r) * 3
                rows = slice(r * tq, (r + 1) * tq)
                o_r = gates[:, gc:gc + 1] * o_c[rows] + gates[:, gc + 1:gc + 2] * o_s[rows] + gates[:, gc + 2:gc + 3] * o_w[rows]
                out_tiles[r] = jnp.where(half, o_r, out_tiles[r])
        for r in range(rep):
            o_ref[:, (j * rep + r) * LANES:(j * rep + r + 1) * LANES] = out_tiles[r].astype(o_ref.dtype)


def _nsa_prompt_attn(q, gates, kcvc, kv4_b, kvw_b, bsz, t):
    nq = t // NSA_TQ
    nb = t // NSA_BLOCK
    return pl.pallas_call(
        functools.partial(_nsa_prompt_body, t_len=t),
        name="nsa_attn",
        grid=(bsz, nq),
        in_specs=[pl.BlockSpec((NSA_TQ, NSA_QD), lambda b, i: (b * nq + i, 0)),
                  pl.BlockSpec((NSA_TQ, LANES), lambda b, i: (b * nq + i, 0)),
                  pl.BlockSpec((1, nb, 2 * NSA_KVD), lambda b, i: (b, 0, 0)),
                  pl.BlockSpec((t, 2 * NSA_KVD), lambda b, i: (b, 1)),
                  pl.BlockSpec((t, 2 * NSA_KVD), lambda b, i: (b, 0))],
        out_specs=pl.BlockSpec((NSA_TQ, NSA_QD), lambda b, i: (b * nq + i, 0)),
        out_shape=jax.ShapeDtypeStruct((bsz * t, NSA_QD), bf16),
        compiler_params=_cparams(("parallel", "arbitrary")),
    )(q, gates, kcvc, kv4_b, kvw_b)


def _pair_layout_cols(w):
    lead = w.shape[:-1]
    w = w.reshape(*lead, NSA_KV // 2, 2, NSA_REP, NSA_DH)
    return jnp.swapaxes(w, -3, -2).reshape(*lead, NSA_QD)


def _nsa_weights(w_in, pos_w, w_out, wdt, pair):
    lay = _pair_layout_cols if pair else (lambda w: w)
    kv0 = NSA_QD
    g0 = NSA_QD + 6 * NSA_KVD
    wg = jnp.zeros((D_MODEL, LANES), f32).at[:, :3 * NSA_HEADS].set(w_in[:, g0:])
    return dict(
        w_q=lay(w_in[:, :NSA_QD]).astype(wdt),
        w_kv4=w_in[:, kv0:kv0 + 4 * NSA_KVD].astype(wdt),
        w_kvw=w_in[:, kv0 + 4 * NSA_KVD:g0].astype(wdt),
        w_g=wg.astype(wdt),
        pw=pos_w.transpose(1, 0, 2).reshape(NSA_BLOCK, 2 * NSA_KV),
        w_out=lay(w_out.T).T.astype(wdt))


def _nsa_prompt(x2d, nw, bsz, t, ln_g, ln_b):
    (q,) = _mm(x2d, nw["w_q"], tm=512, tn=1024, out_dtypes=(bf16,), scale=NSA_DH ** -0.5, name="nsa_in_q")
    kv4, kv4_b = _mm(x2d, nw["w_kv4"], tm=512, tn=1024, out_dtypes=(f32, bf16), name="nsa_in_kv4")
    kvw, kvw_b = _mm(x2d, nw["w_kvw"], tm=512, tn=512, out_dtypes=(f32, bf16), name="nsa_in_kvw")
    (gates,) = _mm(x2d, nw["w_g"], tm=512, tn=LANES, out_dtypes=(f32,), act="sigmoid", name="nsa_in_gates")
    kcvc = _nsa_compress(kv4, nw["pw"], bsz, t)
    o = _nsa_prompt_attn(q, gates, kcvc, kv4_b, kvw_b, bsz, t)
    h1 = _mm_ln(o, nw["w_out"], x2d, ln_g, ln_b, tm=512, name="nsa_out_ln")
    w_keep = min(NSA_WINDOW, t)
    new_kv = kv4.reshape(bsz, t, 4, NSA_KV, NSA_DH)
    new_win = kvw.reshape(bsz, t, 2, NSA_KV, NSA_DH)[:, t - w_keep:]
    return h1, new_kv, new_win


PAGES_PER_STEP = 16


def _pos_weights_body(pw_ref, o_ref):
    o_ref[...] = _pos_weights(pw_ref)


def _pos_weights_slab(pw):
    wx = pl.pallas_call(_pos_weights_body, name="nsa_pos_weights",
                        out_shape=jax.ShapeDtypeStruct((NSA_BLOCK, 2 * NSA_KVD), f32))(pw)
    return wx.reshape(NSA_BLOCK, 2, NSA_KV, NSA_DH)


def _page_compress_body(pt_ref, *refs):
    del pt_ref
    page_refs, wx_ref, o_ref = refs[:PAGES_PER_STEP], refs[PAGES_PER_STEP], refs[PAGES_PER_STEP + 1]
    bpp = PAGE_SIZE // NSA_BLOCK
    wx = wx_ref[...]
    for i, p_ref in enumerate(page_refs):
        x = p_ref[0].reshape(bpp, NSA_BLOCK, 2, NSA_KV, NSA_DH)
        o_ref[0, i] = jnp.sum(x * wx[None], axis=1)


def _page_compress(pool, page_table, wx):
    bsz, n_pages = page_table.shape
    bpp = PAGE_SIZE // NSA_BLOCK
    steps = n_pages // PAGES_PER_STEP
    assert n_pages % PAGES_PER_STEP == 0
    page_specs = [pl.BlockSpec((1, PAGE_SIZE, 2, NSA_KV, NSA_DH),
                               lambda b, s, pt, i=i: (pt[b, s * PAGES_PER_STEP + i], 0, 0, 0, 0))
                  for i in range(PAGES_PER_STEP)]
    return pl.pallas_call(
        _page_compress_body,
        name="nsa_page_compress",
        grid_spec=pltpu.PrefetchScalarGridSpec(
            num_scalar_prefetch=1, grid=(bsz, steps),
            in_specs=page_specs + [pl.BlockSpec((NSA_BLOCK, 2, NSA_KV, NSA_DH), lambda b, s, pt: (0, 0, 0, 0))],
            out_specs=pl.BlockSpec((1, PAGES_PER_STEP, bpp, 2, NSA_KV, NSA_DH), lambda b, s, pt: (b, s, 0, 0, 0, 0))),
        out_shape=jax.ShapeDtypeStruct((bsz, n_pages, bpp, 2, NSA_KV, NSA_DH), f32),
        compiler_params=_cparams(("parallel", "arbitrary")),
    )(page_table, *([pool] * PAGES_PER_STEP), wx)


def _step_slopes(kv_head):
    r = lax.broadcasted_iota(i32, (8, 1), 0)
    return jnp.exp2(-0.5 * (kv_head * NSA_REP + r + 1).astype(f32))


def _head_rows(x):
    return x.reshape(x.shape[0] * NSA_KV, NSA_DH)


def _gate_col(gates, kv_head, branch):
    cols = [(kv_head * NSA_REP + r) * 3 + branch for r in range(NSA_REP)]
    return jnp.concatenate([gates[:, c:c + 1] for c in cols], axis=0)


def _nsa_step_cw_body(q_ref, g_ref, kc_ref, vc_ref, kvn_ref, wx0_ref, win_ref, kvwn_ref, o_ref, idx_ref, *, past_len):
    rep, kv = NSA_REP, NSA_KV
    nbp = past_len // NSA_BLOCK
    nbt = nbp + 8
    w_buf = win_ref.shape[1]
    zrows = jnp.zeros((kv, NSA_DH), f32)
    new_c = wx0_ref[0] * kvn_ref[0][0:2]
    k_c = jnp.concatenate([_head_rows(kc_ref[0].reshape(nbp, kv, NSA_DH)), new_c[0], zrows], axis=0)
    v_c = jnp.concatenate([_head_rows(vc_ref[0].reshape(nbp, kv, NSA_DH)), new_c[1], zrows], axis=0)
    k_w = jnp.concatenate([_head_rows(win_ref[0][:, 0]), kvwn_ref[0][0], zrows], axis=0)
    v_w = jnp.concatenate([_head_rows(win_ref[0][:, 1]), kvwn_ref[0][1], zrows], axis=0)
    col_c = lax.broadcasted_iota(i32, (1, (nbp + 2) * kv), 1)
    dist_c = past_len - ((col_c // kv + 1) * NSA_BLOCK - 1)
    col_w = lax.broadcasted_iota(i32, (1, (w_buf + 2) * kv), 1)
    dist_w = jnp.where(col_w // kv <= w_buf, w_buf - col_w // kv, -1)
    blk = lax.broadcasted_iota(i32, (1, nbt), 1)
    cur = past_len // NSA_BLOCK
    gates = g_ref[0]
    pick_r = lax.broadcasted_iota(i32, ((nbp + 2) * kv, nbt), 0)
    pick_c = lax.broadcasted_iota(i32, ((nbp + 2) * kv, nbt), 1)
    for k in range(kv):
        q8 = _pad_rows(q_ref[0, k])
        slope = _step_slopes(k)
        s_c = _mxu_nt(q8, k_c, True) - slope * dist_c.astype(f32)
        p_c = _masked_softmax(s_c, jnp.broadcast_to((dist_c >= 0) & (col_c % kv == k), s_c.shape))
        o_c = _mxu(p_c, v_c, True)
        imp_rows = jnp.sum(p_c[0:rep], axis=0, keepdims=True)
        imp = _mxu(imp_rows, (pick_r == pick_c * kv + k).astype(f32), True)
        imp = jnp.where((blk == cur) | (blk == 0), 1e4, imp)
        imp = jnp.where(blk > cur, -2.0, imp)
        rank = _topk_rank(imp)
        pick = lax.broadcasted_iota(i32, (NSA_TOP, nbt), 0) == rank
        idx_ref[0, k] = jnp.sum(jnp.where(pick, blk.astype(f32), 0.0), axis=1, keepdims=True).astype(i32)
        s_w = _mxu_nt(q8, k_w, True) - slope * dist_w.astype(f32)
        mask_w = (dist_w >= 0) & (dist_w < NSA_WINDOW) & (col_w % kv == k)
        p_w = _masked_softmax(s_w, jnp.broadcast_to(mask_w, s_w.shape))
        o_w = _mxu(p_w, v_w, True)
        o_ref[0, k] = _gate_col(gates, k, 0) * o_c[0:rep] + _gate_col(gates, k, 2) * o_w[0:rep]


def _nsa_step_sel_body(idx_ref, pt_ref, *refs, past_len):
    del pt_ref
    n_in = 2 * NSA_TOP
    blk_refs = refs[:n_in]
    q_ref, g_ref, kn_ref, vn_ref, part_ref, o_ref = refs[n_in:]
    b, k = pl.program_id(0), pl.program_id(1)
    nbp = past_len // NSA_BLOCK
    rep, lb, kv = NSA_REP, NSA_BLOCK, NSA_KV
    first = lax.broadcasted_iota(i32, (lb, kv, NSA_DH), 0) == 0
    new_k = jnp.where(first, kn_ref[0, 0][None], 0.0)
    new_v = jnp.where(first, vn_ref[0, 0][None], 0.0)
    col = lax.broadcasted_iota(i32, (1, NSA_TOP * lb * kv), 1)
    spos = (col % (lb * kv)) // kv
    ks, vs = [], []
    for i in range(NSA_TOP):
        n = idx_ref[(b * kv + k) * NSA_TOP + i]
        is_new = n >= nbp
        ks.append(_head_rows(jnp.where(is_new, new_k, blk_refs[2 * i][0, 0, :, 0])))
        vs.append(_head_rows(jnp.where(is_new, new_v, blk_refs[2 * i + 1][0, 0, :, 0])))
        spos = spos + jnp.where(col // (lb * kv) == i, n * lb, 0)
    k_all = jnp.concatenate(ks, axis=0)
    v_all = jnp.concatenate(vs, axis=0)
    dist = past_len - spos
    q8 = _pad_rows(q_ref[0, 0])
    s = _mxu_nt(q8, k_all, True) - _step_slopes(k) * dist.astype(f32)
    p = _masked_softmax(s, jnp.broadcast_to((dist >= 0) & (col % kv == k), s.shape))
    o_s = _mxu(p, v_all, True)
    g = g_ref[0, 0]
    g_sel = jnp.concatenate([g[:, 3 * r + 1:3 * r + 2] for r in range(rep)], axis=0)
    o_ref[0, 0] = part_ref[0, 0] + g_sel * o_s[0:rep]


def _nsa_step(x2d, pool, page_table, win_cache, nw, ln_g, ln_b):
    bsz = x2d.shape[0]
    n_pool = pool.shape[0]
    n_pages = page_table.shape[1]
    past_len = n_pages * PAGE_SIZE
    w_buf = win_cache.shape[1]
    rep, kv, dh = NSA_REP, NSA_KV, NSA_DH
    bpp = PAGE_SIZE // NSA_BLOCK
    (q,) = _mm(x2d, nw["w_q"], tm=bsz, tn=1024, out_dtypes=(f32,), scale=NSA_DH ** -0.5, name="nsa_in_q")
    (kv4,) = _mm(x2d, nw["w_kv4"], tm=bsz, tn=1024, out_dtypes=(f32,), name="nsa_in_kv4")
    (kvw,) = _mm(x2d, nw["w_kvw"], tm=bsz, tn=512, out_dtypes=(f32,), name="nsa_in_kvw")
    (gates,) = _mm(x2d, nw["w_g"], tm=bsz, tn=LANES, out_dtypes=(f32,), act="sigmoid", name="nsa_in_gates")
    wx = _pos_weights_slab(nw["pw"])
    cmp = _page_compress(pool, page_table, wx)
    q4 = q.reshape(bsz, kv, rep, dh)
    kv_new = kv4.reshape(bsz, 4, kv, dh)
    slab = lambda lead: pl.BlockSpec((1, lead, kv, dh), lambda b: (b, 0, 0, 0))
    cmp_spec = lambda c: pl.BlockSpec((1, n_pages, bpp, 1, kv, dh), lambda b, c=c: (b, 0, 0, c, 0, 0))
    part, idx = pl.pallas_call(
        functools.partial(_nsa_step_cw_body, past_len=past_len),
        name="nsa_step_cw",
        grid=(bsz,),
        in_specs=[slab(kv), pl.BlockSpec((1, 1, LANES), lambda b: (b, 0, 0)), cmp_spec(0), cmp_spec(1), slab(4),
                  pl.BlockSpec((1, 2, kv, dh), lambda b: (0, 0, 0, 0)),
                  pl.BlockSpec((1, w_buf, 2, kv, dh), lambda b: (b, 0, 0, 0, 0)), slab(2)],
        out_specs=[slab(kv), pl.BlockSpec((1, kv, NSA_TOP, 1), lambda b: (b, 0, 0, 0))],
        out_shape=[jax.ShapeDtypeStruct((bsz, kv, rep, dh), f32),
                   jax.ShapeDtypeStruct((bsz, kv, NSA_TOP, 1), i32)],
        compiler_params=_cparams(("parallel",)),
    )(q4, gates.reshape(bsz, 1, -1), cmp, cmp, kv_new, wx, win_cache, kvw.reshape(bsz, 2, kv, dh))

    pool6 = pool.reshape(n_pool, bpp, NSA_BLOCK, 4, kv, dh)
    nbp = past_len // NSA_BLOCK

    def blk_spec(i, slot):
        def imap(b, k, idx_r, pt_r):
            n = jnp.minimum(idx_r[(b * kv + k) * NSA_TOP + i], nbp - 1)
            return (pt_r[b * n_pages + n // bpp], n % bpp, 0, slot, 0, 0)
        return pl.BlockSpec((1, 1, NSA_BLOCK, 1, kv, dh), imap)

    blk_specs = []
    for i in range(NSA_TOP):
        blk_specs += [blk_spec(i, 2), blk_spec(i, 3)]
    head = pl.BlockSpec((1, 1, rep, dh), lambda b, k, *_: (b, k, 0, 0))
    o = pl.pallas_call(
        functools.partial(_nsa_step_sel_body, past_len=past_len),
        name="nsa_step_sel",
        grid_spec=pltpu.PrefetchScalarGridSpec(
            num_scalar_prefetch=2, grid=(bsz, kv),
            in_specs=blk_specs + [
                head, pl.BlockSpec((1, 1, 1, 3 * rep), lambda b, k, *_: (b, k, 0, 0)),
                pl.BlockSpec((1, 1, kv, dh), lambda b, k, *_: (b, 2, 0, 0)),
                pl.BlockSpec((1, 1, kv, dh), lambda b, k, *_: (b, 3, 0, 0)), head],
            out_specs=head),
        out_shape=jax.ShapeDtypeStruct((bsz, kv, rep, dh), f32),
        compiler_params=_cparams(("arbitrary", "arbitrary")),
    )(idx.reshape(-1), page_table.reshape(-1), *([pool6] * (2 * NSA_TOP)),
      q4, gates[:, :3 * NSA_HEADS].reshape(bsz, kv, 1, 3 * rep), kv_new, kv_new, part)
    h1 = _mm_ln(o.reshape(bsz, NSA_QD), nw["w_out"], x2d, ln_g, ln_b, tm=bsz, name="nsa_out_ln")
    new_kv = kv4.reshape(bsz, 1, 4, kv, dh)
    new_win = jnp.concatenate([win_cache[:, 1:], kvw.reshape(bsz, 1, 2, kv, dh)], axis=1)
    return h1, new_kv, new_win


def kernel(x_prompt, x_sample, state_ssm, state_conv, cache_kv, cache_win, page_table, p_prompt, p_sample,
           ssd_w_in, ssd_conv_w, ssd_conv_b, ssd_dt_bias, ssd_a_log, ssd_d, ssd_norm_g, ssd_w_out,
           nsa_w_in, nsa_pos_w, nsa_w_out, ln1_g, ln1_b, ln2_g, ln2_b, router_w, router_bias,
           moe_w_gate, moe_w_up, moe_w_down, ple_proj, ple_gate):
    bp, t, d = x_prompt.shape
    bs = x_sample.shape[0]
    xp = x_prompt.reshape(bp * t, d)
    xs = x_sample.reshape(bs, d)
    rb = router_bias.reshape(N_EXPERTS, 1)
    ssm_p, conv_p, kv_p, win_p, ssm_s, conv_s, kv_s, win_s = [], [], [], [], [], [], [], []
    for i in range(DEPTH):
        j = i // 2
        if i % 2 == 0:
            ssd = (ssd_w_in[j], ssd_conv_w[j], ssd_conv_b[j], ssd_dt_bias[j], ssd_a_log[j], ssd_d[j], ssd_norm_g[j],
                   ssd_w_out[j])
            h1p, c_new, s_new = _ssd_prompt(xp, _ssd_weights(*ssd, bf16), bp, t, ln1_g[i], ln1_b[i])
            conv_p.append(c_new)
            ssm_p.append(s_new)
            h1s, c_new, s_new = _ssd_step(xs, state_conv[j], state_ssm[j], _ssd_weights(*ssd, f32), ln1_g[i], ln1_b[i])
            conv_s.append(c_new)
            ssm_s.append(s_new)
        else:
            nsa = (nsa_w_in[j], nsa_pos_w[j], nsa_w_out[j])
            h1p, r_new, w_new = _nsa_prompt(xp, _nsa_weights(*nsa, bf16, True), bp, t, ln1_g[i], ln1_b[i])
            kv_p.append(r_new)
            win_p.append(w_new)
            h1s, r_new, w_new = _nsa_step(xs, cache_kv[j], page_table, cache_win[j], _nsa_weights(*nsa, f32, False),
                                          ln1_g[i], ln1_b[i])
            kv_s.append(r_new)
            win_s.append(w_new)

        def tail_weights(wdt, i=i):
            return dict(rwt=router_w.T.astype(wdt), rb=rb, wg=moe_w_gate[i].astype(wdt), wu=moe_w_up[i].astype(wdt),
                        wd=moe_w_down[i].astype(wdt), ln2_g=ln2_g[i], ln2_b=ln2_b[i],
                        ple_gate=ple_gate[i].astype(wdt), ple_proj=ple_proj[i].astype(wdt))

        xp = _layer_tail(h1p, p_prompt[i].reshape(bp * t, PLE_DIM), tail_weights(bf16), tm=512)
        xs = _layer_tail(h1s, p_sample[i].reshape(bs, PLE_DIM), tail_weights(f32), tm=bs)
    return (xp.reshape(bp, t, d), xs.reshape(bs, 1, d), jnp.stack(ssm_p), jnp.stack(conv_p), jnp.stack(kv_p),
            jnp.stack(win_p), jnp.stack(ssm_s), jnp.stack(conv_s), jnp.stack(kv_s), jnp.stack(win_s))
```

```python
import functools

import jax
import jax.numpy as jnp
from jax import lax
from jax.experimental import pallas as pl
from jax.experimental.pallas import tpu as pltpu

f32, bf16, i32 = jnp.float32, jnp.bfloat16, jnp.int32
HIGHEST = lax.Precision.HIGHEST

D_MODEL = 1024
DEPTH = 2
PLE_DIM = 256
SSD_D_INNER = 2048
SSD_HEADDIM = 64
SSD_HEADS = 32
SSD_GROUPS = 4
SSD_HPG = 8
SSD_STATE = 128
SSD_CONV = 4
SSD_CHUNK = 128
SSD_GW = SSD_HPG * SSD_HEADDIM
SSD_CONV_DIM = SSD_D_INNER + 2 * SSD_GROUPS * SSD_STATE
NSA_HEADS = 16
NSA_KV = 4
NSA_REP = 4
NSA_DH = 64
NSA_BLOCK = 64
NSA_TOP = 16
NSA_WINDOW = 512
NSA_QD = NSA_HEADS * NSA_DH
NSA_KVD = NSA_KV * NSA_DH
PAGE_SIZE = 128
N_EXPERTS = 16
N_EGROUPS = 4
EPG = 4
D_FF = 512
ALPHA = (2.0 * DEPTH) ** 0.25
LN_EPS = 1e-5
NEG = -1e30
LANES = 128
VMEM_LIMIT = 56 * 1024 * 1024


def _cparams(sem):
    return pltpu.CompilerParams(dimension_semantics=sem, vmem_limit_bytes=VMEM_LIMIT)


def _ln(v, g, b):
    mu = jnp.mean(v, axis=-1, keepdims=True)
    d = v - mu
    var = jnp.mean(d * d, axis=-1, keepdims=True)
    return d * lax.rsqrt(var + LN_EPS) * g + b


def _dot(a, b):
    return jnp.dot(a, b, preferred_element_type=f32)


def _mxu(a, b, precise):
    if precise:
        return jnp.dot(a.astype(f32), b.astype(f32), precision=HIGHEST, preferred_element_type=f32)
    return jnp.dot(a.astype(bf16), b.astype(bf16), preferred_element_type=f32)


def _mxu_nt(a, b, precise):
    if precise:
        return _dot_nt(a.astype(f32), b.astype(f32), precision=HIGHEST)
    return _dot_nt(a.astype(bf16), b.astype(bf16))


def _dot_nt(a, b, precision=None):
    return lax.dot_general(a, b, (((1,), (1,)), ((), ())), precision=precision, preferred_element_type=f32)


def _dot_tn(a, b, precision=None):
    return lax.dot_general(a, b, (((0,), (0,)), ((), ())), precision=precision, preferred_element_type=f32)


def _is_f32(ref):
    return ref.dtype == jnp.float32


def _mm_body(x_ref, w_ref, *o_refs, act, scale):
    acc = _mxu(x_ref[...], w_ref[...], _is_f32(w_ref))
    if scale != 1.0:
        acc = acc * scale
    if act == "sigmoid":
        acc = jax.nn.sigmoid(acc)
    for o_ref in o_refs:
        o_ref[...] = acc.astype(o_ref.dtype)


def _mm(x, w, *, tm, tn, out_dtypes, name, act=None, scale=1.0):
    m, k = x.shape
    n = w.shape[1]
    tm, tn = min(tm, m), min(tn, n)
    assert m % tm == 0 and n % tn == 0
    outs = pl.pallas_call(
        functools.partial(_mm_body, act=act, scale=scale),
        name=f"{name}_r{m}",
        grid=(m // tm, n // tn),
        in_specs=[pl.BlockSpec((tm, k), lambda i, j: (i, 0)), pl.BlockSpec((k, tn), lambda i, j: (0, j))],
        out_specs=[pl.BlockSpec((tm, tn), lambda i, j: (i, j)) for _ in out_dtypes],
        out_shape=[jax.ShapeDtypeStruct((m, n), dt) for dt in out_dtypes],
        compiler_params=_cparams(("parallel", "arbitrary")),
    )(x, w)
    return outs


def _mm_ln_body(x_ref, w_ref, res_ref, g_ref, b_ref, o_ref):
    acc = _mxu(x_ref[...], w_ref[...], _is_f32(w_ref))
    o_ref[...] = _ln(ALPHA * res_ref[...] + acc, g_ref[...], b_ref[...])


def _mm_ln(x, w, res, g, b, *, tm, name):
    m, k = x.shape
    n = w.shape[1]
    tm = min(tm, m)
    assert m % tm == 0
    return pl.pallas_call(
        _mm_ln_body,
        name=f"{name}_r{m}",
        grid=(m // tm,),
        in_specs=[pl.BlockSpec((tm, k), lambda i: (i, 0)), pl.BlockSpec((k, n), lambda i: (0, 0)),
                  pl.BlockSpec((tm, n), lambda i: (i, 0)), pl.BlockSpec((1, n), lambda i: (0, 0)),
                  pl.BlockSpec((1, n), lambda i: (0, 0))],
        out_specs=pl.BlockSpec((tm, n), lambda i: (i, 0)),
        out_shape=jax.ShapeDtypeStruct((m, n), f32),
        compiler_params=_cparams(("parallel",)),
    )(x, w, res, g.reshape(1, n), b.reshape(1, n))


def _router_body(h_ref, rwt_ref, rb_ref, gate_ref):
    logits = _mxu_nt(rwt_ref[...], h_ref[...], _is_f32(rwt_ref))
    score = jax.nn.sigmoid(logits)
    sel = score + rb_ref[...]
    gsum = []
    for g in range(N_EGROUPS):
        a, b, c, d = (sel[EPG * g + i:EPG * g + i + 1] for i in range(EPG))
        hi1, lo1, hi2, lo2 = jnp.maximum(a, b), jnp.minimum(a, b), jnp.maximum(c, d), jnp.minimum(c, d)
        gsum.append(jnp.maximum(hi1, hi2) + jnp.maximum(jnp.minimum(hi1, hi2), jnp.maximum(lo1, lo2)))
    best = jnp.zeros_like(gsum[0], dtype=i32)
    top = gsum[0]
    for g in range(1, N_EGROUPS):
        upd = gsum[g] > top
        best = jnp.where(upd, g, best)
        top = jnp.where(upd, gsum[g], top)
    selg = sel[0:EPG]
    scg = score[0:EPG]
    for g in range(1, N_EGROUPS):
        selg = jnp.where(best == g, sel[EPG * g:EPG * (g + 1)], selg)
        scg = jnp.where(best == g, score[EPG * g:EPG * (g + 1)], scg)
    rows = [selg[i:i + 1] for i in range(EPG)]
    chosen = []
    for i in range(EPG):
        rank = jnp.zeros_like(best)
        for j in range(EPG):
            if j == i:
                continue
            ahead = (rows[j] > rows[i]) | ((rows[j] == rows[i]) if j < i else False)
            rank = rank + ahead.astype(i32)
        chosen.append(rank < 2)
    wsum = sum(jnp.where(chosen[i], scg[i:i + 1], 0.0) for i in range(EPG))
    gates = [jnp.where(chosen[i], scg[i:i + 1] / wsum, 0.0) for i in range(EPG)]
    out_rows = []
    for g in range(N_EGROUPS):
        for i in range(EPG):
            out_rows.append(jnp.where(best == g, gates[i], 0.0))
    gate_ref[...] = jnp.concatenate(out_rows, axis=0).T


def _router(h, rwt, rb, *, tm):
    m, d = h.shape
    tm = min(tm, m)
    return pl.pallas_call(
        _router_body,
        name=f"router_r{m}",
        grid=(m // tm,),
        in_specs=[pl.BlockSpec((tm, d), lambda i: (i, 0)), pl.BlockSpec((N_EXPERTS, d), lambda i: (0, 0)),
                  pl.BlockSpec((N_EXPERTS, 1), lambda i: (0, 0))],
        out_specs=pl.BlockSpec((tm, N_EXPERTS), lambda i: (i, 0)),
        out_shape=jax.ShapeDtypeStruct((m, N_EXPERTS), f32),
        compiler_params=_cparams(("parallel",)),
    )(h, rwt, rb)


def _moe_body(h_ref, gate_ref, wg_ref, wu_ref, wd_ref, g2_ref, b2_ref, o_ref, acc_ref):
    e = pl.program_id(1)

    @pl.when(e == 0)
    def _():
        acc_ref[...] = jnp.zeros_like(acc_ref)

    precise = _is_f32(wg_ref)
    h = h_ref[...]
    hid = jax.nn.silu(_mxu(h, wg_ref[0], precise)) * _mxu(h, wu_ref[0], precise)
    out = _mxu(hid, wd_ref[0], precise)
    gate = gate_ref[...]
    lane = lax.broadcasted_iota(i32, gate.shape, 1)
    gcol = jnp.sum(jnp.where(lane == e, gate, 0.0), axis=1, keepdims=True)
    acc_ref[...] += gcol * out

    @pl.when(e == N_EXPERTS - 1)
    def _():
        o_ref[...] = _ln(ALPHA * h_ref[...] + acc_ref[...], g2_ref[...], b2_ref[...])


def _moe_ln(h, gate, wg, wu, wd, g2, b2, *, tm):
    m, d = h.shape
    tm = min(tm, m)
    return pl.pallas_call(
        _moe_body,
        name=f"moe_ln_r{m}",
        grid=(m // tm, N_EXPERTS),
        in_specs=[pl.BlockSpec((tm, d), lambda i, e: (i, 0)), pl.BlockSpec((tm, N_EXPERTS), lambda i, e: (i, 0)),
                  pl.BlockSpec((1, d, D_FF), lambda i, e: (e, 0, 0)), pl.BlockSpec((1, d, D_FF), lambda i, e: (e, 0, 0)),
                  pl.BlockSpec((1, D_FF, d), lambda i, e: (e, 0, 0)),
                  pl.BlockSpec((1, d), lambda i, e: (0, 0)), pl.BlockSpec((1, d), lambda i, e: (0, 0))],
        out_specs=pl.BlockSpec((tm, d), lambda i, e: (i, 0)),
        out_shape=jax.ShapeDtypeStruct((m, d), f32),
        scratch_shapes=[pltpu.VMEM((tm, d), f32)],
        compiler_params=_cparams(("parallel", "arbitrary")),
    )(h, gate, wg, wu, wd, g2.reshape(1, d), b2.reshape(1, d))


def _ple_body(h_ref, p_ref, wg_ref, wp_ref, o_ref):
    h = h_ref[...]
    precise = _is_f32(wg_ref)
    gate = jax.nn.sigmoid(_mxu(h, wg_ref[...], precise))
    o_ref[...] = h + gate * _mxu(p_ref[...], wp_ref[...], precise)


def _ple(h, p, wg, wp, *, tm):
    m, d = h.shape
    tm = min(tm, m)
    return pl.pallas_call(
        _ple_body,
        name=f"ple_r{m}",
        grid=(m // tm,),
        in_specs=[pl.BlockSpec((tm, d), lambda i: (i, 0)), pl.BlockSpec((tm, PLE_DIM), lambda i: (i, 0)),
                  pl.BlockSpec((d, d), lambda i: (0, 0)), pl.BlockSpec((PLE_DIM, d), lambda i: (0, 0))],
        out_specs=pl.BlockSpec((tm, d), lambda i: (i, 0)),
        out_shape=jax.ShapeDtypeStruct((m, d), f32),
        compiler_params=_cparams(("parallel",)),
    )(h, p, wg, wp)


def _layer_tail(h1, p, tw, *, tm):
    gate = _router(h1, tw["rwt"], tw["rb"], tm=tm)
    h2 = _moe_ln(h1, gate, tw["wg"], tw["wu"], tw["wd"], tw["ln2_g"], tw["ln2_b"], tm=tm)
    return _ple(h2, p, tw["ple_gate"], tw["ple_proj"], tm=tm)


def _ssd_conv(x_ref, xp_ref, w_ref, b_ref):
    q = x_ref.shape[0]
    xp_ref[8:8 + q, :] = x_ref[...]
    w = w_ref[...]
    acc = b_ref[...] + xp_ref[5:5 + q, :] * w[0:1]
    for k in range(1, SSD_CONV):
        acc = acc + xp_ref[5 + k:5 + k + q, :] * w[k:k + 1]
    xp_ref[0:8, :] = xp_ref[q:q + 8, :]
    return jax.nn.silu(acc)


def _head_expand(width):
    r = lax.broadcasted_iota(i32, (SSD_HPG, SSD_HPG * width), 0)
    c = lax.broadcasted_iota(i32, (SSD_HPG, SSD_HPG * width), 1)
    return (c // width == r).astype(f32)


def _ssd_body(xs_ref, bm_ref, cm_ref, z_ref, dt_ref, dtt_ref, wx_ref, wb_ref, wc_ref, bx_ref, bb_ref, bc_ref,
              dtb_ref, dtbt_ref, alog_ref, alogt_ref, dsk_ref, ng_ref, y_ref, st_ref,
              stt_ref, xpx_ref, xpb_ref, xpc_ref):
    c = pl.program_id(2)
    q = SSD_CHUNK

    @pl.when(c == 0)
    def _():
        stt_ref[...] = jnp.zeros_like(stt_ref)
        xpx_ref[0:8, :] = jnp.zeros((8, xpx_ref.shape[1]), f32)
        xpb_ref[0:8, :] = jnp.zeros((8, xpb_ref.shape[1]), f32)
        xpc_ref[0:8, :] = jnp.zeros((8, xpc_ref.shape[1]), f32)

    xs = _ssd_conv(xs_ref, xpx_ref, wx_ref, bx_ref)
    bm = _ssd_conv(bm_ref, xpb_ref, wb_ref, bb_ref)
    cm = _ssd_conv(cm_ref, xpc_ref, wc_ref, bc_ref)
    dt = jax.nn.softplus(dt_ref[0] + dtb_ref[0])
    dtt = jax.nn.softplus(dtt_ref[0] + dtbt_ref[0])
    dta = dt * (-jnp.exp(alog_ref[0]))
    dtat = dtt * (-jnp.exp(alogt_ref[0]))
    row = lax.broadcasted_iota(i32, (q, q), 0)
    col = lax.broadcasted_iota(i32, (q, q), 1)
    causal = row >= col
    cum = jnp.dot(causal.astype(f32), dta, precision=HIGHEST, preferred_element_type=f32)
    cumt = jnp.dot(dtat, (row <= col).astype(f32), precision=HIGHEST, preferred_element_type=f32)
    e64 = _head_expand(SSD_HEADDIM)
    dtx = jnp.dot(dt, e64, precision=HIGHEST, preferred_element_type=f32)
    cumx = jnp.dot(cum, e64, precision=HIGHEST, preferred_element_type=f32)
    cum128 = jnp.dot(cum, _head_expand(q), precision=HIGHEST, preferred_element_type=f32)
    cum_last = cumx[q - 1:q, :]
    xdt = xs * dtx
    xdt_b = xdt.astype(bf16)
    bm_b = bm.astype(bf16)
    cm_b = cm.astype(bf16)
    cb = _dot_nt(cm_b, bm_b)
    head = lax.broadcasted_iota(i32, (q, SSD_GW), 1) // SSD_HEADDIM
    y = jnp.zeros((q, SSD_GW), f32)
    for r in range(SSD_HPG):
        seg = cum128[:, r * q:(r + 1) * q] - cumt[r:r + 1, :]
        decay = jnp.where(causal, jnp.exp(jnp.where(causal, seg, 0.0)), 0.0)
        yr = _dot((cb * decay).astype(bf16), xdt_b)
        y = jnp.where(head == r, yr, y)
    stt = stt_ref[...]
    y = y + _dot(cm_b, stt.astype(bf16)) * jnp.exp(cumx) + dsk_ref[0] * xs
    to_end = jnp.exp(cum_last - cumx)
    stt_new = stt * jnp.exp(cum_last) + _dot_tn(bm_b, (xdt * to_end).astype(bf16))
    stt_ref[...] = stt_new
    yz = y * jax.nn.silu(z_ref[...])
    yn = yz * lax.rsqrt(jnp.mean(yz * yz, axis=-1, keepdims=True) + 1e-5) * ng_ref[...]
    y_ref[...] = yn.astype(y_ref.dtype)

    @pl.when(c == pl.num_programs(2) - 1)
    def _():
        st_ref[0] = stt_new.T.reshape(SSD_HPG, SSD_HEADDIM, SSD_STATE)


def _ssd_prompt_scan(xbc, z, dt_raw, sw, bsz, t):
    m = bsz * t
    nc = t // SSD_CHUNK
    q, gw, n, g_, hpg = SSD_CHUNK, SSD_GW, SSD_STATE, SSD_GROUPS, SSD_HPG
    dt_g = dt_raw.reshape(m, g_, hpg).transpose(1, 0, 2)
    dtt_g = dt_g.transpose(0, 2, 1)
    nxb = SSD_D_INNER // n
    row = lambda b, g, c: b * nc + c
    in_specs = [
        pl.BlockSpec((q, gw), lambda b, g, c: (row(b, g, c), g)),
        pl.BlockSpec((q, n), lambda b, g, c: (row(b, g, c), nxb + g)),
        pl.BlockSpec((q, n), lambda b, g, c: (row(b, g, c), nxb + g_ + g)),
        pl.BlockSpec((q, gw), lambda b, g, c: (row(b, g, c), g)),
        pl.BlockSpec((1, q, hpg), lambda b, g, c: (g, row(b, g, c), 0)),
        pl.BlockSpec((1, hpg, q), lambda b, g, c: (g, 0, row(b, g, c))),
        pl.BlockSpec((SSD_CONV, gw), lambda b, g, c: (0, g)),
        pl.BlockSpec((SSD_CONV, n), lambda b, g, c: (0, nxb + g)),
        pl.BlockSpec((SSD_CONV, n), lambda b, g, c: (0, nxb + g_ + g)),
        pl.BlockSpec((1, gw), lambda b, g, c: (0, g)),
        pl.BlockSpec((1, n), lambda b, g, c: (0, nxb + g)),
        pl.BlockSpec((1, n), lambda b, g, c: (0, nxb + g_ + g)),
        pl.BlockSpec((1, 1, hpg), lambda b, g, c: (g, 0, 0)),
        pl.BlockSpec((1, hpg, 1), lambda b, g, c: (g, 0, 0)),
        pl.BlockSpec((1, 1, hpg), lambda b, g, c: (g, 0, 0)),
        pl.BlockSpec((1, hpg, 1), lambda b, g, c: (g, 0, 0)),
        pl.BlockSpec((1, 1, gw), lambda b, g, c: (g, 0, 0)),
        pl.BlockSpec((1, gw), lambda b, g, c: (0, g)),
    ]
    y, st = pl.pallas_call(
        _ssd_body,
        name="ssd_scan",
        grid=(bsz, g_, nc),
        in_specs=in_specs,
        out_specs=[pl.BlockSpec((q, gw), lambda b, g, c: (row(b, g, c), g)),
                   pl.BlockSpec((1, hpg, SSD_HEADDIM, n), lambda b, g, c: (b, g, 0, 0))],
        out_shape=[jax.ShapeDtypeStruct((m, SSD_D_INNER), bf16),
                   jax.ShapeDtypeStruct((bsz, SSD_HEADS, SSD_HEADDIM, n), f32)],
        scratch_shapes=[pltpu.VMEM((n, gw), f32), pltpu.VMEM((q + 8, gw), f32),
                        pltpu.VMEM((q + 8, n), f32), pltpu.VMEM((q + 8, n), f32)],
        compiler_params=_cparams(("arbitrary", "arbitrary", "arbitrary")),
    )(xbc, xbc, xbc, z, dt_g, dtt_g, sw["conv_w"], sw["conv_w"], sw["conv_w"], sw["conv_b"], sw["conv_b"], sw["conv_b"],
      sw["dtb"], sw["dtbt"], sw["alog"], sw["alogt"], sw["dskx"], sw["norm_g"])
    return y, st


def _ssd_weights(w_in, conv_w, conv_b, dt_bias, a_log, d_skip, norm_g, w_out, wdt):
    g_, hpg = SSD_GROUPS, SSD_HPG
    return dict(
        w_z=w_in[:, :SSD_D_INNER].astype(wdt),
        w_xbc=w_in[:, SSD_D_INNER:SSD_D_INNER + SSD_CONV_DIM].astype(wdt),
        w_dt=w_in[:, SSD_D_INNER + SSD_CONV_DIM:].astype(wdt),
        conv_w=conv_w, conv_b=conv_b.reshape(1, SSD_CONV_DIM),
        dtb=dt_bias.reshape(g_, 1, hpg), dtbt=dt_bias.reshape(g_, hpg, 1),
        alog=a_log.reshape(g_, 1, hpg), alogt=a_log.reshape(g_, hpg, 1),
        dskx=jnp.repeat(d_skip, SSD_HEADDIM).reshape(g_, 1, SSD_GW),
        dsk=d_skip, dt_bias=dt_bias, a_log=a_log,
        norm_g=norm_g.reshape(1, SSD_D_INNER), w_out=w_out.astype(wdt))


def _ssd_prompt(x2d, sw, bsz, t, ln_g, ln_b):
    (z,) = _mm(x2d, sw["w_z"], tm=512, tn=1024, out_dtypes=(f32,), name="ssd_in_z")
    (xbc,) = _mm(x2d, sw["w_xbc"], tm=512, tn=1024, out_dtypes=(f32,), name="ssd_in_xbc")
    (dt_raw,) = _mm(x2d, sw["w_dt"], tm=512, tn=SSD_HEADS, out_dtypes=(f32,), name="ssd_in_dt")
    y, st = _ssd_prompt_scan(xbc, z, dt_raw, sw, bsz, t)
    h1 = _mm_ln(y, sw["w_out"], x2d, ln_g, ln_b, tm=512, name="ssd_out_ln")
    new_conv = xbc.reshape(bsz, t, SSD_CONV_DIM)[:, t - (SSD_CONV - 1):]
    return h1, new_conv, st


def _pad_rows(x, rows=8):
    return jnp.concatenate([x, jnp.zeros((rows - x.shape[0], x.shape[1]), x.dtype)], axis=0)


def _ssd_step_body(z_ref, xbc_ref, dt_ref, cs_ref, st_ref, cw_ref, cb_ref, dtb_ref, alog_ref, dsk_ref, ng_ref,
                   y_ref, nc_ref, ns_ref):
    n, gw, hd = SSD_STATE, SSD_GW, SSD_HEADDIM
    xbc = xbc_ref[0]
    cs = cs_ref[0]
    w = cw_ref[...]
    conv = cb_ref[...] + cs[0:1] * w[0:1]
    for k in range(1, SSD_CONV - 1):
        conv = conv + cs[k:k + 1] * w[k:k + 1]
    conv = jax.nn.silu(conv + xbc * w[SSD_CONV - 1:SSD_CONV])
    nc_ref[0] = jnp.concatenate([cs[1:], xbc], axis=0)
    xs = conv[:, :SSD_D_INNER]
    b_g = _pad_rows(jnp.concatenate(
        [conv[:, SSD_D_INNER + g * n:SSD_D_INNER + (g + 1) * n] for g in range(SSD_GROUPS)], axis=0))
    c_g = _pad_rows(jnp.concatenate(
        [conv[:, SSD_D_INNER + (SSD_GROUPS + g) * n:SSD_D_INNER + (SSD_GROUPS + g + 1) * n] for g in range(SSD_GROUPS)],
        axis=0))
    dt = jax.nn.softplus(dt_ref[0] + dtb_ref[...])
    decay = jnp.exp(dt * (-jnp.exp(alog_ref[...])))
    er = lax.broadcasted_iota(i32, (SSD_HEADS, SSD_D_INNER), 0)
    ec = lax.broadcasted_iota(i32, (SSD_HEADS, SSD_D_INNER), 1)
    per_head = _pad_rows(jnp.concatenate([dt, decay, dsk_ref[...]], axis=0))
    hx = jnp.dot(per_head, (ec // hd == er).astype(f32), precision=HIGHEST, preferred_element_type=f32)
    dtx, decx, dskx = hx[0:1], hx[1:2], hx[2:3]
    xdt = xs * dtx
    gr = lax.broadcasted_iota(i32, (8, SSD_D_INNER), 0)
    gc = lax.broadcasted_iota(i32, (8, SSD_D_INNER), 1)
    gmask = (gc // gw == gr).astype(f32)
    row0 = (gr == 0).astype(f32)
    st = st_ref[0].reshape(SSD_D_INNER, n)
    upd = _dot_tn(gmask * xdt, b_g, precision=HIGHEST)
    dec_full = _dot_tn(row0 * decx, jnp.ones((8, n), f32), precision=HIGHEST)
    ns_ref[0] = (st * dec_full + upd).reshape(SSD_HEADS, hd, n)
    cst = _dot_nt(c_g, st, precision=HIGHEST)
    y_state = jnp.sum(gmask * cst, axis=0, keepdims=True)
    cbx = jnp.sum(gmask * jnp.sum(c_g * b_g, axis=1, keepdims=True), axis=0, keepdims=True)
    y = cbx * xdt + decx * y_state + dskx * xs
    yz = y * jax.nn.silu(z_ref[0])
    parts = []
    for g in range(SSD_GROUPS):
        seg = yz[:, g * gw:(g + 1) * gw]
        parts.append(seg * lax.rsqrt(jnp.mean(seg * seg, axis=-1, keepdims=True) + 1e-5))
    y_ref[0] = jnp.concatenate(parts, axis=1) * ng_ref[...]


def _ssd_step(x2d, conv_state, ssm_state, sw, ln_g, ln_b):
    bsz = x2d.shape[0]
    (z,) = _mm(x2d, sw["w_z"], tm=bsz, tn=1024, out_dtypes=(f32,), name="ssd_in_z")
    (xbc,) = _mm(x2d, sw["w_xbc"], tm=bsz, tn=1024, out_dtypes=(f32,), name="ssd_in_xbc")
    (dt_raw,) = _mm(x2d, sw["w_dt"], tm=bsz, tn=SSD_HEADS, out_dtypes=(f32,), name="ssd_in_dt")
    rowspec = lambda width: pl.BlockSpec((1, 1, width), lambda b: (b, 0, 0))
    full = lambda r, c: pl.BlockSpec((r, c), lambda b: (0, 0))
    y, new_conv, new_state = pl.pallas_call(
        _ssd_step_body,
        name="ssd_step",
        grid=(bsz,),
        in_specs=[rowspec(SSD_D_INNER), rowspec(SSD_CONV_DIM), rowspec(SSD_HEADS),
                  pl.BlockSpec((1, SSD_CONV - 1, SSD_CONV_DIM), lambda b: (b, 0, 0)),
                  pl.BlockSpec((1, SSD_HEADS, SSD_HEADDIM, SSD_STATE), lambda b: (b, 0, 0, 0)),
                  full(SSD_CONV, SSD_CONV_DIM), full(1, SSD_CONV_DIM), full(1, SSD_HEADS), full(1, SSD_HEADS),
                  full(1, SSD_HEADS), full(1, SSD_D_INNER)],
        out_specs=[rowspec(SSD_D_INNER),
                   pl.BlockSpec((1, SSD_CONV - 1, SSD_CONV_DIM), lambda b: (b, 0, 0)),
                   pl.BlockSpec((1, SSD_HEADS, SSD_HEADDIM, SSD_STATE), lambda b: (b, 0, 0, 0))],
        out_shape=[jax.ShapeDtypeStruct((bsz, 1, SSD_D_INNER), f32),
                   jax.ShapeDtypeStruct((bsz, SSD_CONV - 1, SSD_CONV_DIM), f32),
                   jax.ShapeDtypeStruct((bsz, SSD_HEADS, SSD_HEADDIM, SSD_STATE), f32)],
        compiler_params=_cparams(("parallel",)),
    )(z.reshape(bsz, 1, -1), xbc.reshape(bsz, 1, -1), dt_raw.reshape(bsz, 1, -1), conv_state, ssm_state,
      sw["conv_w"], sw["conv_b"], sw["dt_bias"].reshape(1, -1), sw["a_log"].reshape(1, -1), sw["dsk"].reshape(1, -1),
      sw["norm_g"])
    h1 = _mm_ln(y.reshape(bsz, SSD_D_INNER), sw["w_out"], x2d, ln_g, ln_b, tm=bsz, name="ssd_out_ln")
    return h1, new_conv, new_state


def _slope(head):
    return 2.0 ** (-8.0 * (head + 1) / NSA_HEADS)


def _masked_softmax(s, mask):
    s = jnp.where(mask, s, NEG)
    m = jnp.max(s, axis=-1, keepdims=True)
    e = jnp.exp(s - m) * mask.astype(f32)
    return e / jnp.maximum(jnp.sum(e, axis=-1, keepdims=True), 1e-30)


def _tile_rows(x, n):
    return jnp.concatenate([x] * n, axis=0)


def _slope_col(kv_head, rows):
    return jnp.concatenate([jnp.full((rows, 1), _slope(kv_head * NSA_REP + r), f32) for r in range(NSA_REP)], axis=0)


def _topk_rank(imp):
    nb = imp.shape[1]
    lane = lax.broadcasted_iota(i32, imp.shape, 1)
    rank = jnp.zeros(imp.shape, i32)
    for j in range(nb):
        cj = imp[:, j:j + 1]
        rank = rank + ((cj > imp) | ((cj == imp) & (lane > j))).astype(i32)
    return rank


def _topk_mask(imp, n_sel):
    return _topk_rank(imp) < n_sel


def _pos_weights(pw_ref):
    w = jax.nn.softmax(pw_ref[...], axis=0)
    r = lax.broadcasted_iota(i32, (2 * NSA_KV, 2 * NSA_KVD), 0)
    c = lax.broadcasted_iota(i32, (2 * NSA_KV, 2 * NSA_KVD), 1)
    return jnp.dot(w, (c // NSA_DH == r).astype(f32), precision=HIGHEST, preferred_element_type=f32)


def _compress_body(x_ref, pw_ref, o_ref):
    wx = _pos_weights(pw_ref)
    x = x_ref[...]
    nb = x.shape[0] // NSA_BLOCK
    o_ref[0] = jnp.sum(x.reshape(nb, NSA_BLOCK, 2 * NSA_KVD) * wx[None], axis=1)


def _nsa_compress(kv4, pw, bsz, t):
    nb = t // NSA_BLOCK
    return pl.pallas_call(
        _compress_body,
        name="nsa_compress",
        grid=(bsz,),
        in_specs=[pl.BlockSpec((t, 2 * NSA_KVD), lambda b: (b, 0)),
                  pl.BlockSpec((NSA_BLOCK, 2 * NSA_KV), lambda b: (0, 0))],
        out_specs=pl.BlockSpec((1, nb, 2 * NSA_KVD), lambda b: (b, 0, 0)),
        out_shape=jax.ShapeDtypeStruct((bsz, nb, 2 * NSA_KVD), f32),
        compiler_params=_cparams(("parallel",)),
    )(kv4, pw)


NSA_TQ = 128
NSA_CK = 512


def _nsa_prompt_body(q_ref, g_ref, kcvc_ref, kvs_ref, kvw_ref, o_ref, *, t_len):
    i = pl.program_id(1)
    tq, ck, rep = NSA_TQ, NSA_CK, NSA_REP
    ck = min(ck, t_len)
    nb = t_len // NSA_BLOCK
    n_sel = min(NSA_TOP, nb)
    wlen = min(NSA_WINDOW + tq, t_len)
    t0 = i * tq
    t_col = t0 + lax.broadcasted_iota(i32, (tq, 1), 0)
    lane128 = lax.broadcasted_iota(i32, (1, LANES), 1)
    gates = g_ref[...]
    kcvc = kcvc_ref[0]
    blk = lax.broadcasted_iota(i32, (1, nb), 1)
    dist_c = t_col - ((blk + 1) * NSA_BLOCK - 1)
    dist_c4 = _tile_rows(dist_c, rep)
    cur = t_col // NSA_BLOCK
    n_chunks = (t0 + tq + ck - 1) // ck
    w_start = jnp.clip(t0 - NSA_WINDOW, 0, t_len - wlen)
    w_start = pl.multiple_of(w_start, tq)
    dist_w = t_col - (w_start + lax.broadcasted_iota(i32, (1, wlen), 1))
    dist_w4 = _tile_rows(dist_w, rep)
    mask_w4 = (dist_w4 >= 0) & (dist_w4 < NSA_WINDOW)

    for j in range(NSA_KV // 2):
        cols = slice(j * LANES, (j + 1) * LANES)
        vcols = slice(NSA_KVD + j * LANES, NSA_KVD + (j + 1) * LANES)
        q_tiles = [q_ref[:, (j * rep + r) * LANES:(j * rep + r + 1) * LANES] for r in range(rep)]
        q_rows = jnp.concatenate(q_tiles, axis=0)
        kc_b = kcvc[:, cols].astype(bf16)
        vc_b = kcvc[:, vcols].astype(bf16)
        out_tiles = [jnp.zeros((tq, LANES), f32) for _ in range(rep)]
        for h in range(2):
            k = 2 * j + h
            half = (lane128 // NSA_DH) == h
            q_pad = jnp.where(half, q_rows, jnp.zeros_like(q_rows))
            slope = _slope_col(k, tq)
            s_c = _dot_nt(q_pad, kc_b) - slope * dist_c4.astype(f32)
            p_c = _masked_softmax(s_c, dist_c4 >= 0)
            o_c = _dot(p_c.astype(bf16), vc_b)
            imp = sum(p_c[r * tq:(r + 1) * tq] for r in range(rep))
            imp = jnp.where((blk == cur) | (blk == 0), 1e4, imp)
            imp = jnp.where(blk > cur, -1.0, imp)
            sel_b = _topk_mask(imp, n_sel).astype(bf16)

            def sel_step(c, carry, q_pad=q_pad, slope=slope, sel_b=sel_b, cols=cols, vcols=vcols):
                m, l, acc = carry
                k0 = pl.multiple_of(c * ck, ck)
                ks = kvs_ref[pl.ds(k0, ck), cols]
                vs = kvs_ref[pl.ds(k0, ck), vcols]
                kpos = k0 + lax.broadcasted_iota(i32, (1, ck), 1)
                brow = lax.broadcasted_iota(i32, (nb, ck), 0)
                expand = ((k0 + lax.broadcasted_iota(i32, (nb, ck), 1)) // NSA_BLOCK == brow).astype(bf16)
                selx = _dot(sel_b, expand)
                dist = t_col - kpos
                mask = _tile_rows((selx > 0.5) & (dist >= 0), rep)
                s = _dot_nt(q_pad, ks) - slope * _tile_rows(dist, rep).astype(f32)
                s = jnp.where(mask, s, NEG)
                m_new = jnp.maximum(m, jnp.max(s, axis=-1, keepdims=True))
                a = jnp.exp(m - m_new)
                p = jnp.exp(s - m_new) * mask.astype(f32)
                l = a * l + jnp.sum(p, axis=-1, keepdims=True)
                acc = a * acc + _dot(p.astype(bf16), vs)
                return m_new, l, acc

            init = (jnp.full((rep * tq, 1), NEG, f32), jnp.zeros((rep * tq, 1), f32), jnp.zeros((rep * tq, LANES), f32))
            _, l_s, acc_s = lax.fori_loop(0, n_chunks, sel_step, init)
            o_s = acc_s / jnp.maximum(l_s, 1e-30)
            kw = kvw_ref[pl.ds(w_start, wlen), cols]
            vw = kvw_ref[pl.ds(w_start, wlen), vcols]
            s_w = _dot_nt(q_pad, kw) - slope * dist_w4.astype(f32)
            p_w = _masked_softmax(s_w, mask_w4)
            o_w = _dot(p_w.astype(bf16), vw)
            for r in range(rep):
                gc = (k * rep + r) * 3
                rows = slice(r * tq, (r + 1) * tq)
                o_r = gates[:, gc:gc + 1] * o_c[rows] + gates[:, gc + 1:gc + 2] * o_s[rows] + gates[:, gc + 2:gc + 3] * o_w[rows]
                out_tiles[r] = jnp.where(half, o_r, out_tiles[r])
        for r in range(rep):
            o_ref[:, (j * rep + r) * LANES:(j * rep + r + 1) * LANES] = out_tiles[r].astype(o_ref.dtype)


def _nsa_prompt_attn(q, gates, kcvc, kv4_b, kvw_b, bsz, t):
    nq = t // NSA_TQ
    nb = t // NSA_BLOCK
    return pl.pallas_call(
        functools.partial(_nsa_prompt_body, t_len=t),
        name="nsa_attn",
        grid=(bsz, nq),
        in_specs=[pl.BlockSpec((NSA_TQ, NSA_QD), lambda b, i: (b * nq + i, 0)),
                  pl.BlockSpec((NSA_TQ, LANES), lambda b, i: (b * nq + i, 0)),
                  pl.BlockSpec((1, nb, 2 * NSA_KVD), lambda b, i: (b, 0, 0)),
                  pl.BlockSpec((t, 2 * NSA_KVD), lambda b, i: (b, 1)),
                  pl.BlockSpec((t, 2 * NSA_KVD), lambda b, i: (b, 0))],
        out_specs=pl.BlockSpec((NSA_TQ, NSA_QD), lambda b, i: (b * nq + i, 0)),
        out_shape=jax.ShapeDtypeStruct((bsz * t, NSA_QD), bf16),
        compiler_params=_cparams(("parallel", "arbitrary")),
    )(q, gates, kcvc, kv4_b, kvw_b)


def _pair_layout_cols(w):
    lead = w.shape[:-1]
    w = w.reshape(*lead, NSA_KV // 2, 2, NSA_REP, NSA_DH)
    return jnp.swapaxes(w, -3, -2).reshape(*lead, NSA_QD)


def _nsa_weights(w_in, pos_w, w_out, wdt, pair):
    lay = _pair_layout_cols if pair else (lambda w: w)
    kv0 = NSA_QD
    g0 = NSA_QD + 6 * NSA_KVD
    wg = jnp.zeros((D_MODEL, LANES), f32).at[:, :3 * NSA_HEADS].set(w_in[:, g0:])
    return dict(
        w_q=lay(w_in[:, :NSA_QD]).astype(wdt),
        w_kv4=w_in[:, kv0:kv0 + 4 * NSA_KVD].astype(wdt),
        w_kvw=w_in[:, kv0 + 4 * NSA_KVD:g0].astype(wdt),
        w_g=wg.astype(wdt),
        pw=pos_w.transpose(1, 0, 2).reshape(NSA_BLOCK, 2 * NSA_KV),
        pwt=pos_w.transpose(0, 2, 1).reshape(2 * NSA_KV, NSA_BLOCK),
        w_kvt=w_in[:, kv0:g0].T.astype(wdt),
        w_out=lay(w_out.T).T.astype(wdt))


def _nsa_prompt(x2d, nw, bsz, t, ln_g, ln_b):
    (q,) = _mm(x2d, nw["w_q"], tm=512, tn=1024, out_dtypes=(bf16,), scale=NSA_DH ** -0.5, name="nsa_in_q")
    kv4, kv4_b = _mm(x2d, nw["w_kv4"], tm=512, tn=1024, out_dtypes=(f32, bf16), name="nsa_in_kv4")
    kvw, kvw_b = _mm(x2d, nw["w_kvw"], tm=512, tn=512, out_dtypes=(f32, bf16), name="nsa_in_kvw")
    (gates,) = _mm(x2d, nw["w_g"], tm=512, tn=LANES, out_dtypes=(f32,), act="sigmoid", name="nsa_in_gates")
    kcvc = _nsa_compress(kv4, nw["pw"], bsz, t)
    o = _nsa_prompt_attn(q, gates, kcvc, kv4_b, kvw_b, bsz, t)
    h1 = _mm_ln(o, nw["w_out"], x2d, ln_g, ln_b, tm=512, name="nsa_out_ln")
    w_keep = min(NSA_WINDOW, t)
    new_kv = kv4.reshape(bsz, t, 4, NSA_KV, NSA_DH)
    new_win = kvw.reshape(bsz, t, 2, NSA_KV, NSA_DH)[:, t - w_keep:]
    return h1, new_kv, new_win


PAGES_PER_STEP = 16


def _lane_weights(pwt_ref):
    w = jax.nn.softmax(pwt_ref[...], axis=1)
    return jnp.concatenate([w] * (PAGE_SIZE // NSA_BLOCK), axis=1)


def _page_compress_body(pt_ref, *refs):
    del pt_ref
    page_refs, pwt_ref, o_ref = refs[:PAGES_PER_STEP], refs[PAGES_PER_STEP], refs[PAGES_PER_STEP + 1]
    s = pl.program_id(1)
    bpp = PAGE_SIZE // NSA_BLOCK
    nbp = o_ref.shape[-1]

    @pl.when(s == 0)
    def _():
        o_ref[...] = jnp.zeros_like(o_ref)

    w = _lane_weights(pwt_ref)
    tok_blk = lax.broadcasted_iota(i32, (1, PAGE_SIZE), 1) // NSA_BLOCK
    out_lane = lax.broadcasted_iota(i32, (1, nbp), 1)
    for c in range(2):
        for k in range(NSA_KV):
            wk = w[c * NSA_KV + k:c * NSA_KV + k + 1]
            upd = jnp.zeros((NSA_DH, nbp), f32)
            for i, p_ref in enumerate(page_refs):
                xw = p_ref[0, c, k] * wk
                first = (s * PAGES_PER_STEP + i) * bpp
                for h in range(bpp):
                    r = jnp.sum(jnp.where(tok_blk == h, xw, 0.0), axis=1, keepdims=True)
                    upd = upd + jnp.where(out_lane == first + h, r, 0.0)
            o_ref[0, c, k] += upd


def _page_compress(pool_t, page_table, pwt):
    bsz, n_pages = page_table.shape
    nbp = n_pages * (PAGE_SIZE // NSA_BLOCK)
    steps = n_pages // PAGES_PER_STEP
    assert n_pages % PAGES_PER_STEP == 0
    page_specs = [pl.BlockSpec((1, 2, NSA_KV, NSA_DH, PAGE_SIZE),
                               lambda b, s, pt, i=i: (pt[b, s * PAGES_PER_STEP + i], 0, 0, 0, 0))
                  for i in range(PAGES_PER_STEP)]
    return pl.pallas_call(
        _page_compress_body,
        name="nsa_page_compress",
        grid_spec=pltpu.PrefetchScalarGridSpec(
            num_scalar_prefetch=1, grid=(bsz, steps),
            in_specs=page_specs + [pl.BlockSpec((2 * NSA_KV, NSA_BLOCK), lambda b, s, pt: (0, 0))],
            out_specs=pl.BlockSpec((1, 2, NSA_KV, NSA_DH, nbp), lambda b, s, pt: (b, 0, 0, 0, 0))),
        out_shape=jax.ShapeDtypeStruct((bsz, 2, NSA_KV, NSA_DH, nbp), f32),
        compiler_params=_cparams(("parallel", "arbitrary")),
    )(page_table, *([pool_t] * PAGES_PER_STEP), pwt)


def _mm_nt_body(wt_ref, x_ref, o_ref):
    o_ref[...] = _mxu_nt(wt_ref[...], x_ref[...], _is_f32(wt_ref))


def _mm_nt(wt, x, *, tn, name):
    n, k = wt.shape
    m = x.shape[0]
    return pl.pallas_call(
        _mm_nt_body, name=f"{name}_r{m}", grid=(n // tn,),
        in_specs=[pl.BlockSpec((tn, k), lambda j: (j, 0)), pl.BlockSpec((m, k), lambda j: (0, 0))],
        out_specs=pl.BlockSpec((tn, m), lambda j: (j, 0)),
        out_shape=jax.ShapeDtypeStruct((n, m), f32),
        compiler_params=_cparams(("parallel",)),
    )(wt, x)


def _step_slopes(kv_head):
    r = lax.broadcasted_iota(i32, (8, 1), 0)
    return jnp.exp2(-0.5 * (kv_head * NSA_REP + r + 1).astype(f32))


def _gate_col(gates, kv_head, branch):
    cols = [(kv_head * NSA_REP + r) * 3 + branch for r in range(NSA_REP)]
    return jnp.concatenate([gates[:, c:c + 1] for c in cols], axis=0)


def _own_col(x, b):
    lane = lax.broadcasted_iota(i32, x.shape, x.ndim - 1)
    return jnp.sum(jnp.where(lane == b, x, 0.0), axis=-1, keepdims=True)


def _new_key_tile(col):
    lane = lax.broadcasted_iota(i32, (col.shape[0], LANES), 1)
    return jnp.where(lane == 0, col, 0.0)


def _nsa_step_cw_body(q_ref, g_ref, cmp_ref, kvt_ref, pwt_ref, win_ref, o_ref, idx_ref, *, past_len):
    b = pl.program_id(0)
    rep, kv = NSA_REP, NSA_KV
    nbp = past_len // NSA_BLOCK
    nbt = nbp + 8
    w_buf = win_ref.shape[-1]
    w0 = jax.nn.softmax(pwt_ref[...], axis=1)[:, 0:1]
    lane_c = lax.broadcasted_iota(i32, (1, nbp + LANES), 1)
    dist_c = past_len - ((lane_c + 1) * NSA_BLOCK - 1)
    lane_w = lax.broadcasted_iota(i32, (1, w_buf + LANES), 1)
    dist_w = jnp.where(lane_w <= w_buf, w_buf - lane_w, -1)
    mask_w = (dist_w >= 0) & (dist_w < NSA_WINDOW)
    blk = lax.broadcasted_iota(i32, (1, nbt), 1)
    cur = past_len // NSA_BLOCK
    gates = g_ref[0]
    for k in range(kv):
        q8 = _pad_rows(q_ref[0, k])
        slope = _step_slopes(k)
        new = [_own_col(kvt_ref[s, k], b) for s in (0, 1, 4, 5)]
        k_c = jnp.concatenate([cmp_ref[0, 0, k], _new_key_tile(new[0] * w0[k:k + 1])], axis=1)
        v_c = jnp.concatenate([cmp_ref[0, 1, k], _new_key_tile(new[1] * w0[kv + k:kv + k + 1])], axis=1)
        s_c = _mxu(q8, k_c, True) - slope * dist_c.astype(f32)
        p_c = _masked_softmax(s_c, jnp.broadcast_to(dist_c >= 0, s_c.shape))
        o_c = _mxu_nt(p_c, v_c, True)
        imp = jnp.sum(p_c[0:rep], axis=0, keepdims=True)[:, :nbt]
        imp = jnp.where((blk == cur) | (blk == 0), 1e4, imp)
        imp = jnp.where(blk > cur, -2.0, imp)
        rank = _topk_rank(imp)
        pick = lax.broadcasted_iota(i32, (NSA_TOP, nbt), 0) == rank
        idx_ref[0, k] = jnp.sum(jnp.where(pick, blk.astype(f32), 0.0), axis=1, keepdims=True).astype(i32)
        k_w = jnp.concatenate([win_ref[0, 0, k], _new_key_tile(new[2])], axis=1)
        v_w = jnp.concatenate([win_ref[0, 1, k], _new_key_tile(new[3])], axis=1)
        s_w = _mxu(q8, k_w, True) - slope * dist_w.astype(f32)
        p_w = _masked_softmax(s_w, jnp.broadcast_to(mask_w, s_w.shape))
        o_w = _mxu_nt(p_w, v_w, True)
        o_ref[0, k] = _gate_col(gates, k, 0) * o_c[0:rep] + _gate_col(gates, k, 2) * o_w[0:rep]


def _nsa_step_sel_body(idx_ref, pt_ref, *refs, past_len):
    del pt_ref
    n_in = 2 * NSA_TOP
    tile_refs = refs[:n_in]
    q_ref, g_ref, kvt_ref, part_ref, o_ref = refs[n_in:]
    b, k = pl.program_id(0), pl.program_id(1)
    nbp = past_len // NSA_BLOCK
    rep, lb, kv = NSA_REP, NSA_BLOCK, NSA_KV
    bpp = PAGE_SIZE // lb
    new_k = _new_key_tile(_own_col(kvt_ref[0, 0], b))
    new_v = _new_key_tile(_own_col(kvt_ref[1, 0], b))
    lane = lax.broadcasted_iota(i32, (1, NSA_TOP * PAGE_SIZE), 1)
    tok = lane % PAGE_SIZE
    spos = tok % lb
    valid = lane < 0
    ks, vs = [], []
    for i in range(NSA_TOP):
        n = idx_ref[(b * kv + k) * NSA_TOP + i]
        is_new = n >= nbp
        ks.append(jnp.where(is_new, new_k, tile_refs[2 * i][0, 0, 0]))
        vs.append(jnp.where(is_new, new_v, tile_refs[2 * i + 1][0, 0, 0]))
        mine = lane // PAGE_SIZE == i
        spos = spos + jnp.where(mine, n * lb, 0)
        half = jnp.where(is_new, 0, n % bpp)
        last = jnp.where(is_new, 0, lb - 1)
        valid = valid | (mine & (tok // lb == half) & (tok % lb <= last))
    k_all = jnp.concatenate(ks, axis=1)
    v_all = jnp.concatenate(vs, axis=1)
    dist = past_len - spos
    q8 = _pad_rows(q_ref[0, 0])
    s = _mxu(q8, k_all, True) - _step_slopes(k) * dist.astype(f32)
    p = _masked_softmax(s, jnp.broadcast_to(valid & (dist >= 0), s.shape))
    o_s = _mxu_nt(p, v_all, True)
    g = g_ref[0, 0]
    g_sel = jnp.concatenate([g[:, 3 * r + 1:3 * r + 2] for r in range(rep)], axis=0)
    o_ref[0, 0] = part_ref[0, 0] + g_sel * o_s[0:rep]


def _nsa_step(x2d, pool, page_table, win_cache, nw, ln_g, ln_b):
    bsz = x2d.shape[0]
    n_pages = page_table.shape[1]
    past_len = n_pages * PAGE_SIZE
    w_buf = win_cache.shape[1]
    rep, kv, dh = NSA_REP, NSA_KV, NSA_DH
    bpp = PAGE_SIZE // NSA_BLOCK
    nbp = past_len // NSA_BLOCK
    (q,) = _mm(x2d, nw["w_q"], tm=bsz, tn=1024, out_dtypes=(f32,), scale=NSA_DH ** -0.5, name="nsa_in_q")
    (kv4,) = _mm(x2d, nw["w_kv4"], tm=bsz, tn=1024, out_dtypes=(f32,), name="nsa_in_kv4")
    (kvw,) = _mm(x2d, nw["w_kvw"], tm=bsz, tn=512, out_dtypes=(f32,), name="nsa_in_kvw")
    (gates,) = _mm(x2d, nw["w_g"], tm=bsz, tn=LANES, out_dtypes=(f32,), act="sigmoid", name="nsa_in_gates")
    kvt = _mm_nt(nw["w_kvt"], x2d, tn=512, name="nsa_in_kvt").reshape(6, kv, dh, bsz)
    pool_t = jnp.transpose(pool, (0, 2, 3, 4, 1))
    win_t = jnp.transpose(win_cache, (0, 2, 3, 4, 1))
    cmp = _page_compress(pool_t, page_table, nw["pwt"])
    q4 = q.reshape(bsz, kv, rep, dh)
    slab = pl.BlockSpec((1, kv, rep, dh), lambda b: (b, 0, 0, 0))
    whole = lambda shape: pl.BlockSpec(shape, lambda b: (0,) * len(shape))
    part, idx = pl.pallas_call(
        functools.partial(_nsa_step_cw_body, past_len=past_len),
        name="nsa_step_cw",
        grid=(bsz,),
        in_specs=[slab, pl.BlockSpec((1, 1, LANES), lambda b: (b, 0, 0)),
                  pl.BlockSpec((1, 2, kv, dh, nbp), lambda b: (b, 0, 0, 0, 0)), whole((6, kv, dh, bsz)),
                  whole((2 * kv, NSA_BLOCK)), pl.BlockSpec((1, 2, kv, dh, w_buf), lambda b: (b, 0, 0, 0, 0))],
        out_specs=[slab, pl.BlockSpec((1, kv, NSA_TOP, 1), lambda b: (b, 0, 0, 0))],
        out_shape=[jax.ShapeDtypeStruct((bsz, kv, rep, dh), f32),
                   jax.ShapeDtypeStruct((bsz, kv, NSA_TOP, 1), i32)],
        compiler_params=_cparams(("parallel",)),
    )(q4, gates.reshape(bsz, 1, -1), cmp, kvt, nw["pwt"], win_t)

    def tile_spec(i, slot):
        def imap(b, k, idx_r, pt_r):
            n = jnp.minimum(idx_r[(b * kv + k) * NSA_TOP + i], nbp - 1)
            return (pt_r[b * n_pages + n // bpp], slot, k, 0, 0)
        return pl.BlockSpec((1, 1, 1, dh, PAGE_SIZE), imap)

    tile_specs = []
    for i in range(NSA_TOP):
        tile_specs += [tile_spec(i, 2), tile_spec(i, 3)]
    head = pl.BlockSpec((1, 1, rep, dh), lambda b, k, *_: (b, k, 0, 0))
    o = pl.pallas_call(
        functools.partial(_nsa_step_sel_body, past_len=past_len),
        name="nsa_step_sel",
        grid_spec=pltpu.PrefetchScalarGridSpec(
            num_scalar_prefetch=2, grid=(bsz, kv),
            in_specs=tile_specs + [
                head, pl.BlockSpec((1, 1, 1, 3 * rep), lambda b, k, *_: (b, k, 0, 0)),
                pl.BlockSpec((2, 1, dh, bsz), lambda b, k, *_: (1, k, 0, 0)), head],
            out_specs=head),
        out_shape=jax.ShapeDtypeStruct((bsz, kv, rep, dh), f32),
        compiler_params=_cparams(("arbitrary", "arbitrary")),
    )(idx.reshape(-1), page_table.reshape(-1), *([pool_t] * (2 * NSA_TOP)),
      q4, gates[:, :3 * NSA_HEADS].reshape(bsz, kv, 1, 3 * rep), kvt, part)
    h1 = _mm_ln(o.reshape(bsz, NSA_QD), nw["w_out"], x2d, ln_g, ln_b, tm=bsz, name="nsa_out_ln")
    new_kv = kv4.reshape(bsz, 1, 4, kv, dh)
    new_win = jnp.concatenate([win_cache[:, 1:], kvw.reshape(bsz, 1, 2, kv, dh)], axis=1)
    return h1, new_kv, new_win


def kernel(x_prompt, x_sample, state_ssm, state_conv, cache_kv, cache_win, page_table, p_prompt, p_sample,
           ssd_w_in, ssd_conv_w, ssd_conv_b, ssd_dt_bias, ssd_a_log, ssd_d, ssd_norm_g, ssd_w_out,
           nsa_w_in, nsa_pos_w, nsa_w_out, ln1_g, ln1_b, ln2_g, ln2_b, router_w, router_bias,
           moe_w_gate, moe_w_up, moe_w_down, ple_proj, ple_gate):
    bp, t, d = x_prompt.shape
    bs = x_sample.shape[0]
    xp = x_prompt.reshape(bp * t, d)
    xs = x_sample.reshape(bs, d)
    rb = router_bias.reshape(N_EXPERTS, 1)
    ssm_p, conv_p, kv_p, win_p, ssm_s, conv_s, kv_s, win_s = [], [], [], [], [], [], [], []
    for i in range(DEPTH):
        j = i // 2
        if i % 2 == 0:
            ssd = (ssd_w_in[j], ssd_conv_w[j], ssd_conv_b[j], ssd_dt_bias[j], ssd_a_log[j], ssd_d[j], ssd_norm_g[j],
                   ssd_w_out[j])
            h1p, c_new, s_new = _ssd_prompt(xp, _ssd_weights(*ssd, bf16), bp, t, ln1_g[i], ln1_b[i])
            conv_p.append(c_new)
            ssm_p.append(s_new)
            h1s, c_new, s_new = _ssd_step(xs, state_conv[j], state_ssm[j], _ssd_weights(*ssd, f32), ln1_g[i], ln1_b[i])
            conv_s.append(c_new)
            ssm_s.append(s_new)
        else:
            nsa = (nsa_w_in[j], nsa_pos_w[j], nsa_w_out[j])
            h1p, r_new, w_new = _nsa_prompt(xp, _nsa_weights(*nsa, bf16, True), bp, t, ln1_g[i], ln1_b[i])
            kv_p.append(r_new)
            win_p.append(w_new)
            h1s, r_new, w_new = _nsa_step(xs, cache_kv[j], page_table, cache_win[j], _nsa_weights(*nsa, f32, False),
                                          ln1_g[i], ln1_b[i])
            kv_s.append(r_new)
            win_s.append(w_new)

        def tail_weights(wdt, i=i):
            return dict(rwt=router_w.T.astype(wdt), rb=rb, wg=moe_w_gate[i].astype(wdt), wu=moe_w_up[i].astype(wdt),
                        wd=moe_w_down[i].astype(wdt), ln2_g=ln2_g[i], ln2_b=ln2_b[i],
                        ple_gate=ple_gate[i].astype(wdt), ple_proj=ple_proj[i].astype(wdt))

        xp = _layer_tail(h1p, p_prompt[i].reshape(bp * t, PLE_DIM), tail_weights(bf16), tm=512)
        xs = _layer_tail(h1s, p_sample[i].reshape(bs, PLE_DIM), tail_weights(f32), tm=bs)
    return (xp.reshape(bp, t, d), xs.reshape(bs, 1, d), jnp.stack(ssm_p), jnp.stack(conv_p), jnp.stack(kv_p),
            jnp.stack(win_p), jnp.stack(ssm_s), jnp.stack(conv_s), jnp.stack(kv_s), jnp.stack(win_s))
```

```python
import functools

import jax
import jax.numpy as jnp
from jax import lax
from jax.experimental import pallas as pl
from jax.experimental.pallas import tpu as pltpu

f32, bf16, i32 = jnp.float32, jnp.bfloat16, jnp.int32
HIGHEST = lax.Precision.HIGHEST

D_MODEL = 1024
DEPTH = 2
PLE_DIM = 256
SSD_D_INNER = 2048
SSD_HEADDIM = 64
SSD_HEADS = 32
SSD_GROUPS = 4
SSD_HPG = 8
SSD_STATE = 128
SSD_CONV = 4
SSD_CHUNK = 128
SSD_GW = SSD_HPG * SSD_HEADDIM
SSD_CONV_DIM = SSD_D_INNER + 2 * SSD_GROUPS * SSD_STATE
NSA_HEADS = 16
NSA_KV = 4
NSA_REP = 4
NSA_DH = 64
NSA_BLOCK = 64
NSA_TOP = 16
NSA_WINDOW = 512
NSA_QD = NSA_HEADS * NSA_DH
NSA_KVD = NSA_KV * NSA_DH
PAGE_SIZE = 128
N_EXPERTS = 16
N_EGROUPS = 4
EPG = 4
D_FF = 512
ALPHA = (2.0 * DEPTH) ** 0.25
LN_EPS = 1e-5
NEG = -1e30
LANES = 128
VMEM_LIMIT = 56 * 1024 * 1024


def _cparams(sem):
    return pltpu.CompilerParams(dimension_semantics=sem, vmem_limit_bytes=VMEM_LIMIT)


def _ln(v, g, b):
    mu = jnp.mean(v, axis=-1, keepdims=True)
    d = v - mu
    var = jnp.mean(d * d, axis=-1, keepdims=True)
    return d * lax.rsqrt(var + LN_EPS) * g + b


def _dot(a, b):
    return jnp.dot(a, b, preferred_element_type=f32)


def _mxu(a, b, precise):
    if precise:
        return jnp.dot(a.astype(f32), b.astype(f32), precision=HIGHEST, preferred_element_type=f32)
    return jnp.dot(a.astype(bf16), b.astype(bf16), preferred_element_type=f32)


def _mxu_nt(a, b, precise):
    if precise:
        return _dot_nt(a.astype(f32), b.astype(f32), precision=HIGHEST)
    return _dot_nt(a.astype(bf16), b.astype(bf16))


def _dot_nt(a, b, precision=None):
    return lax.dot_general(a, b, (((1,), (1,)), ((), ())), precision=precision, preferred_element_type=f32)


def _dot_tn(a, b, precision=None):
    return lax.dot_general(a, b, (((0,), (0,)), ((), ())), precision=precision, preferred_element_type=f32)


def _is_f32(ref):
    return ref.dtype == jnp.float32


def _mm_body(x_ref, w_ref, *o_refs, act, scale):
    acc = _mxu(x_ref[...], w_ref[...], _is_f32(w_ref))
    if scale != 1.0:
        acc = acc * scale
    if act == "sigmoid":
        acc = jax.nn.sigmoid(acc)
    for o_ref in o_refs:
        o_ref[...] = acc.astype(o_ref.dtype)


def _mm(x, w, *, tm, tn, out_dtypes, name, act=None, scale=1.0):
    m, k = x.shape
    n = w.shape[1]
    tm, tn = min(tm, m), min(tn, n)
    assert m % tm == 0 and n % tn == 0
    outs = pl.pallas_call(
        functools.partial(_mm_body, act=act, scale=scale),
        name=f"{name}_r{m}",
        grid=(m // tm, n // tn),
        in_specs=[pl.BlockSpec((tm, k), lambda i, j: (i, 0)), pl.BlockSpec((k, tn), lambda i, j: (0, j))],
        out_specs=[pl.BlockSpec((tm, tn), lambda i, j: (i, j)) for _ in out_dtypes],
        out_shape=[jax.ShapeDtypeStruct((m, n), dt) for dt in out_dtypes],
        compiler_params=_cparams(("parallel", "arbitrary")),
    )(x, w)
    return outs


def _mm_ln_body(x_ref, w_ref, res_ref, g_ref, b_ref, o_ref):
    acc = _mxu(x_ref[...], w_ref[...], _is_f32(w_ref))
    o_ref[...] = _ln(ALPHA * res_ref[...] + acc, g_ref[...], b_ref[...])


def _mm_ln(x, w, res, g, b, *, tm, name):
    m, k = x.shape
    n = w.shape[1]
    tm = min(tm, m)
    assert m % tm == 0
    return pl.pallas_call(
        _mm_ln_body,
        name=f"{name}_r{m}",
        grid=(m // tm,),
        in_specs=[pl.BlockSpec((tm, k), lambda i: (i, 0)), pl.BlockSpec((k, n), lambda i: (0, 0)),
                  pl.BlockSpec((tm, n), lambda i: (i, 0)), pl.BlockSpec((1, n), lambda i: (0, 0)),
                  pl.BlockSpec((1, n), lambda i: (0, 0))],
        out_specs=pl.BlockSpec((tm, n), lambda i: (i, 0)),
        out_shape=jax.ShapeDtypeStruct((m, n), f32),
        compiler_params=_cparams(("parallel",)),
    )(x, w, res, g.reshape(1, n), b.reshape(1, n))


def _router_body(h_ref, rwt_ref, rb_ref, gate_ref):
    logits = _mxu_nt(rwt_ref[...], h_ref[...], _is_f32(rwt_ref))
    score = jax.nn.sigmoid(logits)
    sel = score + rb_ref[...]
    gsum = []
    for g in range(N_EGROUPS):
        a, b, c, d = (sel[EPG * g + i:EPG * g + i + 1] for i in range(EPG))
        hi1, lo1, hi2, lo2 = jnp.maximum(a, b), jnp.minimum(a, b), jnp.maximum(c, d), jnp.minimum(c, d)
        gsum.append(jnp.maximum(hi1, hi2) + jnp.maximum(jnp.minimum(hi1, hi2), jnp.maximum(lo1, lo2)))
    best = jnp.zeros_like(gsum[0], dtype=i32)
    top = gsum[0]
    for g in range(1, N_EGROUPS):
        upd = gsum[g] > top
        best = jnp.where(upd, g, best)
        top = jnp.where(upd, gsum[g], top)
    selg = sel[0:EPG]
    scg = score[0:EPG]
    for g in range(1, N_EGROUPS):
        selg = jnp.where(best == g, sel[EPG * g:EPG * (g + 1)], selg)
        scg = jnp.where(best == g, score[EPG * g:EPG * (g + 1)], scg)
    rows = [selg[i:i + 1] for i in range(EPG)]
    chosen = []
    for i in range(EPG):
        rank = jnp.zeros_like(best)
        for j in range(EPG):
            if j == i:
                continue
            ahead = (rows[j] > rows[i]) | ((rows[j] == rows[i]) if j < i else False)
            rank = rank + ahead.astype(i32)
        chosen.append(rank < 2)
    wsum = sum(jnp.where(chosen[i], scg[i:i + 1], 0.0) for i in range(EPG))
    gates = [jnp.where(chosen[i], scg[i:i + 1] / wsum, 0.0) for i in range(EPG)]
    out_rows = []
    for g in range(N_EGROUPS):
        for i in range(EPG):
            out_rows.append(jnp.where(best == g, gates[i], 0.0))
    gate_ref[...] = jnp.concatenate(out_rows, axis=0).T


def _router(h, rwt, rb, *, tm):
    m, d = h.shape
    tm = min(tm, m)
    return pl.pallas_call(
        _router_body,
        name=f"router_r{m}",
        grid=(m // tm,),
        in_specs=[pl.BlockSpec((tm, d), lambda i: (i, 0)), pl.BlockSpec((N_EXPERTS, d), lambda i: (0, 0)),
                  pl.BlockSpec((N_EXPERTS, 1), lambda i: (0, 0))],
        out_specs=pl.BlockSpec((tm, N_EXPERTS), lambda i: (i, 0)),
        out_shape=jax.ShapeDtypeStruct((m, N_EXPERTS), f32),
        compiler_params=_cparams(("parallel",)),
    )(h, rwt, rb)


def _moe_body(h_ref, gate_ref, wg_ref, wu_ref, wd_ref, g2_ref, b2_ref, o_ref, acc_ref):
    e = pl.program_id(1)

    @pl.when(e == 0)
    def _():
        acc_ref[...] = jnp.zeros_like(acc_ref)

    precise = _is_f32(wg_ref)
    h = h_ref[...]
    hid = jax.nn.silu(_mxu(h, wg_ref[0, 0], precise)) * _mxu(h, wu_ref[0, 0], precise)
    out = _mxu(hid, wd_ref[0, 0], precise)
    gate = gate_ref[...]
    lane = lax.broadcasted_iota(i32, gate.shape, 1)
    gcol = jnp.sum(jnp.where(lane == e, gate, 0.0), axis=1, keepdims=True)
    acc_ref[...] += gcol * out

    @pl.when(e == N_EXPERTS - 1)
    def _():
        o_ref[...] = _ln(ALPHA * h_ref[...] + acc_ref[...], g2_ref[...], b2_ref[...])


def _moe_ln(h, gate, wg, wu, wd, g2, b2, *, tm, layer):
    m, d = h.shape
    tm = min(tm, m)
    return pl.pallas_call(
        _moe_body,
        name=f"moe_ln_r{m}",
        grid=(m // tm, N_EXPERTS),
        in_specs=[pl.BlockSpec((tm, d), lambda i, e: (i, 0)), pl.BlockSpec((tm, N_EXPERTS), lambda i, e: (i, 0)),
                  pl.BlockSpec((1, 1, d, D_FF), lambda i, e: (layer, e, 0, 0)),
                  pl.BlockSpec((1, 1, d, D_FF), lambda i, e: (layer, e, 0, 0)),
                  pl.BlockSpec((1, 1, D_FF, d), lambda i, e: (layer, e, 0, 0)),
                  pl.BlockSpec((1, d), lambda i, e: (0, 0)), pl.BlockSpec((1, d), lambda i, e: (0, 0))],
        out_specs=pl.BlockSpec((tm, d), lambda i, e: (i, 0)),
        out_shape=jax.ShapeDtypeStruct((m, d), f32),
        scratch_shapes=[pltpu.VMEM((tm, d), f32)],
        compiler_params=_cparams(("parallel", "arbitrary")),
    )(h, gate, wg, wu, wd, g2.reshape(1, d), b2.reshape(1, d))


def _ple_body(h_ref, p_ref, wg_ref, wp_ref, o_ref):
    h = h_ref[...]
    precise = _is_f32(wg_ref)
    gate = jax.nn.sigmoid(_mxu(h, wg_ref[...], precise))
    o_ref[...] = h + gate * _mxu(p_ref[...], wp_ref[...], precise)


def _ple(h, p, wg, wp, *, tm):
    m, d = h.shape
    tm = min(tm, m)
    return pl.pallas_call(
        _ple_body,
        name=f"ple_r{m}",
        grid=(m // tm,),
        in_specs=[pl.BlockSpec((tm, d), lambda i: (i, 0)), pl.BlockSpec((tm, PLE_DIM), lambda i: (i, 0)),
                  pl.BlockSpec((d, d), lambda i: (0, 0)), pl.BlockSpec((PLE_DIM, d), lambda i: (0, 0))],
        out_specs=pl.BlockSpec((tm, d), lambda i: (i, 0)),
        out_shape=jax.ShapeDtypeStruct((m, d), f32),
        compiler_params=_cparams(("parallel",)),
    )(h, p, wg, wp)


def _layer_tail(h1, p, tw, *, tm):
    gate = _router(h1, tw["rwt"], tw["rb"], tm=tm)
    h2 = _moe_ln(h1, gate, tw["wg"], tw["wu"], tw["wd"], tw["ln2_g"], tw["ln2_b"], tm=tm, layer=tw["layer"])
    return _ple(h2, p, tw["ple_gate"], tw["ple_proj"], tm=tm)


def _ssd_conv(x_ref, xp_ref, w_ref, b_ref):
    q = x_ref.shape[0]
    xp_ref[8:8 + q, :] = x_ref[...]
    w = w_ref[...]
    acc = b_ref[...] + xp_ref[5:5 + q, :] * w[0:1]
    for k in range(1, SSD_CONV):
        acc = acc + xp_ref[5 + k:5 + k + q, :] * w[k:k + 1]
    xp_ref[0:8, :] = xp_ref[q:q + 8, :]
    return jax.nn.silu(acc)


def _head_expand(width):
    r = lax.broadcasted_iota(i32, (SSD_HPG, SSD_HPG * width), 0)
    c = lax.broadcasted_iota(i32, (SSD_HPG, SSD_HPG * width), 1)
    return (c // width == r).astype(bf16)


def _split3(v):
    hi = v.astype(bf16)
    r1 = v - hi.astype(f32)
    mid = r1.astype(bf16)
    return hi, mid, (r1 - mid.astype(f32)).astype(bf16)


def _dot_exact01(v, onehot, left=False):
    parts = _split3(v)
    prods = [_dot(onehot, p) if left else _dot(p, onehot) for p in parts]
    return (prods[0] + prods[1]) + prods[2]


def _ssd_body(xs_ref, bm_ref, cm_ref, z_ref, dt_ref, dtt_ref, wx_ref, wb_ref, wc_ref, bx_ref, bb_ref, bc_ref,
              dtb_ref, dtbt_ref, alog_ref, alogt_ref, dsk_ref, ng_ref, y_ref, st_ref,
              stt_ref, xpx_ref, xpb_ref, xpc_ref):
    c = pl.program_id(2)
    q = SSD_CHUNK

    @pl.when(c == 0)
    def _():
        stt_ref[...] = jnp.zeros_like(stt_ref)
        xpx_ref[0:8, :] = jnp.zeros((8, xpx_ref.shape[1]), f32)
        xpb_ref[0:8, :] = jnp.zeros((8, xpb_ref.shape[1]), f32)
        xpc_ref[0:8, :] = jnp.zeros((8, xpc_ref.shape[1]), f32)

    xs = _ssd_conv(xs_ref, xpx_ref, wx_ref, bx_ref)
    bm = _ssd_conv(bm_ref, xpb_ref, wb_ref, bb_ref)
    cm = _ssd_conv(cm_ref, xpc_ref, wc_ref, bc_ref)
    dt = jax.nn.softplus(dt_ref[0] + dtb_ref[0])
    dtt = jax.nn.softplus(dtt_ref[0] + dtbt_ref[0])
    dta = dt * (-jnp.exp(alog_ref[0]))
    dtat = dtt * (-jnp.exp(alogt_ref[0]))
    row = lax.broadcasted_iota(i32, (q, q), 0)
    col = lax.broadcasted_iota(i32, (q, q), 1)
    causal = row >= col
    cum = _dot_exact01(dta, causal.astype(bf16), left=True)
    cumt = _dot_exact01(dtat, (row <= col).astype(bf16))
    both = _dot_exact01(jnp.concatenate([dt, cum], axis=0), _head_expand(SSD_HEADDIM))
    dtx, cumx = both[:q], both[q:]
    cum128 = _dot_exact01(cum, _head_expand(q))
    cum_last = cumx[q - 1:q, :]
    xdt = xs * dtx
    xdt_b = xdt.astype(bf16)
    bm_b = bm.astype(bf16)
    cm_b = cm.astype(bf16)
    cb = _dot_nt(cm_b, bm_b)
    head = lax.broadcasted_iota(i32, (q, SSD_GW), 1) // SSD_HEADDIM
    y = jnp.zeros((q, SSD_GW), f32)
    for r in range(SSD_HPG):
        seg = cum128[:, r * q:(r + 1) * q] - cumt[r:r + 1, :]
        decay = jnp.where(causal, jnp.exp(jnp.where(causal, seg, 0.0)), 0.0)
        yr = _dot((cb * decay).astype(bf16), xdt_b)
        y = jnp.where(head == r, yr, y)
    stt = stt_ref[...]
    y = y + _dot(cm_b, stt.astype(bf16)) * jnp.exp(cumx) + dsk_ref[0] * xs
    to_end = jnp.exp(cum_last - cumx)
    stt_new = stt * jnp.exp(cum_last) + _dot_tn(bm_b, (xdt * to_end).astype(bf16))
    stt_ref[...] = stt_new
    yz = y * jax.nn.silu(z_ref[...])
    yn = yz * lax.rsqrt(jnp.mean(yz * yz, axis=-1, keepdims=True) + 1e-5) * ng_ref[...]
    y_ref[...] = yn.astype(y_ref.dtype)

    @pl.when(c == pl.num_programs(2) - 1)
    def _():
        st_ref[0] = stt_new.T.reshape(SSD_HPG, SSD_HEADDIM, SSD_STATE)


def _ssd_prompt_scan(xbc, z, dt_raw, sw, bsz, t):
    m = bsz * t
    nc = t // SSD_CHUNK
    q, gw, n, g_, hpg = SSD_CHUNK, SSD_GW, SSD_STATE, SSD_GROUPS, SSD_HPG
    dt_g = dt_raw.reshape(m, g_, hpg).transpose(1, 0, 2)
    dtt_g = dt_g.transpose(0, 2, 1)
    nxb = SSD_D_INNER // n
    row = lambda b, g, c: b * nc + c
    in_specs = [
        pl.BlockSpec((q, gw), lambda b, g, c: (row(b, g, c), g)),
        pl.BlockSpec((q, n), lambda b, g, c: (row(b, g, c), nxb + g)),
        pl.BlockSpec((q, n), lambda b, g, c: (row(b, g, c), nxb + g_ + g)),
        pl.BlockSpec((q, gw), lambda b, g, c: (row(b, g, c), g)),
        pl.BlockSpec((1, q, hpg), lambda b, g, c: (g, row(b, g, c), 0)),
        pl.BlockSpec((1, hpg, q), lambda b, g, c: (g, 0, row(b, g, c))),
        pl.BlockSpec((SSD_CONV, gw), lambda b, g, c: (0, g)),
        pl.BlockSpec((SSD_CONV, n), lambda b, g, c: (0, nxb + g)),
        pl.BlockSpec((SSD_CONV, n), lambda b, g, c: (0, nxb + g_ + g)),
        pl.BlockSpec((1, gw), lambda b, g, c: (0, g)),
        pl.BlockSpec((1, n), lambda b, g, c: (0, nxb + g)),
        pl.BlockSpec((1, n), lambda b, g, c: (0, nxb + g_ + g)),
        pl.BlockSpec((1, 1, hpg), lambda b, g, c: (g, 0, 0)),
        pl.BlockSpec((1, hpg, 1), lambda b, g, c: (g, 0, 0)),
        pl.BlockSpec((1, 1, hpg), lambda b, g, c: (g, 0, 0)),
        pl.BlockSpec((1, hpg, 1), lambda b, g, c: (g, 0, 0)),
        pl.BlockSpec((1, 1, gw), lambda b, g, c: (g, 0, 0)),
        pl.BlockSpec((1, gw), lambda b, g, c: (0, g)),
    ]
    y, st = pl.pallas_call(
        _ssd_body,
        name="ssd_scan",
        grid=(bsz, g_, nc),
        in_specs=in_specs,
        out_specs=[pl.BlockSpec((q, gw), lambda b, g, c: (row(b, g, c), g)),
                   pl.BlockSpec((1, hpg, SSD_HEADDIM, n), lambda b, g, c: (b, g, 0, 0))],
        out_shape=[jax.ShapeDtypeStruct((m, SSD_D_INNER), bf16),
                   jax.ShapeDtypeStruct((bsz, SSD_HEADS, SSD_HEADDIM, n), f32)],
        scratch_shapes=[pltpu.VMEM((n, gw), f32), pltpu.VMEM((q + 8, gw), f32),
                        pltpu.VMEM((q + 8, n), f32), pltpu.VMEM((q + 8, n), f32)],
        compiler_params=_cparams(("arbitrary", "arbitrary", "arbitrary")),
    )(xbc, xbc, xbc, z, dt_g, dtt_g, sw["conv_w"], sw["conv_w"], sw["conv_w"], sw["conv_b"], sw["conv_b"], sw["conv_b"],
      sw["dtb"], sw["dtbt"], sw["alog"], sw["alogt"], sw["dskx"], sw["norm_g"])
    return y, st


def _ssd_weights(w_in, conv_w, conv_b, dt_bias, a_log, d_skip, norm_g, w_out, wdt):
    g_, hpg = SSD_GROUPS, SSD_HPG
    return dict(
        w_z=w_in[:, :SSD_D_INNER].astype(wdt),
        w_xbc=w_in[:, SSD_D_INNER:SSD_D_INNER + SSD_CONV_DIM].astype(wdt),
        w_dt=w_in[:, SSD_D_INNER + SSD_CONV_DIM:].astype(wdt),
        conv_w=conv_w, conv_b=conv_b.reshape(1, SSD_CONV_DIM),
        dtb=dt_bias.reshape(g_, 1, hpg), dtbt=dt_bias.reshape(g_, hpg, 1),
        alog=a_log.reshape(g_, 1, hpg), alogt=a_log.reshape(g_, hpg, 1),
        dskx=jnp.repeat(d_skip, SSD_HEADDIM).reshape(g_, 1, SSD_GW),
        dsk=d_skip, dt_bias=dt_bias, a_log=a_log,
        norm_g=norm_g.reshape(1, SSD_D_INNER), w_out=w_out.astype(wdt))


def _ssd_prompt(x2d, sw, bsz, t, ln_g, ln_b):
    (z,) = _mm(x2d, sw["w_z"], tm=512, tn=1024, out_dtypes=(f32,), name="ssd_in_z")
    (xbc,) = _mm(x2d, sw["w_xbc"], tm=512, tn=1024, out_dtypes=(f32,), name="ssd_in_xbc")
    (dt_raw,) = _mm(x2d, sw["w_dt"], tm=512, tn=SSD_HEADS, out_dtypes=(f32,), name="ssd_in_dt")
    y, st = _ssd_prompt_scan(xbc, z, dt_raw, sw, bsz, t)
    h1 = _mm_ln(y, sw["w_out"], x2d, ln_g, ln_b, tm=512, name="ssd_out_ln")
    new_conv = xbc.reshape(bsz, t, SSD_CONV_DIM)[:, t - (SSD_CONV - 1):]
    return h1, new_conv, st


def _pad_rows(x, rows=8):
    return jnp.concatenate([x, jnp.zeros((rows - x.shape[0], x.shape[1]), x.dtype)], axis=0)


def _ssd_step_body(z_ref, xbc_ref, dt_ref, cs_ref, st_ref, cw_ref, cb_ref, dtb_ref, alog_ref, dsk_ref, ng_ref,
                   y_ref, nc_ref, ns_ref):
    n, gw, hd = SSD_STATE, SSD_GW, SSD_HEADDIM
    xbc = xbc_ref[0]
    cs = cs_ref[0]
    w = cw_ref[...]
    conv = cb_ref[...] + cs[0:1] * w[0:1]
    for k in range(1, SSD_CONV - 1):
        conv = conv + cs[k:k + 1] * w[k:k + 1]
    conv = jax.nn.silu(conv + xbc * w[SSD_CONV - 1:SSD_CONV])
    nc_ref[0] = jnp.concatenate([cs[1:], xbc], axis=0)
    xs = conv[:, :SSD_D_INNER]
    b_g = _pad_rows(jnp.concatenate(
        [conv[:, SSD_D_INNER + g * n:SSD_D_INNER + (g + 1) * n] for g in range(SSD_GROUPS)], axis=0))
    c_g = _pad_rows(jnp.concatenate(
        [conv[:, SSD_D_INNER + (SSD_GROUPS + g) * n:SSD_D_INNER + (SSD_GROUPS + g + 1) * n] for g in range(SSD_GROUPS)],
        axis=0))
    dt = jax.nn.softplus(dt_ref[0] + dtb_ref[...])
    decay = jnp.exp(dt * (-jnp.exp(alog_ref[...])))
    er = lax.broadcasted_iota(i32, (SSD_HEADS, SSD_D_INNER), 0)
    ec = lax.broadcasted_iota(i32, (SSD_HEADS, SSD_D_INNER), 1)
    per_head = _pad_rows(jnp.concatenate([dt, decay, dsk_ref[...]], axis=0))
    hx = jnp.dot(per_head, (ec // hd == er).astype(f32), precision=HIGHEST, preferred_element_type=f32)
    dtx, decx, dskx = hx[0:1], hx[1:2], hx[2:3]
    xdt = xs * dtx
    gr = lax.broadcasted_iota(i32, (8, SSD_D_INNER), 0)
    gc = lax.broadcasted_iota(i32, (8, SSD_D_INNER), 1)
    gmask = (gc // gw == gr).astype(f32)
    row0 = (gr == 0).astype(f32)
    st = st_ref[0].reshape(SSD_D_INNER, n)
    upd = _dot_tn(gmask * xdt, b_g, precision=HIGHEST)
    dec_full = _dot_tn(row0 * decx, jnp.ones((8, n), f32), precision=HIGHEST)
    ns_ref[0] = (st * dec_full + upd).reshape(SSD_HEADS, hd, n)
    cst = _dot_nt(c_g, st, precision=HIGHEST)
    y_state = jnp.sum(gmask * cst, axis=0, keepdims=True)
    cbx = jnp.sum(gmask * jnp.sum(c_g * b_g, axis=1, keepdims=True), axis=0, keepdims=True)
    y = cbx * xdt + decx * y_state + dskx * xs
    yz = y * jax.nn.silu(z_ref[0])
    parts = []
    for g in range(SSD_GROUPS):
        seg = yz[:, g * gw:(g + 1) * gw]
        parts.append(seg * lax.rsqrt(jnp.mean(seg * seg, axis=-1, keepdims=True) + 1e-5))
    y_ref[0] = jnp.concatenate(parts, axis=1) * ng_ref[...]


def _ssd_step(x2d, conv_state, ssm_state, sw, ln_g, ln_b):
    bsz = x2d.shape[0]
    (z,) = _mm(x2d, sw["w_z"], tm=bsz, tn=1024, out_dtypes=(f32,), name="ssd_in_z")
    (xbc,) = _mm(x2d, sw["w_xbc"], tm=bsz, tn=1024, out_dtypes=(f32,), name="ssd_in_xbc")
    (dt_raw,) = _mm(x2d, sw["w_dt"], tm=bsz, tn=SSD_HEADS, out_dtypes=(f32,), name="ssd_in_dt")
    rowspec = lambda width: pl.BlockSpec((1, 1, width), lambda b: (b, 0, 0))
    full = lambda r, c: pl.BlockSpec((r, c), lambda b: (0, 0))
    y, new_conv, new_state = pl.pallas_call(
        _ssd_step_body,
        name="ssd_step",
        grid=(bsz,),
        in_specs=[rowspec(SSD_D_INNER), rowspec(SSD_CONV_DIM), rowspec(SSD_HEADS),
                  pl.BlockSpec((1, SSD_CONV - 1, SSD_CONV_DIM), lambda b: (b, 0, 0)),
                  pl.BlockSpec((1, SSD_HEADS, SSD_HEADDIM, SSD_STATE), lambda b: (b, 0, 0, 0)),
                  full(SSD_CONV, SSD_CONV_DIM), full(1, SSD_CONV_DIM), full(1, SSD_HEADS), full(1, SSD_HEADS),
                  full(1, SSD_HEADS), full(1, SSD_D_INNER)],
        out_specs=[rowspec(SSD_D_INNER),
                   pl.BlockSpec((1, SSD_CONV - 1, SSD_CONV_DIM), lambda b: (b, 0, 0)),
                   pl.BlockSpec((1, SSD_HEADS, SSD_HEADDIM, SSD_STATE), lambda b: (b, 0, 0, 0))],
        out_shape=[jax.ShapeDtypeStruct((bsz, 1, SSD_D_INNER), f32),
                   jax.ShapeDtypeStruct((bsz, SSD_CONV - 1, SSD_CONV_DIM), f32),
                   jax.ShapeDtypeStruct((bsz, SSD_HEADS, SSD_HEADDIM, SSD_STATE), f32)],
        compiler_params=_cparams(("parallel",)),
    )(z.reshape(bsz, 1, -1), xbc.reshape(bsz, 1, -1), dt_raw.reshape(bsz, 1, -1), conv_state, ssm_state,
      sw["conv_w"], sw["conv_b"], sw["dt_bias"].reshape(1, -1), sw["a_log"].reshape(1, -1), sw["dsk"].reshape(1, -1),
      sw["norm_g"])
    h1 = _mm_ln(y.reshape(bsz, SSD_D_INNER), sw["w_out"], x2d, ln_g, ln_b, tm=bsz, name="ssd_out_ln")
    return h1, new_conv, new_state


def _slope(head):
    return 2.0 ** (-8.0 * (head + 1) / NSA_HEADS)


def _masked_softmax(s, mask):
    s = jnp.where(mask, s, NEG)
    m = jnp.max(s, axis=-1, keepdims=True)
    e = jnp.exp(s - m) * mask.astype(f32)
    return e / jnp.maximum(jnp.sum(e, axis=-1, keepdims=True), 1e-30)


def _tile_rows(x, n):
    return jnp.concatenate([x] * n, axis=0)


def _slope_col(kv_head, rows):
    return jnp.concatenate([jnp.full((rows, 1), _slope(kv_head * NSA_REP + r), f32) for r in range(NSA_REP)], axis=0)


def _topk_rank(imp):
    nb = imp.shape[1]
    lane = lax.broadcasted_iota(i32, imp.shape, 1)
    rank = jnp.zeros(imp.shape, i32)
    for j in range(nb):
        cj = imp[:, j:j + 1]
        rank = rank + ((cj > imp) | ((cj == imp) & (lane > j))).astype(i32)
    return rank


def _topk_mask_rows(imp, n_sel):
    it = imp.T
    nb = it.shape[0]
    sub = lax.broadcasted_iota(i32, it.shape, 0)
    rank = jnp.zeros(it.shape, i32)
    for j in range(nb):
        rj = it[j:j + 1, :]
        rank = rank + ((rj > it) | ((rj == it) & (sub > j))).astype(i32)
    return (rank < n_sel).astype(f32).T


def _pos_weights(pw_ref):
    w = jax.nn.softmax(pw_ref[...], axis=0)
    r = lax.broadcasted_iota(i32, (2 * NSA_KV, 2 * NSA_KVD), 0)
    c = lax.broadcasted_iota(i32, (2 * NSA_KV, 2 * NSA_KVD), 1)
    return jnp.dot(w, (c // NSA_DH == r).astype(f32), precision=HIGHEST, preferred_element_type=f32)


def _compress_body(x_ref, pw_ref, o_ref):
    wx = _pos_weights(pw_ref)
    x = x_ref[...]
    nb = x.shape[0] // NSA_BLOCK
    o_ref[0] = jnp.sum(x.reshape(nb, NSA_BLOCK, 2 * NSA_KVD) * wx[None], axis=1)


def _nsa_compress(kv4, pw, bsz, t):
    nb = t // NSA_BLOCK
    return pl.pallas_call(
        _compress_body,
        name="nsa_compress",
        grid=(bsz,),
        in_specs=[pl.BlockSpec((t, 2 * NSA_KVD), lambda b: (b, 0)),
                  pl.BlockSpec((NSA_BLOCK, 2 * NSA_KV), lambda b: (0, 0))],
        out_specs=pl.BlockSpec((1, nb, 2 * NSA_KVD), lambda b: (b, 0, 0)),
        out_shape=jax.ShapeDtypeStruct((bsz, nb, 2 * NSA_KVD), f32),
        compiler_params=_cparams(("parallel",)),
    )(kv4, pw)


NSA_TQ = 128
NSA_CK = 512


def _nsa_prompt_body(q_ref, g_ref, kcvc_ref, kvs_ref, kvw_ref, o_ref, *, t_len):
    i = pl.program_id(1)
    tq, ck, rep = NSA_TQ, NSA_CK, NSA_REP
    ck = min(ck, t_len)
    nb = t_len // NSA_BLOCK
    n_sel = min(NSA_TOP, nb)
    wlen = min(NSA_WINDOW + tq, t_len)
    t0 = i * tq
    t_col = t0 + lax.broadcasted_iota(i32, (tq, 1), 0)
    lane128 = lax.broadcasted_iota(i32, (1, LANES), 1)
    gates = g_ref[...]
    kcvc = kcvc_ref[0]
    blk = lax.broadcasted_iota(i32, (1, nb), 1)
    dist_c = t_col - ((blk + 1) * NSA_BLOCK - 1)
    dist_c4 = _tile_rows(dist_c, rep)
    cur = t_col // NSA_BLOCK
    n_chunks = (t0 + tq + ck - 1) // ck
    w_start = jnp.clip(t0 - NSA_WINDOW, 0, t_len - wlen)
    w_start = pl.multiple_of(w_start, tq)
    dist_w = t_col - (w_start + lax.broadcasted_iota(i32, (1, wlen), 1))
    dist_w4 = _tile_rows(dist_w, rep)
    mask_w4 = (dist_w4 >= 0) & (dist_w4 < NSA_WINDOW)

    for j in range(NSA_KV // 2):
        cols = slice(j * LANES, (j + 1) * LANES)
        vcols = slice(NSA_KVD + j * LANES, NSA_KVD + (j + 1) * LANES)
        q_tiles = [q_ref[:, (j * rep + r) * LANES:(j * rep + r + 1) * LANES] for r in range(rep)]
        q_rows = jnp.concatenate(q_tiles, axis=0)
        kc_b = kcvc[:, cols].astype(bf16)
        vc_b = kcvc[:, vcols].astype(bf16)
        out_tiles = [jnp.zeros((tq, LANES), f32) for _ in range(rep)]
        for h in range(2):
            k = 2 * j + h
            half = (lane128 // NSA_DH) == h
            q_pad = jnp.where(half, q_rows, jnp.zeros_like(q_rows))
            slope = _slope_col(k, tq)
            s_c = _dot_nt(q_pad, kc_b) - slope * dist_c4.astype(f32)
            p_c = _masked_softmax(s_c, dist_c4 >= 0)
            o_c = _dot(p_c.astype(bf16), vc_b)
            imp = sum(p_c[r * tq:(r + 1) * tq] for r in range(rep))
            imp = jnp.where((blk == cur) | (blk == 0), 1e4, imp)
            imp = jnp.where(blk > cur, -1.0, imp)
            sel_b = _topk_mask_rows(imp, n_sel).astype(bf16)

            def sel_step(c, carry, q_pad=q_pad, slope=slope, sel_b=sel_b, cols=cols, vcols=vcols):
                m, l, acc = carry
                k0 = pl.multiple_of(c * ck, ck)
                ks = kvs_ref[pl.ds(k0, ck), cols]
                vs = kvs_ref[pl.ds(k0, ck), vcols]
                kpos = k0 + lax.broadcasted_iota(i32, (1, ck), 1)
                brow = lax.broadcasted_iota(i32, (nb, ck), 0)
                expand = ((k0 + lax.broadcasted_iota(i32, (nb, ck), 1)) // NSA_BLOCK == brow).astype(bf16)
                selx = _dot(sel_b, expand)
                dist = t_col - kpos
                mask = _tile_rows((selx > 0.5) & (dist >= 0), rep)
                s = _dot_nt(q_pad, ks) - slope * _tile_rows(dist, rep).astype(f32)
                s = jnp.where(mask, s, NEG)
                m_new = jnp.maximum(m, jnp.max(s, axis=-1, keepdims=True))
                a = jnp.exp(m - m_new)
                p = jnp.exp(s - m_new) * mask.astype(f32)
                l = a * l + jnp.sum(p, axis=-1, keepdims=True)
                acc = a * acc + _dot(p.astype(bf16), vs)
                return m_new, l, acc

            init = (jnp.full((rep * tq, 1), NEG, f32), jnp.zeros((rep * tq, 1), f32), jnp.zeros((rep * tq, LANES), f32))
            _, l_s, acc_s = lax.fori_loop(0, n_chunks, sel_step, init)
            o_s = acc_s / jnp.maximum(l_s, 1e-30)
            kw = kvw_ref[pl.ds(w_start, wlen), cols]
            vw = kvw_ref[pl.ds(w_start, wlen), vcols]
            s_w = _dot_nt(q_pad, kw) - slope * dist_w4.astype(f32)
            p_w = _masked_softmax(s_w, mask_w4)
            o_w = _dot(p_w.astype(bf16), vw)
            for r in range(rep):
                gc = (k * rep + r) * 3
                rows = slice(r * tq, (r + 1) * tq)
                o_r = gates[:, gc:gc + 1] * o_c[rows] + gates[:, gc + 1:gc + 2] * o_s[rows] + gates[:, gc + 2:gc + 3] * o_w[rows]
                out_tiles[r] = jnp.where(half, o_r, out_tiles[r])
        for r in range(rep):
            o_ref[:, (j * rep + r) * LANES:(j * rep + r + 1) * LANES] = out_tiles[r].astype(o_ref.dtype)


def _nsa_prompt_attn(q, gates, kcvc, kv4_b, kvw_b, bsz, t):
    nq = t // NSA_TQ
    nb = t // NSA_BLOCK
    return pl.pallas_call(
        functools.partial(_nsa_prompt_body, t_len=t),
        name="nsa_attn",
        grid=(bsz, nq),
        in_specs=[pl.BlockSpec((NSA_TQ, NSA_QD), lambda b, i: (b * nq + i, 0)),
                  pl.BlockSpec((NSA_TQ, LANES), lambda b, i: (b * nq + i, 0)),
                  pl.BlockSpec((1, nb, 2 * NSA_KVD), lambda b, i: (b, 0, 0)),
                  pl.BlockSpec((t, 2 * NSA_KVD), lambda b, i: (b, 1)),
                  pl.BlockSpec((t, 2 * NSA_KVD), lambda b, i: (b, 0))],
        out_specs=pl.BlockSpec((NSA_TQ, NSA_QD), lambda b, i: (b * nq + i, 0)),
        out_shape=jax.ShapeDtypeStruct((bsz * t, NSA_QD), bf16),
        compiler_params=_cparams(("parallel", "arbitrary")),
    )(q, gates, kcvc, kv4_b, kvw_b)


def _pair_layout_cols(w):
    lead = w.shape[:-1]
    w = w.reshape(*lead, NSA_KV // 2, 2, NSA_REP, NSA_DH)
    return jnp.swapaxes(w, -3, -2).reshape(*lead, NSA_QD)


def _nsa_weights(w_in, pos_w, w_out, wdt, pair):
    lay = _pair_layout_cols if pair else (lambda w: w)
    kv0 = NSA_QD
    g0 = NSA_QD + 6 * NSA_KVD
    wg = jnp.zeros((D_MODEL, LANES), f32).at[:, :3 * NSA_HEADS].set(w_in[:, g0:])
    return dict(
        w_q=lay(w_in[:, :NSA_QD]).astype(wdt),
        w_kv4=w_in[:, kv0:kv0 + 4 * NSA_KVD].astype(wdt),
        w_kvw=w_in[:, kv0 + 4 * NSA_KVD:g0].astype(wdt),
        w_g=wg.astype(wdt),
        pw=pos_w.transpose(1, 0, 2).reshape(NSA_BLOCK, 2 * NSA_KV),
        pwt=pos_w.transpose(0, 2, 1).reshape(2 * NSA_KV, NSA_BLOCK),
        w_kvt=w_in[:, kv0:g0].T.astype(wdt),
        w_out=lay(w_out.T).T.astype(wdt))


def _nsa_prompt(x2d, nw, bsz, t, ln_g, ln_b):
    (q,) = _mm(x2d, nw["w_q"], tm=512, tn=1024, out_dtypes=(bf16,), scale=NSA_DH ** -0.5, name="nsa_in_q")
    kv4, kv4_b = _mm(x2d, nw["w_kv4"], tm=512, tn=1024, out_dtypes=(f32, bf16), name="nsa_in_kv4")
    kvw, kvw_b = _mm(x2d, nw["w_kvw"], tm=512, tn=512, out_dtypes=(f32, bf16), name="nsa_in_kvw")
    (gates,) = _mm(x2d, nw["w_g"], tm=512, tn=LANES, out_dtypes=(f32,), act="sigmoid", name="nsa_in_gates")
    kcvc = _nsa_compress(kv4, nw["pw"], bsz, t)
    o = _nsa_prompt_attn(q, gates, kcvc, kv4_b, kvw_b, bsz, t)
    h1 = _mm_ln(o, nw["w_out"], x2d, ln_g, ln_b, tm=512, name="nsa_out_ln")
    w_keep = min(NSA_WINDOW, t)
    new_kv = kv4.reshape(bsz, t, 4, NSA_KV, NSA_DH)
    new_win = kvw.reshape(bsz, t, 2, NSA_KV, NSA_DH)[:, t - w_keep:]
    return h1, new_kv, new_win


PAGES_PER_STEP = 16


def _lane_weights(pwt_ref):
    w = jax.nn.softmax(pwt_ref[...], axis=1)
    return jnp.concatenate([w] * (PAGE_SIZE // NSA_BLOCK), axis=1)


def _page_compress_body(pt_ref, *refs):
    del pt_ref
    page_refs, pwt_ref, o_ref = refs[:PAGES_PER_STEP], refs[PAGES_PER_STEP], refs[PAGES_PER_STEP + 1]
    s = pl.program_id(1)
    bpp = PAGE_SIZE // NSA_BLOCK
    nbp = o_ref.shape[-1]

    @pl.when(s == 0)
    def _():
        o_ref[...] = jnp.zeros_like(o_ref)

    w = _lane_weights(pwt_ref)
    tok_blk = lax.broadcasted_iota(i32, (1, PAGE_SIZE), 1) // NSA_BLOCK
    out_lane = lax.broadcasted_iota(i32, (1, nbp), 1)
    for c in range(2):
        for k in range(NSA_KV):
            wk = w[c * NSA_KV + k:c * NSA_KV + k + 1]
            upd = jnp.zeros((NSA_DH, nbp), f32)
            for i, p_ref in enumerate(page_refs):
                xw = p_ref[0, c, k] * wk
                first = (s * PAGES_PER_STEP + i) * bpp
                for h in range(bpp):
                    r = jnp.sum(jnp.where(tok_blk == h, xw, 0.0), axis=1, keepdims=True)
                    upd = upd + jnp.where(out_lane == first + h, r, 0.0)
            o_ref[0, c, k] += upd


def _page_compress(pool_t, page_table, pwt):
    bsz, n_pages = page_table.shape
    nbp = n_pages * (PAGE_SIZE // NSA_BLOCK)
    steps = n_pages // PAGES_PER_STEP
    assert n_pages % PAGES_PER_STEP == 0
    page_specs = [pl.BlockSpec((1, 2, NSA_KV, NSA_DH, PAGE_SIZE),
                               lambda b, s, pt, i=i: (pt[b, s * PAGES_PER_STEP + i], 0, 0, 0, 0))
                  for i in range(PAGES_PER_STEP)]
    return pl.pallas_call(
        _page_compress_body,
        name="nsa_page_compress",
        grid_spec=pltpu.PrefetchScalarGridSpec(
            num_scalar_prefetch=1, grid=(bsz, steps),
            in_specs=page_specs + [pl.BlockSpec((2 * NSA_KV, NSA_BLOCK), lambda b, s, pt: (0, 0))],
            out_specs=pl.BlockSpec((1, 2, NSA_KV, NSA_DH, nbp), lambda b, s, pt: (b, 0, 0, 0, 0))),
        out_shape=jax.ShapeDtypeStruct((bsz, 2, NSA_KV, NSA_DH, nbp), f32),
        compiler_params=_cparams(("parallel", "arbitrary")),
    )(page_table, *([pool_t] * PAGES_PER_STEP), pwt)


def _mm_nt_body(wt_ref, x_ref, o_ref):
    o_ref[...] = _mxu_nt(wt_ref[...], x_ref[...], _is_f32(wt_ref))


def _mm_nt(wt, x, *, tn, name):
    n, k = wt.shape
    m = x.shape[0]
    return pl.pallas_call(
        _mm_nt_body, name=f"{name}_r{m}", grid=(n // tn,),
        in_specs=[pl.BlockSpec((tn, k), lambda j: (j, 0)), pl.BlockSpec((m, k), lambda j: (0, 0))],
        out_specs=pl.BlockSpec((tn, m), lambda j: (j, 0)),
        out_shape=jax.ShapeDtypeStruct((n, m), f32),
        compiler_params=_cparams(("parallel",)),
    )(wt, x)


def _step_slopes(kv_head):
    r = lax.broadcasted_iota(i32, (8, 1), 0)
    return jnp.exp2(-0.5 * (kv_head * NSA_REP + r + 1).astype(f32))


def _gate_col(gates, kv_head, branch):
    cols = [(kv_head * NSA_REP + r) * 3 + branch for r in range(NSA_REP)]
    return jnp.concatenate([gates[:, c:c + 1] for c in cols], axis=0)


def _own_col(x, b):
    lane = lax.broadcasted_iota(i32, x.shape, x.ndim - 1)
    return jnp.sum(jnp.where(lane == b, x, 0.0), axis=-1, keepdims=True)


def _new_key_tile(col):
    lane = lax.broadcasted_iota(i32, (col.shape[0], LANES), 1)
    return jnp.where(lane == 0, col, 0.0)


def _nsa_step_cw_body(q_ref, g_ref, cmp_ref, kvt_ref, pwt_ref, win_ref, o_ref, idx_ref, *, past_len):
    b = pl.program_id(0)
    rep, kv = NSA_REP, NSA_KV
    nbp = past_len // NSA_BLOCK
    nbt = nbp + 8
    w_buf = win_ref.shape[-1]
    w0 = jax.nn.softmax(pwt_ref[...], axis=1)[:, 0:1]
    lane_c = lax.broadcasted_iota(i32, (1, nbp + LANES), 1)
    dist_c = past_len - ((lane_c + 1) * NSA_BLOCK - 1)
    lane_w = lax.broadcasted_iota(i32, (1, w_buf + LANES), 1)
    dist_w = jnp.where(lane_w <= w_buf, w_buf - lane_w, -1)
    mask_w = (dist_w >= 0) & (dist_w < NSA_WINDOW)
    blk = lax.broadcasted_iota(i32, (1, nbt), 1)
    cur = past_len // NSA_BLOCK
    gates = g_ref[0]
    for k in range(kv):
        q8 = _pad_rows(q_ref[0, k])
        slope = _step_slopes(k)
        new = [_own_col(kvt_ref[s, k], b) for s in (0, 1, 4, 5)]
        k_c = jnp.concatenate([cmp_ref[0, 0, k], _new_key_tile(new[0] * w0[k:k + 1])], axis=1)
        v_c = jnp.concatenate([cmp_ref[0, 1, k], _new_key_tile(new[1] * w0[kv + k:kv + k + 1])], axis=1)
        s_c = _mxu(q8, k_c, True) - slope * dist_c.astype(f32)
        p_c = _masked_softmax(s_c, jnp.broadcast_to(dist_c >= 0, s_c.shape))
        o_c = _mxu_nt(p_c, v_c, True)
        imp = jnp.sum(p_c[0:rep], axis=0, keepdims=True)[:, :nbt]
        imp = jnp.where((blk == cur) | (blk == 0), 1e4, imp)
        imp = jnp.where(blk > cur, -2.0, imp)
        rank = _topk_rank(imp)
        pick = lax.broadcasted_iota(i32, (NSA_TOP, nbt), 0) == rank
        idx_ref[0, k] = jnp.sum(jnp.where(pick, blk.astype(f32), 0.0), axis=1, keepdims=True).astype(i32)
        k_w = jnp.concatenate([win_ref[0, 0, k], _new_key_tile(new[2])], axis=1)
        v_w = jnp.concatenate([win_ref[0, 1, k], _new_key_tile(new[3])], axis=1)
        s_w = _mxu(q8, k_w, True) - slope * dist_w.astype(f32)
        p_w = _masked_softmax(s_w, jnp.broadcast_to(mask_w, s_w.shape))
        o_w = _mxu_nt(p_w, v_w, True)
        o_ref[0, k] = _gate_col(gates, k, 0) * o_c[0:rep] + _gate_col(gates, k, 2) * o_w[0:rep]


def _nsa_step_sel_body(idx_ref, pt_ref, *refs, past_len):
    del pt_ref
    n_in = 2 * NSA_TOP
    tile_refs = refs[:n_in]
    q_ref, g_ref, kvt_ref, part_ref, o_ref = refs[n_in:]
    b, k = pl.program_id(0), pl.program_id(1)
    nbp = past_len // NSA_BLOCK
    rep, lb, kv = NSA_REP, NSA_BLOCK, NSA_KV
    bpp = PAGE_SIZE // lb
    new_k = _new_key_tile(_own_col(kvt_ref[0, 0], b))
    new_v = _new_key_tile(_own_col(kvt_ref[1, 0], b))
    lane = lax.broadcasted_iota(i32, (1, NSA_TOP * PAGE_SIZE), 1)
    tok = lane % PAGE_SIZE
    spos = tok % lb
    valid = lane < 0
    ks, vs = [], []
    for i in range(NSA_TOP):
        n = idx_ref[(b * kv + k) * NSA_TOP + i]
        is_new = n >= nbp
        ks.append(jnp.where(is_new, new_k, tile_refs[2 * i][0, 0, 0]))
        vs.append(jnp.where(is_new, new_v, tile_refs[2 * i + 1][0, 0, 0]))
        mine = lane // PAGE_SIZE == i
        spos = spos + jnp.where(mine, n * lb, 0)
        half = jnp.where(is_new, 0, n % bpp)
        last = jnp.where(is_new, 0, lb - 1)
        valid = valid | (mine & (tok // lb == half) & (tok % lb <= last))
    k_all = jnp.concatenate(ks, axis=1)
    v_all = jnp.concatenate(vs, axis=1)
    dist = past_len - spos
    q8 = _pad_rows(q_ref[0, 0])
    s = _mxu(q8, k_all, True) - _step_slopes(k) * dist.astype(f32)
    p = _masked_softmax(s, jnp.broadcast_to(valid & (dist >= 0), s.shape))
    o_s = _mxu_nt(p, v_all, True)
    g = g_ref[0, 0]
    g_sel = jnp.concatenate([g[:, 3 * r + 1:3 * r + 2] for r in range(rep)], axis=0)
    o_ref[0, 0] = part_ref[0, 0] + g_sel * o_s[0:rep]


def _nsa_step(x2d, pool, page_table, win_cache, nw, ln_g, ln_b):
    bsz = x2d.shape[0]
    n_pages = page_table.shape[1]
    past_len = n_pages * PAGE_SIZE
    w_buf = win_cache.shape[1]
    rep, kv, dh = NSA_REP, NSA_KV, NSA_DH
    bpp = PAGE_SIZE // NSA_BLOCK
    nbp = past_len // NSA_BLOCK
    (q,) = _mm(x2d, nw["w_q"], tm=bsz, tn=1024, out_dtypes=(f32,), scale=NSA_DH ** -0.5, name="nsa_in_q")
    (kv4,) = _mm(x2d, nw["w_kv4"], tm=bsz, tn=1024, out_dtypes=(f32,), name="nsa_in_kv4")
    (kvw,) = _mm(x2d, nw["w_kvw"], tm=bsz, tn=512, out_dtypes=(f32,), name="nsa_in_kvw")
    (gates,) = _mm(x2d, nw["w_g"], tm=bsz, tn=LANES, out_dtypes=(f32,), act="sigmoid", name="nsa_in_gates")
    kvt = _mm_nt(nw["w_kvt"], x2d, tn=512, name="nsa_in_kvt").reshape(6, kv, dh, bsz)
    pool_t = jnp.transpose(pool, (0, 2, 3, 4, 1))
    win_t = jnp.transpose(win_cache, (0, 2, 3, 4, 1))
    cmp = _page_compress(pool_t, page_table, nw["pwt"])
    q4 = q.reshape(bsz, kv, rep, dh)
    slab = pl.BlockSpec((1, kv, rep, dh), lambda b: (b, 0, 0, 0))
    whole = lambda shape: pl.BlockSpec(shape, lambda b: (0,) * len(shape))
    part, idx = pl.pallas_call(
        functools.partial(_nsa_step_cw_body, past_len=past_len),
        name="nsa_step_cw",
        grid=(bsz,),
        in_specs=[slab, pl.BlockSpec((1, 1, LANES), lambda b: (b, 0, 0)),
                  pl.BlockSpec((1, 2, kv, dh, nbp), lambda b: (b, 0, 0, 0, 0)), whole((6, kv, dh, bsz)),
                  whole((2 * kv, NSA_BLOCK)), pl.BlockSpec((1, 2, kv, dh, w_buf), lambda b: (b, 0, 0, 0, 0))],
        out_specs=[slab, pl.BlockSpec((1, kv, NSA_TOP, 1), lambda b: (b, 0, 0, 0))],
        out_shape=[jax.ShapeDtypeStruct((bsz, kv, rep, dh), f32),
                   jax.ShapeDtypeStruct((bsz, kv, NSA_TOP, 1), i32)],
        compiler_params=_cparams(("parallel",)),
    )(q4, gates.reshape(bsz, 1, -1), cmp, kvt, nw["pwt"], win_t)

    def tile_spec(i, slot):
        def imap(b, k, idx_r, pt_r):
            n = jnp.minimum(idx_r[(b * kv + k) * NSA_TOP + i], nbp - 1)
            return (pt_r[b * n_pages + n // bpp], slot, k, 0, 0)
        return pl.BlockSpec((1, 1, 1, dh, PAGE_SIZE), imap)

    tile_specs = []
    for i in range(NSA_TOP):
        tile_specs += [tile_spec(i, 2), tile_spec(i, 3)]
    head = pl.BlockSpec((1, 1, rep, dh), lambda b, k, *_: (b, k, 0, 0))
    o = pl.pallas_call(
        functools.partial(_nsa_step_sel_body, past_len=past_len),
        name="nsa_step_sel",
        grid_spec=pltpu.PrefetchScalarGridSpec(
            num_scalar_prefetch=2, grid=(bsz, kv),
            in_specs=tile_specs + [
                head, pl.BlockSpec((1, 1, 1, 3 * rep), lambda b, k, *_: (b, k, 0, 0)),
                pl.BlockSpec((2, 1, dh, bsz), lambda b, k, *_: (1, k, 0, 0)), head],
            out_specs=head),
        out_shape=jax.ShapeDtypeStruct((bsz, kv, rep, dh), f32),
        compiler_params=_cparams(("arbitrary", "arbitrary")),
    )(idx.reshape(-1), page_table.reshape(-1), *([pool_t] * (2 * NSA_TOP)),
      q4, gates[:, :3 * NSA_HEADS].reshape(bsz, kv, 1, 3 * rep), kvt, part)
    h1 = _mm_ln(o.reshape(bsz, NSA_QD), nw["w_out"], x2d, ln_g, ln_b, tm=bsz, name="nsa_out_ln")
    new_kv = kv4.reshape(bsz, 1, 4, kv, dh)
    new_win = jnp.concatenate([win_cache[:, 1:], kvw.reshape(bsz, 1, 2, kv, dh)], axis=1)
    return h1, new_kv, new_win


def kernel(x_prompt, x_sample, state_ssm, state_conv, cache_kv, cache_win, page_table, p_prompt, p_sample,
           ssd_w_in, ssd_conv_w, ssd_conv_b, ssd_dt_bias, ssd_a_log, ssd_d, ssd_norm_g, ssd_w_out,
           nsa_w_in, nsa_pos_w, nsa_w_out, ln1_g, ln1_b, ln2_g, ln2_b, router_w, router_bias,
           moe_w_gate, moe_w_up, moe_w_down, ple_proj, ple_gate):
    bp, t, d = x_prompt.shape
    bs = x_sample.shape[0]
    xp = x_prompt.reshape(bp * t, d)
    xs = x_sample.reshape(bs, d)
    rb = router_bias.reshape(N_EXPERTS, 1)
    ssm_p, conv_p, kv_p, win_p, ssm_s, conv_s, kv_s, win_s = [], [], [], [], [], [], [], []
    for i in range(DEPTH):
        j = i // 2
        if i % 2 == 0:
            ssd = (ssd_w_in[j], ssd_conv_w[j], ssd_conv_b[j], ssd_dt_bias[j], ssd_a_log[j], ssd_d[j], ssd_norm_g[j],
                   ssd_w_out[j])
            h1p, c_new, s_new = _ssd_prompt(xp, _ssd_weights(*ssd, bf16), bp, t, ln1_g[i], ln1_b[i])
            conv_p.append(c_new)
            ssm_p.append(s_new)
            h1s, c_new, s_new = _ssd_step(xs, state_conv[j], state_ssm[j], _ssd_weights(*ssd, f32), ln1_g[i], ln1_b[i])
            conv_s.append(c_new)
            ssm_s.append(s_new)
        else:
            nsa = (nsa_w_in[j], nsa_pos_w[j], nsa_w_out[j])
            h1p, r_new, w_new = _nsa_prompt(xp, _nsa_weights(*nsa, bf16, True), bp, t, ln1_g[i], ln1_b[i])
            kv_p.append(r_new)
            win_p.append(w_new)
            h1s, r_new, w_new = _nsa_step(xs, cache_kv[j], page_table, cache_win[j], _nsa_weights(*nsa, f32, False),
                                          ln1_g[i], ln1_b[i])
            kv_s.append(r_new)
            win_s.append(w_new)

        def tail_weights(wdt, i=i):
            return dict(rwt=router_w.T.astype(wdt), rb=rb, layer=i, wg=moe_w_gate.astype(wdt), wu=moe_w_up.astype(wdt),
                        wd=moe_w_down.astype(wdt), ln2_g=ln2_g[i], ln2_b=ln2_b[i],
                        ple_gate=ple_gate[i].astype(wdt), ple_proj=ple_proj[i].astype(wdt))

        xp = _layer_tail(h1p, p_prompt[i].reshape(bp * t, PLE_DIM), tail_weights(bf16), tm=512)
        xs = _layer_tail(h1s, p_sample[i].reshape(bs, PLE_DIM), tail_weights(f32), tm=bs)
    return (xp.reshape(bp, t, d), xs.reshape(bs, 1, d), jnp.stack(ssm_p), jnp.stack(conv_p), jnp.stack(kv_p),
            jnp.stack(win_p), jnp.stack(ssm_s), jnp.stack(conv_s), jnp.stack(kv_s), jnp.stack(win_s))
```

```python
import functools

import jax
import jax.numpy as jnp
from jax import lax
from jax.experimental import pallas as pl
from jax.experimental.pallas import tpu as pltpu

f32, bf16, i32 = jnp.float32, jnp.bfloat16, jnp.int32
HIGHEST = lax.Precision.HIGHEST

D_MODEL = 1024
DEPTH = 2
PLE_DIM = 256
SSD_D_INNER = 2048
SSD_HEADDIM = 64
SSD_HEADS = 32
SSD_GROUPS = 4
SSD_HPG = 8
SSD_STATE = 128
SSD_CONV = 4
SSD_CHUNK = 128
SSD_GW = SSD_HPG * SSD_HEADDIM
SSD_CONV_DIM = SSD_D_INNER + 2 * SSD_GROUPS * SSD_STATE
NSA_HEADS = 16
NSA_KV = 4
NSA_REP = 4
NSA_DH = 64
NSA_BLOCK = 64
NSA_TOP = 16
NSA_WINDOW = 512
NSA_QD = NSA_HEADS * NSA_DH
NSA_KVD = NSA_KV * NSA_DH
PAGE_SIZE = 128
N_EXPERTS = 16
N_EGROUPS = 4
EPG = 4
D_FF = 512
ALPHA = (2.0 * DEPTH) ** 0.25
LN_EPS = 1e-5
NEG = -1e30
LANES = 128
VMEM_LIMIT = 56 * 1024 * 1024


def _cparams(sem):
    return pltpu.CompilerParams(dimension_semantics=sem, vmem_limit_bytes=VMEM_LIMIT)


def _ln(v, g, b):
    mu = jnp.mean(v, axis=-1, keepdims=True)
    d = v - mu
    var = jnp.mean(d * d, axis=-1, keepdims=True)
    return d * lax.rsqrt(var + LN_EPS) * g + b


def _dot(a, b):
    return jnp.dot(a, b, preferred_element_type=f32)


def _mxu(a, b, precise):
    if precise:
        return jnp.dot(a.astype(f32), b.astype(f32), precision=HIGHEST, preferred_element_type=f32)
    return jnp.dot(a.astype(bf16), b.astype(bf16), preferred_element_type=f32)


def _mxu_nt(a, b, precise):
    if precise:
        return _dot_nt(a.astype(f32), b.astype(f32), precision=HIGHEST)
    return _dot_nt(a.astype(bf16), b.astype(bf16))


def _dot_nt(a, b, precision=None):
    return lax.dot_general(a, b, (((1,), (1,)), ((), ())), precision=precision, preferred_element_type=f32)


def _dot_tn(a, b, precision=None):
    return lax.dot_general(a, b, (((0,), (0,)), ((), ())), precision=precision, preferred_element_type=f32)


def _is_f32(ref):
    return ref.dtype == jnp.float32


def _mm_body(x_ref, w_ref, *o_refs, act, scale):
    acc = _mxu(x_ref[...], w_ref[...], _is_f32(w_ref))
    if scale != 1.0:
        acc = acc * scale
    if act == "sigmoid":
        acc = jax.nn.sigmoid(acc)
    for o_ref in o_refs:
        o_ref[...] = acc.astype(o_ref.dtype)


def _mm(x, w, *, tm, tn, out_dtypes, name, act=None, scale=1.0):
    m, k = x.shape
    n = w.shape[1]
    tm, tn = min(tm, m), min(tn, n)
    assert m % tm == 0 and n % tn == 0
    outs = pl.pallas_call(
        functools.partial(_mm_body, act=act, scale=scale),
        name=f"{name}_r{m}",
        grid=(m // tm, n // tn),
        in_specs=[pl.BlockSpec((tm, k), lambda i, j: (i, 0)), pl.BlockSpec((k, tn), lambda i, j: (0, j))],
        out_specs=[pl.BlockSpec((tm, tn), lambda i, j: (i, j)) for _ in out_dtypes],
        out_shape=[jax.ShapeDtypeStruct((m, n), dt) for dt in out_dtypes],
        compiler_params=_cparams(("parallel", "arbitrary")),
    )(x, w)
    return outs


def _mm_ln_body(x_ref, w_ref, res_ref, g_ref, b_ref, o_ref):
    acc = _mxu(x_ref[...], w_ref[...], _is_f32(w_ref))
    o_ref[...] = _ln(ALPHA * res_ref[...] + acc, g_ref[...], b_ref[...])


def _mm_ln(x, w, res, g, b, *, tm, name):
    m, k = x.shape
    n = w.shape[1]
    tm = min(tm, m)
    assert m % tm == 0
    return pl.pallas_call(
        _mm_ln_body,
        name=f"{name}_r{m}",
        grid=(m // tm,),
        in_specs=[pl.BlockSpec((tm, k), lambda i: (i, 0)), pl.BlockSpec((k, n), lambda i: (0, 0)),
                  pl.BlockSpec((tm, n), lambda i: (i, 0)), pl.BlockSpec((1, n), lambda i: (0, 0)),
                  pl.BlockSpec((1, n), lambda i: (0, 0))],
        out_specs=pl.BlockSpec((tm, n), lambda i: (i, 0)),
        out_shape=jax.ShapeDtypeStruct((m, n), f32),
        compiler_params=_cparams(("parallel",)),
    )(x, w, res, g.reshape(1, n), b.reshape(1, n))


def _router_body(h_ref, rwt_ref, rb_ref, gate_ref, best_ref):
    logits = _mxu_nt(rwt_ref[...], h_ref[...], _is_f32(rwt_ref))
    score = jax.nn.sigmoid(logits)
    sel = score + rb_ref[...]
    gsum = []
    for g in range(N_EGROUPS):
        a, b, c, d = (sel[EPG * g + i:EPG * g + i + 1] for i in range(EPG))
        hi1, lo1, hi2, lo2 = jnp.maximum(a, b), jnp.minimum(a, b), jnp.maximum(c, d), jnp.minimum(c, d)
        gsum.append(jnp.maximum(hi1, hi2) + jnp.maximum(jnp.minimum(hi1, hi2), jnp.maximum(lo1, lo2)))
    best = jnp.zeros_like(gsum[0], dtype=i32)
    top = gsum[0]
    for g in range(1, N_EGROUPS):
        upd = gsum[g] > top
        best = jnp.where(upd, g, best)
        top = jnp.where(upd, gsum[g], top)
    selg = sel[0:EPG]
    scg = score[0:EPG]
    for g in range(1, N_EGROUPS):
        selg = jnp.where(best == g, sel[EPG * g:EPG * (g + 1)], selg)
        scg = jnp.where(best == g, score[EPG * g:EPG * (g + 1)], scg)
    rows = [selg[i:i + 1] for i in range(EPG)]
    chosen = []
    for i in range(EPG):
        rank = jnp.zeros_like(best)
        for j in range(EPG):
            if j == i:
                continue
            ahead = (rows[j] > rows[i]) | ((rows[j] == rows[i]) if j < i else False)
            rank = rank + ahead.astype(i32)
        chosen.append(rank < 2)
    wsum = sum(jnp.where(chosen[i], scg[i:i + 1], 0.0) for i in range(EPG))
    gates = [jnp.where(chosen[i], scg[i:i + 1] / wsum, 0.0) for i in range(EPG)]
    out_rows = []
    for g in range(N_EGROUPS):
        for i in range(EPG):
            out_rows.append(jnp.where(best == g, gates[i], 0.0))
    gate_ref[...] = jnp.concatenate(out_rows, axis=0).T
    best_ref[...] = best


def _router(h, rwt, rb, *, tm):
    m, d = h.shape
    tm = min(tm, m)
    return pl.pallas_call(
        _router_body,
        name=f"router_r{m}",
        grid=(m // tm,),
        in_specs=[pl.BlockSpec((tm, d), lambda i: (i, 0)), pl.BlockSpec((N_EXPERTS, d), lambda i: (0, 0)),
                  pl.BlockSpec((N_EXPERTS, 1), lambda i: (0, 0))],
        out_specs=[pl.BlockSpec((tm, N_EXPERTS), lambda i: (i, 0)), pl.BlockSpec((1, tm), lambda i: (0, i))],
        out_shape=[jax.ShapeDtypeStruct((m, N_EXPERTS), f32), jax.ShapeDtypeStruct((1, m), i32)],
        compiler_params=_cparams(("parallel",)),
    )(h, rwt, rb)


def _moe_body(h_ref, gate_ref, wg_ref, wu_ref, wd_ref, g2_ref, b2_ref, o_ref, acc_ref):
    e = pl.program_id(1)

    @pl.when(e == 0)
    def _():
        acc_ref[...] = jnp.zeros_like(acc_ref)

    precise = _is_f32(wg_ref)
    h = h_ref[...]
    hid = jax.nn.silu(_mxu(h, wg_ref[0, 0], precise)) * _mxu(h, wu_ref[0, 0], precise)
    out = _mxu(hid, wd_ref[0, 0], precise)
    gate = gate_ref[...]
    lane = lax.broadcasted_iota(i32, gate.shape, 1)
    gcol = jnp.sum(jnp.where(lane == e, gate, 0.0), axis=1, keepdims=True)
    acc_ref[...] += gcol * out

    @pl.when(e == N_EXPERTS - 1)
    def _():
        o_ref[...] = _ln(ALPHA * h_ref[...] + acc_ref[...], g2_ref[...], b2_ref[...])


def _moe_ln(h, gate, wg, wu, wd, g2, b2, *, tm, layer):
    m, d = h.shape
    tm = min(tm, m)
    return pl.pallas_call(
        _moe_body,
        name=f"moe_ln_r{m}",
        grid=(m // tm, N_EXPERTS),
        in_specs=[pl.BlockSpec((tm, d), lambda i, e: (i, 0)), pl.BlockSpec((tm, N_EXPERTS), lambda i, e: (i, 0)),
                  pl.BlockSpec((1, 1, d, D_FF), lambda i, e: (layer, e, 0, 0)),
                  pl.BlockSpec((1, 1, d, D_FF), lambda i, e: (layer, e, 0, 0)),
                  pl.BlockSpec((1, 1, D_FF, d), lambda i, e: (layer, e, 0, 0)),
                  pl.BlockSpec((1, d), lambda i, e: (0, 0)), pl.BlockSpec((1, d), lambda i, e: (0, 0))],
        out_specs=pl.BlockSpec((tm, d), lambda i, e: (i, 0)),
        out_shape=jax.ShapeDtypeStruct((m, d), f32),
        scratch_shapes=[pltpu.VMEM((tm, d), f32)],
        compiler_params=_cparams(("parallel", "arbitrary")),
    )(h, gate, wg, wu, wd, g2.reshape(1, d), b2.reshape(1, d))


def _ple_body(h_ref, p_ref, wg_ref, wp_ref, o_ref):
    h = h_ref[...]
    precise = _is_f32(wg_ref)
    gate = jax.nn.sigmoid(_mxu(h, wg_ref[...], precise))
    o_ref[...] = h + gate * _mxu(p_ref[...], wp_ref[...], precise)


def _ple(h, p, wg, wp, *, tm):
    m, d = h.shape
    tm = min(tm, m)
    return pl.pallas_call(
        _ple_body,
        name=f"ple_r{m}",
        grid=(m // tm,),
        in_specs=[pl.BlockSpec((tm, d), lambda i: (i, 0)), pl.BlockSpec((tm, PLE_DIM), lambda i: (i, 0)),
                  pl.BlockSpec((d, d), lambda i: (0, 0)), pl.BlockSpec((PLE_DIM, d), lambda i: (0, 0))],
        out_specs=pl.BlockSpec((tm, d), lambda i: (i, 0)),
        out_shape=jax.ShapeDtypeStruct((m, d), f32),
        compiler_params=_cparams(("parallel",)),
    )(h, p, wg, wp)


MOE_TM = 1024
MOE_CH = 384
MOE_ALIGN = 16
MOE_ROWS = MOE_TM + N_EGROUPS * MOE_ALIGN + MOE_CH


def _moe_grouped_body(h_ref, gate_ref, best_ref, wg_ref, wu_ref, wd_ref, g2_ref, b2_ref, o_ref,
                      p_ref, xs_ref, ys_ref, gs_ref, meta_ref):
    e = pl.program_id(1)
    tm, rows, ng = MOE_TM, MOE_ROWS, N_EGROUPS

    @pl.when(e == 0)
    def _():
        best = best_ref[...]
        ind = (best == lax.broadcasted_iota(i32, (8, tm), 0)).astype(f32)
        upper = (lax.broadcasted_iota(i32, (tm, tm), 0) <= lax.broadcasted_iota(i32, (tm, tm), 1)).astype(bf16)
        prefix = _dot(ind.astype(bf16), upper)
        cnt = prefix[:, tm - 1:tm]
        cnt_al = jnp.floor((cnt + (MOE_ALIGN - 1)) * (1.0 / MOE_ALIGN)) * MOE_ALIGN
        offs = [jnp.zeros((1, 1), f32)]
        for g in range(1, ng):
            offs.append(offs[-1] + cnt_al[g - 1:g])
        off_col = jnp.concatenate(offs + [jnp.zeros((8 - ng, 1), f32)], axis=0)
        pos = jnp.sum(ind * (off_col + prefix - 1.0), axis=0, keepdims=True).astype(i32)
        perm = (lax.broadcasted_iota(i32, (rows, tm), 0) == pos).astype(bf16)
        p_ref[...] = perm
        xs_ref[...] = _dot(perm, h_ref[...].astype(bf16)).astype(bf16)
        gs_ref[...] = _dot_exact01(gate_ref[...], perm, left=True)
        ys_ref[...] = jnp.zeros_like(ys_ref)
        for g in range(ng):
            meta_ref[g] = offs[g][0, 0].astype(i32)
            meta_ref[ng + g] = cnt[g:g + 1][0, 0].astype(i32)

    g = e // EPG
    start = meta_ref[g]
    n_ch = (meta_ref[ng + g] + (MOE_CH - 1)) // MOE_CH

    def chunk(c, carry):
        r0 = pl.multiple_of(start + c * MOE_CH, MOE_ALIGN)
        x = xs_ref[pl.ds(r0, MOE_CH), :]
        hid = jax.nn.silu(_dot(x, wg_ref[0, 0])) * _dot(x, wu_ref[0, 0])
        y = _dot(hid.astype(bf16), wd_ref[0, 0])
        gs = gs_ref[pl.ds(r0, MOE_CH), :]
        lane = lax.broadcasted_iota(i32, gs.shape, 1)
        gcol = jnp.sum(jnp.where(lane == e, gs, 0.0), axis=1, keepdims=True)
        ys_ref[pl.ds(r0, MOE_CH), :] += gcol * y
        return carry

    lax.fori_loop(0, n_ch, chunk, 0)

    @pl.when(e == N_EXPERTS - 1)
    def _():
        ys = ys_ref[...]
        hi = ys.astype(bf16)
        lo = (ys - hi.astype(f32)).astype(bf16)
        perm = p_ref[...]
        out = _dot_tn(perm, hi) + _dot_tn(perm, lo)
        o_ref[...] = _ln(ALPHA * h_ref[...] + out, g2_ref[...], b2_ref[...])


def _moe_grouped_ln(h, gate, best, wg, wu, wd, g2, b2, *, layer):
    m, d = h.shape
    tm = MOE_TM
    assert m % tm == 0
    return pl.pallas_call(
        _moe_grouped_body,
        name=f"moe_grouped_ln_r{m}",
        grid=(m // tm, N_EXPERTS),
        in_specs=[pl.BlockSpec((tm, d), lambda i, e: (i, 0)), pl.BlockSpec((tm, N_EXPERTS), lambda i, e: (i, 0)),
                  pl.BlockSpec((1, tm), lambda i, e: (0, i)),
                  pl.BlockSpec((1, 1, d, D_FF), lambda i, e: (layer, e, 0, 0)),
                  pl.BlockSpec((1, 1, d, D_FF), lambda i, e: (layer, e, 0, 0)),
                  pl.BlockSpec((1, 1, D_FF, d), lambda i, e: (layer, e, 0, 0)),
                  pl.BlockSpec((1, d), lambda i, e: (0, 0)), pl.BlockSpec((1, d), lambda i, e: (0, 0))],
        out_specs=pl.BlockSpec((tm, d), lambda i, e: (i, 0)),
        out_shape=jax.ShapeDtypeStruct((m, d), f32),
        scratch_shapes=[pltpu.VMEM((MOE_ROWS, tm), bf16), pltpu.VMEM((MOE_ROWS, d), bf16),
                        pltpu.VMEM((MOE_ROWS, d), f32), pltpu.VMEM((MOE_ROWS, N_EXPERTS), f32),
                        pltpu.SMEM((2 * N_EGROUPS,), i32)],
        compiler_params=_cparams(("arbitrary", "arbitrary")),
    )(h, gate, best, wg, wu, wd, g2.reshape(1, d), b2.reshape(1, d))


def _layer_tail(h1, p, tw, *, tm):
    gate, best = _router(h1, tw["rwt"], tw["rb"], tm=tm)
    if h1.shape[0] % MOE_TM == 0 and tw["wg"].dtype == bf16:
        h2 = _moe_grouped_ln(h1, gate, best, tw["wg"], tw["wu"], tw["wd"], tw["ln2_g"], tw["ln2_b"], layer=tw["layer"])
    else:
        h2 = _moe_ln(h1, gate, tw["wg"], tw["wu"], tw["wd"], tw["ln2_g"], tw["ln2_b"], tm=tm, layer=tw["layer"])
    return _ple(h2, p, tw["ple_gate"], tw["ple_proj"], tm=tm)


def _ssd_conv(x_ref, xp_ref, w_ref, b_ref):
    q = x_ref.shape[0]
    xp_ref[8:8 + q, :] = x_ref[...]
    w = w_ref[...]
    acc = b_ref[...] + xp_ref[5:5 + q, :] * w[0:1]
    for k in range(1, SSD_CONV):
        acc = acc + xp_ref[5 + k:5 + k + q, :] * w[k:k + 1]
    xp_ref[0:8, :] = xp_ref[q:q + 8, :]
    return jax.nn.silu(acc)


def _head_expand(width):
    r = lax.broadcasted_iota(i32, (SSD_HPG, SSD_HPG * width), 0)
    c = lax.broadcasted_iota(i32, (SSD_HPG, SSD_HPG * width), 1)
    return (c // width == r).astype(bf16)


def _split3(v):
    hi = v.astype(bf16)
    r1 = v - hi.astype(f32)
    mid = r1.astype(bf16)
    return hi, mid, (r1 - mid.astype(f32)).astype(bf16)


def _dot_exact01(v, onehot, left=False):
    parts = _split3(v)
    prods = [_dot(onehot, p) if left else _dot(p, onehot) for p in parts]
    return (prods[0] + prods[1]) + prods[2]


def _ssd_body(xs_ref, bm_ref, cm_ref, z_ref, dt_ref, dtt_ref, wx_ref, wb_ref, wc_ref, bx_ref, bb_ref, bc_ref,
              dtb_ref, dtbt_ref, alog_ref, alogt_ref, dsk_ref, ng_ref, y_ref, st_ref,
              stt_ref, xpx_ref, xpb_ref, xpc_ref):
    c = pl.program_id(2)
    q = SSD_CHUNK

    @pl.when(c == 0)
    def _():
        stt_ref[...] = jnp.zeros_like(stt_ref)
        xpx_ref[0:8, :] = jnp.zeros((8, xpx_ref.shape[1]), f32)
        xpb_ref[0:8, :] = jnp.zeros((8, xpb_ref.shape[1]), f32)
        xpc_ref[0:8, :] = jnp.zeros((8, xpc_ref.shape[1]), f32)

    xs = _ssd_conv(xs_ref, xpx_ref, wx_ref, bx_ref)
    bm = _ssd_conv(bm_ref, xpb_ref, wb_ref, bb_ref)
    cm = _ssd_conv(cm_ref, xpc_ref, wc_ref, bc_ref)
    dt = jax.nn.softplus(dt_ref[0] + dtb_ref[0])
    dtt = jax.nn.softplus(dtt_ref[0] + dtbt_ref[0])
    dta = dt * (-jnp.exp(alog_ref[0]))
    dtat = dtt * (-jnp.exp(alogt_ref[0]))
    row = lax.broadcasted_iota(i32, (q, q), 0)
    col = lax.broadcasted_iota(i32, (q, q), 1)
    causal = row >= col
    cum = _dot_exact01(dta, causal.astype(bf16), left=True)
    cumt = _dot_exact01(dtat, (row <= col).astype(bf16))
    both = _dot_exact01(jnp.concatenate([dt, cum], axis=0), _head_expand(SSD_HEADDIM))
    dtx, cumx = both[:q], both[q:]
    cum128 = _dot_exact01(cum, _head_expand(q))
    cum_last = cumx[q - 1:q, :]
    xdt = xs * dtx
    xdt_b = xdt.astype(bf16)
    bm_b = bm.astype(bf16)
    cm_b = cm.astype(bf16)
    cb = _dot_nt(cm_b, bm_b)
    head = lax.broadcasted_iota(i32, (q, SSD_GW), 1) // SSD_HEADDIM
    y = jnp.zeros((q, SSD_GW), f32)
    for r in range(SSD_HPG):
        seg = cum128[:, r * q:(r + 1) * q] - cumt[r:r + 1, :]
        decay = jnp.where(causal, jnp.exp(jnp.where(causal, seg, 0.0)), 0.0)
        yr = _dot((cb * decay).astype(bf16), xdt_b)
        y = jnp.where(head == r, yr, y)
    stt = stt_ref[...]
    y = y + _dot(cm_b, stt.astype(bf16)) * jnp.exp(cumx) + dsk_ref[0] * xs
    to_end = jnp.exp(cum_last - cumx)
    stt_new = stt * jnp.exp(cum_last) + _dot_tn(bm_b, (xdt * to_end).astype(bf16))
    stt_ref[...] = stt_new
    yz = y * jax.nn.silu(z_ref[...])
    yn = yz * lax.rsqrt(jnp.mean(yz * yz, axis=-1, keepdims=True) + 1e-5) * ng_ref[...]
    y_ref[...] = yn.astype(y_ref.dtype)

    @pl.when(c == pl.num_programs(2) - 1)
    def _():
        st_ref[0] = stt_new.T.reshape(SSD_HPG, SSD_HEADDIM, SSD_STATE)


def _ssd_prompt_scan(xbc, z, dt_raw, sw, bsz, t):
    m = bsz * t
    nc = t // SSD_CHUNK
    q, gw, n, g_, hpg = SSD_CHUNK, SSD_GW, SSD_STATE, SSD_GROUPS, SSD_HPG
    dt_g = dt_raw.reshape(m, g_, hpg).transpose(1, 0, 2)
    dtt_g = dt_g.transpose(0, 2, 1)
    nxb = SSD_D_INNER // n
    row = lambda b, g, c: b * nc + c
    in_specs = [
        pl.BlockSpec((q, gw), lambda b, g, c: (row(b, g, c), g)),
        pl.BlockSpec((q, n), lambda b, g, c: (row(b, g, c), nxb + g)),
        pl.BlockSpec((q, n), lambda b, g, c: (row(b, g, c), nxb + g_ + g)),
        pl.BlockSpec((q, gw), lambda b, g, c: (row(b, g, c), g)),
        pl.BlockSpec((1, q, hpg), lambda b, g, c: (g, row(b, g, c), 0)),
        pl.BlockSpec((1, hpg, q), lambda b, g, c: (g, 0, row(b, g, c))),
        pl.BlockSpec((SSD_CONV, gw), lambda b, g, c: (0, g)),
        pl.BlockSpec((SSD_CONV, n), lambda b, g, c: (0, nxb + g)),
        pl.BlockSpec((SSD_CONV, n), lambda b, g, c: (0, nxb + g_ + g)),
        pl.BlockSpec((1, gw), lambda b, g, c: (0, g)),
        pl.BlockSpec((1, n), lambda b, g, c: (0, nxb + g)),
        pl.BlockSpec((1, n), lambda b, g, c: (0, nxb + g_ + g)),
        pl.BlockSpec((1, 1, hpg), lambda b, g, c: (g, 0, 0)),
        pl.BlockSpec((1, hpg, 1), lambda b, g, c: (g, 0, 0)),
        pl.BlockSpec((1, 1, hpg), lambda b, g, c: (g, 0, 0)),
        pl.BlockSpec((1, hpg, 1), lambda b, g, c: (g, 0, 0)),
        pl.BlockSpec((1, 1, gw), lambda b, g, c: (g, 0, 0)),
        pl.BlockSpec((1, gw), lambda b, g, c: (0, g)),
    ]
    y, st = pl.pallas_call(
        _ssd_body,
        name="ssd_scan",
        grid=(bsz, g_, nc),
        in_specs=in_specs,
        out_specs=[pl.BlockSpec((q, gw), lambda b, g, c: (row(b, g, c), g)),
                   pl.BlockSpec((1, hpg, SSD_HEADDIM, n), lambda b, g, c: (b, g, 0, 0))],
        out_shape=[jax.ShapeDtypeStruct((m, SSD_D_INNER), bf16),
                   jax.ShapeDtypeStruct((bsz, SSD_HEADS, SSD_HEADDIM, n), f32)],
        scratch_shapes=[pltpu.VMEM((n, gw), f32), pltpu.VMEM((q + 8, gw), f32),
                        pltpu.VMEM((q + 8, n), f32), pltpu.VMEM((q + 8, n), f32)],
        compiler_params=_cparams(("arbitrary", "arbitrary", "arbitrary")),
    )(xbc, xbc, xbc, z, dt_g, dtt_g, sw["conv_w"], sw["conv_w"], sw["conv_w"], sw["conv_b"], sw["conv_b"], sw["conv_b"],
      sw["dtb"], sw["dtbt"], sw["alog"], sw["alogt"], sw["dskx"], sw["norm_g"])
    return y, st


def _ssd_weights(w_in, conv_w, conv_b, dt_bias, a_log, d_skip, norm_g, w_out, wdt):
    g_, hpg = SSD_GROUPS, SSD_HPG
    return dict(
        w_z=w_in[:, :SSD_D_INNER].astype(wdt),
        w_xbc=w_in[:, SSD_D_INNER:SSD_D_INNER + SSD_CONV_DIM].astype(wdt),
        w_dt=w_in[:, SSD_D_INNER + SSD_CONV_DIM:].astype(wdt),
        conv_w=conv_w, conv_b=conv_b.reshape(1, SSD_CONV_DIM),
        dtb=dt_bias.reshape(g_, 1, hpg), dtbt=dt_bias.reshape(g_, hpg, 1),
        alog=a_log.reshape(g_, 1, hpg), alogt=a_log.reshape(g_, hpg, 1),
        dskx=jnp.repeat(d_skip, SSD_HEADDIM).reshape(g_, 1, SSD_GW),
        dsk=d_skip, dt_bias=dt_bias, a_log=a_log,
        norm_g=norm_g.reshape(1, SSD_D_INNER), w_out=w_out.astype(wdt))


def _ssd_prompt(x2d, sw, bsz, t, ln_g, ln_b):
    (z,) = _mm(x2d, sw["w_z"], tm=512, tn=1024, out_dtypes=(f32,), name="ssd_in_z")
    (xbc,) = _mm(x2d, sw["w_xbc"], tm=512, tn=1024, out_dtypes=(f32,), name="ssd_in_xbc")
    (dt_raw,) = _mm(x2d, sw["w_dt"], tm=512, tn=SSD_HEADS, out_dtypes=(f32,), name="ssd_in_dt")
    y, st = _ssd_prompt_scan(xbc, z, dt_raw, sw, bsz, t)
    h1 = _mm_ln(y, sw["w_out"], x2d, ln_g, ln_b, tm=512, name="ssd_out_ln")
    new_conv = xbc.reshape(bsz, t, SSD_CONV_DIM)[:, t - (SSD_CONV - 1):]
    return h1, new_conv, st


def _pad_rows(x, rows=8):
    return jnp.concatenate([x, jnp.zeros((rows - x.shape[0], x.shape[1]), x.dtype)], axis=0)


def _ssd_step_body(z_ref, xbc_ref, dt_ref, cs_ref, st_ref, cw_ref, cb_ref, dtb_ref, alog_ref, dsk_ref, ng_ref,
                   y_ref, nc_ref, ns_ref):
    n, gw, hd = SSD_STATE, SSD_GW, SSD_HEADDIM
    xbc = xbc_ref[0]
    cs = cs_ref[0]
    w = cw_ref[...]
    conv = cb_ref[...] + cs[0:1] * w[0:1]
    for k in range(1, SSD_CONV - 1):
        conv = conv + cs[k:k + 1] * w[k:k + 1]
    conv = jax.nn.silu(conv + xbc * w[SSD_CONV - 1:SSD_CONV])
    nc_ref[0] = jnp.concatenate([cs[1:], xbc], axis=0)
    xs = conv[:, :SSD_D_INNER]
    b_g = _pad_rows(jnp.concatenate(
        [conv[:, SSD_D_INNER + g * n:SSD_D_INNER + (g + 1) * n] for g in range(SSD_GROUPS)], axis=0))
    c_g = _pad_rows(jnp.concatenate(
        [conv[:, SSD_D_INNER + (SSD_GROUPS + g) * n:SSD_D_INNER + (SSD_GROUPS + g + 1) * n] for g in range(SSD_GROUPS)],
        axis=0))
    dt = jax.nn.softplus(dt_ref[0] + dtb_ref[...])
    decay = jnp.exp(dt * (-jnp.exp(alog_ref[...])))
    er = lax.broadcasted_iota(i32, (SSD_HEADS, SSD_D_INNER), 0)
    ec = lax.broadcasted_iota(i32, (SSD_HEADS, SSD_D_INNER), 1)
    per_head = _pad_rows(jnp.concatenate([dt, decay, dsk_ref[...]], axis=0))
    hx = jnp.dot(per_head, (ec // hd == er).astype(f32), precision=HIGHEST, preferred_element_type=f32)
    dtx, decx, dskx = hx[0:1], hx[1:2], hx[2:3]
    xdt = xs * dtx
    gr = lax.broadcasted_iota(i32, (8, SSD_D_INNER), 0)
    gc = lax.broadcasted_iota(i32, (8, SSD_D_INNER), 1)
    gmask = (gc // gw == gr).astype(f32)
    row0 = (gr == 0).astype(f32)
    st = st_ref[0].reshape(SSD_D_INNER, n)
    upd = _dot_tn(gmask * xdt, b_g, precision=HIGHEST)
    dec_full = _dot_tn(row0 * decx, jnp.ones((8, n), f32), precision=HIGHEST)
    ns_ref[0] = (st * dec_full + upd).reshape(SSD_HEADS, hd, n)
    cst = _dot_nt(c_g, st, precision=HIGHEST)
    y_state = jnp.sum(gmask * cst, axis=0, keepdims=True)
    cbx = jnp.sum(gmask * jnp.sum(c_g * b_g, axis=1, keepdims=True), axis=0, keepdims=True)
    y = cbx * xdt + decx * y_state + dskx * xs
    yz = y * jax.nn.silu(z_ref[0])
    parts = []
    for g in range(SSD_GROUPS):
        seg = yz[:, g * gw:(g + 1) * gw]
        parts.append(seg * lax.rsqrt(jnp.mean(seg * seg, axis=-1, keepdims=True) + 1e-5))
    y_ref[0] = jnp.concatenate(parts, axis=1) * ng_ref[...]


def _ssd_step(x2d, conv_state, ssm_state, sw, ln_g, ln_b):
    bsz = x2d.shape[0]
    (z,) = _mm(x2d, sw["w_z"], tm=bsz, tn=1024, out_dtypes=(f32,), name="ssd_in_z")
    (xbc,) = _mm(x2d, sw["w_xbc"], tm=bsz, tn=1024, out_dtypes=(f32,), name="ssd_in_xbc")
    (dt_raw,) = _mm(x2d, sw["w_dt"], tm=bsz, tn=SSD_HEADS, out_dtypes=(f32,), name="ssd_in_dt")
    rowspec = lambda width: pl.BlockSpec((1, 1, width), lambda b: (b, 0, 0))
    full = lambda r, c: pl.BlockSpec((r, c), lambda b: (0, 0))
    y, new_conv, new_state = pl.pallas_call(
        _ssd_step_body,
        name="ssd_step",
        grid=(bsz,),
        in_specs=[rowspec(SSD_D_INNER), rowspec(SSD_CONV_DIM), rowspec(SSD_HEADS),
                  pl.BlockSpec((1, SSD_CONV - 1, SSD_CONV_DIM), lambda b: (b, 0, 0)),
                  pl.BlockSpec((1, SSD_HEADS, SSD_HEADDIM, SSD_STATE), lambda b: (b, 0, 0, 0)),
                  full(SSD_CONV, SSD_CONV_DIM), full(1, SSD_CONV_DIM), full(1, SSD_HEADS), full(1, SSD_HEADS),
                  full(1, SSD_HEADS), full(1, SSD_D_INNER)],
        out_specs=[rowspec(SSD_D_INNER),
                   pl.BlockSpec((1, SSD_CONV - 1, SSD_CONV_DIM), lambda b: (b, 0, 0)),
                   pl.BlockSpec((1, SSD_HEADS, SSD_HEADDIM, SSD_STATE), lambda b: (b, 0, 0, 0))],
        out_shape=[jax.ShapeDtypeStruct((bsz, 1, SSD_D_INNER), f32),
                   jax.ShapeDtypeStruct((bsz, SSD_CONV - 1, SSD_CONV_DIM), f32),
                   jax.ShapeDtypeStruct((bsz, SSD_HEADS, SSD_HEADDIM, SSD_STATE), f32)],
        compiler_params=_cparams(("parallel",)),
    )(z.reshape(bsz, 1, -1), xbc.reshape(bsz, 1, -1), dt_raw.reshape(bsz, 1, -1), conv_state, ssm_state,
      sw["conv_w"], sw["conv_b"], sw["dt_bias"].reshape(1, -1), sw["a_log"].reshape(1, -1), sw["dsk"].reshape(1, -1),
      sw["norm_g"])
    h1 = _mm_ln(y.reshape(bsz, SSD_D_INNER), sw["w_out"], x2d, ln_g, ln_b, tm=bsz, name="ssd_out_ln")
    return h1, new_conv, new_state


def _slope(head):
    return 2.0 ** (-8.0 * (head + 1) / NSA_HEADS)


def _masked_softmax(s, mask):
    s = jnp.where(mask, s, NEG)
    m = jnp.max(s, axis=-1, keepdims=True)
    e = jnp.exp(s - m) * mask.astype(f32)
    return e / jnp.maximum(jnp.sum(e, axis=-1, keepdims=True), 1e-30)


def _tile_rows(x, n):
    return jnp.concatenate([x] * n, axis=0)


def _slope_col(kv_head, rows):
    return jnp.concatenate([jnp.full((rows, 1), _slope(kv_head * NSA_REP + r), f32) for r in range(NSA_REP)], axis=0)


def _topk_rank(imp):
    nb = imp.shape[1]
    lane = lax.broadcasted_iota(i32, imp.shape, 1)
    rank = jnp.zeros(imp.shape, i32)
    for j in range(nb):
        cj = imp[:, j:j + 1]
        rank = rank + ((cj > imp) | ((cj == imp) & (lane > j))).astype(i32)
    return rank


def _topk_mask_rows(imp, n_sel):
    it = imp.T
    nb = it.shape[0]
    sub = lax.broadcasted_iota(i32, it.shape, 0)
    rank = jnp.zeros(it.shape, i32)
    for j in range(nb):
        rj = it[j:j + 1, :]
        rank = rank + ((rj > it) | ((rj == it) & (sub > j))).astype(i32)
    return (rank < n_sel).astype(f32).T


def _pos_weights(pw_ref):
    w = jax.nn.softmax(pw_ref[...], axis=0)
    r = lax.broadcasted_iota(i32, (2 * NSA_KV, 2 * NSA_KVD), 0)
    c = lax.broadcasted_iota(i32, (2 * NSA_KV, 2 * NSA_KVD), 1)
    return jnp.dot(w, (c // NSA_DH == r).astype(f32), precision=HIGHEST, preferred_element_type=f32)


def _compress_body(x_ref, pw_ref, o_ref):
    wx = _pos_weights(pw_ref)
    x = x_ref[...]
    nb = x.shape[0] // NSA_BLOCK
    o_ref[0] = jnp.sum(x.reshape(nb, NSA_BLOCK, 2 * NSA_KVD) * wx[None], axis=1)


def _nsa_compress(kv4, pw, bsz, t):
    nb = t // NSA_BLOCK
    return pl.pallas_call(
        _compress_body,
        name="nsa_compress",
        grid=(bsz,),
        in_specs=[pl.BlockSpec((t, 2 * NSA_KVD), lambda b: (b, 0)),
                  pl.BlockSpec((NSA_BLOCK, 2 * NSA_KV), lambda b: (0, 0))],
        out_specs=pl.BlockSpec((1, nb, 2 * NSA_KVD), lambda b: (b, 0, 0)),
        out_shape=jax.ShapeDtypeStruct((bsz, nb, 2 * NSA_KVD), f32),
        compiler_params=_cparams(("parallel",)),
    )(kv4, pw)


NSA_TQ = 128
NSA_CK = 512


def _nsa_prompt_body(q_ref, g_ref, kcvc_ref, kvs_ref, kvw_ref, o_ref, *, t_len):
    i = pl.program_id(1)
    tq, ck, rep = NSA_TQ, NSA_CK, NSA_REP
    ck = min(ck, t_len)
    nb = t_len // NSA_BLOCK
    n_sel = min(NSA_TOP, nb)
    wlen = min(NSA_WINDOW + tq, t_len)
    t0 = i * tq
    t_col = t0 + lax.broadcasted_iota(i32, (tq, 1), 0)
    lane128 = lax.broadcasted_iota(i32, (1, LANES), 1)
    gates = g_ref[...]
    kcvc = kcvc_ref[0]
    blk = lax.broadcasted_iota(i32, (1, nb), 1)
    dist_c = t_col - ((blk + 1) * NSA_BLOCK - 1)
    dist_c4 = _tile_rows(dist_c, rep)
    cur = t_col // NSA_BLOCK
    n_chunks = (t0 + tq + ck - 1) // ck
    w_start = jnp.clip(t0 - NSA_WINDOW, 0, t_len - wlen)
    w_start = pl.multiple_of(w_start, tq)
    dist_w = t_col - (w_start + lax.broadcasted_iota(i32, (1, wlen), 1))
    dist_w4 = _tile_rows(dist_w, rep)
    mask_w4 = (dist_w4 >= 0) & (dist_w4 < NSA_WINDOW)

    for j in range(NSA_KV // 2):
        cols = slice(j * LANES, (j + 1) * LANES)
        vcols = slice(NSA_KVD + j * LANES, NSA_KVD + (j + 1) * LANES)
        q_tiles = [q_ref[:, (j * rep + r) * LANES:(j * rep + r + 1) * LANES] for r in range(rep)]
        q_rows = jnp.concatenate(q_tiles, axis=0)
        kc_b = kcvc[:, cols].astype(bf16)
        vc_b = kcvc[:, vcols].astype(bf16)
        out_tiles = [jnp.zeros((tq, LANES), f32) for _ in range(rep)]
        for h in range(2):
            k = 2 * j + h
            half = (lane128 // NSA_DH) == h
            q_pad = jnp.where(half, q_rows, jnp.zeros_like(q_rows))
            slope = _slope_col(k, tq)
            s_c = _dot_nt(q_pad, kc_b) - slope * dist_c4.astype(f32)
            p_c = _masked_softmax(s_c, dist_c4 >= 0)
            o_c = _dot(p_c.astype(bf16), vc_b)
            imp = sum(p_c[r * tq:(r + 1) * tq] for r in range(rep))
            imp = jnp.where((blk == cur) | (blk == 0), 1e4, imp)
            imp = jnp.where(blk > cur, -1.0, imp)
            sel_b = _topk_mask_rows(imp, n_sel).astype(bf16)

            def sel_step(c, carry, q_pad=q_pad, slope=slope, sel_b=sel_b, cols=cols, vcols=vcols):
                m, l, acc = carry
                k0 = pl.multiple_of(c * ck, ck)
                ks = kvs_ref[pl.ds(k0, ck), cols]
                vs = kvs_ref[pl.ds(k0, ck), vcols]
                kpos = k0 + lax.broadcasted_iota(i32, (1, ck), 1)
                brow = lax.broadcasted_iota(i32, (nb, ck), 0)
                expand = ((k0 + lax.broadcasted_iota(i32, (nb, ck), 1)) // NSA_BLOCK == brow).astype(bf16)
                selx = _dot(sel_b, expand)
                dist = t_col - kpos
                mask = _tile_rows((selx > 0.5) & (dist >= 0), rep)
                s = _dot_nt(q_pad, ks) - slope * _tile_rows(dist, rep).astype(f32)
                s = jnp.where(mask, s, NEG)
                m_new = jnp.maximum(m, jnp.max(s, axis=-1, keepdims=True))
                a = jnp.exp(m - m_new)
                p = jnp.exp(s - m_new) * mask.astype(f32)
                l = a * l + jnp.sum(p, axis=-1, keepdims=True)
                acc = a * acc + _dot(p.astype(bf16), vs)
                return m_new, l, acc

            init = (jnp.full((rep * tq, 1), NEG, f32), jnp.zeros((rep * tq, 1), f32), jnp.zeros((rep * tq, LANES), f32))
            _, l_s, acc_s = lax.fori_loop(0, n_chunks, sel_step, init)
            o_s = acc_s / jnp.maximum(l_s, 1e-30)
            kw = kvw_ref[pl.ds(w_start, wlen), cols]
            vw = kvw_ref[pl.ds(w_start, wlen), vcols]
            s_w = _dot_nt(q_pad, kw) - slope * dist_w4.astype(f32)
            p_w = _masked_softmax(s_w, mask_w4)
            o_w = _dot(p_w.astype(bf16), vw)
            for r in range(rep):
                gc = (k * rep + r) * 3
                rows = slice(r * tq, (r + 1) * tq)
                o_r = gates[:, gc:gc + 1] * o_c[rows] + gates[:, gc + 1:gc + 2] * o_s[rows] + gates[:, gc + 2:gc + 3] * o_w[rows]
                out_tiles[r] = jnp.where(half, o_r, out_tiles[r])
        for r in range(rep):
            o_ref[:, (j * rep + r) * LANES:(j * rep + r + 1) * LANES] = out_tiles[r].astype(o_ref.dtype)


def _nsa_prompt_attn(q, gates, kcvc, kv4_b, kvw_b, bsz, t):
    nq = t // NSA_TQ
    nb = t // NSA_BLOCK
    return pl.pallas_call(
        functools.partial(_nsa_prompt_body, t_len=t),
        name="nsa_attn",
        grid=(bsz, nq),
        in_specs=[pl.BlockSpec((NSA_TQ, NSA_QD), lambda b, i: (b * nq + i, 0)),
                  pl.BlockSpec((NSA_TQ, LANES), lambda b, i: (b * nq + i, 0)),
                  pl.BlockSpec((1, nb, 2 * NSA_KVD), lambda b, i: (b, 0, 0)),
                  pl.BlockSpec((t, 2 * NSA_KVD), lambda b, i: (b, 1)),
                  pl.BlockSpec((t, 2 * NSA_KVD), lambda b, i: (b, 0))],
        out_specs=pl.BlockSpec((NSA_TQ, NSA_QD), lambda b, i: (b * nq + i, 0)),
        out_shape=jax.ShapeDtypeStruct((bsz * t, NSA_QD), bf16),
        compiler_params=_cparams(("parallel", "arbitrary")),
    )(q, gates, kcvc, kv4_b, kvw_b)


def _pair_layout_cols(w):
    lead = w.shape[:-1]
    w = w.reshape(*lead, NSA_KV // 2, 2, NSA_REP, NSA_DH)
    return jnp.swapaxes(w, -3, -2).reshape(*lead, NSA_QD)


def _nsa_weights(w_in, pos_w, w_out, wdt, pair):
    lay = _pair_layout_cols if pair else (lambda w: w)
    kv0 = NSA_QD
    g0 = NSA_QD + 6 * NSA_KVD
    wg = jnp.zeros((D_MODEL, LANES), f32).at[:, :3 * NSA_HEADS].set(w_in[:, g0:])
    return dict(
        w_q=lay(w_in[:, :NSA_QD]).astype(wdt),
        w_kv4=w_in[:, kv0:kv0 + 4 * NSA_KVD].astype(wdt),
        w_kvw=w_in[:, kv0 + 4 * NSA_KVD:g0].astype(wdt),
        w_g=wg.astype(wdt),
        pw=pos_w.transpose(1, 0, 2).reshape(NSA_BLOCK, 2 * NSA_KV),
        pwt=pos_w.transpose(0, 2, 1).reshape(2 * NSA_KV, NSA_BLOCK),
        w_kvt=w_in[:, kv0:g0].T.astype(wdt),
        w_out=lay(w_out.T).T.astype(wdt))


def _nsa_prompt(x2d, nw, bsz, t, ln_g, ln_b):
    (q,) = _mm(x2d, nw["w_q"], tm=512, tn=1024, out_dtypes=(bf16,), scale=NSA_DH ** -0.5, name="nsa_in_q")
    kv4, kv4_b = _mm(x2d, nw["w_kv4"], tm=512, tn=1024, out_dtypes=(f32, bf16), name="nsa_in_kv4")
    kvw, kvw_b = _mm(x2d, nw["w_kvw"], tm=512, tn=512, out_dtypes=(f32, bf16), name="nsa_in_kvw")
    (gates,) = _mm(x2d, nw["w_g"], tm=512, tn=LANES, out_dtypes=(f32,), act="sigmoid", name="nsa_in_gates")
    kcvc = _nsa_compress(kv4, nw["pw"], bsz, t)
    o = _nsa_prompt_attn(q, gates, kcvc, kv4_b, kvw_b, bsz, t)
    h1 = _mm_ln(o, nw["w_out"], x2d, ln_g, ln_b, tm=512, name="nsa_out_ln")
    w_keep = min(NSA_WINDOW, t)
    new_kv = kv4.reshape(bsz, t, 4, NSA_KV, NSA_DH)
    new_win = kvw.reshape(bsz, t, 2, NSA_KV, NSA_DH)[:, t - w_keep:]
    return h1, new_kv, new_win


PAGES_PER_STEP = 16


def _lane_weights(pwt_ref):
    w = jax.nn.softmax(pwt_ref[...], axis=1)
    return jnp.concatenate([w] * (PAGE_SIZE // NSA_BLOCK), axis=1)


def _page_compress_body(pt_ref, *refs):
    del pt_ref
    page_refs, pwt_ref, o_ref = refs[:PAGES_PER_STEP], refs[PAGES_PER_STEP], refs[PAGES_PER_STEP + 1]
    s = pl.program_id(1)
    bpp = PAGE_SIZE // NSA_BLOCK
    nbp = o_ref.shape[-1]

    @pl.when(s == 0)
    def _():
        o_ref[...] = jnp.zeros_like(o_ref)

    w = _lane_weights(pwt_ref)
    tok_blk = lax.broadcasted_iota(i32, (1, PAGE_SIZE), 1) // NSA_BLOCK
    out_lane = lax.broadcasted_iota(i32, (1, nbp), 1)
    for c in range(2):
        for k in range(NSA_KV):
            wk = w[c * NSA_KV + k:c * NSA_KV + k + 1]
            upd = jnp.zeros((NSA_DH, nbp), f32)
            for i, p_ref in enumerate(page_refs):
                xw = p_ref[0, c, k] * wk
                first = (s * PAGES_PER_STEP + i) * bpp
                for h in range(bpp):
                    r = jnp.sum(jnp.where(tok_blk == h, xw, 0.0), axis=1, keepdims=True)
                    upd = upd + jnp.where(out_lane == first + h, r, 0.0)
            o_ref[0, c, k] += upd


def _page_compress(pool_t, page_table, pwt):
    bsz, n_pages = page_table.shape
    nbp = n_pages * (PAGE_SIZE // NSA_BLOCK)
    steps = n_pages // PAGES_PER_STEP
    assert n_pages % PAGES_PER_STEP == 0
    page_specs = [pl.BlockSpec((1, 2, NSA_KV, NSA_DH, PAGE_SIZE),
                               lambda b, s, pt, i=i: (pt[b, s * PAGES_PER_STEP + i], 0, 0, 0, 0))
                  for i in range(PAGES_PER_STEP)]
    return pl.pallas_call(
        _page_compress_body,
        name="nsa_page_compress",
        grid_spec=pltpu.PrefetchScalarGridSpec(
            num_scalar_prefetch=1, grid=(bsz, steps),
            in_specs=page_specs + [pl.BlockSpec((2 * NSA_KV, NSA_BLOCK), lambda b, s, pt: (0, 0))],
            out_specs=pl.BlockSpec((1, 2, NSA_KV, NSA_DH, nbp), lambda b, s, pt: (b, 0, 0, 0, 0))),
        out_shape=jax.ShapeDtypeStruct((bsz, 2, NSA_KV, NSA_DH, nbp), f32),
        compiler_params=_cparams(("parallel", "arbitrary")),
    )(page_table, *([pool_t] * PAGES_PER_STEP), pwt)


def _mm_nt_body(wt_ref, x_ref, o_ref):
    o_ref[...] = _mxu_nt(wt_ref[...], x_ref[...], _is_f32(wt_ref))


def _mm_nt(wt, x, *, tn, name):
    n, k = wt.shape
    m = x.shape[0]
    return pl.pallas_call(
        _mm_nt_body, name=f"{name}_r{m}", grid=(n // tn,),
        in_specs=[pl.BlockSpec((tn, k), lambda j: (j, 0)), pl.BlockSpec((m, k), lambda j: (0, 0))],
        out_specs=pl.BlockSpec((tn, m), lambda j: (j, 0)),
        out_shape=jax.ShapeDtypeStruct((n, m), f32),
        compiler_params=_cparams(("parallel",)),
    )(wt, x)


def _step_slopes(kv_head):
    r = lax.broadcasted_iota(i32, (8, 1), 0)
    return jnp.exp2(-0.5 * (kv_head * NSA_REP + r + 1).astype(f32))


def _gate_col(gates, kv_head, branch):
    cols = [(kv_head * NSA_REP + r) * 3 + branch for r in range(NSA_REP)]
    return jnp.concatenate([gates[:, c:c + 1] for c in cols], axis=0)


def _own_col(x, b):
    lane = lax.broadcasted_iota(i32, x.shape, x.ndim - 1)
    return jnp.sum(jnp.where(lane == b, x, 0.0), axis=-1, keepdims=True)


def _new_key_tile(col):
    lane = lax.broadcasted_iota(i32, (col.shape[0], LANES), 1)
    return jnp.where(lane == 0, col, 0.0)


def _nsa_step_cw_body(q_ref, g_ref, cmp_ref, kvt_ref, pwt_ref, win_ref, o_ref, idx_ref, *, past_len):
    b = pl.program_id(0)
    rep, kv = NSA_REP, NSA_KV
    nbp = past_len // NSA_BLOCK
    nbt = nbp + 8
    w_buf = win_ref.shape[-1]
    w0 = jax.nn.softmax(pwt_ref[...], axis=1)[:, 0:1]
    lane_c = lax.broadcasted_iota(i32, (1, nbp + LANES), 1)
    dist_c = past_len - ((lane_c + 1) * NSA_BLOCK - 1)
    lane_w = lax.broadcasted_iota(i32, (1, w_buf + LANES), 1)
    dist_w = jnp.where(lane_w <= w_buf, w_buf - lane_w, -1)
    mask_w = (dist_w >= 0) & (dist_w < NSA_WINDOW)
    blk = lax.broadcasted_iota(i32, (1, nbt), 1)
    cur = past_len // NSA_BLOCK
    gates = g_ref[0]
    for k in range(kv):
        q8 = _pad_rows(q_ref[0, k])
        slope = _step_slopes(k)
        new = [_own_col(kvt_ref[s, k], b) for s in (0, 1, 4, 5)]
        k_c = jnp.concatenate([cmp_ref[0, 0, k], _new_key_tile(new[0] * w0[k:k + 1])], axis=1)
        v_c = jnp.concatenate([cmp_ref[0, 1, k], _new_key_tile(new[1] * w0[kv + k:kv + k + 1])], axis=1)
        s_c = _mxu(q8, k_c, True) - slope * dist_c.astype(f32)
        p_c = _masked_softmax(s_c, jnp.broadcast_to(dist_c >= 0, s_c.shape))
        o_c = _mxu_nt(p_c, v_c, True)
        imp = jnp.sum(p_c[0:rep], axis=0, keepdims=True)[:, :nbt]
        imp = jnp.where((blk == cur) | (blk == 0), 1e4, imp)
        imp = jnp.where(blk > cur, -2.0, imp)
        rank = _topk_rank(imp)
        pick = lax.broadcasted_iota(i32, (NSA_TOP, nbt), 0) == rank
        idx_ref[0, k] = jnp.sum(jnp.where(pick, blk.astype(f32), 0.0), axis=1, keepdims=True).astype(i32)
        k_w = jnp.concatenate([win_ref[0, 0, k], _new_key_tile(new[2])], axis=1)
        v_w = jnp.concatenate([win_ref[0, 1, k], _new_key_tile(new[3])], axis=1)
        s_w = _mxu(q8, k_w, True) - slope * dist_w.astype(f32)
        p_w = _masked_softmax(s_w, jnp.broadcast_to(mask_w, s_w.shape))
        o_w = _mxu_nt(p_w, v_w, True)
        o_ref[0, k] = _gate_col(gates, k, 0) * o_c[0:rep] + _gate_col(gates, k, 2) * o_w[0:rep]


def _nsa_step_sel_body(idx_ref, pt_ref, *refs, past_len):
    del pt_ref
    n_in = 2 * NSA_TOP
    tile_refs = refs[:n_in]
    q_ref, g_ref, kvt_ref, part_ref, o_ref = refs[n_in:]
    b, k = pl.program_id(0), pl.program_id(1)
    nbp = past_len // NSA_BLOCK
    rep, lb, kv = NSA_REP, NSA_BLOCK, NSA_KV
    bpp = PAGE_SIZE // lb
    new_k = _new_key_tile(_own_col(kvt_ref[0, 0], b))
    new_v = _new_key_tile(_own_col(kvt_ref[1, 0], b))
    lane = lax.broadcasted_iota(i32, (1, NSA_TOP * PAGE_SIZE), 1)
    tok = lane % PAGE_SIZE
    spos = tok % lb
    valid = lane < 0
    ks, vs = [], []
    for i in range(NSA_TOP):
        n = idx_ref[(b * kv + k) * NSA_TOP + i]
        is_new = n >= nbp
        ks.append(jnp.where(is_new, new_k, tile_refs[2 * i][0, 0, 0]))
        vs.append(jnp.where(is_new, new_v, tile_refs[2 * i + 1][0, 0, 0]))
        mine = lane // PAGE_SIZE == i
        spos = spos + jnp.where(mine, n * lb, 0)
        half = jnp.where(is_new, 0, n % bpp)
        last = jnp.where(is_new, 0, lb - 1)
        valid = valid | (mine & (tok // lb == half) & (tok % lb <= last))
    k_all = jnp.concatenate(ks, axis=1)
    v_all = jnp.concatenate(vs, axis=1)
    dist = past_len - spos
    q8 = _pad_rows(q_ref[0, 0])
    s = _mxu(q8, k_all, True) - _step_slopes(k) * dist.astype(f32)
    p = _masked_softmax(s, jnp.broadcast_to(valid & (dist >= 0), s.shape))
    o_s = _mxu_nt(p, v_all, True)
    g = g_ref[0, 0]
    g_sel = jnp.concatenate([g[:, 3 * r + 1:3 * r + 2] for r in range(rep)], axis=0)
    o_ref[0, 0] = part_ref[0, 0] + g_sel * o_s[0:rep]


def _nsa_step(x2d, pool, page_table, win_cache, nw, ln_g, ln_b):
    bsz = x2d.shape[0]
    n_pages = page_table.shape[1]
    past_len = n_pages * PAGE_SIZE
    w_buf = win_cache.shape[1]
    rep, kv, dh = NSA_REP, NSA_KV, NSA_DH
    bpp = PAGE_SIZE // NSA_BLOCK
    nbp = past_len // NSA_BLOCK
    (q,) = _mm(x2d, nw["w_q"], tm=bsz, tn=1024, out_dtypes=(f32,), scale=NSA_DH ** -0.5, name="nsa_in_q")
    (kv4,) = _mm(x2d, nw["w_kv4"], tm=bsz, tn=1024, out_dtypes=(f32,), name="nsa_in_kv4")
    (kvw,) = _mm(x2d, nw["w_kvw"], tm=bsz, tn=512, out_dtypes=(f32,), name="nsa_in_kvw")
    (gates,) = _mm(x2d, nw["w_g"], tm=bsz, tn=LANES, out_dtypes=(f32,), act="sigmoid", name="nsa_in_gates")
    kvt = _mm_nt(nw["w_kvt"], x2d, tn=512, name="nsa_in_kvt").reshape(6, kv, dh, bsz)
    pool_t = jnp.transpose(pool, (0, 2, 3, 4, 1))
    win_t = jnp.transpose(win_cache, (0, 2, 3, 4, 1))
    cmp = _page_compress(pool_t, page_table, nw["pwt"])
    q4 = q.reshape(bsz, kv, rep, dh)
    slab = pl.BlockSpec((1, kv, rep, dh), lambda b: (b, 0, 0, 0))
    whole = lambda shape: pl.BlockSpec(shape, lambda b: (0,) * len(shape))
    part, idx = pl.pallas_call(
        functools.partial(_nsa_step_cw_body, past_len=past_len),
        name="nsa_step_cw",
        grid=(bsz,),
        in_specs=[slab, pl.BlockSpec((1, 1, LANES), lambda b: (b, 0, 0)),
                  pl.BlockSpec((1, 2, kv, dh, nbp), lambda b: (b, 0, 0, 0, 0)), whole((6, kv, dh, bsz)),
                  whole((2 * kv, NSA_BLOCK)), pl.BlockSpec((1, 2, kv, dh, w_buf), lambda b: (b, 0, 0, 0, 0))],
        out_specs=[slab, pl.BlockSpec((1, kv, NSA_TOP, 1), lambda b: (b, 0, 0, 0))],
        out_shape=[jax.ShapeDtypeStruct((bsz, kv, rep, dh), f32),
                   jax.ShapeDtypeStruct((bsz, kv, NSA_TOP, 1), i32)],
        compiler_params=_cparams(("parallel",)),
    )(q4, gates.reshape(bsz, 1, -1), cmp, kvt, nw["pwt"], win_t)

    def tile_spec(i, slot):
        def imap(b, k, idx_r, pt_r):
            n = jnp.minimum(idx_r[(b * kv + k) * NSA_TOP + i], nbp - 1)
            return (pt_r[b * n_pages + n // bpp], slot, k, 0, 0)
        return pl.BlockSpec((1, 1, 1, dh, PAGE_SIZE), imap)

    tile_specs = []
    for i in range(NSA_TOP):
        tile_specs += [tile_spec(i, 2), tile_spec(i, 3)]
    head = pl.BlockSpec((1, 1, rep, dh), lambda b, k, *_: (b, k, 0, 0))
    o = pl.pallas_call(
        functools.partial(_nsa_step_sel_body, past_len=past_len),
        name="nsa_step_sel",
        grid_spec=pltpu.PrefetchScalarGridSpec(
            num_scalar_prefetch=2, grid=(bsz, kv),
            in_specs=tile_specs + [
                head, pl.BlockSpec((1, 1, 1, 3 * rep), lambda b, k, *_: (b, k, 0, 0)),
                pl.BlockSpec((2, 1, dh, bsz), lambda b, k, *_: (1, k, 0, 0)), head],
            out_specs=head),
        out_shape=jax.ShapeDtypeStruct((bsz, kv, rep, dh), f32),
        compiler_params=_cparams(("arbitrary", "arbitrary")),
    )(idx.reshape(-1), page_table.reshape(-1), *([pool_t] * (2 * NSA_TOP)),
      q4, gates[:, :3 * NSA_HEADS].reshape(bsz, kv, 1, 3 * rep), kvt, part)
    h1 = _mm_ln(o.reshape(bsz, NSA_QD), nw["w_out"], x2d, ln_g, ln_b, tm=bsz, name="nsa_out_ln")
    new_kv = kv4.reshape(bsz, 1, 4, kv, dh)
    new_win = jnp.concatenate([win_cache[:, 1:], kvw.reshape(bsz, 1, 2, kv, dh)], axis=1)
    return h1, new_kv, new_win


def kernel(x_prompt, x_sample, state_ssm, state_conv, cache_kv, cache_win, page_table, p_prompt, p_sample,
           ssd_w_in, ssd_conv_w, ssd_conv_b, ssd_dt_bias, ssd_a_log, ssd_d, ssd_norm_g, ssd_w_out,
           nsa_w_in, nsa_pos_w, nsa_w_out, ln1_g, ln1_b, ln2_g, ln2_b, router_w, router_bias,
           moe_w_gate, moe_w_up, moe_w_down, ple_proj, ple_gate):
    bp, t, d = x_prompt.shape
    bs = x_sample.shape[0]
    xp = x_prompt.reshape(bp * t, d)
    xs = x_sample.reshape(bs, d)
    rb = router_bias.reshape(N_EXPERTS, 1)
    ssm_p, conv_p, kv_p, win_p, ssm_s, conv_s, kv_s, win_s = [], [], [], [], [], [], [], []
    for i in range(DEPTH):
        j = i // 2
        if i % 2 == 0:
            ssd = (ssd_w_in[j], ssd_conv_w[j], ssd_conv_b[j], ssd_dt_bias[j], ssd_a_log[j], ssd_d[j], ssd_norm_g[j],
                   ssd_w_out[j])
            h1p, c_new, s_new = _ssd_prompt(xp, _ssd_weights(*ssd, bf16), bp, t, ln1_g[i], ln1_b[i])
            conv_p.append(c_new)
            ssm_p.append(s_new)
            h1s, c_new, s_new = _ssd_step(xs, state_conv[j], state_ssm[j], _ssd_weights(*ssd, f32), ln1_g[i], ln1_b[i])
            conv_s.append(c_new)
            ssm_s.append(s_new)
        else:
            nsa = (nsa_w_in[j], nsa_pos_w[j], nsa_w_out[j])
            h1p, r_new, w_new = _nsa_prompt(xp, _nsa_weights(*nsa, bf16, True), bp, t, ln1_g[i], ln1_b[i])
            kv_p.append(r_new)
            win_p.append(w_new)
            h1s, r_new, w_new = _nsa_step(xs, cache_kv[j], page_table, cache_win[j], _nsa_weights(*nsa, f32, False),
                                          ln1_g[i], ln1_b[i])
            kv_s.append(r_new)
            win_s.append(w_new)

        def tail_weights(wdt, i=i):
            return dict(rwt=router_w.T.astype(wdt), rb=rb, layer=i, wg=moe_w_gate.astype(wdt), wu=moe_w_up.astype(wdt),
                        wd=moe_w_down.astype(wdt), ln2_g=ln2_g[i], ln2_b=ln2_b[i],
                        ple_gate=ple_gate[i].astype(wdt), ple_proj=ple_proj[i].astype(wdt))

        xp = _layer_tail(h1p, p_prompt[i].reshape(bp * t, PLE_DIM), tail_weights(bf16), tm=512)
        xs = _layer_tail(h1s, p_sample[i].reshape(bs, PLE_DIM), tail_weights(f32), tm=bs)
    return (xp.reshape(bp, t, d), xs.reshape(bs, 1, d), jnp.stack(ssm_p), jnp.stack(conv_p), jnp.stack(kv_p),
            jnp.stack(win_p), jnp.stack(ssm_s), jnp.stack(conv_s), jnp.stack(kv_s), jnp.stack(win_s))
```

```python
import functools

import jax
import jax.numpy as jnp
from jax import lax
from jax.experimental import pallas as pl
from jax.experimental.pallas import tpu as pltpu

f32, bf16, i32 = jnp.float32, jnp.bfloat16, jnp.int32
HIGHEST = lax.Precision.HIGHEST

D_MODEL = 1024
DEPTH = 2
PLE_DIM = 256
SSD_D_INNER = 2048
SSD_HEADDIM = 64
SSD_HEADS = 32
SSD_GROUPS = 4
SSD_HPG = 8
SSD_STATE = 128
SSD_CONV = 4
SSD_CHUNK = 128
SSD_GW = SSD_HPG * SSD_HEADDIM
SSD_CONV_DIM = SSD_D_INNER + 2 * SSD_GROUPS * SSD_STATE
NSA_HEADS = 16
NSA_KV = 4
NSA_REP = 4
NSA_DH = 64
NSA_BLOCK = 64
NSA_TOP = 16
NSA_WINDOW = 512
NSA_QD = NSA_HEADS * NSA_DH
NSA_KVD = NSA_KV * NSA_DH
PAGE_SIZE = 128
N_EXPERTS = 16
N_EGROUPS = 4
EPG = 4
D_FF = 512
ALPHA = (2.0 * DEPTH) ** 0.25
LN_EPS = 1e-5
NEG = -1e30
LANES = 128
VMEM_LIMIT = 56 * 1024 * 1024


def _cparams(sem):
    return pltpu.CompilerParams(dimension_semantics=sem, vmem_limit_bytes=VMEM_LIMIT)


def _ln(v, g, b):
    mu = jnp.mean(v, axis=-1, keepdims=True)
    d = v - mu
    var = jnp.mean(d * d, axis=-1, keepdims=True)
    return d * lax.rsqrt(var + LN_EPS) * g + b


def _dot(a, b):
    return jnp.dot(a, b, preferred_element_type=f32)


def _mxu(a, b, precise):
    if precise:
        return jnp.dot(a.astype(f32), b.astype(f32), precision=HIGHEST, preferred_element_type=f32)
    return jnp.dot(a.astype(bf16), b.astype(bf16), preferred_element_type=f32)


def _mxu_nt(a, b, precise):
    if precise:
        return _dot_nt(a.astype(f32), b.astype(f32), precision=HIGHEST)
    return _dot_nt(a.astype(bf16), b.astype(bf16))


def _dot_nt(a, b, precision=None):
    return lax.dot_general(a, b, (((1,), (1,)), ((), ())), precision=precision, preferred_element_type=f32)


def _dot_tn(a, b, precision=None):
    return lax.dot_general(a, b, (((0,), (0,)), ((), ())), precision=precision, preferred_element_type=f32)


def _is_f32(ref):
    return ref.dtype == jnp.float32


def _mm_body(x_ref, w_ref, *o_refs, act, scale):
    acc = _mxu(x_ref[...], w_ref[...], _is_f32(w_ref))
    if scale != 1.0:
        acc = acc * scale
    if act == "sigmoid":
        acc = jax.nn.sigmoid(acc)
    for o_ref in o_refs:
        o_ref[...] = acc.astype(o_ref.dtype)


def _mm(x, w, *, tm, tn, out_dtypes, name, act=None, scale=1.0):
    m, k = x.shape
    n = w.shape[1]
    tm, tn = min(tm, m), min(tn, n)
    assert m % tm == 0 and n % tn == 0
    outs = pl.pallas_call(
        functools.partial(_mm_body, act=act, scale=scale),
        name=f"{name}_r{m}",
        grid=(m // tm, n // tn),
        in_specs=[pl.BlockSpec((tm, k), lambda i, j: (i, 0)), pl.BlockSpec((k, tn), lambda i, j: (0, j))],
        out_specs=[pl.BlockSpec((tm, tn), lambda i, j: (i, j)) for _ in out_dtypes],
        out_shape=[jax.ShapeDtypeStruct((m, n), dt) for dt in out_dtypes],
        compiler_params=_cparams(("parallel", "arbitrary")),
    )(x, w)
    return outs


def _mm_ln_body(x_ref, w_ref, res_ref, g_ref, b_ref, o_ref):
    acc = _mxu(x_ref[...], w_ref[...], _is_f32(w_ref))
    o_ref[...] = _ln(ALPHA * res_ref[...] + acc, g_ref[...], b_ref[...])


def _mm_ln(x, w, res, g, b, *, tm, name):
    m, k = x.shape
    n = w.shape[1]
    tm = min(tm, m)
    assert m % tm == 0
    return pl.pallas_call(
        _mm_ln_body,
        name=f"{name}_r{m}",
        grid=(m // tm,),
        in_specs=[pl.BlockSpec((tm, k), lambda i: (i, 0)), pl.BlockSpec((k, n), lambda i: (0, 0)),
                  pl.BlockSpec((tm, n), lambda i: (i, 0)), pl.BlockSpec((1, n), lambda i: (0, 0)),
                  pl.BlockSpec((1, n), lambda i: (0, 0))],
        out_specs=pl.BlockSpec((tm, n), lambda i: (i, 0)),
        out_shape=jax.ShapeDtypeStruct((m, n), f32),
        compiler_params=_cparams(("parallel",)),
    )(x, w, res, g.reshape(1, n), b.reshape(1, n))


def _router_body(h_ref, rwt_ref, rb_ref, gate_ref, best_ref):
    logits = _mxu_nt(rwt_ref[...], h_ref[...], _is_f32(rwt_ref))
    score = jax.nn.sigmoid(logits)
    sel = score + rb_ref[...]
    gsum = []
    for g in range(N_EGROUPS):
        a, b, c, d = (sel[EPG * g + i:EPG * g + i + 1] for i in range(EPG))
        hi1, lo1, hi2, lo2 = jnp.maximum(a, b), jnp.minimum(a, b), jnp.maximum(c, d), jnp.minimum(c, d)
        gsum.append(jnp.maximum(hi1, hi2) + jnp.maximum(jnp.minimum(hi1, hi2), jnp.maximum(lo1, lo2)))
    best = jnp.zeros_like(gsum[0], dtype=i32)
    top = gsum[0]
    for g in range(1, N_EGROUPS):
        upd = gsum[g] > top
        best = jnp.where(upd, g, best)
        top = jnp.where(upd, gsum[g], top)
    selg = sel[0:EPG]
    scg = score[0:EPG]
    for g in range(1, N_EGROUPS):
        selg = jnp.where(best == g, sel[EPG * g:EPG * (g + 1)], selg)
        scg = jnp.where(best == g, score[EPG * g:EPG * (g + 1)], scg)
    rows = [selg[i:i + 1] for i in range(EPG)]
    chosen = []
    for i in range(EPG):
        rank = jnp.zeros_like(best)
        for j in range(EPG):
            if j == i:
                continue
            ahead = (rows[j] > rows[i]) | ((rows[j] == rows[i]) if j < i else False)
            rank = rank + ahead.astype(i32)
        chosen.append(rank < 2)
    wsum = sum(jnp.where(chosen[i], scg[i:i + 1], 0.0) for i in range(EPG))
    gates = [jnp.where(chosen[i], scg[i:i + 1] / wsum, 0.0) for i in range(EPG)]
    out_rows = []
    for g in range(N_EGROUPS):
        for i in range(EPG):
            out_rows.append(jnp.where(best == g, gates[i], 0.0))
    gate_ref[...] = jnp.concatenate(out_rows, axis=0).T
    best_ref[...] = best


def _router(h, rwt, rb, *, tm):
    m, d = h.shape
    tm = min(tm, m)
    return pl.pallas_call(
        _router_body,
        name=f"router_r{m}",
        grid=(m // tm,),
        in_specs=[pl.BlockSpec((tm, d), lambda i: (i, 0)), pl.BlockSpec((N_EXPERTS, d), lambda i: (0, 0)),
                  pl.BlockSpec((N_EXPERTS, 1), lambda i: (0, 0))],
        out_specs=[pl.BlockSpec((tm, N_EXPERTS), lambda i: (i, 0)), pl.BlockSpec((1, tm), lambda i: (0, i))],
        out_shape=[jax.ShapeDtypeStruct((m, N_EXPERTS), f32), jax.ShapeDtypeStruct((1, m), i32)],
        compiler_params=_cparams(("parallel",)),
    )(h, rwt, rb)


def _moe_body(h_ref, gate_ref, wg_ref, wu_ref, wd_ref, g2_ref, b2_ref, o_ref, acc_ref):
    e = pl.program_id(1)

    @pl.when(e == 0)
    def _():
        acc_ref[...] = jnp.zeros_like(acc_ref)

    precise = _is_f32(wg_ref)
    h = h_ref[...]
    hid = jax.nn.silu(_mxu(h, wg_ref[0, 0], precise)) * _mxu(h, wu_ref[0, 0], precise)
    out = _mxu(hid, wd_ref[0, 0], precise)
    gate = gate_ref[...]
    lane = lax.broadcasted_iota(i32, gate.shape, 1)
    gcol = jnp.sum(jnp.where(lane == e, gate, 0.0), axis=1, keepdims=True)
    acc_ref[...] += gcol * out

    @pl.when(e == N_EXPERTS - 1)
    def _():
        o_ref[...] = _ln(ALPHA * h_ref[...] + acc_ref[...], g2_ref[...], b2_ref[...])


def _moe_ln(h, gate, wg, wu, wd, g2, b2, *, tm, layer):
    m, d = h.shape
    tm = min(tm, m)
    return pl.pallas_call(
        _moe_body,
        name=f"moe_ln_r{m}",
        grid=(m // tm, N_EXPERTS),
        in_specs=[pl.BlockSpec((tm, d), lambda i, e: (i, 0)), pl.BlockSpec((tm, N_EXPERTS), lambda i, e: (i, 0)),
                  pl.BlockSpec((1, 1, d, D_FF), lambda i, e: (layer, e, 0, 0)),
                  pl.BlockSpec((1, 1, d, D_FF), lambda i, e: (layer, e, 0, 0)),
                  pl.BlockSpec((1, 1, D_FF, d), lambda i, e: (layer, e, 0, 0)),
                  pl.BlockSpec((1, d), lambda i, e: (0, 0)), pl.BlockSpec((1, d), lambda i, e: (0, 0))],
        out_specs=pl.BlockSpec((tm, d), lambda i, e: (i, 0)),
        out_shape=jax.ShapeDtypeStruct((m, d), f32),
        scratch_shapes=[pltpu.VMEM((tm, d), f32)],
        compiler_params=_cparams(("parallel", "arbitrary")),
    )(h, gate, wg, wu, wd, g2.reshape(1, d), b2.reshape(1, d))


def _ple_body(h_ref, p_ref, wg_ref, wp_ref, o_ref):
    h = h_ref[...]
    precise = _is_f32(wg_ref)
    gate = jax.nn.sigmoid(_mxu(h, wg_ref[...], precise))
    o_ref[...] = h + gate * _mxu(p_ref[...], wp_ref[...], precise)


def _ple(h, p, wg, wp, *, tm):
    m, d = h.shape
    tm = min(tm, m)
    return pl.pallas_call(
        _ple_body,
        name=f"ple_r{m}",
        grid=(m // tm,),
        in_specs=[pl.BlockSpec((tm, d), lambda i: (i, 0)), pl.BlockSpec((tm, PLE_DIM), lambda i: (i, 0)),
                  pl.BlockSpec((d, d), lambda i: (0, 0)), pl.BlockSpec((PLE_DIM, d), lambda i: (0, 0))],
        out_specs=pl.BlockSpec((tm, d), lambda i: (i, 0)),
        out_shape=jax.ShapeDtypeStruct((m, d), f32),
        compiler_params=_cparams(("parallel",)),
    )(h, p, wg, wp)


MOE_TM = 1024
MOE_CH = 304
MOE_ALIGN = 16
MOE_ROWS = MOE_TM + N_EGROUPS * MOE_ALIGN + MOE_CH


def _moe_grouped_body(h_ref, gate_ref, best_ref, wg_ref, wu_ref, wd_ref, g2_ref, b2_ref, o_ref,
                      p_ref, xs_ref, ys_ref, gs_ref, meta_ref):
    e = pl.program_id(1)
    tm, rows, ng = MOE_TM, MOE_ROWS, N_EGROUPS

    @pl.when(e == 0)
    def _():
        best = best_ref[...]
        ind = (best == lax.broadcasted_iota(i32, (8, tm), 0)).astype(f32)
        upper = (lax.broadcasted_iota(i32, (tm, tm), 0) <= lax.broadcasted_iota(i32, (tm, tm), 1)).astype(bf16)
        prefix = _dot(ind.astype(bf16), upper)
        cnt = prefix[:, tm - 1:tm]
        cnt_al = jnp.floor((cnt + (MOE_ALIGN - 1)) * (1.0 / MOE_ALIGN)) * MOE_ALIGN
        offs = [jnp.zeros((1, 1), f32)]
        for g in range(1, ng):
            offs.append(offs[-1] + cnt_al[g - 1:g])
        off_col = jnp.concatenate(offs + [jnp.zeros((8 - ng, 1), f32)], axis=0)
        pos = jnp.sum(ind * (off_col + prefix - 1.0), axis=0, keepdims=True).astype(i32)
        perm = (lax.broadcasted_iota(i32, (rows, tm), 0) == pos).astype(bf16)
        p_ref[...] = perm
        xs_ref[...] = _dot(perm, h_ref[...].astype(bf16)).astype(bf16)
        gs_ref[...] = _dot_exact01(gate_ref[...], perm, left=True)
        ys_ref[...] = jnp.zeros_like(ys_ref)
        for g in range(ng):
            meta_ref[g] = offs[g][0, 0].astype(i32)
            meta_ref[ng + g] = cnt[g:g + 1][0, 0].astype(i32)

    g = e // EPG
    start = meta_ref[g]
    n_ch = (meta_ref[ng + g] + (MOE_CH - 1)) // MOE_CH

    def chunk(c, carry):
        r0 = pl.multiple_of(start + c * MOE_CH, MOE_ALIGN)
        x = xs_ref[pl.ds(r0, MOE_CH), :]
        hid = jax.nn.silu(_dot(x, wg_ref[0, 0])) * _dot(x, wu_ref[0, 0])
        y = _dot(hid.astype(bf16), wd_ref[0, 0])
        gs = gs_ref[pl.ds(r0, MOE_CH), :]
        lane = lax.broadcasted_iota(i32, gs.shape, 1)
        gcol = jnp.sum(jnp.where(lane == e, gs, 0.0), axis=1, keepdims=True)
        ys_ref[pl.ds(r0, MOE_CH), :] += gcol * y
        return carry

    lax.fori_loop(0, n_ch, chunk, 0)

    @pl.when(e == N_EXPERTS - 1)
    def _():
        ys = ys_ref[...]
        hi = ys.astype(bf16)
        lo = (ys - hi.astype(f32)).astype(bf16)
        perm = p_ref[...]
        out = _dot_tn(perm, hi) + _dot_tn(perm, lo)
        o_ref[...] = _ln(ALPHA * h_ref[...] + out, g2_ref[...], b2_ref[...])


def _moe_grouped_ln(h, gate, best, wg, wu, wd, g2, b2, *, layer):
    m, d = h.shape
    tm = MOE_TM
    assert m % tm == 0
    return pl.pallas_call(
        _moe_grouped_body,
        name=f"moe_grouped_ln_r{m}",
        grid=(m // tm, N_EXPERTS),
        in_specs=[pl.BlockSpec((tm, d), lambda i, e: (i, 0)), pl.BlockSpec((tm, N_EXPERTS), lambda i, e: (i, 0)),
                  pl.BlockSpec((1, tm), lambda i, e: (0, i)),
                  pl.BlockSpec((1, 1, d, D_FF), lambda i, e: (layer, e, 0, 0)),
                  pl.BlockSpec((1, 1, d, D_FF), lambda i, e: (layer, e, 0, 0)),
                  pl.BlockSpec((1, 1, D_FF, d), lambda i, e: (layer, e, 0, 0)),
                  pl.BlockSpec((1, d), lambda i, e: (0, 0)), pl.BlockSpec((1, d), lambda i, e: (0, 0))],
        out_specs=pl.BlockSpec((tm, d), lambda i, e: (i, 0)),
        out_shape=jax.ShapeDtypeStruct((m, d), f32),
        scratch_shapes=[pltpu.VMEM((MOE_ROWS, tm), bf16), pltpu.VMEM((MOE_ROWS, d), bf16),
                        pltpu.VMEM((MOE_ROWS, d), f32), pltpu.VMEM((MOE_ROWS, N_EXPERTS), f32),
                        pltpu.SMEM((2 * N_EGROUPS,), i32)],
        compiler_params=_cparams(("arbitrary", "arbitrary")),
    )(h, gate, best, wg, wu, wd, g2.reshape(1, d), b2.reshape(1, d))


def _layer_tail(h1, p, tw, *, tm):
    gate, best = _router(h1, tw["rwt"], tw["rb"], tm=tm)
    if h1.shape[0] % MOE_TM == 0 and tw["wg"].dtype == bf16:
        h2 = _moe_grouped_ln(h1, gate, best, tw["wg"], tw["wu"], tw["wd"], tw["ln2_g"], tw["ln2_b"], layer=tw["layer"])
    else:
        h2 = _moe_ln(h1, gate, tw["wg"], tw["wu"], tw["wd"], tw["ln2_g"], tw["ln2_b"], tm=tm, layer=tw["layer"])
    return _ple(h2, p, tw["ple_gate"], tw["ple_proj"], tm=tm)


def _ssd_conv(x_ref, xp_ref, w_ref, b_ref):
    q = x_ref.shape[0]
    xp_ref[8:8 + q, :] = x_ref[...]
    w = w_ref[...]
    acc = b_ref[...] + xp_ref[5:5 + q, :] * w[0:1]
    for k in range(1, SSD_CONV):
        acc = acc + xp_ref[5 + k:5 + k + q, :] * w[k:k + 1]
    xp_ref[0:8, :] = xp_ref[q:q + 8, :]
    return jax.nn.silu(acc)


def _head_expand(width):
    r = lax.broadcasted_iota(i32, (SSD_HPG, SSD_HPG * width), 0)
    c = lax.broadcasted_iota(i32, (SSD_HPG, SSD_HPG * width), 1)
    return (c // width == r).astype(bf16)


def _split3(v):
    hi = v.astype(bf16)
    r1 = v - hi.astype(f32)
    mid = r1.astype(bf16)
    return hi, mid, (r1 - mid.astype(f32)).astype(bf16)


def _dot_exact01(v, onehot, left=False):
    parts = _split3(v)
    prods = [_dot(onehot, p) if left else _dot(p, onehot) for p in parts]
    return (prods[0] + prods[1]) + prods[2]


def _ssd_body(xs_ref, bm_ref, cm_ref, z_ref, dt_ref, dtt_ref, wx_ref, wb_ref, wc_ref, bx_ref, bb_ref, bc_ref,
              dtb_ref, dtbt_ref, alog_ref, alogt_ref, dsk_ref, ng_ref, y_ref, st_ref,
              stt_ref, xpx_ref, xpb_ref, xpc_ref):
    c = pl.program_id(2)
    q = SSD_CHUNK

    @pl.when(c == 0)
    def _():
        stt_ref[...] = jnp.zeros_like(stt_ref)
        xpx_ref[0:8, :] = jnp.zeros((8, xpx_ref.shape[1]), f32)
        xpb_ref[0:8, :] = jnp.zeros((8, xpb_ref.shape[1]), f32)
        xpc_ref[0:8, :] = jnp.zeros((8, xpc_ref.shape[1]), f32)

    xs = _ssd_conv(xs_ref, xpx_ref, wx_ref, bx_ref)
    bm = _ssd_conv(bm_ref, xpb_ref, wb_ref, bb_ref)
    cm = _ssd_conv(cm_ref, xpc_ref, wc_ref, bc_ref)
    dt = jax.nn.softplus(dt_ref[0] + dtb_ref[0])
    dtt = jax.nn.softplus(dtt_ref[0] + dtbt_ref[0])
    dta = dt * (-jnp.exp(alog_ref[0]))
    dtat = dtt * (-jnp.exp(alogt_ref[0]))
    row = lax.broadcasted_iota(i32, (q, q), 0)
    col = lax.broadcasted_iota(i32, (q, q), 1)
    causal = row >= col
    cum = _dot_exact01(dta, causal.astype(bf16), left=True)
    cumt = _dot_exact01(dtat, (row <= col).astype(bf16))
    both = _dot_exact01(jnp.concatenate([dt, cum], axis=0), _head_expand(SSD_HEADDIM))
    dtx, cumx = both[:q], both[q:]
    cum128 = _dot_exact01(cum, _head_expand(q))
    cum_last = cumx[q - 1:q, :]
    xdt = xs * dtx
    xdt_b = xdt.astype(bf16)
    bm_b = bm.astype(bf16)
    cm_b = cm.astype(bf16)
    cb = _dot_nt(cm_b, bm_b)
    head = lax.broadcasted_iota(i32, (q, SSD_GW), 1) // SSD_HEADDIM
    y = jnp.zeros((q, SSD_GW), f32)
    for r in range(SSD_HPG):
        seg = cum128[:, r * q:(r + 1) * q] - cumt[r:r + 1, :]
        decay = jnp.where(causal, jnp.exp(jnp.where(causal, seg, 0.0)), 0.0)
        yr = _dot((cb * decay).astype(bf16), xdt_b)
        y = jnp.where(head == r, yr, y)
    stt = stt_ref[...]
    y = y + _dot(cm_b, stt.astype(bf16)) * jnp.exp(cumx) + dsk_ref[0] * xs
    to_end = jnp.exp(cum_last - cumx)
    stt_new = stt * jnp.exp(cum_last) + _dot_tn(bm_b, (xdt * to_end).astype(bf16))
    stt_ref[...] = stt_new
    yz = y * jax.nn.silu(z_ref[...])
    yn = yz * lax.rsqrt(jnp.mean(yz * yz, axis=-1, keepdims=True) + 1e-5) * ng_ref[...]
    y_ref[...] = yn.astype(y_ref.dtype)

    @pl.when(c == pl.num_programs(2) - 1)
    def _():
        st_ref[0] = stt_new.T.reshape(SSD_HPG, SSD_HEADDIM, SSD_STATE)


def _ssd_prompt_scan(xbc, z, dt_raw, sw, bsz, t):
    m = bsz * t
    nc = t // SSD_CHUNK
    q, gw, n, g_, hpg = SSD_CHUNK, SSD_GW, SSD_STATE, SSD_GROUPS, SSD_HPG
    dt_g = dt_raw.reshape(m, g_, hpg).transpose(1, 0, 2)
    dtt_g = dt_g.transpose(0, 2, 1)
    nxb = SSD_D_INNER // n
    row = lambda b, g, c: b * nc + c
    in_specs = [
        pl.BlockSpec((q, gw), lambda b, g, c: (row(b, g, c), g)),
        pl.BlockSpec((q, n), lambda b, g, c: (row(b, g, c), nxb + g)),
        pl.BlockSpec((q, n), lambda b, g, c: (row(b, g, c), nxb + g_ + g)),
        pl.BlockSpec((q, gw), lambda b, g, c: (row(b, g, c), g)),
        pl.BlockSpec((1, q, hpg), lambda b, g, c: (g, row(b, g, c), 0)),
        pl.BlockSpec((1, hpg, q), lambda b, g, c: (g, 0, row(b, g, c))),
        pl.BlockSpec((SSD_CONV, gw), lambda b, g, c: (0, g)),
        pl.BlockSpec((SSD_CONV, n), lambda b, g, c: (0, nxb + g)),
        pl.BlockSpec((SSD_CONV, n), lambda b, g, c: (0, nxb + g_ + g)),
        pl.BlockSpec((1, gw), lambda b, g, c: (0, g)),
        pl.BlockSpec((1, n), lambda b, g, c: (0, nxb + g)),
        pl.BlockSpec((1, n), lambda b, g, c: (0, nxb + g_ + g)),
        pl.BlockSpec((1, 1, hpg), lambda b, g, c: (g, 0, 0)),
        pl.BlockSpec((1, hpg, 1), lambda b, g, c: (g, 0, 0)),
        pl.BlockSpec((1, 1, hpg), lambda b, g, c: (g, 0, 0)),
        pl.BlockSpec((1, hpg, 1), lambda b, g, c: (g, 0, 0)),
        pl.BlockSpec((1, 1, gw), lambda b, g, c: (g, 0, 0)),
        pl.BlockSpec((1, gw), lambda b, g, c: (0, g)),
    ]
    y, st = pl.pallas_call(
        _ssd_body,
        name="ssd_scan",
        grid=(bsz, g_, nc),
        in_specs=in_specs,
        out_specs=[pl.BlockSpec((q, gw), lambda b, g, c: (row(b, g, c), g)),
                   pl.BlockSpec((1, hpg, SSD_HEADDIM, n), lambda b, g, c: (b, g, 0, 0))],
        out_shape=[jax.ShapeDtypeStruct((m, SSD_D_INNER), bf16),
                   jax.ShapeDtypeStruct((bsz, SSD_HEADS, SSD_HEADDIM, n), f32)],
        scratch_shapes=[pltpu.VMEM((n, gw), f32), pltpu.VMEM((q + 8, gw), f32),
                        pltpu.VMEM((q + 8, n), f32), pltpu.VMEM((q + 8, n), f32)],
        compiler_params=_cparams(("arbitrary", "arbitrary", "arbitrary")),
    )(xbc, xbc, xbc, z, dt_g, dtt_g, sw["conv_w"], sw["conv_w"], sw["conv_w"], sw["conv_b"], sw["conv_b"], sw["conv_b"],
      sw["dtb"], sw["dtbt"], sw["alog"], sw["alogt"], sw["dskx"], sw["norm_g"])
    return y, st


def _ssd_weights(w_in, conv_w, conv_b, dt_bias, a_log, d_skip, norm_g, w_out, wdt):
    g_, hpg = SSD_GROUPS, SSD_HPG
    return dict(
        w_z=w_in[:, :SSD_D_INNER].astype(wdt),
        w_xbc=w_in[:, SSD_D_INNER:SSD_D_INNER + SSD_CONV_DIM].astype(wdt),
        w_dt=w_in[:, SSD_D_INNER + SSD_CONV_DIM:].astype(wdt),
        conv_w=conv_w, conv_b=conv_b.reshape(1, SSD_CONV_DIM),
        dtb=dt_bias.reshape(g_, 1, hpg), dtbt=dt_bias.reshape(g_, hpg, 1),
        alog=a_log.reshape(g_, 1, hpg), alogt=a_log.reshape(g_, hpg, 1),
        dskx=jnp.repeat(d_skip, SSD_HEADDIM).reshape(g_, 1, SSD_GW),
        dsk=d_skip, dt_bias=dt_bias, a_log=a_log,
        norm_g=norm_g.reshape(1, SSD_D_INNER), w_out=w_out.astype(wdt))


def _ssd_prompt(x2d, sw, bsz, t, ln_g, ln_b):
    (z,) = _mm(x2d, sw["w_z"], tm=512, tn=1024, out_dtypes=(f32,), name="ssd_in_z")
    (xbc,) = _mm(x2d, sw["w_xbc"], tm=512, tn=1024, out_dtypes=(f32,), name="ssd_in_xbc")
    (dt_raw,) = _mm(x2d, sw["w_dt"], tm=512, tn=SSD_HEADS, out_dtypes=(f32,), name="ssd_in_dt")
    y, st = _ssd_prompt_scan(xbc, z, dt_raw, sw, bsz, t)
    h1 = _mm_ln(y, sw["w_out"], x2d, ln_g, ln_b, tm=512, name="ssd_out_ln")
    new_conv = xbc.reshape(bsz, t, SSD_CONV_DIM)[:, t - (SSD_CONV - 1):]
    return h1, new_conv, st


def _pad_rows(x, rows=8):
    return jnp.concatenate([x, jnp.zeros((rows - x.shape[0], x.shape[1]), x.dtype)], axis=0)


def _ssd_step_body(z_ref, xbc_ref, dt_ref, cs_ref, st_ref, cw_ref, cb_ref, dtb_ref, alog_ref, dsk_ref, ng_ref,
                   y_ref, nc_ref, ns_ref):
    n, gw, hd = SSD_STATE, SSD_GW, SSD_HEADDIM
    xbc = xbc_ref[0]
    cs = cs_ref[0]
    w = cw_ref[...]
    conv = cb_ref[...] + cs[0:1] * w[0:1]
    for k in range(1, SSD_CONV - 1):
        conv = conv + cs[k:k + 1] * w[k:k + 1]
    conv = jax.nn.silu(conv + xbc * w[SSD_CONV - 1:SSD_CONV])
    nc_ref[0] = jnp.concatenate([cs[1:], xbc], axis=0)
    xs = conv[:, :SSD_D_INNER]
    b_g = _pad_rows(jnp.concatenate(
        [conv[:, SSD_D_INNER + g * n:SSD_D_INNER + (g + 1) * n] for g in range(SSD_GROUPS)], axis=0))
    c_g = _pad_rows(jnp.concatenate(
        [conv[:, SSD_D_INNER + (SSD_GROUPS + g) * n:SSD_D_INNER + (SSD_GROUPS + g + 1) * n] for g in range(SSD_GROUPS)],
        axis=0))
    dt = jax.nn.softplus(dt_ref[0] + dtb_ref[...])
    decay = jnp.exp(dt * (-jnp.exp(alog_ref[...])))
    er = lax.broadcasted_iota(i32, (SSD_HEADS, SSD_D_INNER), 0)
    ec = lax.broadcasted_iota(i32, (SSD_HEADS, SSD_D_INNER), 1)
    per_head = _pad_rows(jnp.concatenate([dt, decay, dsk_ref[...]], axis=0))
    hx = jnp.dot(per_head, (ec // hd == er).astype(f32), precision=HIGHEST, preferred_element_type=f32)
    dtx, decx, dskx = hx[0:1], hx[1:2], hx[2:3]
    xdt = xs * dtx
    gr = lax.broadcasted_iota(i32, (8, SSD_D_INNER), 0)
    gc = lax.broadcasted_iota(i32, (8, SSD_D_INNER), 1)
    gmask = (gc // gw == gr).astype(f32)
    row0 = (gr == 0).astype(f32)
    st = st_ref[0].reshape(SSD_D_INNER, n)
    upd = _dot_tn(gmask * xdt, b_g, precision=HIGHEST)
    dec_full = _dot_tn(row0 * decx, jnp.ones((8, n), f32), precision=HIGHEST)
    ns_ref[0] = (st * dec_full + upd).reshape(SSD_HEADS, hd, n)
    cst = _dot_nt(c_g, st, precision=HIGHEST)
    y_state = jnp.sum(gmask * cst, axis=0, keepdims=True)
    cbx = jnp.sum(gmask * jnp.sum(c_g * b_g, axis=1, keepdims=True), axis=0, keepdims=True)
    y = cbx * xdt + decx * y_state + dskx * xs
    yz = y * jax.nn.silu(z_ref[0])
    parts = []
    for g in range(SSD_GROUPS):
        seg = yz[:, g * gw:(g + 1) * gw]
        parts.append(seg * lax.rsqrt(jnp.mean(seg * seg, axis=-1, keepdims=True) + 1e-5))
    y_ref[0] = jnp.concatenate(parts, axis=1) * ng_ref[...]


def _ssd_step(x2d, conv_state, ssm_state, sw, ln_g, ln_b):
    bsz = x2d.shape[0]
    (z,) = _mm(x2d, sw["w_z"], tm=bsz, tn=1024, out_dtypes=(f32,), name="ssd_in_z")
    (xbc,) = _mm(x2d, sw["w_xbc"], tm=bsz, tn=1024, out_dtypes=(f32,), name="ssd_in_xbc")
    (dt_raw,) = _mm(x2d, sw["w_dt"], tm=bsz, tn=SSD_HEADS, out_dtypes=(f32,), name="ssd_in_dt")
    rowspec = lambda width: pl.BlockSpec((1, 1, width), lambda b: (b, 0, 0))
    full = lambda r, c: pl.BlockSpec((r, c), lambda b: (0, 0))
    y, new_conv, new_state = pl.pallas_call(
        _ssd_step_body,
        name="ssd_step",
        grid=(bsz,),
        in_specs=[rowspec(SSD_D_INNER), rowspec(SSD_CONV_DIM), rowspec(SSD_HEADS),
                  pl.BlockSpec((1, SSD_CONV - 1, SSD_CONV_DIM), lambda b: (b, 0, 0)),
                  pl.BlockSpec((1, SSD_HEADS, SSD_HEADDIM, SSD_STATE), lambda b: (b, 0, 0, 0)),
                  full(SSD_CONV, SSD_CONV_DIM), full(1, SSD_CONV_DIM), full(1, SSD_HEADS), full(1, SSD_HEADS),
                  full(1, SSD_HEADS), full(1, SSD_D_INNER)],
        out_specs=[rowspec(SSD_D_INNER),
                   pl.BlockSpec((1, SSD_CONV - 1, SSD_CONV_DIM), lambda b: (b, 0, 0)),
                   pl.BlockSpec((1, SSD_HEADS, SSD_HEADDIM, SSD_STATE), lambda b: (b, 0, 0, 0))],
        out_shape=[jax.ShapeDtypeStruct((bsz, 1, SSD_D_INNER), f32),
                   jax.ShapeDtypeStruct((bsz, SSD_CONV - 1, SSD_CONV_DIM), f32),
                   jax.ShapeDtypeStruct((bsz, SSD_HEADS, SSD_HEADDIM, SSD_STATE), f32)],
        compiler_params=_cparams(("parallel",)),
    )(z.reshape(bsz, 1, -1), xbc.reshape(bsz, 1, -1), dt_raw.reshape(bsz, 1, -1), conv_state, ssm_state,
      sw["conv_w"], sw["conv_b"], sw["dt_bias"].reshape(1, -1), sw["a_log"].reshape(1, -1), sw["dsk"].reshape(1, -1),
      sw["norm_g"])
    h1 = _mm_ln(y.reshape(bsz, SSD_D_INNER), sw["w_out"], x2d, ln_g, ln_b, tm=bsz, name="ssd_out_ln")
    return h1, new_conv, new_state


def _slope(head):
    return 2.0 ** (-8.0 * (head + 1) / NSA_HEADS)


def _masked_softmax(s, mask):
    s = jnp.where(mask, s, NEG)
    m = jnp.max(s, axis=-1, keepdims=True)
    e = jnp.exp(s - m) * mask.astype(f32)
    return e / jnp.maximum(jnp.sum(e, axis=-1, keepdims=True), 1e-30)


def _tile_rows(x, n):
    return jnp.concatenate([x] * n, axis=0)


def _alibi_bias(ok, dist, kv_head):
    distf = dist.astype(f32)
    return jnp.concatenate(
        [jnp.where(ok, (-_slope(kv_head * NSA_REP + r)) * distf, NEG) for r in range(NSA_REP)], axis=0)


def _slope_col(kv_head, rows):
    return jnp.concatenate([jnp.full((rows, 1), _slope(kv_head * NSA_REP + r), f32) for r in range(NSA_REP)], axis=0)


def _topk_rank(imp):
    nb = imp.shape[1]
    lane = lax.broadcasted_iota(i32, imp.shape, 1)
    rank = jnp.zeros(imp.shape, i32)
    for j in range(nb):
        cj = imp[:, j:j + 1]
        rank = rank + ((cj > imp) | ((cj == imp) & (lane > j))).astype(i32)
    return rank


def _topk_mask_rows(imp, n_sel):
    it = imp.T
    nb = it.shape[0]
    sub = lax.broadcasted_iota(i32, it.shape, 0)
    rank = jnp.zeros(it.shape, i32)
    for j in range(nb):
        rj = it[j:j + 1, :]
        rank = rank + ((rj > it) | ((rj == it) & (sub > j))).astype(i32)
    return (rank < n_sel).astype(f32).T


def _pos_weights(pw_ref):
    w = jax.nn.softmax(pw_ref[...], axis=0)
    r = lax.broadcasted_iota(i32, (2 * NSA_KV, 2 * NSA_KVD), 0)
    c = lax.broadcasted_iota(i32, (2 * NSA_KV, 2 * NSA_KVD), 1)
    return jnp.dot(w, (c // NSA_DH == r).astype(f32), precision=HIGHEST, preferred_element_type=f32)


def _compress_body(x_ref, pw_ref, o_ref):
    wx = _pos_weights(pw_ref)
    x = x_ref[...]
    nb = x.shape[0] // NSA_BLOCK
    o_ref[0] = jnp.sum(x.reshape(nb, NSA_BLOCK, 2 * NSA_KVD) * wx[None], axis=1)


def _nsa_compress(kv4, pw, bsz, t):
    nb = t // NSA_BLOCK
    return pl.pallas_call(
        _compress_body,
        name="nsa_compress",
        grid=(bsz,),
        in_specs=[pl.BlockSpec((t, 2 * NSA_KVD), lambda b: (b, 0)),
                  pl.BlockSpec((NSA_BLOCK, 2 * NSA_KV), lambda b: (0, 0))],
        out_specs=pl.BlockSpec((1, nb, 2 * NSA_KVD), lambda b: (b, 0, 0)),
        out_shape=jax.ShapeDtypeStruct((bsz, nb, 2 * NSA_KVD), f32),
        compiler_params=_cparams(("parallel",)),
    )(kv4, pw)


NSA_TQ = 128
NSA_CK = 512


def _nsa_prompt_body(q_ref, g_ref, kcvc_ref, kvs_ref, kvw_ref, o_ref, *, t_len):
    i = pl.program_id(1)
    tq, ck, rep = NSA_TQ, NSA_CK, NSA_REP
    ck = min(ck, t_len)
    nb = t_len // NSA_BLOCK
    n_sel = min(NSA_TOP, nb)
    wlen = min(NSA_WINDOW + tq, t_len)
    t0 = i * tq
    t_col = t0 + lax.broadcasted_iota(i32, (tq, 1), 0)
    lane128 = lax.broadcasted_iota(i32, (1, LANES), 1)
    gates = g_ref[...]
    kcvc = kcvc_ref[0]
    blk = lax.broadcasted_iota(i32, (1, nb), 1)
    dist_c = t_col - ((blk + 1) * NSA_BLOCK - 1)
    dist_c4 = _tile_rows(dist_c, rep)
    cur = t_col // NSA_BLOCK
    n_chunks = (t0 + tq + ck - 1) // ck
    w_start = jnp.clip(t0 - NSA_WINDOW, 0, t_len - wlen)
    w_start = pl.multiple_of(w_start, tq)
    dist_w = t_col - (w_start + lax.broadcasted_iota(i32, (1, wlen), 1))
    mask_w = (dist_w >= 0) & (dist_w < NSA_WINDOW)

    for j in range(NSA_KV // 2):
        cols = slice(j * LANES, (j + 1) * LANES)
        vcols = slice(NSA_KVD + j * LANES, NSA_KVD + (j + 1) * LANES)
        q_tiles = [q_ref[:, (j * rep + r) * LANES:(j * rep + r + 1) * LANES] for r in range(rep)]
        q_rows = jnp.concatenate(q_tiles, axis=0)
        kc_b = kcvc[:, cols].astype(bf16)
        vc_b = kcvc[:, vcols].astype(bf16)
        out_tiles = [jnp.zeros((tq, LANES), f32) for _ in range(rep)]
        for h in range(2):
            k = 2 * j + h
            half = (lane128 // NSA_DH) == h
            q_pad = jnp.where(half, q_rows, jnp.zeros_like(q_rows))
            slope = _slope_col(k, tq)
            s_c = _dot_nt(q_pad, kc_b) - slope * dist_c4.astype(f32)
            p_c = _masked_softmax(s_c, dist_c4 >= 0)
            o_c = _dot(p_c.astype(bf16), vc_b)
            imp = sum(p_c[r * tq:(r + 1) * tq] for r in range(rep))
            imp = jnp.where((blk == cur) | (blk == 0), 1e4, imp)
            imp = jnp.where(blk > cur, -1.0, imp)
            sel_b = _topk_mask_rows(imp, n_sel).astype(bf16)

            def sel_step(c, carry, q_pad=q_pad, k=k, sel_b=sel_b, cols=cols, vcols=vcols):
                m, l, acc = carry
                k0 = pl.multiple_of(c * ck, ck)
                ks = kvs_ref[pl.ds(k0, ck), cols]
                vs = kvs_ref[pl.ds(k0, ck), vcols]
                kpos = k0 + lax.broadcasted_iota(i32, (1, ck), 1)
                brow = lax.broadcasted_iota(i32, (nb, ck), 0)
                expand = ((k0 + lax.broadcasted_iota(i32, (nb, ck), 1)) // NSA_BLOCK == brow).astype(bf16)
                selx = _dot(sel_b, expand)
                dist = t_col - kpos
                ok = (selx > 0.5) & (dist >= 0)
                s = _dot_nt(q_pad, ks) + _alibi_bias(ok, dist, k)
                m_new = jnp.maximum(m, jnp.max(s, axis=-1, keepdims=True))
                a = jnp.exp(m - m_new)
                p = jnp.exp(s - m_new)
                l = a * l + jnp.sum(p, axis=-1, keepdims=True)
                acc = a * acc + _dot(p.astype(bf16), vs)
                return m_new, l, acc

            init = (jnp.full((rep * tq, 1), NEG, f32), jnp.zeros((rep * tq, 1), f32), jnp.zeros((rep * tq, LANES), f32))
            _, l_s, acc_s = lax.fori_loop(0, n_chunks, sel_step, init)
            o_s = acc_s / jnp.maximum(l_s, 1e-30)
            kw = kvw_ref[pl.ds(w_start, wlen), cols]
            vw = kvw_ref[pl.ds(w_start, wlen), vcols]
            s_w = _dot_nt(q_pad, kw) + _alibi_bias(mask_w, dist_w, k)
            e_w = jnp.exp(s_w - jnp.max(s_w, axis=-1, keepdims=True))
            o_w = _dot(e_w.astype(bf16), vw) / jnp.sum(e_w, axis=-1, keepdims=True)
            for r in range(rep):
                gc = (k * rep + r) * 3
                rows = slice(r * tq, (r + 1) * tq)
                o_r = gates[:, gc:gc + 1] * o_c[rows] + gates[:, gc + 1:gc + 2] * o_s[rows] + gates[:, gc + 2:gc + 3] * o_w[rows]
                out_tiles[r] = jnp.where(half, o_r, out_tiles[r])
        for r in range(rep):
            o_ref[:, (j * rep + r) * LANES:(j * rep + r + 1) * LANES] = out_tiles[r].astype(o_ref.dtype)


def _nsa_prompt_attn(q, gates, kcvc, kv4_b, kvw_b, bsz, t):
    nq = t // NSA_TQ
    nb = t // NSA_BLOCK
    return pl.pallas_call(
        functools.partial(_nsa_prompt_body, t_len=t),
        name="nsa_attn",
        grid=(bsz, nq),
        in_specs=[pl.BlockSpec((NSA_TQ, NSA_QD), lambda b, i: (b * nq + i, 0)),
                  pl.BlockSpec((NSA_TQ, LANES), lambda b, i: (b * nq + i, 0)),
                  pl.BlockSpec((1, nb, 2 * NSA_KVD), lambda b, i: (b, 0, 0)),
                  pl.BlockSpec((t, 2 * NSA_KVD), lambda b, i: (b, 1)),
                  pl.BlockSpec((t, 2 * NSA_KVD), lambda b, i: (b, 0))],
        out_specs=pl.BlockSpec((NSA_TQ, NSA_QD), lambda b, i: (b * nq + i, 0)),
        out_shape=jax.ShapeDtypeStruct((bsz * t, NSA_QD), bf16),
        compiler_params=_cparams(("parallel", "arbitrary")),
    )(q, gates, kcvc, kv4_b, kvw_b)


def _pair_layout_cols(w):
    lead = w.shape[:-1]
    w = w.reshape(*lead, NSA_KV // 2, 2, NSA_REP, NSA_DH)
    return jnp.swapaxes(w, -3, -2).reshape(*lead, NSA_QD)


def _nsa_weights(w_in, pos_w, w_out, wdt, pair):
    lay = _pair_layout_cols if pair else (lambda w: w)
    kv0 = NSA_QD
    g0 = NSA_QD + 6 * NSA_KVD
    wg = jnp.zeros((D_MODEL, LANES), f32).at[:, :3 * NSA_HEADS].set(w_in[:, g0:])
    return dict(
        w_q=lay(w_in[:, :NSA_QD]).astype(wdt),
        w_kv4=w_in[:, kv0:kv0 + 4 * NSA_KVD].astype(wdt),
        w_kvw=w_in[:, kv0 + 4 * NSA_KVD:g0].astype(wdt),
        w_g=wg.astype(wdt),
        pw=pos_w.transpose(1, 0, 2).reshape(NSA_BLOCK, 2 * NSA_KV),
        pwt=pos_w.transpose(0, 2, 1).reshape(2 * NSA_KV, NSA_BLOCK),
        w_kvt=w_in[:, kv0:g0].T.astype(wdt),
        w_out=lay(w_out.T).T.astype(wdt))


def _nsa_prompt(x2d, nw, bsz, t, ln_g, ln_b):
    (q,) = _mm(x2d, nw["w_q"], tm=512, tn=1024, out_dtypes=(bf16,), scale=NSA_DH ** -0.5, name="nsa_in_q")
    kv4, kv4_b = _mm(x2d, nw["w_kv4"], tm=512, tn=1024, out_dtypes=(f32, bf16), name="nsa_in_kv4")
    kvw, kvw_b = _mm(x2d, nw["w_kvw"], tm=512, tn=512, out_dtypes=(f32, bf16), name="nsa_in_kvw")
    (gates,) = _mm(x2d, nw["w_g"], tm=512, tn=LANES, out_dtypes=(f32,), act="sigmoid", name="nsa_in_gates")
    kcvc = _nsa_compress(kv4, nw["pw"], bsz, t)
    o = _nsa_prompt_attn(q, gates, kcvc, kv4_b, kvw_b, bsz, t)
    h1 = _mm_ln(o, nw["w_out"], x2d, ln_g, ln_b, tm=512, name="nsa_out_ln")
    w_keep = min(NSA_WINDOW, t)
    new_kv = kv4.reshape(bsz, t, 4, NSA_KV, NSA_DH)
    new_win = kvw.reshape(bsz, t, 2, NSA_KV, NSA_DH)[:, t - w_keep:]
    return h1, new_kv, new_win


PAGES_PER_STEP = 16


def _lane_weights(pwt_ref):
    w = jax.nn.softmax(pwt_ref[...], axis=1)
    return jnp.concatenate([w] * (PAGE_SIZE // NSA_BLOCK), axis=1)


def _page_compress_body(pt_ref, *refs):
    del pt_ref
    page_refs, pwt_ref, o_ref = refs[:PAGES_PER_STEP], refs[PAGES_PER_STEP], refs[PAGES_PER_STEP + 1]
    s = pl.program_id(1)
    bpp = PAGE_SIZE // NSA_BLOCK
    nbp = o_ref.shape[-1]

    @pl.when(s == 0)
    def _():
        o_ref[...] = jnp.zeros_like(o_ref)

    w = _lane_weights(pwt_ref)
    tok_blk = lax.broadcasted_iota(i32, (1, PAGE_SIZE), 1) // NSA_BLOCK
    out_lane = lax.broadcasted_iota(i32, (1, nbp), 1)
    for c in range(2):
        for k in range(NSA_KV):
            wk = w[c * NSA_KV + k:c * NSA_KV + k + 1]
            upd = jnp.zeros((NSA_DH, nbp), f32)
            for i, p_ref in enumerate(page_refs):
                xw = p_ref[0, c, k] * wk
                first = (s * PAGES_PER_STEP + i) * bpp
                for h in range(bpp):
                    r = jnp.sum(jnp.where(tok_blk == h, xw, 0.0), axis=1, keepdims=True)
                    upd = upd + jnp.where(out_lane == first + h, r, 0.0)
            o_ref[0, c, k] += upd


def _page_compress(pool_t, page_table, pwt):
    bsz, n_pages = page_table.shape
    nbp = n_pages * (PAGE_SIZE // NSA_BLOCK)
    steps = n_pages // PAGES_PER_STEP
    assert n_pages % PAGES_PER_STEP == 0
    page_specs = [pl.BlockSpec((1, 2, NSA_KV, NSA_DH, PAGE_SIZE),
                               lambda b, s, pt, i=i: (pt[b, s * PAGES_PER_STEP + i], 0, 0, 0, 0))
                  for i in range(PAGES_PER_STEP)]
    return pl.pallas_call(
        _page_compress_body,
        name="nsa_page_compress",
        grid_spec=pltpu.PrefetchScalarGridSpec(
            num_scalar_prefetch=1, grid=(bsz, steps),
            in_specs=page_specs + [pl.BlockSpec((2 * NSA_KV, NSA_BLOCK), lambda b, s, pt: (0, 0))],
            out_specs=pl.BlockSpec((1, 2, NSA_KV, NSA_DH, nbp), lambda b, s, pt: (b, 0, 0, 0, 0))),
        out_shape=jax.ShapeDtypeStruct((bsz, 2, NSA_KV, NSA_DH, nbp), f32),
        compiler_params=_cparams(("parallel", "arbitrary")),
    )(page_table, *([pool_t] * PAGES_PER_STEP), pwt)


def _mm_nt_body(wt_ref, x_ref, o_ref):
    o_ref[...] = _mxu_nt(wt_ref[...], x_ref[...], _is_f32(wt_ref))


def _mm_nt(wt, x, *, tn, name):
    n, k = wt.shape
    m = x.shape[0]
    return pl.pallas_call(
        _mm_nt_body, name=f"{name}_r{m}", grid=(n // tn,),
        in_specs=[pl.BlockSpec((tn, k), lambda j: (j, 0)), pl.BlockSpec((m, k), lambda j: (0, 0))],
        out_specs=pl.BlockSpec((tn, m), lambda j: (j, 0)),
        out_shape=jax.ShapeDtypeStruct((n, m), f32),
        compiler_params=_cparams(("parallel",)),
    )(wt, x)


def _step_slopes(kv_head):
    r = lax.broadcasted_iota(i32, (8, 1), 0)
    return jnp.exp2(-0.5 * (kv_head * NSA_REP + r + 1).astype(f32))


def _gate_col(gates, kv_head, branch):
    cols = [(kv_head * NSA_REP + r) * 3 + branch for r in range(NSA_REP)]
    return jnp.concatenate([gates[:, c:c + 1] for c in cols], axis=0)


def _own_col(x, b):
    lane = lax.broadcasted_iota(i32, x.shape, x.ndim - 1)
    return jnp.sum(jnp.where(lane == b, x, 0.0), axis=-1, keepdims=True)


def _new_key_tile(col):
    lane = lax.broadcasted_iota(i32, (col.shape[0], LANES), 1)
    return jnp.where(lane == 0, col, 0.0)


def _nsa_step_cw_body(q_ref, g_ref, cmp_ref, kvt_ref, pwt_ref, win_ref, o_ref, idx_ref, *, past_len):
    b = pl.program_id(0)
    rep, kv = NSA_REP, NSA_KV
    nbp = past_len // NSA_BLOCK
    nbt = nbp + 8
    w_buf = win_ref.shape[-1]
    w0 = jax.nn.softmax(pwt_ref[...], axis=1)[:, 0:1]
    lane_c = lax.broadcasted_iota(i32, (1, nbp + LANES), 1)
    dist_c = past_len - ((lane_c + 1) * NSA_BLOCK - 1)
    lane_w = lax.broadcasted_iota(i32, (1, w_buf + LANES), 1)
    dist_w = jnp.where(lane_w <= w_buf, w_buf - lane_w, -1)
    mask_w = (dist_w >= 0) & (dist_w < NSA_WINDOW)
    blk = lax.broadcasted_iota(i32, (1, nbt), 1)
    cur = past_len // NSA_BLOCK
    gates = g_ref[0]
    for k in range(kv):
        q8 = _pad_rows(q_ref[0, k])
        slope = _step_slopes(k)
        new = [_own_col(kvt_ref[s, k], b) for s in (0, 1, 4, 5)]
        k_c = jnp.concatenate([cmp_ref[0, 0, k], _new_key_tile(new[0] * w0[k:k + 1])], axis=1)
        v_c = jnp.concatenate([cmp_ref[0, 1, k], _new_key_tile(new[1] * w0[kv + k:kv + k + 1])], axis=1)
        s_c = _mxu(q8, k_c, True) - slope * dist_c.astype(f32)
        p_c = _masked_softmax(s_c, jnp.broadcast_to(dist_c >= 0, s_c.shape))
        o_c = _mxu_nt(p_c, v_c, True)
        imp = jnp.sum(p_c[0:rep], axis=0, keepdims=True)[:, :nbt]
        imp = jnp.where((blk == cur) | (blk == 0), 1e4, imp)
        imp = jnp.where(blk > cur, -2.0, imp)
        rank = _topk_rank(imp)
        pick = lax.broadcasted_iota(i32, (NSA_TOP, nbt), 0) == rank
        idx_ref[0, k] = jnp.sum(jnp.where(pick, blk.astype(f32), 0.0), axis=1, keepdims=True).astype(i32)
        k_w = jnp.concatenate([win_ref[0, 0, k], _new_key_tile(new[2])], axis=1)
        v_w = jnp.concatenate([win_ref[0, 1, k], _new_key_tile(new[3])], axis=1)
        s_w = _mxu(q8, k_w, True) - slope * dist_w.astype(f32)
        p_w = _masked_softmax(s_w, jnp.broadcast_to(mask_w, s_w.shape))
        o_w = _mxu_nt(p_w, v_w, True)
        o_ref[0, k] = _gate_col(gates, k, 0) * o_c[0:rep] + _gate_col(gates, k, 2) * o_w[0:rep]


def _nsa_step_sel_body(idx_ref, pt_ref, *refs, past_len):
    del pt_ref
    n_in = 2 * NSA_TOP
    tile_refs = refs[:n_in]
    q_ref, g_ref, kvt_ref, part_ref, o_ref = refs[n_in:]
    b, k = pl.program_id(0), pl.program_id(1)
    nbp = past_len // NSA_BLOCK
    rep, lb, kv = NSA_REP, NSA_BLOCK, NSA_KV
    bpp = PAGE_SIZE // lb
    new_k = _new_key_tile(_own_col(kvt_ref[0, 0], b))
    new_v = _new_key_tile(_own_col(kvt_ref[1, 0], b))
    lane = lax.broadcasted_iota(i32, (1, NSA_TOP * PAGE_SIZE), 1)
    tok = lane % PAGE_SIZE
    spos = tok % lb
    valid = lane < 0
    ks, vs = [], []
    for i in range(NSA_TOP):
        n = idx_ref[(b * kv + k) * NSA_TOP + i]
        is_new = n >= nbp
        ks.append(jnp.where(is_new, new_k, tile_refs[2 * i][0, 0, 0]))
        vs.append(jnp.where(is_new, new_v, tile_refs[2 * i + 1][0, 0, 0]))
        mine = lane // PAGE_SIZE == i
        spos = spos + jnp.where(mine, n * lb, 0)
        half = jnp.where(is_new, 0, n % bpp)
        last = jnp.where(is_new, 0, lb - 1)
        valid = valid | (mine & (tok // lb == half) & (tok % lb <= last))
    k_all = jnp.concatenate(ks, axis=1)
    v_all = jnp.concatenate(vs, axis=1)
    dist = past_len - spos
    q8 = _pad_rows(q_ref[0, 0])
    s = _mxu(q8, k_all, True) - _step_slopes(k) * dist.astype(f32)
    p = _masked_softmax(s, jnp.broadcast_to(valid & (dist >= 0), s.shape))
    o_s = _mxu_nt(p, v_all, True)
    g = g_ref[0, 0]
    g_sel = jnp.concatenate([g[:, 3 * r + 1:3 * r + 2] for r in range(rep)], axis=0)
    o_ref[0, 0] = part_ref[0, 0] + g_sel * o_s[0:rep]


def _nsa_step(x2d, pool, page_table, win_cache, nw, ln_g, ln_b):
    bsz = x2d.shape[0]
    n_pages = page_table.shape[1]
    past_len = n_pages * PAGE_SIZE
    w_buf = win_cache.shape[1]
    rep, kv, dh = NSA_REP, NSA_KV, NSA_DH
    bpp = PAGE_SIZE // NSA_BLOCK
    nbp = past_len // NSA_BLOCK
    (q,) = _mm(x2d, nw["w_q"], tm=bsz, tn=1024, out_dtypes=(f32,), scale=NSA_DH ** -0.5, name="nsa_in_q")
    (kv4,) = _mm(x2d, nw["w_kv4"], tm=bsz, tn=1024, out_dtypes=(f32,), name="nsa_in_kv4")
    (kvw,) = _mm(x2d, nw["w_kvw"], tm=bsz, tn=512, out_dtypes=(f32,), name="nsa_in_kvw")
    (gates,) = _mm(x2d, nw["w_g"], tm=bsz, tn=LANES, out_dtypes=(f32,), act="sigmoid", name="nsa_in_gates")
    kvt = _mm_nt(nw["w_kvt"], x2d, tn=512, name="nsa_in_kvt").reshape(6, kv, dh, bsz)
    pool_t = jnp.transpose(pool, (0, 2, 3, 4, 1))
    win_t = jnp.transpose(win_cache, (0, 2, 3, 4, 1))
    cmp = _page_compress(pool_t, page_table, nw["pwt"])
    q4 = q.reshape(bsz, kv, rep, dh)
    slab = pl.BlockSpec((1, kv, rep, dh), lambda b: (b, 0, 0, 0))
    whole = lambda shape: pl.BlockSpec(shape, lambda b: (0,) * len(shape))
    part, idx = pl.pallas_call(
        functools.partial(_nsa_step_cw_body, past_len=past_len),
        name="nsa_step_cw",
        grid=(bsz,),
        in_specs=[slab, pl.BlockSpec((1, 1, LANES), lambda b: (b, 0, 0)),
                  pl.BlockSpec((1, 2, kv, dh, nbp), lambda b: (b, 0, 0, 0, 0)), whole((6, kv, dh, bsz)),
                  whole((2 * kv, NSA_BLOCK)), pl.BlockSpec((1, 2, kv, dh, w_buf), lambda b: (b, 0, 0, 0, 0))],
        out_specs=[slab, pl.BlockSpec((1, kv, NSA_TOP, 1), lambda b: (b, 0, 0, 0))],
        out_shape=[jax.ShapeDtypeStruct((bsz, kv, rep, dh), f32),
                   jax.ShapeDtypeStruct((bsz, kv, NSA_TOP, 1), i32)],
        compiler_params=_cparams(("parallel",)),
    )(q4, gates.reshape(bsz, 1, -1), cmp, kvt, nw["pwt"], win_t)

    def tile_spec(i, slot):
        def imap(b, k, idx_r, pt_r):
            n = jnp.minimum(idx_r[(b * kv + k) * NSA_TOP + i], nbp - 1)
            return (pt_r[b * n_pages + n // bpp], slot, k, 0, 0)
        return pl.BlockSpec((1, 1, 1, dh, PAGE_SIZE), imap)

    tile_specs = []
    for i in range(NSA_TOP):
        tile_specs += [tile_spec(i, 2), tile_spec(i, 3)]
    head = pl.BlockSpec((1, 1, rep, dh), lambda b, k, *_: (b, k, 0, 0))
    o = pl.pallas_call(
        functools.partial(_nsa_step_sel_body, past_len=past_len),
        name="nsa_step_sel",
        grid_spec=pltpu.PrefetchScalarGridSpec(
            num_scalar_prefetch=2, grid=(bsz, kv),
            in_specs=tile_specs + [
                head, pl.BlockSpec((1, 1, 1, 3 * rep), lambda b, k, *_: (b, k, 0, 0)),
                pl.BlockSpec((2, 1, dh, bsz), lambda b, k, *_: (1, k, 0, 0)), head],
            out_specs=head),
        out_shape=jax.ShapeDtypeStruct((bsz, kv, rep, dh), f32),
        compiler_params=_cparams(("arbitrary", "arbitrary")),
    )(idx.reshape(-1), page_table.reshape(-1), *([pool_t] * (2 * NSA_TOP)),
      q4, gates[:, :3 * NSA_HEADS].reshape(bsz, kv, 1, 3 * rep), kvt, part)
    h1 = _mm_ln(o.reshape(bsz, NSA_QD), nw["w_out"], x2d, ln_g, ln_b, tm=bsz, name="nsa_out_ln")
    new_kv = kv4.reshape(bsz, 1, 4, kv, dh)
    new_win = jnp.concatenate([win_cache[:, 1:], kvw.reshape(bsz, 1, 2, kv, dh)], axis=1)
    return h1, new_kv, new_win


def kernel(x_prompt, x_sample, state_ssm, state_conv, cache_kv, cache_win, page_table, p_prompt, p_sample,
           ssd_w_in, ssd_conv_w, ssd_conv_b, ssd_dt_bias, ssd_a_log, ssd_d, ssd_norm_g, ssd_w_out,
           nsa_w_in, nsa_pos_w, nsa_w_out, ln1_g, ln1_b, ln2_g, ln2_b, router_w, router_bias,
           moe_w_gate, moe_w_up, moe_w_down, ple_proj, ple_gate):
    bp, t, d = x_prompt.shape
    bs = x_sample.shape[0]
    xp = x_prompt.reshape(bp * t, d)
    xs = x_sample.reshape(bs, d)
    rb = router_bias.reshape(N_EXPERTS, 1)
    ssm_p, conv_p, kv_p, win_p, ssm_s, conv_s, kv_s, win_s = [], [], [], [], [], [], [], []
    for i in range(DEPTH):
        j = i // 2
        if i % 2 == 0:
            ssd = (ssd_w_in[j], ssd_conv_w[j], ssd_conv_b[j], ssd_dt_bias[j], ssd_a_log[j], ssd_d[j], ssd_norm_g[j],
                   ssd_w_out[j])
            h1p, c_new, s_new = _ssd_prompt(xp, _ssd_weights(*ssd, bf16), bp, t, ln1_g[i], ln1_b[i])
            conv_p.append(c_new)
            ssm_p.append(s_new)
            h1s, c_new, s_new = _ssd_step(xs, state_conv[j], state_ssm[j], _ssd_weights(*ssd, f32), ln1_g[i], ln1_b[i])
            conv_s.append(c_new)
            ssm_s.append(s_new)
        else:
            nsa = (nsa_w_in[j], nsa_pos_w[j], nsa_w_out[j])
            h1p, r_new, w_new = _nsa_prompt(xp, _nsa_weights(*nsa, bf16, True), bp, t, ln1_g[i], ln1_b[i])
            kv_p.append(r_new)
            win_p.append(w_new)
            h1s, r_new, w_new = _nsa_step(xs, cache_kv[j], page_table, cache_win[j], _nsa_weights(*nsa, f32, False),
                                          ln1_g[i], ln1_b[i])
            kv_s.append(r_new)
            win_s.append(w_new)

        def tail_weights(wdt, i=i):
            return dict(rwt=router_w.T.astype(wdt), rb=rb, layer=i, wg=moe_w_gate.astype(wdt), wu=moe_w_up.astype(wdt),
                        wd=moe_w_down.astype(wdt), ln2_g=ln2_g[i], ln2_b=ln2_b[i],
                        ple_gate=ple_gate[i].astype(wdt), ple_proj=ple_proj[i].astype(wdt))

        xp = _layer_tail(h1p, p_prompt[i].reshape(bp * t, PLE_DIM), tail_weights(bf16), tm=512)
        xs = _layer_tail(h1s, p_sample[i].reshape(bs, PLE_DIM), tail_weights(f32), tm=bs)
    return (xp.reshape(bp, t, d), xs.reshape(bs, 1, d), jnp.stack(ssm_p), jnp.stack(conv_p), jnp.stack(kv_p),
            jnp.stack(win_p), jnp.stack(ssm_s), jnp.stack(conv_s), jnp.stack(kv_s), jnp.stack(win_s))
```

```python
import functools

import jax
import jax.numpy as jnp
from jax import lax
from jax.experimental import pallas as pl
from jax.experimental.pallas import tpu as pltpu

f32, bf16, i32 = jnp.float32, jnp.bfloat16, jnp.int32
HIGHEST = lax.Precision.HIGHEST

D_MODEL = 1024
DEPTH = 2
PLE_DIM = 256
SSD_D_INNER = 2048
SSD_HEADDIM = 64
SSD_HEADS = 32
SSD_GROUPS = 4
SSD_HPG = 8
SSD_STATE = 128
SSD_CONV = 4
SSD_CHUNK = 128
SSD_GW = SSD_HPG * SSD_HEADDIM
SSD_CONV_DIM = SSD_D_INNER + 2 * SSD_GROUPS * SSD_STATE
NSA_HEADS = 16
NSA_KV = 4
NSA_REP = 4
NSA_DH = 64
NSA_BLOCK = 64
NSA_TOP = 16
NSA_WINDOW = 512
NSA_QD = NSA_HEADS * NSA_DH
NSA_KVD = NSA_KV * NSA_DH
PAGE_SIZE = 128
N_EXPERTS = 16
N_EGROUPS = 4
EPG = 4
D_FF = 512
ALPHA = (2.0 * DEPTH) ** 0.25
LN_EPS = 1e-5
NEG = -1e30
LANES = 128
VMEM_LIMIT = 56 * 1024 * 1024
PROJ_TM = 1024


def _cparams(sem):
    return pltpu.CompilerParams(dimension_semantics=sem, vmem_limit_bytes=VMEM_LIMIT)


def _ln(v, g, b):
    mu = jnp.mean(v, axis=-1, keepdims=True)
    d = v - mu
    var = jnp.mean(d * d, axis=-1, keepdims=True)
    return d * lax.rsqrt(var + LN_EPS) * g + b


def _dot(a, b):
    return jnp.dot(a, b, preferred_element_type=f32)


def _mxu(a, b, precise):
    if precise:
        return jnp.dot(a.astype(f32), b.astype(f32), precision=HIGHEST, preferred_element_type=f32)
    return jnp.dot(a.astype(bf16), b.astype(bf16), preferred_element_type=f32)


def _mxu_nt(a, b, precise):
    if precise:
        return _dot_nt(a.astype(f32), b.astype(f32), precision=HIGHEST)
    return _dot_nt(a.astype(bf16), b.astype(bf16))


def _dot_nt(a, b, precision=None):
    return lax.dot_general(a, b, (((1,), (1,)), ((), ())), precision=precision, preferred_element_type=f32)


def _dot_tn(a, b, precision=None):
    return lax.dot_general(a, b, (((0,), (0,)), ((), ())), precision=precision, preferred_element_type=f32)


def _is_f32(ref):
    return ref.dtype == jnp.float32


def _mm_body(x_ref, w_ref, *o_refs, act, scale):
    acc = _mxu(x_ref[...], w_ref[...], _is_f32(w_ref))
    if scale != 1.0:
        acc = acc * scale
    if act == "sigmoid":
        acc = jax.nn.sigmoid(acc)
    for o_ref in o_refs:
        o_ref[...] = acc.astype(o_ref.dtype)


def _mm(x, w, *, tm, tn, out_dtypes, name, act=None, scale=1.0):
    m, k = x.shape
    n = w.shape[1]
    tm, tn = min(tm, m), min(tn, n)
    assert m % tm == 0 and n % tn == 0
    outs = pl.pallas_call(
        functools.partial(_mm_body, act=act, scale=scale),
        name=f"{name}_r{m}",
        grid=(m // tm, n // tn),
        in_specs=[pl.BlockSpec((tm, k), lambda i, j: (i, 0)), pl.BlockSpec((k, tn), lambda i, j: (0, j))],
        out_specs=[pl.BlockSpec((tm, tn), lambda i, j: (i, j)) for _ in out_dtypes],
        out_shape=[jax.ShapeDtypeStruct((m, n), dt) for dt in out_dtypes],
        compiler_params=_cparams(("parallel", "arbitrary")),
    )(x, w)
    return outs


def _mm_ln_body(x_ref, w_ref, res_ref, g_ref, b_ref, o_ref):
    acc = _mxu(x_ref[...], w_ref[...], _is_f32(w_ref))
    o_ref[...] = _ln(ALPHA * res_ref[...] + acc, g_ref[...], b_ref[...])


def _mm_ln(x, w, res, g, b, *, tm, name):
    m, k = x.shape
    n = w.shape[1]
    tm = min(tm, m)
    assert m % tm == 0
    return pl.pallas_call(
        _mm_ln_body,
        name=f"{name}_r{m}",
        grid=(m // tm,),
        in_specs=[pl.BlockSpec((tm, k), lambda i: (i, 0)), pl.BlockSpec((k, n), lambda i: (0, 0)),
                  pl.BlockSpec((tm, n), lambda i: (i, 0)), pl.BlockSpec((1, n), lambda i: (0, 0)),
                  pl.BlockSpec((1, n), lambda i: (0, 0))],
        out_specs=pl.BlockSpec((tm, n), lambda i: (i, 0)),
        out_shape=jax.ShapeDtypeStruct((m, n), f32),
        compiler_params=_cparams(("parallel",)),
    )(x, w, res, g.reshape(1, n), b.reshape(1, n))


def _router_body(h_ref, rwt_ref, rb_ref, gate_ref, best_ref):
    logits = _mxu_nt(rwt_ref[...], h_ref[...], _is_f32(rwt_ref))
    score = jax.nn.sigmoid(logits)
    sel = score + rb_ref[...]
    gsum = []
    for g in range(N_EGROUPS):
        a, b, c, d = (sel[EPG * g + i:EPG * g + i + 1] for i in range(EPG))
        hi1, lo1, hi2, lo2 = jnp.maximum(a, b), jnp.minimum(a, b), jnp.maximum(c, d), jnp.minimum(c, d)
        gsum.append(jnp.maximum(hi1, hi2) + jnp.maximum(jnp.minimum(hi1, hi2), jnp.maximum(lo1, lo2)))
    best = jnp.zeros_like(gsum[0], dtype=i32)
    top = gsum[0]
    for g in range(1, N_EGROUPS):
        upd = gsum[g] > top
        best = jnp.where(upd, g, best)
        top = jnp.where(upd, gsum[g], top)
    selg = sel[0:EPG]
    scg = score[0:EPG]
    for g in range(1, N_EGROUPS):
        selg = jnp.where(best == g, sel[EPG * g:EPG * (g + 1)], selg)
        scg = jnp.where(best == g, score[EPG * g:EPG * (g + 1)], scg)
    rows = [selg[i:i + 1] for i in range(EPG)]
    chosen = []
    for i in range(EPG):
        rank = jnp.zeros_like(best)
        for j in range(EPG):
            if j == i:
                continue
            ahead = (rows[j] > rows[i]) | ((rows[j] == rows[i]) if j < i else False)
            rank = rank + ahead.astype(i32)
        chosen.append(rank < 2)
    wsum = sum(jnp.where(chosen[i], scg[i:i + 1], 0.0) for i in range(EPG))
    gates = [jnp.where(chosen[i], scg[i:i + 1] / wsum, 0.0) for i in range(EPG)]
    out_rows = []
    for g in range(N_EGROUPS):
        for i in range(EPG):
            out_rows.append(jnp.where(best == g, gates[i], 0.0))
    gate_ref[...] = jnp.concatenate(out_rows, axis=0).T
    best_ref[...] = best


def _router(h, rwt, rb, *, tm):
    m, d = h.shape
    tm = min(tm, m)
    return pl.pallas_call(
        _router_body,
        name=f"router_r{m}",
        grid=(m // tm,),
        in_specs=[pl.BlockSpec((tm, d), lambda i: (i, 0)), pl.BlockSpec((N_EXPERTS, d), lambda i: (0, 0)),
                  pl.BlockSpec((N_EXPERTS, 1), lambda i: (0, 0))],
        out_specs=[pl.BlockSpec((tm, N_EXPERTS), lambda i: (i, 0)), pl.BlockSpec((1, tm), lambda i: (0, i))],
        out_shape=[jax.ShapeDtypeStruct((m, N_EXPERTS), f32), jax.ShapeDtypeStruct((1, m), i32)],
        compiler_params=_cparams(("parallel",)),
    )(h, rwt, rb)


def _moe_body(h_ref, gate_ref, wg_ref, wu_ref, wd_ref, g2_ref, b2_ref, o_ref, acc_ref):
    e = pl.program_id(1)

    @pl.when(e == 0)
    def _():
        acc_ref[...] = jnp.zeros_like(acc_ref)

    precise = _is_f32(wg_ref)
    h = h_ref[...]
    hid = jax.nn.silu(_mxu(h, wg_ref[0, 0], precise)) * _mxu(h, wu_ref[0, 0], precise)
    out = _mxu(hid, wd_ref[0, 0], precise)
    gate = gate_ref[...]
    lane = lax.broadcasted_iota(i32, gate.shape, 1)
    gcol = jnp.sum(jnp.where(lane == e, gate, 0.0), axis=1, keepdims=True)
    acc_ref[...] += gcol * out

    @pl.when(e == N_EXPERTS - 1)
    def _():
        o_ref[...] = _ln(ALPHA * h_ref[...] + acc_ref[...], g2_ref[...], b2_ref[...])


def _moe_ln(h, gate, wg, wu, wd, g2, b2, *, tm, layer):
    m, d = h.shape
    tm = min(tm, m)
    return pl.pallas_call(
        _moe_body,
        name=f"moe_ln_r{m}",
        grid=(m // tm, N_EXPERTS),
        in_specs=[pl.BlockSpec((tm, d), lambda i, e: (i, 0)), pl.BlockSpec((tm, N_EXPERTS), lambda i, e: (i, 0)),
                  pl.BlockSpec((1, 1, d, D_FF), lambda i, e: (layer, e, 0, 0)),
                  pl.BlockSpec((1, 1, d, D_FF), lambda i, e: (layer, e, 0, 0)),
                  pl.BlockSpec((1, 1, D_FF, d), lambda i, e: (layer, e, 0, 0)),
                  pl.BlockSpec((1, d), lambda i, e: (0, 0)), pl.BlockSpec((1, d), lambda i, e: (0, 0))],
        out_specs=pl.BlockSpec((tm, d), lambda i, e: (i, 0)),
        out_shape=jax.ShapeDtypeStruct((m, d), f32),
        scratch_shapes=[pltpu.VMEM((tm, d), f32)],
        compiler_params=_cparams(("parallel", "arbitrary")),
    )(h, gate, wg, wu, wd, g2.reshape(1, d), b2.reshape(1, d))


def _ple_body(h_ref, p_ref, wg_ref, wp_ref, o_ref):
    h = h_ref[...]
    precise = _is_f32(wg_ref)
    gate = jax.nn.sigmoid(_mxu(h, wg_ref[...], precise))
    o_ref[...] = h + gate * _mxu(p_ref[...], wp_ref[...], precise)


def _ple(h, p, wg, wp, *, tm):
    m, d = h.shape
    tm = min(tm, m)
    return pl.pallas_call(
        _ple_body,
        name=f"ple_r{m}",
        grid=(m // tm,),
        in_specs=[pl.BlockSpec((tm, d), lambda i: (i, 0)), pl.BlockSpec((tm, PLE_DIM), lambda i: (i, 0)),
                  pl.BlockSpec((d, d), lambda i: (0, 0)), pl.BlockSpec((PLE_DIM, d), lambda i: (0, 0))],
        out_specs=pl.BlockSpec((tm, d), lambda i: (i, 0)),
        out_shape=jax.ShapeDtypeStruct((m, d), f32),
        compiler_params=_cparams(("parallel",)),
    )(h, p, wg, wp)


MOE_TM = 1024
MOE_CH = 304
MOE_ALIGN = 16
MOE_ROWS = MOE_TM + N_EGROUPS * MOE_ALIGN + MOE_CH


def _moe_grouped_body(h_ref, gate_ref, best_ref, wg_ref, wu_ref, wd_ref, g2_ref, b2_ref, o_ref,
                      p_ref, xs_ref, ys_ref, gs_ref, meta_ref):
    e = pl.program_id(1)
    tm, rows, ng = MOE_TM, MOE_ROWS, N_EGROUPS

    @pl.when(e == 0)
    def _():
        best = best_ref[...]
        ind = (best == lax.broadcasted_iota(i32, (8, tm), 0)).astype(f32)
        upper = (lax.broadcasted_iota(i32, (tm, tm), 0) <= lax.broadcasted_iota(i32, (tm, tm), 1)).astype(bf16)
        prefix = _dot(ind.astype(bf16), upper)
        cnt = prefix[:, tm - 1:tm]
        cnt_al = jnp.floor((cnt + (MOE_ALIGN - 1)) * (1.0 / MOE_ALIGN)) * MOE_ALIGN
        offs = [jnp.zeros((1, 1), f32)]
        for g in range(1, ng):
            offs.append(offs[-1] + cnt_al[g - 1:g])
        off_col = jnp.concatenate(offs + [jnp.zeros((8 - ng, 1), f32)], axis=0)
        pos = jnp.sum(ind * (off_col + prefix - 1.0), axis=0, keepdims=True).astype(i32)
        perm = (lax.broadcasted_iota(i32, (rows, tm), 0) == pos).astype(bf16)
        p_ref[...] = perm
        xs_ref[...] = _dot(perm, h_ref[...].astype(bf16)).astype(bf16)
        gs_ref[...] = _dot_exact01(gate_ref[...], perm, left=True)
        ys_ref[...] = jnp.zeros_like(ys_ref)
        for g in range(ng):
            meta_ref[g] = offs[g][0, 0].astype(i32)
            meta_ref[ng + g] = cnt[g:g + 1][0, 0].astype(i32)

    g = e // EPG
    start = meta_ref[g]
    n_ch = (meta_ref[ng + g] + (MOE_CH - 1)) // MOE_CH

    def chunk(c, carry):
        r0 = pl.multiple_of(start + c * MOE_CH, MOE_ALIGN)
        x = xs_ref[pl.ds(r0, MOE_CH), :]
        hid = jax.nn.silu(_dot(x, wg_ref[0, 0])) * _dot(x, wu_ref[0, 0])
        y = _dot(hid.astype(bf16), wd_ref[0, 0])
        gs = gs_ref[pl.ds(r0, MOE_CH), :]
        lane = lax.broadcasted_iota(i32, gs.shape, 1)
        gcol = jnp.sum(jnp.where(lane == e, gs, 0.0), axis=1, keepdims=True)
        ys_ref[pl.ds(r0, MOE_CH), :] += gcol * y
        return carry

    lax.fori_loop(0, n_ch, chunk, 0)

    @pl.when(e == N_EXPERTS - 1)
    def _():
        ys = ys_ref[...]
        hi = ys.astype(bf16)
        lo = (ys - hi.astype(f32)).astype(bf16)
        perm = p_ref[...]
        out = _dot_tn(perm, hi) + _dot_tn(perm, lo)
        o_ref[...] = _ln(ALPHA * h_ref[...] + out, g2_ref[...], b2_ref[...])


def _moe_grouped_ln(h, gate, best, wg, wu, wd, g2, b2, *, layer):
    m, d = h.shape
    tm = MOE_TM
    assert m % tm == 0
    return pl.pallas_call(
        _moe_grouped_body,
        name=f"moe_grouped_ln_r{m}",
        grid=(m // tm, N_EXPERTS),
        in_specs=[pl.BlockSpec((tm, d), lambda i, e: (i, 0)), pl.BlockSpec((tm, N_EXPERTS), lambda i, e: (i, 0)),
                  pl.BlockSpec((1, tm), lambda i, e: (0, i)),
                  pl.BlockSpec((1, 1, d, D_FF), lambda i, e: (layer, e, 0, 0)),
                  pl.BlockSpec((1, 1, d, D_FF), lambda i, e: (layer, e, 0, 0)),
                  pl.BlockSpec((1, 1, D_FF, d), lambda i, e: (layer, e, 0, 0)),
                  pl.BlockSpec((1, d), lambda i, e: (0, 0)), pl.BlockSpec((1, d), lambda i, e: (0, 0))],
        out_specs=pl.BlockSpec((tm, d), lambda i, e: (i, 0)),
        out_shape=jax.ShapeDtypeStruct((m, d), f32),
        scratch_shapes=[pltpu.VMEM((MOE_ROWS, tm), bf16), pltpu.VMEM((MOE_ROWS, d), bf16),
                        pltpu.VMEM((MOE_ROWS, d), f32), pltpu.VMEM((MOE_ROWS, N_EXPERTS), f32),
                        pltpu.SMEM((2 * N_EGROUPS,), i32)],
        compiler_params=_cparams(("arbitrary", "arbitrary")),
    )(h, gate, best, wg, wu, wd, g2.reshape(1, d), b2.reshape(1, d))


def _layer_tail(h1, p, tw, *, tm):
    gate, best = _router(h1, tw["rwt"], tw["rb"], tm=tm)
    if h1.shape[0] % MOE_TM == 0 and tw["wg"].dtype == bf16:
        h2 = _moe_grouped_ln(h1, gate, best, tw["wg"], tw["wu"], tw["wd"], tw["ln2_g"], tw["ln2_b"], layer=tw["layer"])
    else:
        h2 = _moe_ln(h1, gate, tw["wg"], tw["wu"], tw["wd"], tw["ln2_g"], tw["ln2_b"], tm=tm, layer=tw["layer"])
    return _ple(h2, p, tw["ple_gate"], tw["ple_proj"], tm=tm)


def _ssd_conv(x_ref, xp_ref, w_ref, b_ref):
    q = x_ref.shape[0]
    xp_ref[8:8 + q, :] = x_ref[...]
    w = w_ref[...]
    acc = b_ref[...] + xp_ref[5:5 + q, :] * w[0:1]
    for k in range(1, SSD_CONV):
        acc = acc + xp_ref[5 + k:5 + k + q, :] * w[k:k + 1]
    xp_ref[0:8, :] = xp_ref[q:q + 8, :]
    return jax.nn.silu(acc)


def _head_expand(width):
    r = lax.broadcasted_iota(i32, (SSD_HPG, SSD_HPG * width), 0)
    c = lax.broadcasted_iota(i32, (SSD_HPG, SSD_HPG * width), 1)
    return (c // width == r).astype(bf16)


def _split3(v):
    hi = v.astype(bf16)
    r1 = v - hi.astype(f32)
    mid = r1.astype(bf16)
    return hi, mid, (r1 - mid.astype(f32)).astype(bf16)


def _dot_exact01(v, onehot, left=False):
    parts = _split3(v)
    prods = [_dot(onehot, p) if left else _dot(p, onehot) for p in parts]
    return (prods[0] + prods[1]) + prods[2]


def _ssd_body(xs_ref, bm_ref, cm_ref, z_ref, dt_ref, dtt_ref, wx_ref, wb_ref, wc_ref, bx_ref, bb_ref, bc_ref,
              dtb_ref, dtbt_ref, alog_ref, alogt_ref, dsk_ref, ng_ref, y_ref, st_ref,
              stt_ref, xpx_ref, xpb_ref, xpc_ref):
    c = pl.program_id(2)
    q = SSD_CHUNK

    @pl.when(c == 0)
    def _():
        stt_ref[...] = jnp.zeros_like(stt_ref)
        xpx_ref[0:8, :] = jnp.zeros((8, xpx_ref.shape[1]), f32)
        xpb_ref[0:8, :] = jnp.zeros((8, xpb_ref.shape[1]), f32)
        xpc_ref[0:8, :] = jnp.zeros((8, xpc_ref.shape[1]), f32)

    xs = _ssd_conv(xs_ref, xpx_ref, wx_ref, bx_ref)
    bm = _ssd_conv(bm_ref, xpb_ref, wb_ref, bb_ref)
    cm = _ssd_conv(cm_ref, xpc_ref, wc_ref, bc_ref)
    dt = jax.nn.softplus(dt_ref[0] + dtb_ref[0])
    dtt = jax.nn.softplus(dtt_ref[0] + dtbt_ref[0])
    dta = dt * (-jnp.exp(alog_ref[0]))
    dtat = dtt * (-jnp.exp(alogt_ref[0]))
    row = lax.broadcasted_iota(i32, (q, q), 0)
    col = lax.broadcasted_iota(i32, (q, q), 1)
    causal = row >= col
    cum = _dot_exact01(dta, causal.astype(bf16), left=True)
    cumt = _dot_exact01(dtat, (row <= col).astype(bf16))
    both = _dot_exact01(jnp.concatenate([dt, cum], axis=0), _head_expand(SSD_HEADDIM))
    dtx, cumx = both[:q], both[q:]
    cum128 = _dot_exact01(cum, _head_expand(q))
    cum_last = cumx[q - 1:q, :]
    xdt = xs * dtx
    xdt_b = xdt.astype(bf16)
    bm_b = bm.astype(bf16)
    cm_b = cm.astype(bf16)
    cb = _dot_nt(cm_b, bm_b)
    head = lax.broadcasted_iota(i32, (q, SSD_GW), 1) // SSD_HEADDIM
    y = jnp.zeros((q, SSD_GW), f32)
    for r in range(SSD_HPG):
        seg = cum128[:, r * q:(r + 1) * q] - cumt[r:r + 1, :]
        decay = jnp.where(causal, jnp.exp(jnp.where(causal, seg, 0.0)), 0.0)
        yr = _dot((cb * decay).astype(bf16), xdt_b)
        y = jnp.where(head == r, yr, y)
    stt = stt_ref[...]
    y = y + _dot(cm_b, stt.astype(bf16)) * jnp.exp(cumx) + dsk_ref[0] * xs
    to_end = jnp.exp(cum_last - cumx)
    stt_new = stt * jnp.exp(cum_last) + _dot_tn(bm_b, (xdt * to_end).astype(bf16))
    stt_ref[...] = stt_new
    yz = y * jax.nn.silu(z_ref[...])
    yn = yz * lax.rsqrt(jnp.mean(yz * yz, axis=-1, keepdims=True) + 1e-5) * ng_ref[...]
    y_ref[...] = yn.astype(y_ref.dtype)

    @pl.when(c == pl.num_programs(2) - 1)
    def _():
        st_ref[0] = stt_new.T.reshape(SSD_HPG, SSD_HEADDIM, SSD_STATE)


def _ssd_prompt_scan(xbc, z, dt_raw, sw, bsz, t):
    m = bsz * t
    nc = t // SSD_CHUNK
    q, gw, n, g_, hpg = SSD_CHUNK, SSD_GW, SSD_STATE, SSD_GROUPS, SSD_HPG
    dt_g = dt_raw.reshape(m, g_, hpg).transpose(1, 0, 2)
    dtt_g = dt_g.transpose(0, 2, 1)
    nxb = SSD_D_INNER // n
    row = lambda b, g, c: b * nc + c
    in_specs = [
        pl.BlockSpec((q, gw), lambda b, g, c: (row(b, g, c), g)),
        pl.BlockSpec((q, n), lambda b, g, c: (row(b, g, c), nxb + g)),
        pl.BlockSpec((q, n), lambda b, g, c: (row(b, g, c), nxb + g_ + g)),
        pl.BlockSpec((q, gw), lambda b, g, c: (row(b, g, c), g)),
        pl.BlockSpec((1, q, hpg), lambda b, g, c: (g, row(b, g, c), 0)),
        pl.BlockSpec((1, hpg, q), lambda b, g, c: (g, 0, row(b, g, c))),
        pl.BlockSpec((SSD_CONV, gw), lambda b, g, c: (0, g)),
        pl.BlockSpec((SSD_CONV, n), lambda b, g, c: (0, nxb + g)),
        pl.BlockSpec((SSD_CONV, n), lambda b, g, c: (0, nxb + g_ + g)),
        pl.BlockSpec((1, gw), lambda b, g, c: (0, g)),
        pl.BlockSpec((1, n), lambda b, g, c: (0, nxb + g)),
        pl.BlockSpec((1, n), lambda b, g, c: (0, nxb + g_ + g)),
        pl.BlockSpec((1, 1, hpg), lambda b, g, c: (g, 0, 0)),
        pl.BlockSpec((1, hpg, 1), lambda b, g, c: (g, 0, 0)),
        pl.BlockSpec((1, 1, hpg), lambda b, g, c: (g, 0, 0)),
        pl.BlockSpec((1, hpg, 1), lambda b, g, c: (g, 0, 0)),
        pl.BlockSpec((1, 1, gw), lambda b, g, c: (g, 0, 0)),
        pl.BlockSpec((1, gw), lambda b, g, c: (0, g)),
    ]
    y, st = pl.pallas_call(
        _ssd_body,
        name="ssd_scan",
        grid=(bsz, g_, nc),
        in_specs=in_specs,
        out_specs=[pl.BlockSpec((q, gw), lambda b, g, c: (row(b, g, c), g)),
                   pl.BlockSpec((1, hpg, SSD_HEADDIM, n), lambda b, g, c: (b, g, 0, 0))],
        out_shape=[jax.ShapeDtypeStruct((m, SSD_D_INNER), bf16),
                   jax.ShapeDtypeStruct((bsz, SSD_HEADS, SSD_HEADDIM, n), f32)],
        scratch_shapes=[pltpu.VMEM((n, gw), f32), pltpu.VMEM((q + 8, gw), f32),
                        pltpu.VMEM((q + 8, n), f32), pltpu.VMEM((q + 8, n), f32)],
        compiler_params=_cparams(("arbitrary", "arbitrary", "arbitrary")),
    )(xbc, xbc, xbc, z, dt_g, dtt_g, sw["conv_w"], sw["conv_w"], sw["conv_w"], sw["conv_b"], sw["conv_b"], sw["conv_b"],
      sw["dtb"], sw["dtbt"], sw["alog"], sw["alogt"], sw["dskx"], sw["norm_g"])
    return y, st


def _ssd_weights(w_in, conv_w, conv_b, dt_bias, a_log, d_skip, norm_g, w_out, wdt):
    g_, hpg = SSD_GROUPS, SSD_HPG
    return dict(
        w_z=w_in[:, :SSD_D_INNER].astype(wdt),
        w_xbc=w_in[:, SSD_D_INNER:SSD_D_INNER + SSD_CONV_DIM].astype(wdt),
        w_dt=w_in[:, SSD_D_INNER + SSD_CONV_DIM:].astype(wdt),
        conv_w=conv_w, conv_b=conv_b.reshape(1, SSD_CONV_DIM),
        dtb=dt_bias.reshape(g_, 1, hpg), dtbt=dt_bias.reshape(g_, hpg, 1),
        alog=a_log.reshape(g_, 1, hpg), alogt=a_log.reshape(g_, hpg, 1),
        dskx=jnp.repeat(d_skip, SSD_HEADDIM).reshape(g_, 1, SSD_GW),
        dsk=d_skip, dt_bias=dt_bias, a_log=a_log,
        norm_g=norm_g.reshape(1, SSD_D_INNER), w_out=w_out.astype(wdt))


def _ssd_prompt(x2d, sw, bsz, t, ln_g, ln_b):
    (z,) = _mm(x2d, sw["w_z"], tm=PROJ_TM, tn=1024, out_dtypes=(f32,), name="ssd_in_z")
    (xbc,) = _mm(x2d, sw["w_xbc"], tm=PROJ_TM, tn=1024, out_dtypes=(f32,), name="ssd_in_xbc")
    (dt_raw,) = _mm(x2d, sw["w_dt"], tm=PROJ_TM, tn=SSD_HEADS, out_dtypes=(f32,), name="ssd_in_dt")
    y, st = _ssd_prompt_scan(xbc, z, dt_raw, sw, bsz, t)
    h1 = _mm_ln(y, sw["w_out"], x2d, ln_g, ln_b, tm=512, name="ssd_out_ln")
    new_conv = xbc.reshape(bsz, t, SSD_CONV_DIM)[:, t - (SSD_CONV - 1):]
    return h1, new_conv, st


def _pad_rows(x, rows=8):
    return jnp.concatenate([x, jnp.zeros((rows - x.shape[0], x.shape[1]), x.dtype)], axis=0)


def _ssd_step_body(z_ref, xbc_ref, dt_ref, cs_ref, st_ref, cw_ref, cb_ref, dtb_ref, alog_ref, dsk_ref, ng_ref,
                   y_ref, nc_ref, ns_ref):
    n, gw, hd = SSD_STATE, SSD_GW, SSD_HEADDIM
    xbc = xbc_ref[0]
    cs = cs_ref[0]
    w = cw_ref[...]
    conv = cb_ref[...] + cs[0:1] * w[0:1]
    for k in range(1, SSD_CONV - 1):
        conv = conv + cs[k:k + 1] * w[k:k + 1]
    conv = jax.nn.silu(conv + xbc * w[SSD_CONV - 1:SSD_CONV])
    nc_ref[0] = jnp.concatenate([cs[1:], xbc], axis=0)
    xs = conv[:, :SSD_D_INNER]
    b_g = _pad_rows(jnp.concatenate(
        [conv[:, SSD_D_INNER + g * n:SSD_D_INNER + (g + 1) * n] for g in range(SSD_GROUPS)], axis=0))
    c_g = _pad_rows(jnp.concatenate(
        [conv[:, SSD_D_INNER + (SSD_GROUPS + g) * n:SSD_D_INNER + (SSD_GROUPS + g + 1) * n] for g in range(SSD_GROUPS)],
        axis=0))
    dt = jax.nn.softplus(dt_ref[0] + dtb_ref[...])
    decay = jnp.exp(dt * (-jnp.exp(alog_ref[...])))
    er = lax.broadcasted_iota(i32, (SSD_HEADS, SSD_D_INNER), 0)
    ec = lax.broadcasted_iota(i32, (SSD_HEADS, SSD_D_INNER), 1)
    per_head = _pad_rows(jnp.concatenate([dt, decay, dsk_ref[...]], axis=0))
    hx = jnp.dot(per_head, (ec // hd == er).astype(f32), precision=HIGHEST, preferred_element_type=f32)
    dtx, decx, dskx = hx[0:1], hx[1:2], hx[2:3]
    xdt = xs * dtx
    gr = lax.broadcasted_iota(i32, (8, SSD_D_INNER), 0)
    gc = lax.broadcasted_iota(i32, (8, SSD_D_INNER), 1)
    gmask = (gc // gw == gr).astype(f32)
    row0 = (gr == 0).astype(f32)
    st = st_ref[0].reshape(SSD_D_INNER, n)
    upd = _dot_tn(gmask * xdt, b_g, precision=HIGHEST)
    dec_full = _dot_tn(row0 * decx, jnp.ones((8, n), f32), precision=HIGHEST)
    ns_ref[0] = (st * dec_full + upd).reshape(SSD_HEADS, hd, n)
    cst = _dot_nt(c_g, st, precision=HIGHEST)
    y_state = jnp.sum(gmask * cst, axis=0, keepdims=True)
    cbx = jnp.sum(gmask * jnp.sum(c_g * b_g, axis=1, keepdims=True), axis=0, keepdims=True)
    y = cbx * xdt + decx * y_state + dskx * xs
    yz = y * jax.nn.silu(z_ref[0])
    parts = []
    for g in range(SSD_GROUPS):
        seg = yz[:, g * gw:(g + 1) * gw]
        parts.append(seg * lax.rsqrt(jnp.mean(seg * seg, axis=-1, keepdims=True) + 1e-5))
    y_ref[0] = jnp.concatenate(parts, axis=1) * ng_ref[...]


def _ssd_step(x2d, conv_state, ssm_state, sw, ln_g, ln_b):
    bsz = x2d.shape[0]
    (z,) = _mm(x2d, sw["w_z"], tm=bsz, tn=1024, out_dtypes=(f32,), name="ssd_in_z")
    (xbc,) = _mm(x2d, sw["w_xbc"], tm=bsz, tn=1024, out_dtypes=(f32,), name="ssd_in_xbc")
    (dt_raw,) = _mm(x2d, sw["w_dt"], tm=bsz, tn=SSD_HEADS, out_dtypes=(f32,), name="ssd_in_dt")
    rowspec = lambda width: pl.BlockSpec((1, 1, width), lambda b: (b, 0, 0))
    full = lambda r, c: pl.BlockSpec((r, c), lambda b: (0, 0))
    y, new_conv, new_state = pl.pallas_call(
        _ssd_step_body,
        name="ssd_step",
        grid=(bsz,),
        in_specs=[rowspec(SSD_D_INNER), rowspec(SSD_CONV_DIM), rowspec(SSD_HEADS),
                  pl.BlockSpec((1, SSD_CONV - 1, SSD_CONV_DIM), lambda b: (b, 0, 0)),
                  pl.BlockSpec((1, SSD_HEADS, SSD_HEADDIM, SSD_STATE), lambda b: (b, 0, 0, 0)),
                  full(SSD_CONV, SSD_CONV_DIM), full(1, SSD_CONV_DIM), full(1, SSD_HEADS), full(1, SSD_HEADS),
                  full(1, SSD_HEADS), full(1, SSD_D_INNER)],
        out_specs=[rowspec(SSD_D_INNER),
                   pl.BlockSpec((1, SSD_CONV - 1, SSD_CONV_DIM), lambda b: (b, 0, 0)),
                   pl.BlockSpec((1, SSD_HEADS, SSD_HEADDIM, SSD_STATE), lambda b: (b, 0, 0, 0))],
        out_shape=[jax.ShapeDtypeStruct((bsz, 1, SSD_D_INNER), f32),
                   jax.ShapeDtypeStruct((bsz, SSD_CONV - 1, SSD_CONV_DIM), f32),
                   jax.ShapeDtypeStruct((bsz, SSD_HEADS, SSD_HEADDIM, SSD_STATE), f32)],
        compiler_params=_cparams(("parallel",)),
    )(z.reshape(bsz, 1, -1), xbc.reshape(bsz, 1, -1), dt_raw.reshape(bsz, 1, -1), conv_state, ssm_state,
      sw["conv_w"], sw["conv_b"], sw["dt_bias"].reshape(1, -1), sw["a_log"].reshape(1, -1), sw["dsk"].reshape(1, -1),
      sw["norm_g"])
    h1 = _mm_ln(y.reshape(bsz, SSD_D_INNER), sw["w_out"], x2d, ln_g, ln_b, tm=bsz, name="ssd_out_ln")
    return h1, new_conv, new_state


def _slope(head):
    return 2.0 ** (-8.0 * (head + 1) / NSA_HEADS)


def _masked_softmax(s, mask):
    s = jnp.where(mask, s, NEG)
    m = jnp.max(s, axis=-1, keepdims=True)
    e = jnp.exp(s - m) * mask.astype(f32)
    return e / jnp.maximum(jnp.sum(e, axis=-1, keepdims=True), 1e-30)


def _tile_rows(x, n):
    return jnp.concatenate([x] * n, axis=0)


def _alibi_bias(ok, dist, kv_head):
    distf = dist.astype(f32)
    return jnp.concatenate(
        [jnp.where(ok, (-_slope(kv_head * NSA_REP + r)) * distf, NEG) for r in range(NSA_REP)], axis=0)


def _slope_col(kv_head, rows):
    return jnp.concatenate([jnp.full((rows, 1), _slope(kv_head * NSA_REP + r), f32) for r in range(NSA_REP)], axis=0)


def _topk_rank(imp):
    nb = imp.shape[1]
    lane = lax.broadcasted_iota(i32, imp.shape, 1)
    rank = jnp.zeros(imp.shape, i32)
    for j in range(nb):
        cj = imp[:, j:j + 1]
        rank = rank + ((cj > imp) | ((cj == imp) & (lane > j))).astype(i32)
    return rank


def _topk_mask_rows(imp, n_sel):
    it = imp.T
    nb = it.shape[0]
    sub = lax.broadcasted_iota(i32, it.shape, 0)
    rank = jnp.zeros(it.shape, i32)
    for j in range(nb):
        rj = it[j:j + 1, :]
        rank = rank + ((rj > it) | ((rj == it) & (sub > j))).astype(i32)
    return (rank < n_sel).astype(f32).T


def _pos_weights(pw_ref):
    w = jax.nn.softmax(pw_ref[...], axis=0)
    r = lax.broadcasted_iota(i32, (2 * NSA_KV, 2 * NSA_KVD), 0)
    c = lax.broadcasted_iota(i32, (2 * NSA_KV, 2 * NSA_KVD), 1)
    return jnp.dot(w, (c // NSA_DH == r).astype(f32), precision=HIGHEST, preferred_element_type=f32)


def _compress_body(x_ref, pw_ref, o_ref):
    wx = _pos_weights(pw_ref)
    x = x_ref[...]
    nb = x.shape[0] // NSA_BLOCK
    o_ref[0] = jnp.sum(x.reshape(nb, NSA_BLOCK, 2 * NSA_KVD) * wx[None], axis=1)


def _nsa_compress(kv4, pw, bsz, t):
    nb = t // NSA_BLOCK
    return pl.pallas_call(
        _compress_body,
        name="nsa_compress",
        grid=(bsz,),
        in_specs=[pl.BlockSpec((t, 2 * NSA_KVD), lambda b: (b, 0)),
                  pl.BlockSpec((NSA_BLOCK, 2 * NSA_KV), lambda b: (0, 0))],
        out_specs=pl.BlockSpec((1, nb, 2 * NSA_KVD), lambda b: (b, 0, 0)),
        out_shape=jax.ShapeDtypeStruct((bsz, nb, 2 * NSA_KVD), f32),
        compiler_params=_cparams(("parallel",)),
    )(kv4, pw)


NSA_TQ = 128
NSA_CK = 512


def _nsa_prompt_body(q_ref, g_ref, kcvc_ref, kvs_ref, kvw_ref, o_ref, *, t_len):
    i = pl.program_id(1)
    tq, ck, rep = NSA_TQ, NSA_CK, NSA_REP
    ck = min(ck, t_len)
    nb = t_len // NSA_BLOCK
    n_sel = min(NSA_TOP, nb)
    wlen = min(NSA_WINDOW + tq, t_len)
    t0 = i * tq
    t_col = t0 + lax.broadcasted_iota(i32, (tq, 1), 0)
    lane128 = lax.broadcasted_iota(i32, (1, LANES), 1)
    gates = g_ref[...]
    kcvc = kcvc_ref[0]
    blk = lax.broadcasted_iota(i32, (1, nb), 1)
    dist_c = t_col - ((blk + 1) * NSA_BLOCK - 1)
    dist_c4 = _tile_rows(dist_c, rep)
    cur = t_col // NSA_BLOCK
    n_chunks = (t0 + tq + ck - 1) // ck
    w_start = jnp.clip(t0 - NSA_WINDOW, 0, t_len - wlen)
    w_start = pl.multiple_of(w_start, tq)
    dist_w = t_col - (w_start + lax.broadcasted_iota(i32, (1, wlen), 1))
    mask_w = (dist_w >= 0) & (dist_w < NSA_WINDOW)

    for j in range(NSA_KV // 2):
        cols = slice(j * LANES, (j + 1) * LANES)
        vcols = slice(NSA_KVD + j * LANES, NSA_KVD + (j + 1) * LANES)
        q_tiles = [q_ref[:, (j * rep + r) * LANES:(j * rep + r + 1) * LANES] for r in range(rep)]
        q_rows = jnp.concatenate(q_tiles, axis=0)
        kc_b = kcvc[:, cols].astype(bf16)
        vc_b = kcvc[:, vcols].astype(bf16)
        out_tiles = [jnp.zeros((tq, LANES), f32) for _ in range(rep)]
        for h in range(2):
            k = 2 * j + h
            half = (lane128 // NSA_DH) == h
            q_pad = jnp.where(half, q_rows, jnp.zeros_like(q_rows))
            slope = _slope_col(k, tq)
            s_c = _dot_nt(q_pad, kc_b) - slope * dist_c4.astype(f32)
            p_c = _masked_softmax(s_c, dist_c4 >= 0)
            o_c = _dot(p_c.astype(bf16), vc_b)
            imp = sum(p_c[r * tq:(r + 1) * tq] for r in range(rep))
            imp = jnp.where((blk == cur) | (blk == 0), 1e4, imp)
            imp = jnp.where(blk > cur, -1.0, imp)
            sel_b = _topk_mask_rows(imp, n_sel).astype(bf16)

            def sel_step(c, carry, q_pad=q_pad, k=k, sel_b=sel_b, cols=cols, vcols=vcols):
                m, l, acc = carry
                k0 = pl.multiple_of(c * ck, ck)
                ks = kvs_ref[pl.ds(k0, ck), cols]
                vs = kvs_ref[pl.ds(k0, ck), vcols]
                kpos = k0 + lax.broadcasted_iota(i32, (1, ck), 1)
                brow = lax.broadcasted_iota(i32, (nb, ck), 0)
                expand = ((k0 + lax.broadcasted_iota(i32, (nb, ck), 1)) // NSA_BLOCK == brow).astype(bf16)
                selx = _dot(sel_b, expand)
                dist = t_col - kpos
                ok = (selx > 0.5) & (dist >= 0)
                s = _dot_nt(q_pad, ks) + _alibi_bias(ok, dist, k)
                m_new = jnp.maximum(m, jnp.max(s, axis=-1, keepdims=True))
                a = jnp.exp(m - m_new)
                p = jnp.exp(s - m_new)
                l = a * l + jnp.sum(p, axis=-1, keepdims=True)
                acc = a * acc + _dot(p.astype(bf16), vs)
                return m_new, l, acc

            init = (jnp.full((rep * tq, 1), NEG, f32), jnp.zeros((rep * tq, 1), f32), jnp.zeros((rep * tq, LANES), f32))
            _, l_s, acc_s = lax.fori_loop(0, n_chunks, sel_step, init)
            o_s = acc_s / jnp.maximum(l_s, 1e-30)
            kw = kvw_ref[pl.ds(w_start, wlen), cols]
            vw = kvw_ref[pl.ds(w_start, wlen), vcols]
            s_w = _dot_nt(q_pad, kw) + _alibi_bias(mask_w, dist_w, k)
            e_w = jnp.exp(s_w - jnp.max(s_w, axis=-1, keepdims=True))
            o_w = _dot(e_w.astype(bf16), vw) / jnp.sum(e_w, axis=-1, keepdims=True)
            for r in range(rep):
                gc = (k * rep + r) * 3
                rows = slice(r * tq, (r + 1) * tq)
                o_r = gates[:, gc:gc + 1] * o_c[rows] + gates[:, gc + 1:gc + 2] * o_s[rows] + gates[:, gc + 2:gc + 3] * o_w[rows]
                out_tiles[r] = jnp.where(half, o_r, out_tiles[r])
        for r in range(rep):
            o_ref[:, (j * rep + r) * LANES:(j * rep + r + 1) * LANES] = out_tiles[r].astype(o_ref.dtype)


def _nsa_prompt_attn(q, gates, kcvc, kv4_b, kvw_b, bsz, t):
    nq = t // NSA_TQ
    nb = t // NSA_BLOCK
    return pl.pallas_call(
        functools.partial(_nsa_prompt_body, t_len=t),
        name="nsa_attn",
        grid=(bsz, nq),
        in_specs=[pl.BlockSpec((NSA_TQ, NSA_QD), lambda b, i: (b * nq + i, 0)),
                  pl.BlockSpec((NSA_TQ, LANES), lambda b, i: (b * nq + i, 0)),
                  pl.BlockSpec((1, nb, 2 * NSA_KVD), lambda b, i: (b, 0, 0)),
                  pl.BlockSpec((t, 2 * NSA_KVD), lambda b, i: (b, 1)),
                  pl.BlockSpec((t, 2 * NSA_KVD), lambda b, i: (b, 0))],
        out_specs=pl.BlockSpec((NSA_TQ, NSA_QD), lambda b, i: (b * nq + i, 0)),
        out_shape=jax.ShapeDtypeStruct((bsz * t, NSA_QD), bf16),
        compiler_params=_cparams(("parallel", "arbitrary")),
    )(q, gates, kcvc, kv4_b, kvw_b)


def _pair_layout_cols(w):
    lead = w.shape[:-1]
    w = w.reshape(*lead, NSA_KV // 2, 2, NSA_REP, NSA_DH)
    return jnp.swapaxes(w, -3, -2).reshape(*lead, NSA_QD)


def _nsa_weights(w_in, pos_w, w_out, wdt, pair):
    lay = _pair_layout_cols if pair else (lambda w: w)
    kv0 = NSA_QD
    g0 = NSA_QD + 6 * NSA_KVD
    wg = jnp.zeros((D_MODEL, LANES), f32).at[:, :3 * NSA_HEADS].set(w_in[:, g0:])
    return dict(
        w_q=lay(w_in[:, :NSA_QD]).astype(wdt),
        w_kv4=w_in[:, kv0:kv0 + 4 * NSA_KVD].astype(wdt),
        w_kvw=w_in[:, kv0 + 4 * NSA_KVD:g0].astype(wdt),
        w_g=wg.astype(wdt),
        pw=pos_w.transpose(1, 0, 2).reshape(NSA_BLOCK, 2 * NSA_KV),
        pwt=pos_w.transpose(0, 2, 1).reshape(2 * NSA_KV, NSA_BLOCK),
        w_kvt=w_in[:, kv0:g0].T.astype(wdt),
        w_out=lay(w_out.T).T.astype(wdt))


def _nsa_prompt(x2d, nw, bsz, t, ln_g, ln_b):
    (q,) = _mm(x2d, nw["w_q"], tm=PROJ_TM, tn=1024, out_dtypes=(bf16,), scale=NSA_DH ** -0.5, name="nsa_in_q")
    kv4, kv4_b = _mm(x2d, nw["w_kv4"], tm=PROJ_TM, tn=1024, out_dtypes=(f32, bf16), name="nsa_in_kv4")
    kvw, kvw_b = _mm(x2d, nw["w_kvw"], tm=PROJ_TM, tn=512, out_dtypes=(f32, bf16), name="nsa_in_kvw")
    (gates,) = _mm(x2d, nw["w_g"], tm=PROJ_TM, tn=LANES, out_dtypes=(f32,), act="sigmoid", name="nsa_in_gates")
    kcvc = _nsa_compress(kv4, nw["pw"], bsz, t)
    o = _nsa_prompt_attn(q, gates, kcvc, kv4_b, kvw_b, bsz, t)
    h1 = _mm_ln(o, nw["w_out"], x2d, ln_g, ln_b, tm=512, name="nsa_out_ln")
    w_keep = min(NSA_WINDOW, t)
    new_kv = kv4.reshape(bsz, t, 4, NSA_KV, NSA_DH)
    new_win = kvw.reshape(bsz, t, 2, NSA_KV, NSA_DH)[:, t - w_keep:]
    return h1, new_kv, new_win


PAGES_PER_STEP = 16


def _lane_weights(pwt_ref):
    w = jax.nn.softmax(pwt_ref[...], axis=1)
    return jnp.concatenate([w] * (PAGE_SIZE // NSA_BLOCK), axis=1)


def _page_compress_body(pt_ref, *refs):
    del pt_ref
    page_refs, pwt_ref, o_ref = refs[:PAGES_PER_STEP], refs[PAGES_PER_STEP], refs[PAGES_PER_STEP + 1]
    s = pl.program_id(1)
    bpp = PAGE_SIZE // NSA_BLOCK
    nbp = o_ref.shape[-1]

    @pl.when(s == 0)
    def _():
        o_ref[...] = jnp.zeros_like(o_ref)

    w = _lane_weights(pwt_ref)
    tok_blk = lax.broadcasted_iota(i32, (1, PAGE_SIZE), 1) // NSA_BLOCK
    out_lane = lax.broadcasted_iota(i32, (1, nbp), 1)
    for c in range(2):
        for k in range(NSA_KV):
            wk = w[c * NSA_KV + k:c * NSA_KV + k + 1]
            upd = jnp.zeros((NSA_DH, nbp), f32)
            for i, p_ref in enumerate(page_refs):
                xw = p_ref[0, c, k] * wk
                first = (s * PAGES_PER_STEP + i) * bpp
                for h in range(bpp):
                    r = jnp.sum(jnp.where(tok_blk == h, xw, 0.0), axis=1, keepdims=True)
                    upd = upd + jnp.where(out_lane == first + h, r, 0.0)
            o_ref[0, c, k] += upd


def _page_compress(pool_t, page_table, pwt):
    bsz, n_pages = page_table.shape
    nbp = n_pages * (PAGE_SIZE // NSA_BLOCK)
    steps = n_pages // PAGES_PER_STEP
    assert n_pages % PAGES_PER_STEP == 0
    page_specs = [pl.BlockSpec((1, 2, NSA_KV, NSA_DH, PAGE_SIZE),
                               lambda b, s, pt, i=i: (pt[b, s * PAGES_PER_STEP + i], 0, 0, 0, 0))
                  for i in range(PAGES_PER_STEP)]
    return pl.pallas_call(
        _page_compress_body,
        name="nsa_page_compress",
        grid_spec=pltpu.PrefetchScalarGridSpec(
            num_scalar_prefetch=1, grid=(bsz, steps),
            in_specs=page_specs + [pl.BlockSpec((2 * NSA_KV, NSA_BLOCK), lambda b, s, pt: (0, 0))],
            out_specs=pl.BlockSpec((1, 2, NSA_KV, NSA_DH, nbp), lambda b, s, pt: (b, 0, 0, 0, 0))),
        out_shape=jax.ShapeDtypeStruct((bsz, 2, NSA_KV, NSA_DH, nbp), f32),
        compiler_params=_cparams(("parallel", "arbitrary")),
    )(page_table, *([pool_t] * PAGES_PER_STEP), pwt)


def _mm_nt_body(wt_ref, x_ref, o_ref):
    o_ref[...] = _mxu_nt(wt_ref[...], x_ref[...], _is_f32(wt_ref))


def _mm_nt(wt, x, *, tn, name):
    n, k = wt.shape
    m = x.shape[0]
    return pl.pallas_call(
        _mm_nt_body, name=f"{name}_r{m}", grid=(n // tn,),
        in_specs=[pl.BlockSpec((tn, k), lambda j: (j, 0)), pl.BlockSpec((m, k), lambda j: (0, 0))],
        out_specs=pl.BlockSpec((tn, m), lambda j: (j, 0)),
        out_shape=jax.ShapeDtypeStruct((n, m), f32),
        compiler_params=_cparams(("parallel",)),
    )(wt, x)


def _step_slopes(kv_head):
    r = lax.broadcasted_iota(i32, (8, 1), 0)
    return jnp.exp2(-0.5 * (kv_head * NSA_REP + r + 1).astype(f32))


def _gate_col(gates, kv_head, branch):
    cols = [(kv_head * NSA_REP + r) * 3 + branch for r in range(NSA_REP)]
    return jnp.concatenate([gates[:, c:c + 1] for c in cols], axis=0)


def _own_col(x, b):
    lane = lax.broadcasted_iota(i32, x.shape, x.ndim - 1)
    return jnp.sum(jnp.where(lane == b, x, 0.0), axis=-1, keepdims=True)


def _new_key_tile(col):
    lane = lax.broadcasted_iota(i32, (col.shape[0], LANES), 1)
    return jnp.where(lane == 0, col, 0.0)


def _nsa_step_cw_body(q_ref, g_ref, cmp_ref, kvt_ref, pwt_ref, win_ref, o_ref, idx_ref, *, past_len):
    b = pl.program_id(0)
    rep, kv = NSA_REP, NSA_KV
    nbp = past_len // NSA_BLOCK
    nbt = nbp + 8
    w_buf = win_ref.shape[-1]
    w0 = jax.nn.softmax(pwt_ref[...], axis=1)[:, 0:1]
    lane_c = lax.broadcasted_iota(i32, (1, nbp + LANES), 1)
    dist_c = past_len - ((lane_c + 1) * NSA_BLOCK - 1)
    lane_w = lax.broadcasted_iota(i32, (1, w_buf + LANES), 1)
    dist_w = jnp.where(lane_w <= w_buf, w_buf - lane_w, -1)
    mask_w = (dist_w >= 0) & (dist_w < NSA_WINDOW)
    blk = lax.broadcasted_iota(i32, (1, nbt), 1)
    cur = past_len // NSA_BLOCK
    gates = g_ref[0]
    for k in range(kv):
        q8 = _pad_rows(q_ref[0, k])
        slope = _step_slopes(k)
        new = [_own_col(kvt_ref[s, k], b) for s in (0, 1, 4, 5)]
        k_c = jnp.concatenate([cmp_ref[0, 0, k], _new_key_tile(new[0] * w0[k:k + 1])], axis=1)
        v_c = jnp.concatenate([cmp_ref[0, 1, k], _new_key_tile(new[1] * w0[kv + k:kv + k + 1])], axis=1)
        s_c = _mxu(q8, k_c, True) - slope * dist_c.astype(f32)
        p_c = _masked_softmax(s_c, jnp.broadcast_to(dist_c >= 0, s_c.shape))
        o_c = _mxu_nt(p_c, v_c, True)
        imp = jnp.sum(p_c[0:rep], axis=0, keepdims=True)[:, :nbt]
        imp = jnp.where((blk == cur) | (blk == 0), 1e4, imp)
        imp = jnp.where(blk > cur, -2.0, imp)
        rank = _topk_rank(imp)
        pick = lax.broadcasted_iota(i32, (NSA_TOP, nbt), 0) == rank
        idx_ref[0, k] = jnp.sum(jnp.where(pick, blk.astype(f32), 0.0), axis=1, keepdims=True).astype(i32)
        k_w = jnp.concatenate([win_ref[0, 0, k], _new_key_tile(new[2])], axis=1)
        v_w = jnp.concatenate([win_ref[0, 1, k], _new_key_tile(new[3])], axis=1)
        s_w = _mxu(q8, k_w, True) - slope * dist_w.astype(f32)
        p_w = _masked_softmax(s_w, jnp.broadcast_to(mask_w, s_w.shape))
        o_w = _mxu_nt(p_w, v_w, True)
        o_ref[0, k] = _gate_col(gates, k, 0) * o_c[0:rep] + _gate_col(gates, k, 2) * o_w[0:rep]


def _nsa_step_sel_body(idx_ref, pt_ref, *refs, past_len):
    del pt_ref
    n_in = 2 * NSA_TOP
    tile_refs = refs[:n_in]
    q_ref, g_ref, kvt_ref, part_ref, o_ref = refs[n_in:]
    b, k = pl.program_id(0), pl.program_id(1)
    nbp = past_len // NSA_BLOCK
    rep, lb, kv = NSA_REP, NSA_BLOCK, NSA_KV
    bpp = PAGE_SIZE // lb
    new_k = _new_key_tile(_own_col(kvt_ref[0, 0], b))
    new_v = _new_key_tile(_own_col(kvt_ref[1, 0], b))
    lane = lax.broadcasted_iota(i32, (1, NSA_TOP * PAGE_SIZE), 1)
    tok = lane % PAGE_SIZE
    spos = tok % lb
    valid = lane < 0
    ks, vs = [], []
    for i in range(NSA_TOP):
        n = idx_ref[(b * kv + k) * NSA_TOP + i]
        is_new = n >= nbp
        ks.append(jnp.where(is_new, new_k, tile_refs[2 * i][0, 0, 0]))
        vs.append(jnp.where(is_new, new_v, tile_refs[2 * i + 1][0, 0, 0]))
        mine = lane // PAGE_SIZE == i
        spos = spos + jnp.where(mine, n * lb, 0)
        half = jnp.where(is_new, 0, n % bpp)
        last = jnp.where(is_new, 0, lb - 1)
        valid = valid | (mine & (tok // lb == half) & (tok % lb <= last))
    k_all = jnp.concatenate(ks, axis=1)
    v_all = jnp.concatenate(vs, axis=1)
    dist = past_len - spos
    q8 = _pad_rows(q_ref[0, 0])
    s = _mxu(q8, k_all, True) - _step_slopes(k) * dist.astype(f32)
    p = _masked_softmax(s, jnp.broadcast_to(valid & (dist >= 0), s.shape))
    o_s = _mxu_nt(p, v_all, True)
    g = g_ref[0, 0]
    g_sel = jnp.concatenate([g[:, 3 * r + 1:3 * r + 2] for r in range(rep)], axis=0)
    o_ref[0, 0] = part_ref[0, 0] + g_sel * o_s[0:rep]


def _nsa_step(x2d, pool, page_table, win_cache, nw, ln_g, ln_b):
    bsz = x2d.shape[0]
    n_pages = page_table.shape[1]
    past_len = n_pages * PAGE_SIZE
    w_buf = win_cache.shape[1]
    rep, kv, dh = NSA_REP, NSA_KV, NSA_DH
    bpp = PAGE_SIZE // NSA_BLOCK
    nbp = past_len // NSA_BLOCK
    (q,) = _mm(x2d, nw["w_q"], tm=bsz, tn=1024, out_dtypes=(f32,), scale=NSA_DH ** -0.5, name="nsa_in_q")
    (kv4,) = _mm(x2d, nw["w_kv4"], tm=bsz, tn=1024, out_dtypes=(f32,), name="nsa_in_kv4")
    (kvw,) = _mm(x2d, nw["w_kvw"], tm=bsz, tn=512, out_dtypes=(f32,), name="nsa_in_kvw")
    (gates,) = _mm(x2d, nw["w_g"], tm=bsz, tn=LANES, out_dtypes=(f32,), act="sigmoid", name="nsa_in_gates")
    kvt = _mm_nt(nw["w_kvt"], x2d, tn=512, name="nsa_in_kvt").reshape(6, kv, dh, bsz)
    pool_t = jnp.transpose(pool, (0, 2, 3, 4, 1))
    win_t = jnp.transpose(win_cache, (0, 2, 3, 4, 1))
    cmp = _page_compress(pool_t, page_table, nw["pwt"])
    q4 = q.reshape(bsz, kv, rep, dh)
    slab = pl.BlockSpec((1, kv, rep, dh), lambda b: (b, 0, 0, 0))
    whole = lambda shape: pl.BlockSpec(shape, lambda b: (0,) * len(shape))
    part, idx = pl.pallas_call(
        functools.partial(_nsa_step_cw_body, past_len=past_len),
        name="nsa_step_cw",
        grid=(bsz,),
        in_specs=[slab, pl.BlockSpec((1, 1, LANES), lambda b: (b, 0, 0)),
                  pl.BlockSpec((1, 2, kv, dh, nbp), lambda b: (b, 0, 0, 0, 0)), whole((6, kv, dh, bsz)),
                  whole((2 * kv, NSA_BLOCK)), pl.BlockSpec((1, 2, kv, dh, w_buf), lambda b: (b, 0, 0, 0, 0))],
        out_specs=[slab, pl.BlockSpec((1, kv, NSA_TOP, 1), lambda b: (b, 0, 0, 0))],
        out_shape=[jax.ShapeDtypeStruct((bsz, kv, rep, dh), f32),
                   jax.ShapeDtypeStruct((bsz, kv, NSA_TOP, 1), i32)],
        compiler_params=_cparams(("parallel",)),
    )(q4, gates.reshape(bsz, 1, -1), cmp, kvt, nw["pwt"], win_t)

    def tile_spec(i, slot):
        def imap(b, k, idx_r, pt_r):
            n = jnp.minimum(idx_r[(b * kv + k) * NSA_TOP + i], nbp - 1)
            return (pt_r[b * n_pages + n // bpp], slot, k, 0, 0)
        return pl.BlockSpec((1, 1, 1, dh, PAGE_SIZE), imap)

    tile_specs = []
    for i in range(NSA_TOP):
        tile_specs += [tile_spec(i, 2), tile_spec(i, 3)]
    head = pl.BlockSpec((1, 1, rep, dh), lambda b, k, *_: (b, k, 0, 0))
    o = pl.pallas_call(
        functools.partial(_nsa_step_sel_body, past_len=past_len),
        name="nsa_step_sel",
        grid_spec=pltpu.PrefetchScalarGridSpec(
            num_scalar_prefetch=2, grid=(bsz, kv),
            in_specs=tile_specs + [
                head, pl.BlockSpec((1, 1, 1, 3 * rep), lambda b, k, *_: (b, k, 0, 0)),
                pl.BlockSpec((2, 1, dh, bsz), lambda b, k, *_: (1, k, 0, 0)), head],
            out_specs=head),
        out_shape=jax.ShapeDtypeStruct((bsz, kv, rep, dh), f32),
        compiler_params=_cparams(("arbitrary", "arbitrary")),
    )(idx.reshape(-1), page_table.reshape(-1), *([pool_t] * (2 * NSA_TOP)),
      q4, gates[:, :3 * NSA_HEADS].reshape(bsz, kv, 1, 3 * rep), kvt, part)
    h1 = _mm_ln(o.reshape(bsz, NSA_QD), nw["w_out"], x2d, ln_g, ln_b, tm=bsz, name="nsa_out_ln")
    new_kv = kv4.reshape(bsz, 1, 4, kv, dh)
    new_win = jnp.concatenate([win_cache[:, 1:], kvw.reshape(bsz, 1, 2, kv, dh)], axis=1)
    return h1, new_kv, new_win


def kernel(x_prompt, x_sample, state_ssm, state_conv, cache_kv, cache_win, page_table, p_prompt, p_sample,
           ssd_w_in, ssd_conv_w, ssd_conv_b, ssd_dt_bias, ssd_a_log, ssd_d, ssd_norm_g, ssd_w_out,
           nsa_w_in, nsa_pos_w, nsa_w_out, ln1_g, ln1_b, ln2_g, ln2_b, router_w, router_bias,
           moe_w_gate, moe_w_up, moe_w_down, ple_proj, ple_gate):
    bp, t, d = x_prompt.shape
    bs = x_sample.shape[0]
    xp = x_prompt.reshape(bp * t, d)
    xs = x_sample.reshape(bs, d)
    rb = router_bias.reshape(N_EXPERTS, 1)
    ssm_p, conv_p, kv_p, win_p, ssm_s, conv_s, kv_s, win_s = [], [], [], [], [], [], [], []
    for i in range(DEPTH):
        j = i // 2
        if i % 2 == 0:
            ssd = (ssd_w_in[j], ssd_conv_w[j], ssd_conv_b[j], ssd_dt_bias[j], ssd_a_log[j], ssd_d[j], ssd_norm_g[j],
                   ssd_w_out[j])
            h1p, c_new, s_new = _ssd_prompt(xp, _ssd_weights(*ssd, bf16), bp, t, ln1_g[i], ln1_b[i])
            conv_p.append(c_new)
            ssm_p.append(s_new)
            h1s, c_new, s_new = _ssd_step(xs, state_conv[j], state_ssm[j], _ssd_weights(*ssd, f32), ln1_g[i], ln1_b[i])
            conv_s.append(c_new)
            ssm_s.append(s_new)
        else:
            nsa = (nsa_w_in[j], nsa_pos_w[j], nsa_w_out[j])
            h1p, r_new, w_new = _nsa_prompt(xp, _nsa_weights(*nsa, bf16, True), bp, t, ln1_g[i], ln1_b[i])
            kv_p.append(r_new)
            win_p.append(w_new)
            h1s, r_new, w_new = _nsa_step(xs, cache_kv[j], page_table, cache_win[j], _nsa_weights(*nsa, f32, False),
                                          ln1_g[i], ln1_b[i])
            kv_s.append(r_new)
            win_s.append(w_new)

        def tail_weights(wdt, i=i):
            return dict(rwt=router_w.T.astype(wdt), rb=rb, layer=i, wg=moe_w_gate.astype(wdt), wu=moe_w_up.astype(wdt),
                        wd=moe_w_down.astype(wdt), ln2_g=ln2_g[i], ln2_b=ln2_b[i],
                        ple_gate=ple_gate[i].astype(wdt), ple_proj=ple_proj[i].astype(wdt))

        xp = _layer_tail(h1p, p_prompt[i].reshape(bp * t, PLE_DIM), tail_weights(bf16), tm=512)
        xs = _layer_tail(h1s, p_sample[i].reshape(bs, PLE_DIM), tail_weights(f32), tm=bs)
    return (xp.reshape(bp, t, d), xs.reshape(bs, 1, d), jnp.stack(ssm_p), jnp.stack(conv_p), jnp.stack(kv_p),
            jnp.stack(win_p), jnp.stack(ssm_s), jnp.stack(conv_s), jnp.stack(kv_s), jnp.stack(win_s))
```

```python
import functools

import jax
import jax.numpy as jnp
from jax import lax
from jax.experimental import pallas as pl
from jax.experimental.pallas import tpu as pltpu

f32, bf16, i32 = jnp.float32, jnp.bfloat16, jnp.int32
HIGHEST = lax.Precision.HIGHEST

D_MODEL = 1024
DEPTH = 2
PLE_DIM = 256
SSD_D_INNER = 2048
SSD_HEADDIM = 64
SSD_HEADS = 32
SSD_GROUPS = 4
SSD_HPG = 8
SSD_STATE = 128
SSD_CONV = 4
SSD_CHUNK = 128
SSD_GW = SSD_HPG * SSD_HEADDIM
SSD_CONV_DIM = SSD_D_INNER + 2 * SSD_GROUPS * SSD_STATE
NSA_HEADS = 16
NSA_KV = 4
NSA_REP = 4
NSA_DH = 64
NSA_BLOCK = 64
NSA_TOP = 16
NSA_WINDOW = 512
NSA_QD = NSA_HEADS * NSA_DH
NSA_KVD = NSA_KV * NSA_DH
PAGE_SIZE = 128
N_EXPERTS = 16
N_EGROUPS = 4
EPG = 4
D_FF = 512
ALPHA = (2.0 * DEPTH) ** 0.25
LN_EPS = 1e-5
NEG = -1e30
LANES = 128
VMEM_LIMIT = 56 * 1024 * 1024
PROJ_TM = 1024


def _cparams(sem):
    return pltpu.CompilerParams(dimension_semantics=sem, vmem_limit_bytes=VMEM_LIMIT)


def _ln(v, g, b):
    mu = jnp.mean(v, axis=-1, keepdims=True)
    d = v - mu
    var = jnp.mean(d * d, axis=-1, keepdims=True)
    return d * lax.rsqrt(var + LN_EPS) * g + b


def _dot(a, b):
    return jnp.dot(a, b, preferred_element_type=f32)


def _mxu(a, b, precise):
    if precise:
        return jnp.dot(a.astype(f32), b.astype(f32), precision=HIGHEST, preferred_element_type=f32)
    return jnp.dot(a.astype(bf16), b.astype(bf16), preferred_element_type=f32)


def _mxu_nt(a, b, precise):
    if precise:
        return _dot_nt(a.astype(f32), b.astype(f32), precision=HIGHEST)
    return _dot_nt(a.astype(bf16), b.astype(bf16))


def _dot_nt(a, b, precision=None):
    return lax.dot_general(a, b, (((1,), (1,)), ((), ())), precision=precision, preferred_element_type=f32)


def _dot_tn(a, b, precision=None):
    return lax.dot_general(a, b, (((0,), (0,)), ((), ())), precision=precision, preferred_element_type=f32)


def _is_f32(ref):
    return ref.dtype == jnp.float32


def _mm_body(x_ref, w_ref, *o_refs, act, scale):
    acc = _mxu(x_ref[...], w_ref[...], _is_f32(w_ref))
    if scale != 1.0:
        acc = acc * scale
    if act == "sigmoid":
        acc = jax.nn.sigmoid(acc)
    for o_ref in o_refs:
        o_ref[...] = acc.astype(o_ref.dtype)


def _mm(x, w, *, tm, tn, out_dtypes, name, act=None, scale=1.0):
    m, k = x.shape
    n = w.shape[1]
    tm, tn = min(tm, m), min(tn, n)
    assert m % tm == 0 and n % tn == 0
    outs = pl.pallas_call(
        functools.partial(_mm_body, act=act, scale=scale),
        name=f"{name}_r{m}",
        grid=(m // tm, n // tn),
        in_specs=[pl.BlockSpec((tm, k), lambda i, j: (i, 0)), pl.BlockSpec((k, tn), lambda i, j: (0, j))],
        out_specs=[pl.BlockSpec((tm, tn), lambda i, j: (i, j)) for _ in out_dtypes],
        out_shape=[jax.ShapeDtypeStruct((m, n), dt) for dt in out_dtypes],
        compiler_params=_cparams(("parallel", "arbitrary")),
    )(x, w)
    return outs


def _mm_ln_body(x_ref, w_ref, res_ref, g_ref, b_ref, o_ref):
    acc = _mxu(x_ref[...], w_ref[...], _is_f32(w_ref))
    o_ref[...] = _ln(ALPHA * res_ref[...] + acc, g_ref[...], b_ref[...])


def _mm_ln(x, w, res, g, b, *, tm, name):
    m, k = x.shape
    n = w.shape[1]
    tm = min(tm, m)
    assert m % tm == 0
    return pl.pallas_call(
        _mm_ln_body,
        name=f"{name}_r{m}",
        grid=(m // tm,),
        in_specs=[pl.BlockSpec((tm, k), lambda i: (i, 0)), pl.BlockSpec((k, n), lambda i: (0, 0)),
                  pl.BlockSpec((tm, n), lambda i: (i, 0)), pl.BlockSpec((1, n), lambda i: (0, 0)),
                  pl.BlockSpec((1, n), lambda i: (0, 0))],
        out_specs=pl.BlockSpec((tm, n), lambda i: (i, 0)),
        out_shape=jax.ShapeDtypeStruct((m, n), f32),
        compiler_params=_cparams(("parallel",)),
    )(x, w, res, g.reshape(1, n), b.reshape(1, n))


def _router_body(h_ref, rwt_ref, rb_ref, gate_ref, best_ref):
    logits = _mxu_nt(rwt_ref[...], h_ref[...], _is_f32(rwt_ref))
    score = jax.nn.sigmoid(logits)
    sel = score + rb_ref[...]
    gsum = []
    for g in range(N_EGROUPS):
        a, b, c, d = (sel[EPG * g + i:EPG * g + i + 1] for i in range(EPG))
        hi1, lo1, hi2, lo2 = jnp.maximum(a, b), jnp.minimum(a, b), jnp.maximum(c, d), jnp.minimum(c, d)
        gsum.append(jnp.maximum(hi1, hi2) + jnp.maximum(jnp.minimum(hi1, hi2), jnp.maximum(lo1, lo2)))
    best = jnp.zeros_like(gsum[0], dtype=i32)
    top = gsum[0]
    for g in range(1, N_EGROUPS):
        upd = gsum[g] > top
        best = jnp.where(upd, g, best)
        top = jnp.where(upd, gsum[g], top)
    selg = sel[0:EPG]
    scg = score[0:EPG]
    for g in range(1, N_EGROUPS):
        selg = jnp.where(best == g, sel[EPG * g:EPG * (g + 1)], selg)
        scg = jnp.where(best == g, score[EPG * g:EPG * (g + 1)], scg)
    rows = [selg[i:i + 1] for i in range(EPG)]
    chosen = []
    for i in range(EPG):
        rank = jnp.zeros_like(best)
        for j in range(EPG):
            if j == i:
                continue
            ahead = (rows[j] > rows[i]) | ((rows[j] == rows[i]) if j < i else False)
            rank = rank + ahead.astype(i32)
        chosen.append(rank < 2)
    wsum = sum(jnp.where(chosen[i], scg[i:i + 1], 0.0) for i in range(EPG))
    gates = [jnp.where(chosen[i], scg[i:i + 1] / wsum, 0.0) for i in range(EPG)]
    out_rows = []
    for g in range(N_EGROUPS):
        for i in range(EPG):
            out_rows.append(jnp.where(best == g, gates[i], 0.0))
    gate_ref[...] = jnp.concatenate(out_rows, axis=0).T
    best_ref[...] = best


def _router(h, rwt, rb, *, tm):
    m, d = h.shape
    tm = min(tm, m)
    return pl.pallas_call(
        _router_body,
        name=f"router_r{m}",
        grid=(m // tm,),
        in_specs=[pl.BlockSpec((tm, d), lambda i: (i, 0)), pl.BlockSpec((N_EXPERTS, d), lambda i: (0, 0)),
                  pl.BlockSpec((N_EXPERTS, 1), lambda i: (0, 0))],
        out_specs=[pl.BlockSpec((tm, N_EXPERTS), lambda i: (i, 0)), pl.BlockSpec((1, tm), lambda i: (0, i))],
        out_shape=[jax.ShapeDtypeStruct((m, N_EXPERTS), f32), jax.ShapeDtypeStruct((1, m), i32)],
        compiler_params=_cparams(("parallel",)),
    )(h, rwt, rb)


def _moe_body(h_ref, gate_ref, wg_ref, wu_ref, wd_ref, g2_ref, b2_ref, o_ref, acc_ref):
    e = pl.program_id(1)

    @pl.when(e == 0)
    def _():
        acc_ref[...] = jnp.zeros_like(acc_ref)

    precise = _is_f32(wg_ref)
    h = h_ref[...]
    hid = jax.nn.silu(_mxu(h, wg_ref[0, 0], precise)) * _mxu(h, wu_ref[0, 0], precise)
    out = _mxu(hid, wd_ref[0, 0], precise)
    gate = gate_ref[...]
    lane = lax.broadcasted_iota(i32, gate.shape, 1)
    gcol = jnp.sum(jnp.where(lane == e, gate, 0.0), axis=1, keepdims=True)
    acc_ref[...] += gcol * out

    @pl.when(e == N_EXPERTS - 1)
    def _():
        o_ref[...] = _ln(ALPHA * h_ref[...] + acc_ref[...], g2_ref[...], b2_ref[...])


def _moe_ln(h, gate, wg, wu, wd, g2, b2, *, tm, layer):
    m, d = h.shape
    tm = min(tm, m)
    return pl.pallas_call(
        _moe_body,
        name=f"moe_ln_r{m}",
        grid=(m // tm, N_EXPERTS),
        in_specs=[pl.BlockSpec((tm, d), lambda i, e: (i, 0)), pl.BlockSpec((tm, N_EXPERTS), lambda i, e: (i, 0)),
                  pl.BlockSpec((1, 1, d, D_FF), lambda i, e: (layer, e, 0, 0)),
                  pl.BlockSpec((1, 1, d, D_FF), lambda i, e: (layer, e, 0, 0)),
                  pl.BlockSpec((1, 1, D_FF, d), lambda i, e: (layer, e, 0, 0)),
                  pl.BlockSpec((1, d), lambda i, e: (0, 0)), pl.BlockSpec((1, d), lambda i, e: (0, 0))],
        out_specs=pl.BlockSpec((tm, d), lambda i, e: (i, 0)),
        out_shape=jax.ShapeDtypeStruct((m, d), f32),
        scratch_shapes=[pltpu.VMEM((tm, d), f32)],
        compiler_params=_cparams(("parallel", "arbitrary")),
    )(h, gate, wg, wu, wd, g2.reshape(1, d), b2.reshape(1, d))


def _ple_body(h_ref, p_ref, wg_ref, wp_ref, o_ref):
    h = h_ref[...]
    precise = _is_f32(wg_ref)
    gate = jax.nn.sigmoid(_mxu(h, wg_ref[...], precise))
    o_ref[...] = h + gate * _mxu(p_ref[...], wp_ref[...], precise)


def _ple(h, p, wg, wp, *, tm):
    m, d = h.shape
    tm = min(tm, m)
    return pl.pallas_call(
        _ple_body,
        name=f"ple_r{m}",
        grid=(m // tm,),
        in_specs=[pl.BlockSpec((tm, d), lambda i: (i, 0)), pl.BlockSpec((tm, PLE_DIM), lambda i: (i, 0)),
                  pl.BlockSpec((d, d), lambda i: (0, 0)), pl.BlockSpec((PLE_DIM, d), lambda i: (0, 0))],
        out_specs=pl.BlockSpec((tm, d), lambda i: (i, 0)),
        out_shape=jax.ShapeDtypeStruct((m, d), f32),
        compiler_params=_cparams(("parallel",)),
    )(h, p, wg, wp)


MOE_TM = 1024
MOE_CH = 304
MOE_ALIGN = 16
MOE_ROWS = MOE_TM + N_EGROUPS * MOE_ALIGN + MOE_CH


def _moe_grouped_body(h_ref, gate_ref, best_ref, wg_ref, wu_ref, wd_ref, g2_ref, b2_ref, o_ref,
                      p_ref, xs_ref, ys_ref, gs_ref, meta_ref):
    e = pl.program_id(1)
    tm, rows, ng = MOE_TM, MOE_ROWS, N_EGROUPS

    @pl.when(e == 0)
    def _():
        best = best_ref[...]
        ind = (best == lax.broadcasted_iota(i32, (8, tm), 0)).astype(f32)
        prefix = ind
        lane = lax.broadcasted_iota(i32, (8, tm), 1)
        shift = 1
        while shift < tm:
            prefix = prefix + jnp.where(lane >= shift, pltpu.roll(prefix, shift, axis=1), 0.0)
            shift *= 2
        cnt = prefix[:, tm - 1:tm]
        cnt_al = jnp.floor((cnt + (MOE_ALIGN - 1)) * (1.0 / MOE_ALIGN)) * MOE_ALIGN
        offs = [jnp.zeros((1, 1), f32)]
        for g in range(1, ng):
            offs.append(offs[-1] + cnt_al[g - 1:g])
        off_col = jnp.concatenate(offs + [jnp.zeros((8 - ng, 1), f32)], axis=0)
        pos = jnp.sum(ind * (off_col + prefix - 1.0), axis=0, keepdims=True).astype(i32)
        perm = (lax.broadcasted_iota(i32, (rows, tm), 0) == pos).astype(bf16)
        p_ref[...] = perm
        xs_ref[...] = _dot(perm, h_ref[...].astype(bf16)).astype(bf16)
        parts = _dot(perm, jnp.concatenate(_split3(gate_ref[...]), axis=1))
        gs_ref[...] = (parts[:, :N_EXPERTS] + parts[:, N_EXPERTS:2 * N_EXPERTS]) + parts[:, 2 * N_EXPERTS:]
        ys_ref[...] = jnp.zeros_like(ys_ref)
        for g in range(ng):
            meta_ref[g] = offs[g][0, 0].astype(i32)
            meta_ref[ng + g] = cnt[g:g + 1][0, 0].astype(i32)

    g = e // EPG
    start = meta_ref[g]
    n_ch = (meta_ref[ng + g] + (MOE_CH - 1)) // MOE_CH

    def chunk(c, carry):
        r0 = pl.multiple_of(start + c * MOE_CH, MOE_ALIGN)
        x = xs_ref[pl.ds(r0, MOE_CH), :]
        hid = jax.nn.silu(_dot(x, wg_ref[0, 0])) * _dot(x, wu_ref[0, 0])
        y = _dot(hid.astype(bf16), wd_ref[0, 0])
        gs = gs_ref[pl.ds(r0, MOE_CH), :]
        lane = lax.broadcasted_iota(i32, gs.shape, 1)
        gcol = jnp.sum(jnp.where(lane == e, gs, 0.0), axis=1, keepdims=True)
        ys_ref[pl.ds(r0, MOE_CH), :] += gcol * y
        return carry

    lax.fori_loop(0, n_ch, chunk, 0)

    @pl.when(e == N_EXPERTS - 1)
    def _():
        ys = ys_ref[...]
        hi = ys.astype(bf16)
        lo = (ys - hi.astype(f32)).astype(bf16)
        perm = p_ref[...]
        out = _dot_tn(perm, hi) + _dot_tn(perm, lo)
        o_ref[...] = _ln(ALPHA * h_ref[...] + out, g2_ref[...], b2_ref[...])


def _moe_grouped_ln(h, gate, best, wg, wu, wd, g2, b2, *, layer):
    m, d = h.shape
    tm = MOE_TM
    assert m % tm == 0
    return pl.pallas_call(
        _moe_grouped_body,
        name=f"moe_grouped_ln_r{m}",
        grid=(m // tm, N_EXPERTS),
        in_specs=[pl.BlockSpec((tm, d), lambda i, e: (i, 0)), pl.BlockSpec((tm, N_EXPERTS), lambda i, e: (i, 0)),
                  pl.BlockSpec((1, tm), lambda i, e: (0, i)),
                  pl.BlockSpec((1, 1, d, D_FF), lambda i, e: (layer, e, 0, 0)),
                  pl.BlockSpec((1, 1, d, D_FF), lambda i, e: (layer, e, 0, 0)),
                  pl.BlockSpec((1, 1, D_FF, d), lambda i, e: (layer, e, 0, 0)),
                  pl.BlockSpec((1, d), lambda i, e: (0, 0)), pl.BlockSpec((1, d), lambda i, e: (0, 0))],
        out_specs=pl.BlockSpec((tm, d), lambda i, e: (i, 0)),
        out_shape=jax.ShapeDtypeStruct((m, d), f32),
        scratch_shapes=[pltpu.VMEM((MOE_ROWS, tm), bf16), pltpu.VMEM((MOE_ROWS, d), bf16),
                        pltpu.VMEM((MOE_ROWS, d), f32), pltpu.VMEM((MOE_ROWS, N_EXPERTS), f32),
                        pltpu.SMEM((2 * N_EGROUPS,), i32)],
        compiler_params=_cparams(("arbitrary", "arbitrary")),
    )(h, gate, best, wg, wu, wd, g2.reshape(1, d), b2.reshape(1, d))


def _layer_tail(h1, p, tw, *, tm):
    gate, best = _router(h1, tw["rwt"], tw["rb"], tm=tm)
    if h1.shape[0] % MOE_TM == 0 and tw["wg"].dtype == bf16:
        h2 = _moe_grouped_ln(h1, gate, best, tw["wg"], tw["wu"], tw["wd"], tw["ln2_g"], tw["ln2_b"], layer=tw["layer"])
    else:
        h2 = _moe_ln(h1, gate, tw["wg"], tw["wu"], tw["wd"], tw["ln2_g"], tw["ln2_b"], tm=tm, layer=tw["layer"])
    return _ple(h2, p, tw["ple_gate"], tw["ple_proj"], tm=tm)


def _ssd_conv(x_ref, xp_ref, w_ref, b_ref):
    q = x_ref.shape[0]
    xp_ref[8:8 + q, :] = x_ref[...]
    w = w_ref[...]
    acc = b_ref[...] + xp_ref[5:5 + q, :] * w[0:1]
    for k in range(1, SSD_CONV):
        acc = acc + xp_ref[5 + k:5 + k + q, :] * w[k:k + 1]
    xp_ref[0:8, :] = xp_ref[q:q + 8, :]
    return jax.nn.silu(acc)


def _head_expand(width):
    r = lax.broadcasted_iota(i32, (SSD_HPG, SSD_HPG * width), 0)
    c = lax.broadcasted_iota(i32, (SSD_HPG, SSD_HPG * width), 1)
    return (c // width == r).astype(bf16)


def _split3(v):
    hi = v.astype(bf16)
    r1 = v - hi.astype(f32)
    mid = r1.astype(bf16)
    return hi, mid, (r1 - mid.astype(f32)).astype(bf16)


def _dot_exact01(v, onehot, left=False):
    parts = _split3(v)
    prods = [_dot(onehot, p) if left else _dot(p, onehot) for p in parts]
    return (prods[0] + prods[1]) + prods[2]


def _ssd_body(xs_ref, bm_ref, cm_ref, z_ref, dt_ref, dtt_ref, wx_ref, wb_ref, wc_ref, bx_ref, bb_ref, bc_ref,
              dtb_ref, dtbt_ref, alog_ref, alogt_ref, dsk_ref, ng_ref, y_ref, st_ref,
              stt_ref, xpx_ref, xpb_ref, xpc_ref):
    c = pl.program_id(2)
    q = SSD_CHUNK

    @pl.when(c == 0)
    def _():
        stt_ref[...] = jnp.zeros_like(stt_ref)
        xpx_ref[0:8, :] = jnp.zeros((8, xpx_ref.shape[1]), f32)
        xpb_ref[0:8, :] = jnp.zeros((8, xpb_ref.shape[1]), f32)
        xpc_ref[0:8, :] = jnp.zeros((8, xpc_ref.shape[1]), f32)

    xs = _ssd_conv(xs_ref, xpx_ref, wx_ref, bx_ref)
    bm = _ssd_conv(bm_ref, xpb_ref, wb_ref, bb_ref)
    cm = _ssd_conv(cm_ref, xpc_ref, wc_ref, bc_ref)
    dt = jax.nn.softplus(dt_ref[0] + dtb_ref[0])
    dtt = jax.nn.softplus(dtt_ref[0] + dtbt_ref[0])
    dta = dt * (-jnp.exp(alog_ref[0]))
    dtat = dtt * (-jnp.exp(alogt_ref[0]))
    row = lax.broadcasted_iota(i32, (q, q), 0)
    col = lax.broadcasted_iota(i32, (q, q), 1)
    causal = row >= col
    cum = _dot_exact01(dta, causal.astype(bf16), left=True)
    cumt = _dot_exact01(dtat, (row <= col).astype(bf16))
    both = _dot_exact01(jnp.concatenate([dt, cum], axis=0), _head_expand(SSD_HEADDIM))
    dtx, cumx = both[:q], both[q:]
    cum128 = _dot_exact01(cum, _head_expand(q))
    cum_last = cumx[q - 1:q, :]
    xdt = xs * dtx
    xdt_b = xdt.astype(bf16)
    bm_b = bm.astype(bf16)
    cm_b = cm.astype(bf16)
    cb = _dot_nt(cm_b, bm_b)
    head = lax.broadcasted_iota(i32, (q, SSD_GW), 1) // SSD_HEADDIM
    y = jnp.zeros((q, SSD_GW), f32)
    for r in range(SSD_HPG):
        seg = cum128[:, r * q:(r + 1) * q] - cumt[r:r + 1, :]
        decay = jnp.where(causal, jnp.exp(jnp.where(causal, seg, 0.0)), 0.0)
        yr = _dot((cb * decay).astype(bf16), xdt_b)
        y = jnp.where(head == r, yr, y)
    stt = stt_ref[...]
    y = y + _dot(cm_b, stt.astype(bf16)) * jnp.exp(cumx) + dsk_ref[0] * xs
    to_end = jnp.exp(cum_last - cumx)
    stt_new = stt * jnp.exp(cum_last) + _dot_tn(bm_b, (xdt * to_end).astype(bf16))
    stt_ref[...] = stt_new
    yz = y * jax.nn.silu(z_ref[...])
    yn = yz * lax.rsqrt(jnp.mean(yz * yz, axis=-1, keepdims=True) + 1e-5) * ng_ref[...]
    y_ref[...] = yn.astype(y_ref.dtype)

    @pl.when(c == pl.num_programs(2) - 1)
    def _():
        st_ref[0] = stt_new.T.reshape(SSD_HPG, SSD_HEADDIM, SSD_STATE)


def _ssd_prompt_scan(xbc, z, dt_raw, sw, bsz, t):
    m = bsz * t
    nc = t // SSD_CHUNK
    q, gw, n, g_, hpg = SSD_CHUNK, SSD_GW, SSD_STATE, SSD_GROUPS, SSD_HPG
    dt_g = dt_raw.reshape(m, g_, hpg).transpose(1, 0, 2)
    dtt_g = dt_g.transpose(0, 2, 1)
    nxb = SSD_D_INNER // n
    row = lambda b, g, c: b * nc + c
    in_specs = [
        pl.BlockSpec((q, gw), lambda b, g, c: (row(b, g, c), g)),
        pl.BlockSpec((q, n), lambda b, g, c: (row(b, g, c), nxb + g)),
        pl.BlockSpec((q, n), lambda b, g, c: (row(b, g, c), nxb + g_ + g)),
        pl.BlockSpec((q, gw), lambda b, g, c: (row(b, g, c), g)),
        pl.BlockSpec((1, q, hpg), lambda b, g, c: (g, row(b, g, c), 0)),
        pl.BlockSpec((1, hpg, q), lambda b, g, c: (g, 0, row(b, g, c))),
        pl.BlockSpec((SSD_CONV, gw), lambda b, g, c: (0, g)),
        pl.BlockSpec((SSD_CONV, n), lambda b, g, c: (0, nxb + g)),
        pl.BlockSpec((SSD_CONV, n), lambda b, g, c: (0, nxb + g_ + g)),
        pl.BlockSpec((1, gw), lambda b, g, c: (0, g)),
        pl.BlockSpec((1, n), lambda b, g, c: (0, nxb + g)),
        pl.BlockSpec((1, n), lambda b, g, c: (0, nxb + g_ + g)),
        pl.BlockSpec((1, 1, hpg), lambda b, g, c: (g, 0, 0)),
        pl.BlockSpec((1, hpg, 1), lambda b, g, c: (g, 0, 0)),
        pl.BlockSpec((1, 1, hpg), lambda b, g, c: (g, 0, 0)),
        pl.BlockSpec((1, hpg, 1), lambda b, g, c: (g, 0, 0)),
        pl.BlockSpec((1, 1, gw), lambda b, g, c: (g, 0, 0)),
        pl.BlockSpec((1, gw), lambda b, g, c: (0, g)),
    ]
    y, st = pl.pallas_call(
        _ssd_body,
        name="ssd_scan",
        grid=(bsz, g_, nc),
        in_specs=in_specs,
        out_specs=[pl.BlockSpec((q, gw), lambda b, g, c: (row(b, g, c), g)),
                   pl.BlockSpec((1, hpg, SSD_HEADDIM, n), lambda b, g, c: (b, g, 0, 0))],
        out_shape=[jax.ShapeDtypeStruct((m, SSD_D_INNER), bf16),
                   jax.ShapeDtypeStruct((bsz, SSD_HEADS, SSD_HEADDIM, n), f32)],
        scratch_shapes=[pltpu.VMEM((n, gw), f32), pltpu.VMEM((q + 8, gw), f32),
                        pltpu.VMEM((q + 8, n), f32), pltpu.VMEM((q + 8, n), f32)],
        compiler_params=_cparams(("arbitrary", "arbitrary", "arbitrary")),
    )(xbc, xbc, xbc, z, dt_g, dtt_g, sw["conv_w"], sw["conv_w"], sw["conv_w"], sw["conv_b"], sw["conv_b"], sw["conv_b"],
      sw["dtb"], sw["dtbt"], sw["alog"], sw["alogt"], sw["dskx"], sw["norm_g"])
    return y, st


def _ssd_weights(w_in, conv_w, conv_b, dt_bias, a_log, d_skip, norm_g, w_out, wdt):
    g_, hpg = SSD_GROUPS, SSD_HPG
    return dict(
        w_z=w_in[:, :SSD_D_INNER].astype(wdt),
        w_xbc=w_in[:, SSD_D_INNER:SSD_D_INNER + SSD_CONV_DIM].astype(wdt),
        w_dt=w_in[:, SSD_D_INNER + SSD_CONV_DIM:].astype(wdt),
        conv_w=conv_w, conv_b=conv_b.reshape(1, SSD_CONV_DIM),
        dtb=dt_bias.reshape(g_, 1, hpg), dtbt=dt_bias.reshape(g_, hpg, 1),
        alog=a_log.reshape(g_, 1, hpg), alogt=a_log.reshape(g_, hpg, 1),
        dskx=jnp.repeat(d_skip, SSD_HEADDIM).reshape(g_, 1, SSD_GW),
        dsk=d_skip, dt_bias=dt_bias, a_log=a_log,
        norm_g=norm_g.reshape(1, SSD_D_INNER), w_out=w_out.astype(wdt))


def _ssd_prompt(x2d, sw, bsz, t, ln_g, ln_b):
    (z,) = _mm(x2d, sw["w_z"], tm=PROJ_TM, tn=1024, out_dtypes=(f32,), name="ssd_in_z")
    (xbc,) = _mm(x2d, sw["w_xbc"], tm=PROJ_TM, tn=1024, out_dtypes=(f32,), name="ssd_in_xbc")
    (dt_raw,) = _mm(x2d, sw["w_dt"], tm=PROJ_TM, tn=SSD_HEADS, out_dtypes=(f32,), name="ssd_in_dt")
    y, st = _ssd_prompt_scan(xbc, z, dt_raw, sw, bsz, t)
    h1 = _mm_ln(y, sw["w_out"], x2d, ln_g, ln_b, tm=PROJ_TM, name="ssd_out_ln")
    new_conv = xbc.reshape(bsz, t, SSD_CONV_DIM)[:, t - (SSD_CONV - 1):]
    return h1, new_conv, st


def _pad_rows(x, rows=8):
    return jnp.concatenate([x, jnp.zeros((rows - x.shape[0], x.shape[1]), x.dtype)], axis=0)


def _ssd_step_body(z_ref, xbc_ref, dt_ref, cs_ref, st_ref, cw_ref, cb_ref, dtb_ref, alog_ref, dsk_ref, ng_ref,
                   y_ref, nc_ref, ns_ref):
    n, gw, hd = SSD_STATE, SSD_GW, SSD_HEADDIM
    xbc = xbc_ref[0]
    cs = cs_ref[0]
    w = cw_ref[...]
    conv = cb_ref[...] + cs[0:1] * w[0:1]
    for k in range(1, SSD_CONV - 1):
        conv = conv + cs[k:k + 1] * w[k:k + 1]
    conv = jax.nn.silu(conv + xbc * w[SSD_CONV - 1:SSD_CONV])
    nc_ref[0] = jnp.concatenate([cs[1:], xbc], axis=0)
    xs = conv[:, :SSD_D_INNER]
    b_g = _pad_rows(jnp.concatenate(
        [conv[:, SSD_D_INNER + g * n:SSD_D_INNER + (g + 1) * n] for g in range(SSD_GROUPS)], axis=0))
    c_g = _pad_rows(jnp.concatenate(
        [conv[:, SSD_D_INNER + (SSD_GROUPS + g) * n:SSD_D_INNER + (SSD_GROUPS + g + 1) * n] for g in range(SSD_GROUPS)],
        axis=0))
    dt = jax.nn.softplus(dt_ref[0] + dtb_ref[...])
    decay = jnp.exp(dt * (-jnp.exp(alog_ref[...])))
    er = lax.broadcasted_iota(i32, (SSD_HEADS, SSD_D_INNER), 0)
    ec = lax.broadcasted_iota(i32, (SSD_HEADS, SSD_D_INNER), 1)
    per_head = _pad_rows(jnp.concatenate([dt, decay, dsk_ref[...]], axis=0))
    hx = jnp.dot(per_head, (ec // hd == er).astype(f32), precision=HIGHEST, preferred_element_type=f32)
    dtx, decx, dskx = hx[0:1], hx[1:2], hx[2:3]
    xdt = xs * dtx
    gr = lax.broadcasted_iota(i32, (8, SSD_D_INNER), 0)
    gc = lax.broadcasted_iota(i32, (8, SSD_D_INNER), 1)
    gmask = (gc // gw == gr).astype(f32)
    row0 = (gr == 0).astype(f32)
    st = st_ref[0].reshape(SSD_D_INNER, n)
    upd = _dot_tn(gmask * xdt, b_g, precision=HIGHEST)
    dec_full = _dot_tn(row0 * decx, jnp.ones((8, n), f32), precision=HIGHEST)
    ns_ref[0] = (st * dec_full + upd).reshape(SSD_HEADS, hd, n)
    cst = _dot_nt(c_g, st, precision=HIGHEST)
    y_state = jnp.sum(gmask * cst, axis=0, keepdims=True)
    cbx = jnp.sum(gmask * jnp.sum(c_g * b_g, axis=1, keepdims=True), axis=0, keepdims=True)
    y = cbx * xdt + decx * y_state + dskx * xs
    yz = y * jax.nn.silu(z_ref[0])
    parts = []
    for g in range(SSD_GROUPS):
        seg = yz[:, g * gw:(g + 1) * gw]
        parts.append(seg * lax.rsqrt(jnp.mean(seg * seg, axis=-1, keepdims=True) + 1e-5))
    y_ref[0] = jnp.concatenate(parts, axis=1) * ng_ref[...]


def _ssd_step(x2d, conv_state, ssm_state, sw, ln_g, ln_b):
    bsz = x2d.shape[0]
    (z,) = _mm(x2d, sw["w_z"], tm=bsz, tn=1024, out_dtypes=(f32,), name="ssd_in_z")
    (xbc,) = _mm(x2d, sw["w_xbc"], tm=bsz, tn=1024, out_dtypes=(f32,), name="ssd_in_xbc")
    (dt_raw,) = _mm(x2d, sw["w_dt"], tm=bsz, tn=SSD_HEADS, out_dtypes=(f32,), name="ssd_in_dt")
    rowspec = lambda width: pl.BlockSpec((1, 1, width), lambda b: (b, 0, 0))
    full = lambda r, c: pl.BlockSpec((r, c), lambda b: (0, 0))
    y, new_conv, new_state = pl.pallas_call(
        _ssd_step_body,
        name="ssd_step",
        grid=(bsz,),
        in_specs=[rowspec(SSD_D_INNER), rowspec(SSD_CONV_DIM), rowspec(SSD_HEADS),
                  pl.BlockSpec((1, SSD_CONV - 1, SSD_CONV_DIM), lambda b: (b, 0, 0)),
                  pl.BlockSpec((1, SSD_HEADS, SSD_HEADDIM, SSD_STATE), lambda b: (b, 0, 0, 0)),
                  full(SSD_CONV, SSD_CONV_DIM), full(1, SSD_CONV_DIM), full(1, SSD_HEADS), full(1, SSD_HEADS),
                  full(1, SSD_HEADS), full(1, SSD_D_INNER)],
        out_specs=[rowspec(SSD_D_INNER),
                   pl.BlockSpec((1, SSD_CONV - 1, SSD_CONV_DIM), lambda b: (b, 0, 0)),
                   pl.BlockSpec((1, SSD_HEADS, SSD_HEADDIM, SSD_STATE), lambda b: (b, 0, 0, 0))],
        out_shape=[jax.ShapeDtypeStruct((bsz, 1, SSD_D_INNER), f32),
                   jax.ShapeDtypeStruct((bsz, SSD_CONV - 1, SSD_CONV_DIM), f32),
                   jax.ShapeDtypeStruct((bsz, SSD_HEADS, SSD_HEADDIM, SSD_STATE), f32)],
        compiler_params=_cparams(("parallel",)),
    )(z.reshape(bsz, 1, -1), xbc.reshape(bsz, 1, -1), dt_raw.reshape(bsz, 1, -1), conv_state, ssm_state,
      sw["conv_w"], sw["conv_b"], sw["dt_bias"].reshape(1, -1), sw["a_log"].reshape(1, -1), sw["dsk"].reshape(1, -1),
      sw["norm_g"])
    h1 = _mm_ln(y.reshape(bsz, SSD_D_INNER), sw["w_out"], x2d, ln_g, ln_b, tm=bsz, name="ssd_out_ln")
    return h1, new_conv, new_state


def _slope(head):
    return 2.0 ** (-8.0 * (head + 1) / NSA_HEADS)


def _masked_softmax(s, mask):
    s = jnp.where(mask, s, NEG)
    m = jnp.max(s, axis=-1, keepdims=True)
    e = jnp.exp(s - m) * mask.astype(f32)
    return e / jnp.maximum(jnp.sum(e, axis=-1, keepdims=True), 1e-30)


def _tile_rows(x, n):
    return jnp.concatenate([x] * n, axis=0)


def _alibi_bias(ok, dist, kv_head):
    distf = dist.astype(f32)
    return jnp.concatenate(
        [jnp.where(ok, (-_slope(kv_head * NSA_REP + r)) * distf, NEG) for r in range(NSA_REP)], axis=0)


def _slope_col(kv_head, rows):
    return jnp.concatenate([jnp.full((rows, 1), _slope(kv_head * NSA_REP + r), f32) for r in range(NSA_REP)], axis=0)


def _topk_rank(imp):
    nb = imp.shape[1]
    lane = lax.broadcasted_iota(i32, imp.shape, 1)
    rank = jnp.zeros(imp.shape, i32)
    for j in range(nb):
        cj = imp[:, j:j + 1]
        rank = rank + ((cj > imp) | ((cj == imp) & (lane > j))).astype(i32)
    return rank


def _topk_mask_rows(imp, n_sel):
    it = imp.T
    nb = it.shape[0]
    sub = lax.broadcasted_iota(i32, it.shape, 0)
    rank = jnp.zeros(it.shape, i32)
    for j in range(nb):
        rj = it[j:j + 1, :]
        rank = rank + ((rj > it) | ((rj == it) & (sub > j))).astype(i32)
    return (rank < n_sel).astype(f32).T


def _pos_weights(pw_ref):
    w = jax.nn.softmax(pw_ref[...], axis=0)
    r = lax.broadcasted_iota(i32, (2 * NSA_KV, 2 * NSA_KVD), 0)
    c = lax.broadcasted_iota(i32, (2 * NSA_KV, 2 * NSA_KVD), 1)
    return jnp.dot(w, (c // NSA_DH == r).astype(f32), precision=HIGHEST, preferred_element_type=f32)


def _compress_body(x_ref, pw_ref, o_ref):
    wx = _pos_weights(pw_ref)
    x = x_ref[...]
    nb = x.shape[0] // NSA_BLOCK
    o_ref[0] = jnp.sum(x.reshape(nb, NSA_BLOCK, 2 * NSA_KVD) * wx[None], axis=1)


def _nsa_compress(kv4, pw, bsz, t):
    nb = t // NSA_BLOCK
    return pl.pallas_call(
        _compress_body,
        name="nsa_compress",
        grid=(bsz,),
        in_specs=[pl.BlockSpec((t, 2 * NSA_KVD), lambda b: (b, 0)),
                  pl.BlockSpec((NSA_BLOCK, 2 * NSA_KV), lambda b: (0, 0))],
        out_specs=pl.BlockSpec((1, nb, 2 * NSA_KVD), lambda b: (b, 0, 0)),
        out_shape=jax.ShapeDtypeStruct((bsz, nb, 2 * NSA_KVD), f32),
        compiler_params=_cparams(("parallel",)),
    )(kv4, pw)


NSA_TQ = 128
NSA_CK = 512


def _nsa_prompt_body(q_ref, g_ref, kcvc_ref, kvs_ref, kvw_ref, o_ref, *, t_len):
    i = pl.program_id(1)
    tq, ck, rep = NSA_TQ, NSA_CK, NSA_REP
    ck = min(ck, t_len)
    nb = t_len // NSA_BLOCK
    n_sel = min(NSA_TOP, nb)
    wlen = min(NSA_WINDOW + tq, t_len)
    t0 = i * tq
    t_col = t0 + lax.broadcasted_iota(i32, (tq, 1), 0)
    lane128 = lax.broadcasted_iota(i32, (1, LANES), 1)
    gates = g_ref[...]
    kcvc = kcvc_ref[0]
    blk = lax.broadcasted_iota(i32, (1, nb), 1)
    dist_c = t_col - ((blk + 1) * NSA_BLOCK - 1)
    dist_c4 = _tile_rows(dist_c, rep)
    cur = t_col // NSA_BLOCK
    n_chunks = (t0 + tq + ck - 1) // ck
    w_start = jnp.clip(t0 - NSA_WINDOW, 0, t_len - wlen)
    w_start = pl.multiple_of(w_start, tq)
    dist_w = t_col - (w_start + lax.broadcasted_iota(i32, (1, wlen), 1))
    mask_w = (dist_w >= 0) & (dist_w < NSA_WINDOW)

    for j in range(NSA_KV // 2):
        cols = slice(j * LANES, (j + 1) * LANES)
        vcols = slice(NSA_KVD + j * LANES, NSA_KVD + (j + 1) * LANES)
        q_tiles = [q_ref[:, (j * rep + r) * LANES:(j * rep + r + 1) * LANES] for r in range(rep)]
        q_rows = jnp.concatenate(q_tiles, axis=0)
        kc_b = kcvc[:, cols].astype(bf16)
        vc_b = kcvc[:, vcols].astype(bf16)
        out_tiles = [jnp.zeros((tq, LANES), f32) for _ in range(rep)]
        for h in range(2):
            k = 2 * j + h
            half = (lane128 // NSA_DH) == h
            q_pad = jnp.where(half, q_rows, jnp.zeros_like(q_rows))
            slope = _slope_col(k, tq)
            s_c = _dot_nt(q_pad, kc_b) - slope * dist_c4.astype(f32)
            p_c = _masked_softmax(s_c, dist_c4 >= 0)
            o_c = _dot(p_c.astype(bf16), vc_b)
            imp = sum(p_c[r * tq:(r + 1) * tq] for r in range(rep))
            imp = jnp.where((blk == cur) | (blk == 0), 1e4, imp)
            imp = jnp.where(blk > cur, -1.0, imp)
            sel_b = _topk_mask_rows(imp, n_sel).astype(bf16)

            def sel_step(c, carry, q_pad=q_pad, k=k, sel_b=sel_b, cols=cols, vcols=vcols):
                m, l, acc = carry
                k0 = pl.multiple_of(c * ck, ck)
                ks = kvs_ref[pl.ds(k0, ck), cols]
                vs = kvs_ref[pl.ds(k0, ck), vcols]
                kpos = k0 + lax.broadcasted_iota(i32, (1, ck), 1)
                brow = lax.broadcasted_iota(i32, (nb, ck), 0)
                expand = ((k0 + lax.broadcasted_iota(i32, (nb, ck), 1)) // NSA_BLOCK == brow).astype(bf16)
                selx = _dot(sel_b, expand)
                dist = t_col - kpos
                ok = (selx > 0.5) & (dist >= 0)
                s = _dot_nt(q_pad, ks) + _alibi_bias(ok, dist, k)
                m_new = jnp.maximum(m, jnp.max(s, axis=-1, keepdims=True))
                a = jnp.exp(m - m_new)
                p = jnp.exp(s - m_new)
                l = a * l + jnp.sum(p, axis=-1, keepdims=True)
                acc = a * acc + _dot(p.astype(bf16), vs)
                return m_new, l, acc

            init = (jnp.full((rep * tq, 1), NEG, f32), jnp.zeros((rep * tq, 1), f32), jnp.zeros((rep * tq, LANES), f32))
            _, l_s, acc_s = lax.fori_loop(0, n_chunks, sel_step, init)
            o_s = acc_s / jnp.maximum(l_s, 1e-30)
            kw = kvw_ref[pl.ds(w_start, wlen), cols]
            vw = kvw_ref[pl.ds(w_start, wlen), vcols]
            s_w = _dot_nt(q_pad, kw) + _alibi_bias(mask_w, dist_w, k)
            e_w = jnp.exp(s_w - jnp.max(s_w, axis=-1, keepdims=True))
            o_w = _dot(e_w.astype(bf16), vw) / jnp.sum(e_w, axis=-1, keepdims=True)
            for r in range(rep):
                gc = (k * rep + r) * 3
                rows = slice(r * tq, (r + 1) * tq)
                o_r = gates[:, gc:gc + 1] * o_c[rows] + gates[:, gc + 1:gc + 2] * o_s[rows] + gates[:, gc + 2:gc + 3] * o_w[rows]
                out_tiles[r] = jnp.where(half, o_r, out_tiles[r])
        for r in range(rep):
            o_ref[:, (j * rep + r) * LANES:(j * rep + r + 1) * LANES] = out_tiles[r].astype(o_ref.dtype)


def _nsa_prompt_attn(q, gates, kcvc, kv4_b, kvw_b, bsz, t):
    nq = t // NSA_TQ
    nb = t // NSA_BLOCK
    return pl.pallas_call(
        functools.partial(_nsa_prompt_body, t_len=t),
        name="nsa_attn",
        grid=(bsz, nq),
        in_specs=[pl.BlockSpec((NSA_TQ, NSA_QD), lambda b, i: (b * nq + i, 0)),
                  pl.BlockSpec((NSA_TQ, LANES), lambda b, i: (b * nq + i, 0)),
                  pl.BlockSpec((1, nb, 2 * NSA_KVD), lambda b, i: (b, 0, 0)),
                  pl.BlockSpec((t, 2 * NSA_KVD), lambda b, i: (b, 1)),
                  pl.BlockSpec((t, 2 * NSA_KVD), lambda b, i: (b, 0))],
        out_specs=pl.BlockSpec((NSA_TQ, NSA_QD), lambda b, i: (b * nq + i, 0)),
        out_shape=jax.ShapeDtypeStruct((bsz * t, NSA_QD), bf16),
        compiler_params=_cparams(("parallel", "arbitrary")),
    )(q, gates, kcvc, kv4_b, kvw_b)


def _pair_layout_cols(w):
    lead = w.shape[:-1]
    w = w.reshape(*lead, NSA_KV // 2, 2, NSA_REP, NSA_DH)
    return jnp.swapaxes(w, -3, -2).reshape(*lead, NSA_QD)


def _nsa_weights(w_in, pos_w, w_out, wdt, pair):
    lay = _pair_layout_cols if pair else (lambda w: w)
    kv0 = NSA_QD
    g0 = NSA_QD + 6 * NSA_KVD
    wg = jnp.zeros((D_MODEL, LANES), f32).at[:, :3 * NSA_HEADS].set(w_in[:, g0:])
    return dict(
        w_q=lay(w_in[:, :NSA_QD]).astype(wdt),
        w_kv4=w_in[:, kv0:kv0 + 4 * NSA_KVD].astype(wdt),
        w_kvw=w_in[:, kv0 + 4 * NSA_KVD:g0].astype(wdt),
        w_g=wg.astype(wdt),
        pw=pos_w.transpose(1, 0, 2).reshape(NSA_BLOCK, 2 * NSA_KV),
        pwt=pos_w.transpose(0, 2, 1).reshape(2 * NSA_KV, NSA_BLOCK),
        w_kvt=w_in[:, kv0:g0].T.astype(wdt),
        w_out=lay(w_out.T).T.astype(wdt))


def _nsa_prompt(x2d, nw, bsz, t, ln_g, ln_b):
    (q,) = _mm(x2d, nw["w_q"], tm=PROJ_TM, tn=1024, out_dtypes=(bf16,), scale=NSA_DH ** -0.5, name="nsa_in_q")
    kv4, kv4_b = _mm(x2d, nw["w_kv4"], tm=PROJ_TM, tn=1024, out_dtypes=(f32, bf16), name="nsa_in_kv4")
    kvw, kvw_b = _mm(x2d, nw["w_kvw"], tm=PROJ_TM, tn=512, out_dtypes=(f32, bf16), name="nsa_in_kvw")
    (gates,) = _mm(x2d, nw["w_g"], tm=PROJ_TM, tn=LANES, out_dtypes=(f32,), act="sigmoid", name="nsa_in_gates")
    kcvc = _nsa_compress(kv4, nw["pw"], bsz, t)
    o = _nsa_prompt_attn(q, gates, kcvc, kv4_b, kvw_b, bsz, t)
    h1 = _mm_ln(o, nw["w_out"], x2d, ln_g, ln_b, tm=PROJ_TM, name="nsa_out_ln")
    w_keep = min(NSA_WINDOW, t)
    new_kv = kv4.reshape(bsz, t, 4, NSA_KV, NSA_DH)
    new_win = kvw.reshape(bsz, t, 2, NSA_KV, NSA_DH)[:, t - w_keep:]
    return h1, new_kv, new_win


PAGES_PER_STEP = 16


def _lane_weights(pwt_ref):
    w = jax.nn.softmax(pwt_ref[...], axis=1)
    return jnp.concatenate([w] * (PAGE_SIZE // NSA_BLOCK), axis=1)


def _page_compress_body(pt_ref, *refs):
    del pt_ref
    page_refs, pwt_ref, o_ref = refs[:PAGES_PER_STEP], refs[PAGES_PER_STEP], refs[PAGES_PER_STEP + 1]
    s = pl.program_id(1)
    bpp = PAGE_SIZE // NSA_BLOCK
    nbp = o_ref.shape[-1]

    @pl.when(s == 0)
    def _():
        o_ref[...] = jnp.zeros_like(o_ref)

    w = _lane_weights(pwt_ref)
    tok_blk = lax.broadcasted_iota(i32, (1, PAGE_SIZE), 1) // NSA_BLOCK
    out_lane = lax.broadcasted_iota(i32, (1, nbp), 1)
    for c in range(2):
        for k in range(NSA_KV):
            wk = w[c * NSA_KV + k:c * NSA_KV + k + 1]
            upd = jnp.zeros((NSA_DH, nbp), f32)
            for i, p_ref in enumerate(page_refs):
                xw = p_ref[0, c, k] * wk
                first = (s * PAGES_PER_STEP + i) * bpp
                for h in range(bpp):
                    r = jnp.sum(jnp.where(tok_blk == h, xw, 0.0), axis=1, keepdims=True)
                    upd = upd + jnp.where(out_lane == first + h, r, 0.0)
            o_ref[0, c, k] += upd


def _page_compress(pool_t, page_table, pwt):
    bsz, n_pages = page_table.shape
    nbp = n_pages * (PAGE_SIZE // NSA_BLOCK)
    steps = n_pages // PAGES_PER_STEP
    assert n_pages % PAGES_PER_STEP == 0
    page_specs = [pl.BlockSpec((1, 2, NSA_KV, NSA_DH, PAGE_SIZE),
                               lambda b, s, pt, i=i: (pt[b, s * PAGES_PER_STEP + i], 0, 0, 0, 0))
                  for i in range(PAGES_PER_STEP)]
    return pl.pallas_call(
        _page_compress_body,
        name="nsa_page_compress",
        grid_spec=pltpu.PrefetchScalarGridSpec(
            num_scalar_prefetch=1, grid=(bsz, steps),
            in_specs=page_specs + [pl.BlockSpec((2 * NSA_KV, NSA_BLOCK), lambda b, s, pt: (0, 0))],
            out_specs=pl.BlockSpec((1, 2, NSA_KV, NSA_DH, nbp), lambda b, s, pt: (b, 0, 0, 0, 0))),
        out_shape=jax.ShapeDtypeStruct((bsz, 2, NSA_KV, NSA_DH, nbp), f32),
        compiler_params=_cparams(("parallel", "arbitrary")),
    )(page_table, *([pool_t] * PAGES_PER_STEP), pwt)


def _mm_nt_body(wt_ref, x_ref, o_ref):
    o_ref[...] = _mxu_nt(wt_ref[...], x_ref[...], _is_f32(wt_ref))


def _mm_nt(wt, x, *, tn, name):
    n, k = wt.shape
    m = x.shape[0]
    return pl.pallas_call(
        _mm_nt_body, name=f"{name}_r{m}", grid=(n // tn,),
        in_specs=[pl.BlockSpec((tn, k), lambda j: (j, 0)), pl.BlockSpec((m, k), lambda j: (0, 0))],
        out_specs=pl.BlockSpec((tn, m), lambda j: (j, 0)),
        out_shape=jax.ShapeDtypeStruct((n, m), f32),
        compiler_params=_cparams(("parallel",)),
    )(wt, x)


def _step_slopes(kv_head):
    r = lax.broadcasted_iota(i32, (8, 1), 0)
    return jnp.exp2(-0.5 * (kv_head * NSA_REP + r + 1).astype(f32))


def _gate_col(gates, kv_head, branch):
    cols = [(kv_head * NSA_REP + r) * 3 + branch for r in range(NSA_REP)]
    return jnp.concatenate([gates[:, c:c + 1] for c in cols], axis=0)


def _own_col(x, b):
    lane = lax.broadcasted_iota(i32, x.shape, x.ndim - 1)
    return jnp.sum(jnp.where(lane == b, x, 0.0), axis=-1, keepdims=True)


def _new_key_tile(col):
    lane = lax.broadcasted_iota(i32, (col.shape[0], LANES), 1)
    return jnp.where(lane == 0, col, 0.0)


def _nsa_step_cw_body(q_ref, g_ref, cmp_ref, kvt_ref, pwt_ref, win_ref, o_ref, idx_ref, *, past_len):
    b = pl.program_id(0)
    rep, kv = NSA_REP, NSA_KV
    nbp = past_len // NSA_BLOCK
    nbt = nbp + 8
    w_buf = win_ref.shape[-1]
    w0 = jax.nn.softmax(pwt_ref[...], axis=1)[:, 0:1]
    lane_c = lax.broadcasted_iota(i32, (1, nbp + LANES), 1)
    dist_c = past_len - ((lane_c + 1) * NSA_BLOCK - 1)
    lane_w = lax.broadcasted_iota(i32, (1, w_buf + LANES), 1)
    dist_w = jnp.where(lane_w <= w_buf, w_buf - lane_w, -1)
    mask_w = (dist_w >= 0) & (dist_w < NSA_WINDOW)
    blk = lax.broadcasted_iota(i32, (1, nbt), 1)
    cur = past_len // NSA_BLOCK
    gates = g_ref[0]
    for k in range(kv):
        q8 = _pad_rows(q_ref[0, k])
        slope = _step_slopes(k)
        new = [_own_col(kvt_ref[s, k], b) for s in (0, 1, 4, 5)]
        k_c = jnp.concatenate([cmp_ref[0, 0, k], _new_key_tile(new[0] * w0[k:k + 1])], axis=1)
        v_c = jnp.concatenate([cmp_ref[0, 1, k], _new_key_tile(new[1] * w0[kv + k:kv + k + 1])], axis=1)
        s_c = _mxu(q8, k_c, True) - slope * dist_c.astype(f32)
        p_c = _masked_softmax(s_c, jnp.broadcast_to(dist_c >= 0, s_c.shape))
        o_c = _mxu_nt(p_c, v_c, True)
        imp = jnp.sum(p_c[0:rep], axis=0, keepdims=True)[:, :nbt]
        imp = jnp.where((blk == cur) | (blk == 0), 1e4, imp)
        imp = jnp.where(blk > cur, -2.0, imp)
        rank = _topk_rank(imp)
        pick = lax.broadcasted_iota(i32, (NSA_TOP, nbt), 0) == rank
        idx_ref[0, k] = jnp.sum(jnp.where(pick, blk.astype(f32), 0.0), axis=1, keepdims=True).astype(i32)
        k_w = jnp.concatenate([win_ref[0, 0, k], _new_key_tile(new[2])], axis=1)
        v_w = jnp.concatenate([win_ref[0, 1, k], _new_key_tile(new[3])], axis=1)
        s_w = _mxu(q8, k_w, True) - slope * dist_w.astype(f32)
        p_w = _masked_softmax(s_w, jnp.broadcast_to(mask_w, s_w.shape))
        o_w = _mxu_nt(p_w, v_w, True)
        o_ref[0, k] = _gate_col(gates, k, 0) * o_c[0:rep] + _gate_col(gates, k, 2) * o_w[0:rep]


def _nsa_step_sel_body(idx_ref, pt_ref, *refs, past_len):
    del pt_ref
    n_in = 2 * NSA_TOP
    tile_refs = refs[:n_in]
    q_ref, g_ref, kvt_ref, part_ref, o_ref = refs[n_in:]
    b, k = pl.program_id(0), pl.program_id(1)
    nbp = past_len // NSA_BLOCK
    rep, lb, kv = NSA_REP, NSA_BLOCK, NSA_KV
    bpp = PAGE_SIZE // lb
    new_k = _new_key_tile(_own_col(kvt_ref[0, 0], b))
    new_v = _new_key_tile(_own_col(kvt_ref[1, 0], b))
    lane = lax.broadcasted_iota(i32, (1, NSA_TOP * PAGE_SIZE), 1)
    tok = lane % PAGE_SIZE
    spos = tok % lb
    valid = lane < 0
    ks, vs = [], []
    for i in range(NSA_TOP):
        n = idx_ref[(b * kv + k) * NSA_TOP + i]
        is_new = n >= nbp
        ks.append(jnp.where(is_new, new_k, tile_refs[2 * i][0, 0, 0]))
        vs.append(jnp.where(is_new, new_v, tile_refs[2 * i + 1][0, 0, 0]))
        mine = lane // PAGE_SIZE == i
        spos = spos + jnp.where(mine, n * lb, 0)
        half = jnp.where(is_new, 0, n % bpp)
        last = jnp.where(is_new, 0, lb - 1)
        valid = valid | (mine & (tok // lb == half) & (tok % lb <= last))
    k_all = jnp.concatenate(ks, axis=1)
    v_all = jnp.concatenate(vs, axis=1)
    dist = past_len - spos
    q8 = _pad_rows(q_ref[0, 0])
    s = _mxu(q8, k_all, True) - _step_slopes(k) * dist.astype(f32)
    p = _masked_softmax(s, jnp.broadcast_to(valid & (dist >= 0), s.shape))
    o_s = _mxu_nt(p, v_all, True)
    g = g_ref[0, 0]
    g_sel = jnp.concatenate([g[:, 3 * r + 1:3 * r + 2] for r in range(rep)], axis=0)
    o_ref[0, 0] = part_ref[0, 0] + g_sel * o_s[0:rep]


def _nsa_step(x2d, pool, page_table, win_cache, nw, ln_g, ln_b):
    bsz = x2d.shape[0]
    n_pages = page_table.shape[1]
    past_len = n_pages * PAGE_SIZE
    w_buf = win_cache.shape[1]
    rep, kv, dh = NSA_REP, NSA_KV, NSA_DH
    bpp = PAGE_SIZE // NSA_BLOCK
    nbp = past_len // NSA_BLOCK
    (q,) = _mm(x2d, nw["w_q"], tm=bsz, tn=1024, out_dtypes=(f32,), scale=NSA_DH ** -0.5, name="nsa_in_q")
    (kv4,) = _mm(x2d, nw["w_kv4"], tm=bsz, tn=1024, out_dtypes=(f32,), name="nsa_in_kv4")
    (kvw,) = _mm(x2d, nw["w_kvw"], tm=bsz, tn=512, out_dtypes=(f32,), name="nsa_in_kvw")
    (gates,) = _mm(x2d, nw["w_g"], tm=bsz, tn=LANES, out_dtypes=(f32,), act="sigmoid", name="nsa_in_gates")
    kvt = _mm_nt(nw["w_kvt"], x2d, tn=512, name="nsa_in_kvt").reshape(6, kv, dh, bsz)
    pool_t = jnp.transpose(pool, (0, 2, 3, 4, 1))
    win_t = jnp.transpose(win_cache, (0, 2, 3, 4, 1))
    cmp = _page_compress(pool_t, page_table, nw["pwt"])
    q4 = q.reshape(bsz, kv, rep, dh)
    slab = pl.BlockSpec((1, kv, rep, dh), lambda b: (b, 0, 0, 0))
    whole = lambda shape: pl.BlockSpec(shape, lambda b: (0,) * len(shape))
    part, idx = pl.pallas_call(
        functools.partial(_nsa_step_cw_body, past_len=past_len),
        name="nsa_step_cw",
        grid=(bsz,),
        in_specs=[slab, pl.BlockSpec((1, 1, LANES), lambda b: (b, 0, 0)),
                  pl.BlockSpec((1, 2, kv, dh, nbp), lambda b: (b, 0, 0, 0, 0)), whole((6, kv, dh, bsz)),
                  whole((2 * kv, NSA_BLOCK)), pl.BlockSpec((1, 2, kv, dh, w_buf), lambda b: (b, 0, 0, 0, 0))],
        out_specs=[slab, pl.BlockSpec((1, kv, NSA_TOP, 1), lambda b: (b, 0, 0, 0))],
        out_shape=[jax.ShapeDtypeStruct((bsz, kv, rep, dh), f32),
                   jax.ShapeDtypeStruct((bsz, kv, NSA_TOP, 1), i32)],
        compiler_params=_cparams(("parallel",)),
    )(q4, gates.reshape(bsz, 1, -1), cmp, kvt, nw["pwt"], win_t)

    def tile_spec(i, slot):
        def imap(b, k, idx_r, pt_r):
            n = jnp.minimum(idx_r[(b * kv + k) * NSA_TOP + i], nbp - 1)
            return (pt_r[b * n_pages + n // bpp], slot, k, 0, 0)
        return pl.BlockSpec((1, 1, 1, dh, PAGE_SIZE), imap)

    tile_specs = []
    for i in range(NSA_TOP):
        tile_specs += [tile_spec(i, 2), tile_spec(i, 3)]
    head = pl.BlockSpec((1, 1, rep, dh), lambda b, k, *_: (b, k, 0, 0))
    o = pl.pallas_call(
        functools.partial(_nsa_step_sel_body, past_len=past_len),
        name="nsa_step_sel",
        grid_spec=pltpu.PrefetchScalarGridSpec(
            num_scalar_prefetch=2, grid=(bsz, kv),
            in_specs=tile_specs + [
                head, pl.BlockSpec((1, 1, 1, 3 * rep), lambda b, k, *_: (b, k, 0, 0)),
                pl.BlockSpec((2, 1, dh, bsz), lambda b, k, *_: (1, k, 0, 0)), head],
            out_specs=head),
        out_shape=jax.ShapeDtypeStruct((bsz, kv, rep, dh), f32),
        compiler_params=_cparams(("arbitrary", "arbitrary")),
    )(idx.reshape(-1), page_table.reshape(-1), *([pool_t] * (2 * NSA_TOP)),
      q4, gates[:, :3 * NSA_HEADS].reshape(bsz, kv, 1, 3 * rep), kvt, part)
    h1 = _mm_ln(o.reshape(bsz, NSA_QD), nw["w_out"], x2d, ln_g, ln_b, tm=bsz, name="nsa_out_ln")
    new_kv = kv4.reshape(bsz, 1, 4, kv, dh)
    new_win = jnp.concatenate([win_cache[:, 1:], kvw.reshape(bsz, 1, 2, kv, dh)], axis=1)
    return h1, new_kv, new_win


def kernel(x_prompt, x_sample, state_ssm, state_conv, cache_kv, cache_win, page_table, p_prompt, p_sample,
           ssd_w_in, ssd_conv_w, ssd_conv_b, ssd_dt_bias, ssd_a_log, ssd_d, ssd_norm_g, ssd_w_out,
           nsa_w_in, nsa_pos_w, nsa_w_out, ln1_g, ln1_b, ln2_g, ln2_b, router_w, router_bias,
           moe_w_gate, moe_w_up, moe_w_down, ple_proj, ple_gate):
    bp, t, d = x_prompt.shape
    bs = x_sample.shape[0]
    xp = x_prompt.reshape(bp * t, d)
    xs = x_sample.reshape(bs, d)
    rb = router_bias.reshape(N_EXPERTS, 1)
    ssm_p, conv_p, kv_p, win_p, ssm_s, conv_s, kv_s, win_s = [], [], [], [], [], [], [], []
    for i in range(DEPTH):
        j = i // 2
        if i % 2 == 0:
            ssd = (ssd_w_in[j], ssd_conv_w[j], ssd_conv_b[j], ssd_dt_bias[j], ssd_a_log[j], ssd_d[j], ssd_norm_g[j],
                   ssd_w_out[j])
            h1p, c_new, s_new = _ssd_prompt(xp, _ssd_weights(*ssd, bf16), bp, t, ln1_g[i], ln1_b[i])
            conv_p.append(c_new)
            ssm_p.append(s_new)
            h1s, c_new, s_new = _ssd_step(xs, state_conv[j], state_ssm[j], _ssd_weights(*ssd, f32), ln1_g[i], ln1_b[i])
            conv_s.append(c_new)
            ssm_s.append(s_new)
        else:
            nsa = (nsa_w_in[j], nsa_pos_w[j], nsa_w_out[j])
            h1p, r_new, w_new = _nsa_prompt(xp, _nsa_weights(*nsa, bf16, True), bp, t, ln1_g[i], ln1_b[i])
            kv_p.append(r_new)
            win_p.append(w_new)
            h1s, r_new, w_new = _nsa_step(xs, cache_kv[j], page_table, cache_win[j], _nsa_weights(*nsa, f32, False),
                                          ln1_g[i], ln1_b[i])
            kv_s.append(r_new)
            win_s.append(w_new)

        def tail_weights(wdt, i=i):
            return dict(rwt=router_w.T.astype(wdt), rb=rb, layer=i, wg=moe_w_gate.astype(wdt), wu=moe_w_up.astype(wdt),
                        wd=moe_w_down.astype(wdt), ln2_g=ln2_g[i], ln2_b=ln2_b[i],
                        ple_gate=ple_gate[i].astype(wdt), ple_proj=ple_proj[i].astype(wdt))

        xp = _layer_tail(h1p, p_prompt[i].reshape(bp * t, PLE_DIM), tail_weights(bf16), tm=PROJ_TM)
        xs = _layer_tail(h1s, p_sample[i].reshape(bs, PLE_DIM), tail_weights(f32), tm=bs)
    return (xp.reshape(bp, t, d), xs.reshape(bs, 1, d), jnp.stack(ssm_p), jnp.stack(conv_p), jnp.stack(kv_p),
            jnp.stack(win_p), jnp.stack(ssm_s), jnp.stack(conv_s), jnp.stack(kv_s), jnp.stack(win_s))
```

```python
import functools

import jax
import jax.numpy as jnp
from jax import lax
from jax.experimental import pallas as pl
from jax.experimental.pallas import tpu as pltpu

f32, bf16, i32 = jnp.float32, jnp.bfloat16, jnp.int32
HIGHEST = lax.Precision.HIGHEST

D_MODEL = 1024
DEPTH = 2
PLE_DIM = 256
SSD_D_INNER = 2048
SSD_HEADDIM = 64
SSD_HEADS = 32
SSD_GROUPS = 4
SSD_HPG = 8
SSD_STATE = 128
SSD_CONV = 4
SSD_CHUNK = 128
SSD_GW = SSD_HPG * SSD_HEADDIM
SSD_CONV_DIM = SSD_D_INNER + 2 * SSD_GROUPS * SSD_STATE
NSA_HEADS = 16
NSA_KV = 4
NSA_REP = 4
NSA_DH = 64
NSA_BLOCK = 64
NSA_TOP = 16
NSA_WINDOW = 512
NSA_QD = NSA_HEADS * NSA_DH
NSA_KVD = NSA_KV * NSA_DH
PAGE_SIZE = 128
N_EXPERTS = 16
N_EGROUPS = 4
EPG = 4
D_FF = 512
ALPHA = (2.0 * DEPTH) ** 0.25
LN_EPS = 1e-5
NEG = -1e30
LANES = 128
VMEM_LIMIT = 56 * 1024 * 1024
PROJ_TM = 1024


def _cparams(sem):
    return pltpu.CompilerParams(dimension_semantics=sem, vmem_limit_bytes=VMEM_LIMIT)


def _ln(v, g, b):
    mu = jnp.mean(v, axis=-1, keepdims=True)
    d = v - mu
    var = jnp.mean(d * d, axis=-1, keepdims=True)
    return d * lax.rsqrt(var + LN_EPS) * g + b


def _dot(a, b):
    return jnp.dot(a, b, preferred_element_type=f32)


def _mxu(a, b, precise):
    if precise:
        return jnp.dot(a.astype(f32), b.astype(f32), precision=HIGHEST, preferred_element_type=f32)
    return jnp.dot(a.astype(bf16), b.astype(bf16), preferred_element_type=f32)


def _mxu_nt(a, b, precise):
    if precise:
        return _dot_nt(a.astype(f32), b.astype(f32), precision=HIGHEST)
    return _dot_nt(a.astype(bf16), b.astype(bf16))


def _dot_nt(a, b, precision=None):
    return lax.dot_general(a, b, (((1,), (1,)), ((), ())), precision=precision, preferred_element_type=f32)


def _dot_tn(a, b, precision=None):
    return lax.dot_general(a, b, (((0,), (0,)), ((), ())), precision=precision, preferred_element_type=f32)


def _is_f32(ref):
    return ref.dtype == jnp.float32


def _mm_body(x_ref, w_ref, *o_refs, act, scale):
    acc = _mxu(x_ref[...], w_ref[...], _is_f32(w_ref))
    if scale != 1.0:
        acc = acc * scale
    if act == "sigmoid":
        acc = jax.nn.sigmoid(acc)
    for o_ref in o_refs:
        o_ref[...] = acc.astype(o_ref.dtype)


def _mm(x, w, *, tm, tn, out_dtypes, name, act=None, scale=1.0):
    m, k = x.shape
    n = w.shape[1]
    tm, tn = min(tm, m), min(tn, n)
    assert m % tm == 0 and n % tn == 0
    outs = pl.pallas_call(
        functools.partial(_mm_body, act=act, scale=scale),
        name=f"{name}_r{m}",
        grid=(m // tm, n // tn),
        in_specs=[pl.BlockSpec((tm, k), lambda i, j: (i, 0)), pl.BlockSpec((k, tn), lambda i, j: (0, j))],
        out_specs=[pl.BlockSpec((tm, tn), lambda i, j: (i, j)) for _ in out_dtypes],
        out_shape=[jax.ShapeDtypeStruct((m, n), dt) for dt in out_dtypes],
        compiler_params=_cparams(("parallel", "arbitrary")),
    )(x, w)
    return outs


def _mm_ln_body(x_ref, w_ref, res_ref, g_ref, b_ref, o_ref):
    acc = _mxu(x_ref[...], w_ref[...], _is_f32(w_ref))
    o_ref[...] = _ln(ALPHA * res_ref[...] + acc, g_ref[...], b_ref[...])


def _mm_ln(x, w, res, g, b, *, tm, name):
    m, k = x.shape
    n = w.shape[1]
    tm = min(tm, m)
    assert m % tm == 0
    return pl.pallas_call(
        _mm_ln_body,
        name=f"{name}_r{m}",
        grid=(m // tm,),
        in_specs=[pl.BlockSpec((tm, k), lambda i: (i, 0)), pl.BlockSpec((k, n), lambda i: (0, 0)),
                  pl.BlockSpec((tm, n), lambda i: (i, 0)), pl.BlockSpec((1, n), lambda i: (0, 0)),
                  pl.BlockSpec((1, n), lambda i: (0, 0))],
        out_specs=pl.BlockSpec((tm, n), lambda i: (i, 0)),
        out_shape=jax.ShapeDtypeStruct((m, n), f32),
        compiler_params=_cparams(("parallel",)),
    )(x, w, res, g.reshape(1, n), b.reshape(1, n))


def _router_body(h_ref, rwt_ref, rb_ref, gate_ref, best_ref):
    logits = _mxu_nt(rwt_ref[...], h_ref[...], _is_f32(rwt_ref))
    score = jax.nn.sigmoid(logits)
    sel = score + rb_ref[...]
    gsum = []
    for g in range(N_EGROUPS):
        a, b, c, d = (sel[EPG * g + i:EPG * g + i + 1] for i in range(EPG))
        hi1, lo1, hi2, lo2 = jnp.maximum(a, b), jnp.minimum(a, b), jnp.maximum(c, d), jnp.minimum(c, d)
        gsum.append(jnp.maximum(hi1, hi2) + jnp.maximum(jnp.minimum(hi1, hi2), jnp.maximum(lo1, lo2)))
    best = jnp.zeros_like(gsum[0], dtype=i32)
    top = gsum[0]
    for g in range(1, N_EGROUPS):
        upd = gsum[g] > top
        best = jnp.where(upd, g, best)
        top = jnp.where(upd, gsum[g], top)
    selg = sel[0:EPG]
    scg = score[0:EPG]
    for g in range(1, N_EGROUPS):
        selg = jnp.where(best == g, sel[EPG * g:EPG * (g + 1)], selg)
        scg = jnp.where(best == g, score[EPG * g:EPG * (g + 1)], scg)
    rows = [selg[i:i + 1] for i in range(EPG)]
    chosen = []
    for i in range(EPG):
        rank = jnp.zeros_like(best)
        for j in range(EPG):
            if j == i:
                continue
            ahead = (rows[j] > rows[i]) | ((rows[j] == rows[i]) if j < i else False)
            rank = rank + ahead.astype(i32)
        chosen.append(rank < 2)
    wsum = sum(jnp.where(chosen[i], scg[i:i + 1], 0.0) for i in range(EPG))
    gates = [jnp.where(chosen[i], scg[i:i + 1] / wsum, 0.0) for i in range(EPG)]
    out_rows = []
    for g in range(N_EGROUPS):
        for i in range(EPG):
            out_rows.append(jnp.where(best == g, gates[i], 0.0))
    gate_ref[...] = jnp.concatenate(out_rows, axis=0).T
    best_ref[...] = best


def _router(h, rwt, rb, *, tm):
    m, d = h.shape
    tm = min(tm, m)
    return pl.pallas_call(
        _router_body,
        name=f"router_r{m}",
        grid=(m // tm,),
        in_specs=[pl.BlockSpec((tm, d), lambda i: (i, 0)), pl.BlockSpec((N_EXPERTS, d), lambda i: (0, 0)),
                  pl.BlockSpec((N_EXPERTS, 1), lambda i: (0, 0))],
        out_specs=[pl.BlockSpec((tm, N_EXPERTS), lambda i: (i, 0)), pl.BlockSpec((1, tm), lambda i: (0, i))],
        out_shape=[jax.ShapeDtypeStruct((m, N_EXPERTS), f32), jax.ShapeDtypeStruct((1, m), i32)],
        compiler_params=_cparams(("parallel",)),
    )(h, rwt, rb)


def _moe_body(h_ref, gate_ref, wg_ref, wu_ref, wd_ref, g2_ref, b2_ref, o_ref, acc_ref):
    e = pl.program_id(1)

    @pl.when(e == 0)
    def _():
        acc_ref[...] = jnp.zeros_like(acc_ref)

    precise = _is_f32(wg_ref)
    h = h_ref[...]
    hid = jax.nn.silu(_mxu(h, wg_ref[0, 0], precise)) * _mxu(h, wu_ref[0, 0], precise)
    out = _mxu(hid, wd_ref[0, 0], precise)
    gate = gate_ref[...]
    lane = lax.broadcasted_iota(i32, gate.shape, 1)
    gcol = jnp.sum(jnp.where(lane == e, gate, 0.0), axis=1, keepdims=True)
    acc_ref[...] += gcol * out

    @pl.when(e == N_EXPERTS - 1)
    def _():
        o_ref[...] = _ln(ALPHA * h_ref[...] + acc_ref[...], g2_ref[...], b2_ref[...])


def _moe_ln(h, gate, wg, wu, wd, g2, b2, *, tm, layer):
    m, d = h.shape
    tm = min(tm, m)
    return pl.pallas_call(
        _moe_body,
        name=f"moe_ln_r{m}",
        grid=(m // tm, N_EXPERTS),
        in_specs=[pl.BlockSpec((tm, d), lambda i, e: (i, 0)), pl.BlockSpec((tm, N_EXPERTS), lambda i, e: (i, 0)),
                  pl.BlockSpec((1, 1, d, D_FF), lambda i, e: (layer, e, 0, 0)),
                  pl.BlockSpec((1, 1, d, D_FF), lambda i, e: (layer, e, 0, 0)),
                  pl.BlockSpec((1, 1, D_FF, d), lambda i, e: (layer, e, 0, 0)),
                  pl.BlockSpec((1, d), lambda i, e: (0, 0)), pl.BlockSpec((1, d), lambda i, e: (0, 0))],
        out_specs=pl.BlockSpec((tm, d), lambda i, e: (i, 0)),
        out_shape=jax.ShapeDtypeStruct((m, d), f32),
        scratch_shapes=[pltpu.VMEM((tm, d), f32)],
        compiler_params=_cparams(("parallel", "arbitrary")),
    )(h, gate, wg, wu, wd, g2.reshape(1, d), b2.reshape(1, d))


def _ple_body(h_ref, p_ref, wg_ref, wp_ref, o_ref):
    h = h_ref[...]
    precise = _is_f32(wg_ref)
    gate = jax.nn.sigmoid(_mxu(h, wg_ref[...], precise))
    o_ref[...] = h + gate * _mxu(p_ref[...], wp_ref[...], precise)


def _ple(h, p, wg, wp, *, tm):
    m, d = h.shape
    tm = min(tm, m)
    return pl.pallas_call(
        _ple_body,
        name=f"ple_r{m}",
        grid=(m // tm,),
        in_specs=[pl.BlockSpec((tm, d), lambda i: (i, 0)), pl.BlockSpec((tm, PLE_DIM), lambda i: (i, 0)),
                  pl.BlockSpec((d, d), lambda i: (0, 0)), pl.BlockSpec((PLE_DIM, d), lambda i: (0, 0))],
        out_specs=pl.BlockSpec((tm, d), lambda i: (i, 0)),
        out_shape=jax.ShapeDtypeStruct((m, d), f32),
        compiler_params=_cparams(("parallel",)),
    )(h, p, wg, wp)


MOE_TM = 1024
MOE_CH = 304
MOE_ALIGN = 16
MOE_ROWS = MOE_TM + N_EGROUPS * MOE_ALIGN + MOE_CH


def _moe_grouped_body(h_ref, gate_ref, best_ref, wg_ref, wu_ref, wd_ref, g2_ref, b2_ref, o_ref,
                      p_ref, xs_ref, ys_ref, gs_ref, meta_ref):
    e = pl.program_id(1)
    tm, rows, ng = MOE_TM, MOE_ROWS, N_EGROUPS

    @pl.when(e == 0)
    def _():
        best = best_ref[...]
        ind = (best == lax.broadcasted_iota(i32, (8, tm), 0)).astype(f32)
        prefix = ind
        lane = lax.broadcasted_iota(i32, (8, tm), 1)
        shift = 1
        while shift < tm:
            prefix = prefix + jnp.where(lane >= shift, pltpu.roll(prefix, shift, axis=1), 0.0)
            shift *= 2
        cnt = prefix[:, tm - 1:tm]
        cnt_al = jnp.floor((cnt + (MOE_ALIGN - 1)) * (1.0 / MOE_ALIGN)) * MOE_ALIGN
        offs = [jnp.zeros((1, 1), f32)]
        for g in range(1, ng):
            offs.append(offs[-1] + cnt_al[g - 1:g])
        off_col = jnp.concatenate(offs + [jnp.zeros((8 - ng, 1), f32)], axis=0)
        pos = jnp.sum(ind * (off_col + prefix - 1.0), axis=0, keepdims=True).astype(i32)
        perm = (lax.broadcasted_iota(i32, (rows, tm), 0) == pos).astype(bf16)
        p_ref[...] = perm
        xs_ref[...] = _dot(perm, h_ref[...].astype(bf16)).astype(bf16)
        parts = _dot(perm, jnp.concatenate(_split3(gate_ref[...]), axis=1))
        gs_ref[...] = (parts[:, :N_EXPERTS] + parts[:, N_EXPERTS:2 * N_EXPERTS]) + parts[:, 2 * N_EXPERTS:]
        ys_ref[...] = jnp.zeros_like(ys_ref)
        for g in range(ng):
            meta_ref[g] = offs[g][0, 0].astype(i32)
            meta_ref[ng + g] = cnt[g:g + 1][0, 0].astype(i32)

    g = e // EPG
    start = meta_ref[g]
    n_ch = (meta_ref[ng + g] + (MOE_CH - 1)) // MOE_CH

    def chunk(c, carry):
        r0 = pl.multiple_of(start + c * MOE_CH, MOE_ALIGN)
        x = xs_ref[pl.ds(r0, MOE_CH), :]
        hid = jax.nn.silu(_dot(x, wg_ref[0, 0])) * _dot(x, wu_ref[0, 0])
        y = _dot(hid.astype(bf16), wd_ref[0, 0])
        gs = gs_ref[pl.ds(r0, MOE_CH), :]
        lane = lax.broadcasted_iota(i32, gs.shape, 1)
        gcol = jnp.sum(jnp.where(lane == e, gs, 0.0), axis=1, keepdims=True)
        ys_ref[pl.ds(r0, MOE_CH), :] += gcol * y
        return carry

    lax.fori_loop(0, n_ch, chunk, 0)

    @pl.when(e == N_EXPERTS - 1)
    def _():
        ys = ys_ref[...]
        hi = ys.astype(bf16)
        lo = (ys - hi.astype(f32)).astype(bf16)
        perm = p_ref[...]
        out = _dot_tn(perm, hi) + _dot_tn(perm, lo)
        o_ref[...] = _ln(ALPHA * h_ref[...] + out, g2_ref[...], b2_ref[...])


def _moe_grouped_ln(h, gate, best, wg, wu, wd, g2, b2, *, layer):
    m, d = h.shape
    tm = MOE_TM
    assert m % tm == 0
    return pl.pallas_call(
        _moe_grouped_body,
        name=f"moe_grouped_ln_r{m}",
        grid=(m // tm, N_EXPERTS),
        in_specs=[pl.BlockSpec((tm, d), lambda i, e: (i, 0)), pl.BlockSpec((tm, N_EXPERTS), lambda i, e: (i, 0)),
                  pl.BlockSpec((1, tm), lambda i, e: (0, i)),
                  pl.BlockSpec((1, 1, d, D_FF), lambda i, e: (layer, e, 0, 0)),
                  pl.BlockSpec((1, 1, d, D_FF), lambda i, e: (layer, e, 0, 0)),
                  pl.BlockSpec((1, 1, D_FF, d), lambda i, e: (layer, e, 0, 0)),
                  pl.BlockSpec((1, d), lambda i, e: (0, 0)), pl.BlockSpec((1, d), lambda i, e: (0, 0))],
        out_specs=pl.BlockSpec((tm, d), lambda i, e: (i, 0)),
        out_shape=jax.ShapeDtypeStruct((m, d), f32),
        scratch_shapes=[pltpu.VMEM((MOE_ROWS, tm), bf16), pltpu.VMEM((MOE_ROWS, d), bf16),
                        pltpu.VMEM((MOE_ROWS, d), f32), pltpu.VMEM((MOE_ROWS, N_EXPERTS), f32),
                        pltpu.SMEM((2 * N_EGROUPS,), i32)],
        compiler_params=_cparams(("arbitrary", "arbitrary")),
    )(h, gate, best, wg, wu, wd, g2.reshape(1, d), b2.reshape(1, d))


def _layer_tail(h1, p, tw, *, tm):
    gate, best = _router(h1, tw["rwt"], tw["rb"], tm=tm)
    if h1.shape[0] % MOE_TM == 0 and tw["wg"].dtype == bf16:
        h2 = _moe_grouped_ln(h1, gate, best, tw["wg"], tw["wu"], tw["wd"], tw["ln2_g"], tw["ln2_b"], layer=tw["layer"])
    else:
        h2 = _moe_ln(h1, gate, tw["wg"], tw["wu"], tw["wd"], tw["ln2_g"], tw["ln2_b"], tm=tm, layer=tw["layer"])
    return _ple(h2, p, tw["ple_gate"], tw["ple_proj"], tm=tm)


def _ssd_conv(x_ref, xp_ref, w_ref, b_ref):
    q = x_ref.shape[0]
    xp_ref[8:8 + q, :] = x_ref[...]
    w = w_ref[...]
    acc = b_ref[...] + xp_ref[5:5 + q, :] * w[0:1]
    for k in range(1, SSD_CONV):
        acc = acc + xp_ref[5 + k:5 + k + q, :] * w[k:k + 1]
    xp_ref[0:8, :] = xp_ref[q:q + 8, :]
    return jax.nn.silu(acc)


def _head_expand(width):
    r = lax.broadcasted_iota(i32, (SSD_HPG, SSD_HPG * width), 0)
    c = lax.broadcasted_iota(i32, (SSD_HPG, SSD_HPG * width), 1)
    return (c // width == r).astype(bf16)


def _split3(v):
    hi = v.astype(bf16)
    r1 = v - hi.astype(f32)
    mid = r1.astype(bf16)
    return hi, mid, (r1 - mid.astype(f32)).astype(bf16)


def _dot_exact01(v, onehot, left=False):
    parts = _split3(v)
    prods = [_dot(onehot, p) if left else _dot(p, onehot) for p in parts]
    return (prods[0] + prods[1]) + prods[2]


def _ssd_body(xs_ref, bm_ref, cm_ref, z_ref, dt_ref, dtt_ref, wx_ref, wb_ref, wc_ref, bx_ref, bb_ref, bc_ref,
              dtb_ref, dtbt_ref, alog_ref, alogt_ref, dsk_ref, ng_ref, y_ref, st_ref,
              stt_ref, xpx_ref, xpb_ref, xpc_ref):
    c = pl.program_id(2)
    q = SSD_CHUNK

    @pl.when(c == 0)
    def _():
        stt_ref[...] = jnp.zeros_like(stt_ref)
        xpx_ref[0:8, :] = jnp.zeros((8, xpx_ref.shape[1]), f32)
        xpb_ref[0:8, :] = jnp.zeros((8, xpb_ref.shape[1]), f32)
        xpc_ref[0:8, :] = jnp.zeros((8, xpc_ref.shape[1]), f32)

    xs = _ssd_conv(xs_ref, xpx_ref, wx_ref, bx_ref)
    bm = _ssd_conv(bm_ref, xpb_ref, wb_ref, bb_ref)
    cm = _ssd_conv(cm_ref, xpc_ref, wc_ref, bc_ref)
    dt = jax.nn.softplus(dt_ref[0] + dtb_ref[0])
    dtt = jax.nn.softplus(dtt_ref[0] + dtbt_ref[0])
    dta = dt * (-jnp.exp(alog_ref[0]))
    dtat = dtt * (-jnp.exp(alogt_ref[0]))
    row = lax.broadcasted_iota(i32, (q, q), 0)
    col = lax.broadcasted_iota(i32, (q, q), 1)
    causal = row >= col
    cum = _dot_exact01(dta, causal.astype(bf16), left=True)
    cumt = _dot_exact01(dtat, (row <= col).astype(bf16))
    both = _dot_exact01(jnp.concatenate([dt, cum], axis=0), _head_expand(SSD_HEADDIM))
    dtx, cumx = both[:q], both[q:]
    cum128 = _dot_exact01(cum, _head_expand(q))
    cum_last = cumx[q - 1:q, :]
    xdt = xs * dtx
    xdt_b = xdt.astype(bf16)
    bm_b = bm.astype(bf16)
    cm_b = cm.astype(bf16)
    cb = _dot_nt(cm_b, bm_b)
    head = lax.broadcasted_iota(i32, (q, SSD_GW), 1) // SSD_HEADDIM
    y = jnp.zeros((q, SSD_GW), f32)
    for r in range(SSD_HPG):
        seg = cum128[:, r * q:(r + 1) * q] - cumt[r:r + 1, :]
        decay = jnp.where(causal, jnp.exp(jnp.where(causal, seg, 0.0)), 0.0)
        yr = _dot((cb * decay).astype(bf16), xdt_b)
        y = jnp.where(head == r, yr, y)
    stt = stt_ref[...]
    y = y + _dot(cm_b, stt.astype(bf16)) * jnp.exp(cumx) + dsk_ref[0] * xs
    to_end = jnp.exp(cum_last - cumx)
    stt_new = stt * jnp.exp(cum_last) + _dot_tn(bm_b, (xdt * to_end).astype(bf16))
    stt_ref[...] = stt_new
    yz = y * jax.nn.silu(z_ref[...])
    yn = yz * lax.rsqrt(jnp.mean(yz * yz, axis=-1, keepdims=True) + 1e-5) * ng_ref[...]
    y_ref[...] = yn.astype(y_ref.dtype)

    @pl.when(c == pl.num_programs(2) - 1)
    def _():
        st_ref[0] = stt_new.T.reshape(SSD_HPG, SSD_HEADDIM, SSD_STATE)


def _ssd_prompt_scan(xbc, z, dt_raw, sw, bsz, t):
    m = bsz * t
    nc = t // SSD_CHUNK
    q, gw, n, g_, hpg = SSD_CHUNK, SSD_GW, SSD_STATE, SSD_GROUPS, SSD_HPG
    dt_g = dt_raw.reshape(m, g_, hpg).transpose(1, 0, 2)
    dtt_g = dt_g.transpose(0, 2, 1)
    nxb = SSD_D_INNER // n
    row = lambda b, g, c: b * nc + c
    in_specs = [
        pl.BlockSpec((q, gw), lambda b, g, c: (row(b, g, c), g)),
        pl.BlockSpec((q, n), lambda b, g, c: (row(b, g, c), nxb + g)),
        pl.BlockSpec((q, n), lambda b, g, c: (row(b, g, c), nxb + g_ + g)),
        pl.BlockSpec((q, gw), lambda b, g, c: (row(b, g, c), g)),
        pl.BlockSpec((1, q, hpg), lambda b, g, c: (g, row(b, g, c), 0)),
        pl.BlockSpec((1, hpg, q), lambda b, g, c: (g, 0, row(b, g, c))),
        pl.BlockSpec((SSD_CONV, gw), lambda b, g, c: (0, g)),
        pl.BlockSpec((SSD_CONV, n), lambda b, g, c: (0, nxb + g)),
        pl.BlockSpec((SSD_CONV, n), lambda b, g, c: (0, nxb + g_ + g)),
        pl.BlockSpec((1, gw), lambda b, g, c: (0, g)),
        pl.BlockSpec((1, n), lambda b, g, c: (0, nxb + g)),
        pl.BlockSpec((1, n), lambda b, g, c: (0, nxb + g_ + g)),
        pl.BlockSpec((1, 1, hpg), lambda b, g, c: (g, 0, 0)),
        pl.BlockSpec((1, hpg, 1), lambda b, g, c: (g, 0, 0)),
        pl.BlockSpec((1, 1, hpg), lambda b, g, c: (g, 0, 0)),
        pl.BlockSpec((1, hpg, 1), lambda b, g, c: (g, 0, 0)),
        pl.BlockSpec((1, 1, gw), lambda b, g, c: (g, 0, 0)),
        pl.BlockSpec((1, gw), lambda b, g, c: (0, g)),
    ]
    y, st = pl.pallas_call(
        _ssd_body,
        name="ssd_scan",
        grid=(bsz, g_, nc),
        in_specs=in_specs,
        out_specs=[pl.BlockSpec((q, gw), lambda b, g, c: (row(b, g, c), g)),
                   pl.BlockSpec((1, hpg, SSD_HEADDIM, n), lambda b, g, c: (b, g, 0, 0))],
        out_shape=[jax.ShapeDtypeStruct((m, SSD_D_INNER), bf16),
                   jax.ShapeDtypeStruct((bsz, SSD_HEADS, SSD_HEADDIM, n), f32)],
        scratch_shapes=[pltpu.VMEM((n, gw), f32), pltpu.VMEM((q + 8, gw), f32),
                        pltpu.VMEM((q + 8, n), f32), pltpu.VMEM((q + 8, n), f32)],
        compiler_params=_cparams(("arbitrary", "arbitrary", "arbitrary")),
    )(xbc, xbc, xbc, z, dt_g, dtt_g, sw["conv_w"], sw["conv_w"], sw["conv_w"], sw["conv_b"], sw["conv_b"], sw["conv_b"],
      sw["dtb"], sw["dtbt"], sw["alog"], sw["alogt"], sw["dskx"], sw["norm_g"])
    return y, st


def _ssd_weights(w_in, conv_w, conv_b, dt_bias, a_log, d_skip, norm_g, w_out, wdt):
    g_, hpg = SSD_GROUPS, SSD_HPG
    return dict(
        w_z=w_in[:, :SSD_D_INNER].astype(wdt),
        w_xbc=w_in[:, SSD_D_INNER:SSD_D_INNER + SSD_CONV_DIM].astype(wdt),
        w_dt=w_in[:, SSD_D_INNER + SSD_CONV_DIM:].astype(wdt),
        conv_w=conv_w, conv_b=conv_b.reshape(1, SSD_CONV_DIM),
        dtb=dt_bias.reshape(g_, 1, hpg), dtbt=dt_bias.reshape(g_, hpg, 1),
        alog=a_log.reshape(g_, 1, hpg), alogt=a_log.reshape(g_, hpg, 1),
        dskx=jnp.repeat(d_skip, SSD_HEADDIM).reshape(g_, 1, SSD_GW),
        dsk=d_skip, dt_bias=dt_bias, a_log=a_log,
        norm_g=norm_g.reshape(1, SSD_D_INNER), w_out=w_out.astype(wdt))


def _ssd_prompt(x2d, sw, bsz, t, ln_g, ln_b):
    (z,) = _mm(x2d, sw["w_z"], tm=PROJ_TM, tn=1024, out_dtypes=(f32,), name="ssd_in_z")
    (xbc,) = _mm(x2d, sw["w_xbc"], tm=PROJ_TM, tn=1024, out_dtypes=(f32,), name="ssd_in_xbc")
    (dt_raw,) = _mm(x2d, sw["w_dt"], tm=PROJ_TM, tn=SSD_HEADS, out_dtypes=(f32,), name="ssd_in_dt")
    y, st = _ssd_prompt_scan(xbc, z, dt_raw, sw, bsz, t)
    h1 = _mm_ln(y, sw["w_out"], x2d, ln_g, ln_b, tm=PROJ_TM, name="ssd_out_ln")
    new_conv = xbc.reshape(bsz, t, SSD_CONV_DIM)[:, t - (SSD_CONV - 1):]
    return h1, new_conv, st


def _pad_rows(x, rows=8):
    return jnp.concatenate([x, jnp.zeros((rows - x.shape[0], x.shape[1]), x.dtype)], axis=0)


def _ssd_step_body(z_ref, xbc_ref, dt_ref, cs_ref, st_ref, cw_ref, cb_ref, dtb_ref, alog_ref, dsk_ref, ng_ref,
                   y_ref, nc_ref, ns_ref):
    n, gw, hd = SSD_STATE, SSD_GW, SSD_HEADDIM
    xbc = xbc_ref[0]
    cs = cs_ref[0]
    w = cw_ref[...]
    conv = cb_ref[...] + cs[0:1] * w[0:1]
    for k in range(1, SSD_CONV - 1):
        conv = conv + cs[k:k + 1] * w[k:k + 1]
    conv = jax.nn.silu(conv + xbc * w[SSD_CONV - 1:SSD_CONV])
    nc_ref[0] = jnp.concatenate([cs[1:], xbc], axis=0)
    xs = conv[:, :SSD_D_INNER]
    b_g = _pad_rows(jnp.concatenate(
        [conv[:, SSD_D_INNER + g * n:SSD_D_INNER + (g + 1) * n] for g in range(SSD_GROUPS)], axis=0))
    c_g = _pad_rows(jnp.concatenate(
        [conv[:, SSD_D_INNER + (SSD_GROUPS + g) * n:SSD_D_INNER + (SSD_GROUPS + g + 1) * n] for g in range(SSD_GROUPS)],
        axis=0))
    dt = jax.nn.softplus(dt_ref[0] + dtb_ref[...])
    decay = jnp.exp(dt * (-jnp.exp(alog_ref[...])))
    er = lax.broadcasted_iota(i32, (SSD_HEADS, SSD_D_INNER), 0)
    ec = lax.broadcasted_iota(i32, (SSD_HEADS, SSD_D_INNER), 1)
    per_head = _pad_rows(jnp.concatenate([dt, decay, dsk_ref[...]], axis=0))
    hx = jnp.dot(per_head, (ec // hd == er).astype(f32), precision=HIGHEST, preferred_element_type=f32)
    dtx, decx, dskx = hx[0:1], hx[1:2], hx[2:3]
    xdt = xs * dtx
    gr = lax.broadcasted_iota(i32, (8, SSD_D_INNER), 0)
    gc = lax.broadcasted_iota(i32, (8, SSD_D_INNER), 1)
    gmask = (gc // gw == gr).astype(f32)
    row0 = (gr == 0).astype(f32)
    st = st_ref[0].reshape(SSD_D_INNER, n)
    upd = _dot_tn(gmask * xdt, b_g, precision=HIGHEST)
    dec_full = _dot_tn(row0 * decx, jnp.ones((8, n), f32), precision=HIGHEST)
    ns_ref[0] = (st * dec_full + upd).reshape(SSD_HEADS, hd, n)
    cst = _dot_nt(c_g, st, precision=HIGHEST)
    y_state = jnp.sum(gmask * cst, axis=0, keepdims=True)
    cbx = jnp.sum(gmask * jnp.sum(c_g * b_g, axis=1, keepdims=True), axis=0, keepdims=True)
    y = cbx * xdt + decx * y_state + dskx * xs
    yz = y * jax.nn.silu(z_ref[0])
    parts = []
    for g in range(SSD_GROUPS):
        seg = yz[:, g * gw:(g + 1) * gw]
        parts.append(seg * lax.rsqrt(jnp.mean(seg * seg, axis=-1, keepdims=True) + 1e-5))
    y_ref[0] = jnp.concatenate(parts, axis=1) * ng_ref[...]


def _ssd_step(x2d, conv_state, ssm_state, sw, ln_g, ln_b):
    bsz = x2d.shape[0]
    (z,) = _mm(x2d, sw["w_z"], tm=bsz, tn=1024, out_dtypes=(f32,), name="ssd_in_z")
    (xbc,) = _mm(x2d, sw["w_xbc"], tm=bsz, tn=1024, out_dtypes=(f32,), name="ssd_in_xbc")
    (dt_raw,) = _mm(x2d, sw["w_dt"], tm=bsz, tn=SSD_HEADS, out_dtypes=(f32,), name="ssd_in_dt")
    rowspec = lambda width: pl.BlockSpec((1, 1, width), lambda b: (b, 0, 0))
    full = lambda r, c: pl.BlockSpec((r, c), lambda b: (0, 0))
    y, new_conv, new_state = pl.pallas_call(
        _ssd_step_body,
        name="ssd_step",
        grid=(bsz,),
        in_specs=[rowspec(SSD_D_INNER), rowspec(SSD_CONV_DIM), rowspec(SSD_HEADS),
                  pl.BlockSpec((1, SSD_CONV - 1, SSD_CONV_DIM), lambda b: (b, 0, 0)),
                  pl.BlockSpec((1, SSD_HEADS, SSD_HEADDIM, SSD_STATE), lambda b: (b, 0, 0, 0)),
                  full(SSD_CONV, SSD_CONV_DIM), full(1, SSD_CONV_DIM), full(1, SSD_HEADS), full(1, SSD_HEADS),
                  full(1, SSD_HEADS), full(1, SSD_D_INNER)],
        out_specs=[rowspec(SSD_D_INNER),
                   pl.BlockSpec((1, SSD_CONV - 1, SSD_CONV_DIM), lambda b: (b, 0, 0)),
                   pl.BlockSpec((1, SSD_HEADS, SSD_HEADDIM, SSD_STATE), lambda b: (b, 0, 0, 0))],
        out_shape=[jax.ShapeDtypeStruct((bsz, 1, SSD_D_INNER), f32),
                   jax.ShapeDtypeStruct((bsz, SSD_CONV - 1, SSD_CONV_DIM), f32),
                   jax.ShapeDtypeStruct((bsz, SSD_HEADS, SSD_HEADDIM, SSD_STATE), f32)],
        compiler_params=_cparams(("parallel",)),
    )(z.reshape(bsz, 1, -1), xbc.reshape(bsz, 1, -1), dt_raw.reshape(bsz, 1, -1), conv_state, ssm_state,
      sw["conv_w"], sw["conv_b"], sw["dt_bias"].reshape(1, -1), sw["a_log"].reshape(1, -1), sw["dsk"].reshape(1, -1),
      sw["norm_g"])
    h1 = _mm_ln(y.reshape(bsz, SSD_D_INNER), sw["w_out"], x2d, ln_g, ln_b, tm=bsz, name="ssd_out_ln")
    return h1, new_conv, new_state


def _slope(head):
    return 2.0 ** (-8.0 * (head + 1) / NSA_HEADS)


def _masked_softmax(s, mask):
    s = jnp.where(mask, s, NEG)
    m = jnp.max(s, axis=-1, keepdims=True)
    e = jnp.exp(s - m) * mask.astype(f32)
    return e / jnp.maximum(jnp.sum(e, axis=-1, keepdims=True), 1e-30)


def _tile_rows(x, n):
    return jnp.concatenate([x] * n, axis=0)


def _alibi_bias(ok, dist, kv_head):
    distf = dist.astype(f32)
    return jnp.concatenate(
        [jnp.where(ok, (-_slope(kv_head * NSA_REP + r)) * distf, NEG) for r in range(NSA_REP)], axis=0)


def _slope_col(kv_head, rows):
    return jnp.concatenate([jnp.full((rows, 1), _slope(kv_head * NSA_REP + r), f32) for r in range(NSA_REP)], axis=0)


def _topk_rank(imp):
    nb = imp.shape[1]
    lane = lax.broadcasted_iota(i32, imp.shape, 1)
    rank = jnp.zeros(imp.shape, i32)
    for j in range(nb):
        cj = imp[:, j:j + 1]
        rank = rank + ((cj > imp) | ((cj == imp) & (lane > j))).astype(i32)
    return rank


def _topk_mask_cols(it, n_sel):
    nb = it.shape[0]
    sub = lax.broadcasted_iota(i32, it.shape, 0)
    rank = jnp.zeros(it.shape, i32)
    for j in range(nb):
        rj = it[j:j + 1, :]
        rank = rank + ((rj > it) | ((rj == it) & (sub > j))).astype(i32)
    return (rank < n_sel).astype(f32)


def _pos_weights(pw_ref):
    w = jax.nn.softmax(pw_ref[...], axis=0)
    r = lax.broadcasted_iota(i32, (2 * NSA_KV, 2 * NSA_KVD), 0)
    c = lax.broadcasted_iota(i32, (2 * NSA_KV, 2 * NSA_KVD), 1)
    return jnp.dot(w, (c // NSA_DH == r).astype(f32), precision=HIGHEST, preferred_element_type=f32)


def _compress_body(x_ref, pw_ref, o_ref):
    wx = _pos_weights(pw_ref)
    x = x_ref[...]
    nb = x.shape[0] // NSA_BLOCK
    o_ref[0] = jnp.sum(x.reshape(nb, NSA_BLOCK, 2 * NSA_KVD) * wx[None], axis=1)


def _nsa_compress(kv4, pw, bsz, t):
    nb = t // NSA_BLOCK
    return pl.pallas_call(
        _compress_body,
        name="nsa_compress",
        grid=(bsz,),
        in_specs=[pl.BlockSpec((t, 2 * NSA_KVD), lambda b: (b, 0)),
                  pl.BlockSpec((NSA_BLOCK, 2 * NSA_KV), lambda b: (0, 0))],
        out_specs=pl.BlockSpec((1, nb, 2 * NSA_KVD), lambda b: (b, 0, 0)),
        out_shape=jax.ShapeDtypeStruct((bsz, nb, 2 * NSA_KVD), f32),
        compiler_params=_cparams(("parallel",)),
    )(kv4, pw)


NSA_TQ = 128
NSA_CK = 512


def _nsa_prompt_body(q_ref, g_ref, kcvc_ref, kvs_ref, kvw_ref, o_ref, *, t_len):
    i = pl.program_id(1)
    tq, ck, rep = NSA_TQ, NSA_CK, NSA_REP
    ck = min(ck, t_len)
    nb = t_len // NSA_BLOCK
    n_sel = min(NSA_TOP, nb)
    wlen = min(NSA_WINDOW + tq, t_len)
    t0 = i * tq
    t_col = t0 + lax.broadcasted_iota(i32, (tq, 1), 0)
    lane128 = lax.broadcasted_iota(i32, (1, LANES), 1)
    gates = g_ref[...]
    kcvc = kcvc_ref[0]
    t_row = t0 + lax.broadcasted_iota(i32, (1, tq), 1)
    blk = lax.broadcasted_iota(i32, (nb, 1), 0)
    dist_c = t_row - ((blk + 1) * NSA_BLOCK - 1)
    ok_c = dist_c >= 0
    ok_c4 = jnp.concatenate([ok_c] * rep, axis=1).astype(f32)
    cur = t_row // NSA_BLOCK
    n_chunks = (t0 + tq + ck - 1) // ck
    w_start = jnp.clip(t0 - NSA_WINDOW, 0, t_len - wlen)
    w_start = pl.multiple_of(w_start, tq)
    dist_w = t_col - (w_start + lax.broadcasted_iota(i32, (1, wlen), 1))
    mask_w = (dist_w >= 0) & (dist_w < NSA_WINDOW)

    for j in range(NSA_KV // 2):
        cols = slice(j * LANES, (j + 1) * LANES)
        vcols = slice(NSA_KVD + j * LANES, NSA_KVD + (j + 1) * LANES)
        q_tiles = [q_ref[:, (j * rep + r) * LANES:(j * rep + r + 1) * LANES] for r in range(rep)]
        q_rows = jnp.concatenate(q_tiles, axis=0)
        kc_b = kcvc[:, cols].astype(bf16)
        vc_b = kcvc[:, vcols].astype(bf16)
        out_tiles = [jnp.zeros((tq, LANES), f32) for _ in range(rep)]
        for h in range(2):
            k = 2 * j + h
            half = (lane128 // NSA_DH) == h
            q_pad = jnp.where(half, q_rows, jnp.zeros_like(q_rows))
            bias_c = jnp.concatenate([jnp.where(ok_c, (-_slope(k * rep + r)) * dist_c.astype(f32), NEG)
                                      for r in range(rep)], axis=1)
            s_c = _dot_nt(kc_b, q_pad) + bias_c
            e_c = jnp.exp(s_c - jnp.max(s_c, axis=0, keepdims=True)) * ok_c4
            p_c = e_c / jnp.maximum(jnp.sum(e_c, axis=0, keepdims=True), 1e-30)
            o_c = _dot_tn(p_c.astype(bf16), vc_b)
            imp = sum(p_c[:, r * tq:(r + 1) * tq] for r in range(rep))
            imp = jnp.where((blk == cur) | (blk == 0), 1e4, imp)
            imp = jnp.where(blk > cur, -1.0, imp)
            sel_b = _topk_mask_cols(imp, n_sel).T.astype(bf16)

            def sel_step(c, carry, q_pad=q_pad, k=k, sel_b=sel_b, cols=cols, vcols=vcols):
                m, l, acc = carry
                k0 = pl.multiple_of(c * ck, ck)
                ks = kvs_ref[pl.ds(k0, ck), cols]
                vs = kvs_ref[pl.ds(k0, ck), vcols]
                kpos = k0 + lax.broadcasted_iota(i32, (1, ck), 1)
                brow = lax.broadcasted_iota(i32, (nb, ck), 0)
                expand = ((k0 + lax.broadcasted_iota(i32, (nb, ck), 1)) // NSA_BLOCK == brow).astype(bf16)
                selx = _dot(sel_b, expand)
                dist = t_col - kpos
                ok = (selx > 0.5) & (dist >= 0)
                s = _dot_nt(q_pad, ks) + _alibi_bias(ok, dist, k)
                m_new = jnp.maximum(m, jnp.max(s, axis=-1, keepdims=True))
                a = jnp.exp(m - m_new)
                p = jnp.exp(s - m_new)
                l = a * l + jnp.sum(p, axis=-1, keepdims=True)
                acc = a * acc + _dot(p.astype(bf16), vs)
                return m_new, l, acc

            init = (jnp.full((rep * tq, 1), NEG, f32), jnp.zeros((rep * tq, 1), f32), jnp.zeros((rep * tq, LANES), f32))
            _, l_s, acc_s = lax.fori_loop(0, n_chunks, sel_step, init)
            o_s = acc_s / jnp.maximum(l_s, 1e-30)
            kw = kvw_ref[pl.ds(w_start, wlen), cols]
            vw = kvw_ref[pl.ds(w_start, wlen), vcols]
            s_w = _dot_nt(q_pad, kw) + _alibi_bias(mask_w, dist_w, k)
            e_w = jnp.exp(s_w - jnp.max(s_w, axis=-1, keepdims=True))
            o_w = _dot(e_w.astype(bf16), vw) / jnp.sum(e_w, axis=-1, keepdims=True)
            for r in range(rep):
                gc = (k * rep + r) * 3
                rows = slice(r * tq, (r + 1) * tq)
                o_r = gates[:, gc:gc + 1] * o_c[rows] + gates[:, gc + 1:gc + 2] * o_s[rows] + gates[:, gc + 2:gc + 3] * o_w[rows]
                out_tiles[r] = jnp.where(half, o_r, out_tiles[r])
        for r in range(rep):
            o_ref[:, (j * rep + r) * LANES:(j * rep + r + 1) * LANES] = out_tiles[r].astype(o_ref.dtype)


def _nsa_prompt_attn(q, gates, kcvc, kv4_b, kvw_b, bsz, t):
    nq = t // NSA_TQ
    nb = t // NSA_BLOCK
    return pl.pallas_call(
        functools.partial(_nsa_prompt_body, t_len=t),
        name="nsa_attn",
        grid=(bsz, nq),
        in_specs=[pl.BlockSpec((NSA_TQ, NSA_QD), lambda b, i: (b * nq + i, 0)),
                  pl.BlockSpec((NSA_TQ, LANES), lambda b, i: (b * nq + i, 0)),
                  pl.BlockSpec((1, nb, 2 * NSA_KVD), lambda b, i: (b, 0, 0)),
                  pl.BlockSpec((t, 2 * NSA_KVD), lambda b, i: (b, 1)),
                  pl.BlockSpec((t, 2 * NSA_KVD), lambda b, i: (b, 0))],
        out_specs=pl.BlockSpec((NSA_TQ, NSA_QD), lambda b, i: (b * nq + i, 0)),
        out_shape=jax.ShapeDtypeStruct((bsz * t, NSA_QD), bf16),
        compiler_params=_cparams(("parallel", "arbitrary")),
    )(q, gates, kcvc, kv4_b, kvw_b)


def _pair_layout_cols(w):
    lead = w.shape[:-1]
    w = w.reshape(*lead, NSA_KV // 2, 2, NSA_REP, NSA_DH)
    return jnp.swapaxes(w, -3, -2).reshape(*lead, NSA_QD)


def _nsa_weights(w_in, pos_w, w_out, wdt, pair):
    lay = _pair_layout_cols if pair else (lambda w: w)
    kv0 = NSA_QD
    g0 = NSA_QD + 6 * NSA_KVD
    wg = jnp.zeros((D_MODEL, LANES), f32).at[:, :3 * NSA_HEADS].set(w_in[:, g0:])
    return dict(
        w_q=lay(w_in[:, :NSA_QD]).astype(wdt),
        w_kv4=w_in[:, kv0:kv0 + 4 * NSA_KVD].astype(wdt),
        w_kvw=w_in[:, kv0 + 4 * NSA_KVD:g0].astype(wdt),
        w_g=wg.astype(wdt),
        pw=pos_w.transpose(1, 0, 2).reshape(NSA_BLOCK, 2 * NSA_KV),
        pwt=pos_w.transpose(0, 2, 1).reshape(2 * NSA_KV, NSA_BLOCK),
        w_kvt=w_in[:, kv0:g0].T.astype(wdt),
        w_out=lay(w_out.T).T.astype(wdt))


def _nsa_prompt(x2d, nw, bsz, t, ln_g, ln_b):
    (q,) = _mm(x2d, nw["w_q"], tm=PROJ_TM, tn=1024, out_dtypes=(bf16,), scale=NSA_DH ** -0.5, name="nsa_in_q")
    kv4, kv4_b = _mm(x2d, nw["w_kv4"], tm=PROJ_TM, tn=1024, out_dtypes=(f32, bf16), name="nsa_in_kv4")
    kvw, kvw_b = _mm(x2d, nw["w_kvw"], tm=PROJ_TM, tn=512, out_dtypes=(f32, bf16), name="nsa_in_kvw")
    (gates,) = _mm(x2d, nw["w_g"], tm=PROJ_TM, tn=LANES, out_dtypes=(f32,), act="sigmoid", name="nsa_in_gates")
    kcvc = _nsa_compress(kv4, nw["pw"], bsz, t)
    o = _nsa_prompt_attn(q, gates, kcvc, kv4_b, kvw_b, bsz, t)
    h1 = _mm_ln(o, nw["w_out"], x2d, ln_g, ln_b, tm=PROJ_TM, name="nsa_out_ln")
    w_keep = min(NSA_WINDOW, t)
    new_kv = kv4.reshape(bsz, t, 4, NSA_KV, NSA_DH)
    new_win = kvw.reshape(bsz, t, 2, NSA_KV, NSA_DH)[:, t - w_keep:]
    return h1, new_kv, new_win


PAGES_PER_STEP = 16


def _lane_weights(pwt_ref):
    w = jax.nn.softmax(pwt_ref[...], axis=1)
    return jnp.concatenate([w] * (PAGE_SIZE // NSA_BLOCK), axis=1)


def _page_compress_body(pt_ref, *refs):
    del pt_ref
    page_refs, pwt_ref, o_ref = refs[:PAGES_PER_STEP], refs[PAGES_PER_STEP], refs[PAGES_PER_STEP + 1]
    s = pl.program_id(1)
    bpp = PAGE_SIZE // NSA_BLOCK
    nbp = o_ref.shape[-1]

    @pl.when(s == 0)
    def _():
        o_ref[...] = jnp.zeros_like(o_ref)

    w = _lane_weights(pwt_ref)
    tok_blk = lax.broadcasted_iota(i32, (1, PAGE_SIZE), 1) // NSA_BLOCK
    out_lane = lax.broadcasted_iota(i32, (1, nbp), 1)
    for c in range(2):
        for k in range(NSA_KV):
            wk = w[c * NSA_KV + k:c * NSA_KV + k + 1]
            upd = jnp.zeros((NSA_DH, nbp), f32)
            for i, p_ref in enumerate(page_refs):
                xw = p_ref[0, c, k] * wk
                first = (s * PAGES_PER_STEP + i) * bpp
                for h in range(bpp):
                    r = jnp.sum(jnp.where(tok_blk == h, xw, 0.0), axis=1, keepdims=True)
                    upd = upd + jnp.where(out_lane == first + h, r, 0.0)
            o_ref[0, c, k] += upd


def _page_compress(pool_t, page_table, pwt):
    bsz, n_pages = page_table.shape
    nbp = n_pages * (PAGE_SIZE // NSA_BLOCK)
    steps = n_pages // PAGES_PER_STEP
    assert n_pages % PAGES_PER_STEP == 0
    page_specs = [pl.BlockSpec((1, 2, NSA_KV, NSA_DH, PAGE_SIZE),
                               lambda b, s, pt, i=i: (pt[b, s * PAGES_PER_STEP + i], 0, 0, 0, 0))
                  for i in range(PAGES_PER_STEP)]
    return pl.pallas_call(
        _page_compress_body,
        name="nsa_page_compress",
        grid_spec=pltpu.PrefetchScalarGridSpec(
            num_scalar_prefetch=1, grid=(bsz, steps),
            in_specs=page_specs + [pl.BlockSpec((2 * NSA_KV, NSA_BLOCK), lambda b, s, pt: (0, 0))],
            out_specs=pl.BlockSpec((1, 2, NSA_KV, NSA_DH, nbp), lambda b, s, pt: (b, 0, 0, 0, 0))),
        out_shape=jax.ShapeDtypeStruct((bsz, 2, NSA_KV, NSA_DH, nbp), f32),
        compiler_params=_cparams(("parallel", "arbitrary")),
    )(page_table, *([pool_t] * PAGES_PER_STEP), pwt)


def _mm_nt_body(wt_ref, x_ref, o_ref):
    o_ref[...] = _mxu_nt(wt_ref[...], x_ref[...], _is_f32(wt_ref))


def _mm_nt(wt, x, *, tn, name):
    n, k = wt.shape
    m = x.shape[0]
    return pl.pallas_call(
        _mm_nt_body, name=f"{name}_r{m}", grid=(n // tn,),
        in_specs=[pl.BlockSpec((tn, k), lambda j: (j, 0)), pl.BlockSpec((m, k), lambda j: (0, 0))],
        out_specs=pl.BlockSpec((tn, m), lambda j: (j, 0)),
        out_shape=jax.ShapeDtypeStruct((n, m), f32),
        compiler_params=_cparams(("parallel",)),
    )(wt, x)


def _step_slopes(kv_head):
    r = lax.broadcasted_iota(i32, (8, 1), 0)
    return jnp.exp2(-0.5 * (kv_head * NSA_REP + r + 1).astype(f32))


def _gate_col(gates, kv_head, branch):
    cols = [(kv_head * NSA_REP + r) * 3 + branch for r in range(NSA_REP)]
    return jnp.concatenate([gates[:, c:c + 1] for c in cols], axis=0)


def _own_col(x, b):
    lane = lax.broadcasted_iota(i32, x.shape, x.ndim - 1)
    return jnp.sum(jnp.where(lane == b, x, 0.0), axis=-1, keepdims=True)


def _new_key_tile(col):
    lane = lax.broadcasted_iota(i32, (col.shape[0], LANES), 1)
    return jnp.where(lane == 0, col, 0.0)


def _nsa_step_cw_body(q_ref, g_ref, cmp_ref, kvt_ref, pwt_ref, win_ref, o_ref, idx_ref, *, past_len):
    b = pl.program_id(0)
    rep, kv = NSA_REP, NSA_KV
    nbp = past_len // NSA_BLOCK
    nbt = nbp + 8
    w_buf = win_ref.shape[-1]
    w0 = jax.nn.softmax(pwt_ref[...], axis=1)[:, 0:1]
    lane_c = lax.broadcasted_iota(i32, (1, nbp + LANES), 1)
    dist_c = past_len - ((lane_c + 1) * NSA_BLOCK - 1)
    lane_w = lax.broadcasted_iota(i32, (1, w_buf + LANES), 1)
    dist_w = jnp.where(lane_w <= w_buf, w_buf - lane_w, -1)
    mask_w = (dist_w >= 0) & (dist_w < NSA_WINDOW)
    blk = lax.broadcasted_iota(i32, (1, nbt), 1)
    cur = past_len // NSA_BLOCK
    gates = g_ref[0]
    for k in range(kv):
        q8 = _pad_rows(q_ref[0, k])
        slope = _step_slopes(k)
        new = [_own_col(kvt_ref[s, k], b) for s in (0, 1, 4, 5)]
        k_c = jnp.concatenate([cmp_ref[0, 0, k], _new_key_tile(new[0] * w0[k:k + 1])], axis=1)
        v_c = jnp.concatenate([cmp_ref[0, 1, k], _new_key_tile(new[1] * w0[kv + k:kv + k + 1])], axis=1)
        s_c = _mxu(q8, k_c, True) - slope * dist_c.astype(f32)
        p_c = _masked_softmax(s_c, jnp.broadcast_to(dist_c >= 0, s_c.shape))
        o_c = _mxu_nt(p_c, v_c, True)
        imp = jnp.sum(p_c[0:rep], axis=0, keepdims=True)[:, :nbt]
        imp = jnp.where((blk == cur) | (blk == 0), 1e4, imp)
        imp = jnp.where(blk > cur, -2.0, imp)
        rank = _topk_rank(imp)
        pick = lax.broadcasted_iota(i32, (NSA_TOP, nbt), 0) == rank
        idx_ref[0, k] = jnp.sum(jnp.where(pick, blk.astype(f32), 0.0), axis=1, keepdims=True).astype(i32)
        k_w = jnp.concatenate([win_ref[0, 0, k], _new_key_tile(new[2])], axis=1)
        v_w = jnp.concatenate([win_ref[0, 1, k], _new_key_tile(new[3])], axis=1)
        s_w = _mxu(q8, k_w, True) - slope * dist_w.astype(f32)
        p_w = _masked_softmax(s_w, jnp.broadcast_to(mask_w, s_w.shape))
        o_w = _mxu_nt(p_w, v_w, True)
        o_ref[0, k] = _gate_col(gates, k, 0) * o_c[0:rep] + _gate_col(gates, k, 2) * o_w[0:rep]


def _nsa_step_sel_body(idx_ref, pt_ref, *refs, past_len):
    del pt_ref
    n_in = 2 * NSA_TOP
    tile_refs = refs[:n_in]
    q_ref, g_ref, kvt_ref, part_ref, o_ref = refs[n_in:]
    b, k = pl.program_id(0), pl.program_id(1)
    nbp = past_len // NSA_BLOCK
    rep, lb, kv = NSA_REP, NSA_BLOCK, NSA_KV
    bpp = PAGE_SIZE // lb
    new_k = _new_key_tile(_own_col(kvt_ref[0, 0], b))
    new_v = _new_key_tile(_own_col(kvt_ref[1, 0], b))
    lane = lax.broadcasted_iota(i32, (1, NSA_TOP * PAGE_SIZE), 1)
    tok = lane % PAGE_SIZE
    spos = tok % lb
    valid = lane < 0
    ks, vs = [], []
    for i in range(NSA_TOP):
        n = idx_ref[(b * kv + k) * NSA_TOP + i]
        is_new = n >= nbp
        ks.append(jnp.where(is_new, new_k, tile_refs[2 * i][0, 0, 0]))
        vs.append(jnp.where(is_new, new_v, tile_refs[2 * i + 1][0, 0, 0]))
        mine = lane // PAGE_SIZE == i
        spos = spos + jnp.where(mine, n * lb, 0)
        half = jnp.where(is_new, 0, n % bpp)
        last = jnp.where(is_new, 0, lb - 1)
        valid = valid | (mine & (tok // lb == half) & (tok % lb <= last))
    k_all = jnp.concatenate(ks, axis=1)
    v_all = jnp.concatenate(vs, axis=1)
    dist = past_len - spos
    q8 = _pad_rows(q_ref[0, 0])
    s = _mxu(q8, k_all, True) - _step_slopes(k) * dist.astype(f32)
    p = _masked_softmax(s, jnp.broadcast_to(valid & (dist >= 0), s.shape))
    o_s = _mxu_nt(p, v_all, True)
    g = g_ref[0, 0]
    g_sel = jnp.concatenate([g[:, 3 * r + 1:3 * r + 2] for r in range(rep)], axis=0)
    o_ref[0, 0] = part_ref[0, 0] + g_sel * o_s[0:rep]


def _nsa_step(x2d, pool, page_table, win_cache, nw, ln_g, ln_b):
    bsz = x2d.shape[0]
    n_pages = page_table.shape[1]
    past_len = n_pages * PAGE_SIZE
    w_buf = win_cache.shape[1]
    rep, kv, dh = NSA_REP, NSA_KV, NSA_DH
    bpp = PAGE_SIZE // NSA_BLOCK
    nbp = past_len // NSA_BLOCK
    (q,) = _mm(x2d, nw["w_q"], tm=bsz, tn=1024, out_dtypes=(f32,), scale=NSA_DH ** -0.5, name="nsa_in_q")
    (kv4,) = _mm(x2d, nw["w_kv4"], tm=bsz, tn=1024, out_dtypes=(f32,), name="nsa_in_kv4")
    (kvw,) = _mm(x2d, nw["w_kvw"], tm=bsz, tn=512, out_dtypes=(f32,), name="nsa_in_kvw")
    (gates,) = _mm(x2d, nw["w_g"], tm=bsz, tn=LANES, out_dtypes=(f32,), act="sigmoid", name="nsa_in_gates")
    kvt = _mm_nt(nw["w_kvt"], x2d, tn=512, name="nsa_in_kvt").reshape(6, kv, dh, bsz)
    pool_t = jnp.transpose(pool, (0, 2, 3, 4, 1))
    win_t = jnp.transpose(win_cache, (0, 2, 3, 4, 1))
    cmp = _page_compress(pool_t, page_table, nw["pwt"])
    q4 = q.reshape(bsz, kv, rep, dh)
    slab = pl.BlockSpec((1, kv, rep, dh), lambda b: (b, 0, 0, 0))
    whole = lambda shape: pl.BlockSpec(shape, lambda b: (0,) * len(shape))
    part, idx = pl.pallas_call(
        functools.partial(_nsa_step_cw_body, past_len=past_len),
        name="nsa_step_cw",
        grid=(bsz,),
        in_specs=[slab, pl.BlockSpec((1, 1, LANES), lambda b: (b, 0, 0)),
                  pl.BlockSpec((1, 2, kv, dh, nbp), lambda b: (b, 0, 0, 0, 0)), whole((6, kv, dh, bsz)),
                  whole((2 * kv, NSA_BLOCK)), pl.BlockSpec((1, 2, kv, dh, w_buf), lambda b: (b, 0, 0, 0, 0))],
        out_specs=[slab, pl.BlockSpec((1, kv, NSA_TOP, 1), lambda b: (b, 0, 0, 0))],
        out_shape=[jax.ShapeDtypeStruct((bsz, kv, rep, dh), f32),
                   jax.ShapeDtypeStruct((bsz, kv, NSA_TOP, 1), i32)],
        compiler_params=_cparams(("parallel",)),
    )(q4, gates.reshape(bsz, 1, -1), cmp, kvt, nw["pwt"], win_t)

    def tile_spec(i, slot):
        def imap(b, k, idx_r, pt_r):
            n = jnp.minimum(idx_r[(b * kv + k) * NSA_TOP + i], nbp - 1)
            return (pt_r[b * n_pages + n // bpp], slot, k, 0, 0)
        return pl.BlockSpec((1, 1, 1, dh, PAGE_SIZE), imap)

    tile_specs = []
    for i in range(NSA_TOP):
        tile_specs += [tile_spec(i, 2), tile_spec(i, 3)]
    head = pl.BlockSpec((1, 1, rep, dh), lambda b, k, *_: (b, k, 0, 0))
    o = pl.pallas_call(
        functools.partial(_nsa_step_sel_body, past_len=past_len),
        name="nsa_step_sel",
        grid_spec=pltpu.PrefetchScalarGridSpec(
            num_scalar_prefetch=2, grid=(bsz, kv),
            in_specs=tile_specs + [
                head, pl.BlockSpec((1, 1, 1, 3 * rep), lambda b, k, *_: (b, k, 0, 0)),
                pl.BlockSpec((2, 1, dh, bsz), lambda b, k, *_: (1, k, 0, 0)), head],
            out_specs=head),
        out_shape=jax.ShapeDtypeStruct((bsz, kv, rep, dh), f32),
        compiler_params=_cparams(("arbitrary", "arbitrary")),
    )(idx.reshape(-1), page_table.reshape(-1), *([pool_t] * (2 * NSA_TOP)),
      q4, gates[:, :3 * NSA_HEADS].reshape(bsz, kv, 1, 3 * rep), kvt, part)
    h1 = _mm_ln(o.reshape(bsz, NSA_QD), nw["w_out"], x2d, ln_g, ln_b, tm=bsz, name="nsa_out_ln")
    new_kv = kv4.reshape(bsz, 1, 4, kv, dh)
    new_win = jnp.concatenate([win_cache[:, 1:], kvw.reshape(bsz, 1, 2, kv, dh)], axis=1)
    return h1, new_kv, new_win


def kernel(x_prompt, x_sample, state_ssm, state_conv, cache_kv, cache_win, page_table, p_prompt, p_sample,
           ssd_w_in, ssd_conv_w, ssd_conv_b, ssd_dt_bias, ssd_a_log, ssd_d, ssd_norm_g, ssd_w_out,
           nsa_w_in, nsa_pos_w, nsa_w_out, ln1_g, ln1_b, ln2_g, ln2_b, router_w, router_bias,
           moe_w_gate, moe_w_up, moe_w_down, ple_proj, ple_gate):
    bp, t, d = x_prompt.shape
    bs = x_sample.shape[0]
    xp = x_prompt.reshape(bp * t, d)
    xs = x_sample.reshape(bs, d)
    rb = router_bias.reshape(N_EXPERTS, 1)
    ssm_p, conv_p, kv_p, win_p, ssm_s, conv_s, kv_s, win_s = [], [], [], [], [], [], [], []
    for i in range(DEPTH):
        j = i // 2
        if i % 2 == 0:
            ssd = (ssd_w_in[j], ssd_conv_w[j], ssd_conv_b[j], ssd_dt_bias[j], ssd_a_log[j], ssd_d[j], ssd_norm_g[j],
                   ssd_w_out[j])
            h1p, c_new, s_new = _ssd_prompt(xp, _ssd_weights(*ssd, bf16), bp, t, ln1_g[i], ln1_b[i])
            conv_p.append(c_new)
            ssm_p.append(s_new)
            h1s, c_new, s_new = _ssd_step(xs, state_conv[j], state_ssm[j], _ssd_weights(*ssd, f32), ln1_g[i], ln1_b[i])
            conv_s.append(c_new)
            ssm_s.append(s_new)
        else:
            nsa = (nsa_w_in[j], nsa_pos_w[j], nsa_w_out[j])
            h1p, r_new, w_new = _nsa_prompt(xp, _nsa_weights(*nsa, bf16, True), bp, t, ln1_g[i], ln1_b[i])
            kv_p.append(r_new)
            win_p.append(w_new)
            h1s, r_new, w_new = _nsa_step(xs, cache_kv[j], page_table, cache_win[j], _nsa_weights(*nsa, f32, False),
                                          ln1_g[i], ln1_b[i])
            kv_s.append(r_new)
            win_s.append(w_new)

        def tail_weights(wdt, i=i):
            return dict(rwt=router_w.T.astype(wdt), rb=rb, layer=i, wg=moe_w_gate.astype(wdt), wu=moe_w_up.astype(wdt),
                        wd=moe_w_down.astype(wdt), ln2_g=ln2_g[i], ln2_b=ln2_b[i],
                        ple_gate=ple_gate[i].astype(wdt), ple_proj=ple_proj[i].astype(wdt))

        xp = _layer_tail(h1p, p_prompt[i].reshape(bp * t, PLE_DIM), tail_weights(bf16), tm=PROJ_TM)
        xs = _layer_tail(h1s, p_sample[i].reshape(bs, PLE_DIM), tail_weights(f32), tm=bs)
    return (xp.reshape(bp, t, d), xs.reshape(bs, 1, d), jnp.stack(ssm_p), jnp.stack(conv_p), jnp.stack(kv_p),
            jnp.stack(win_p), jnp.stack(ssm_s), jnp.stack(conv_s), jnp.stack(kv_s), jnp.stack(win_s))
```

```python
import functools

import jax
import jax.numpy as jnp
from jax import lax
from jax.experimental import pallas as pl
from jax.experimental.pallas import tpu as pltpu

f32, bf16, i32 = jnp.float32, jnp.bfloat16, jnp.int32
HIGHEST = lax.Precision.HIGHEST

D_MODEL = 1024
DEPTH = 2
PLE_DIM = 256
SSD_D_INNER = 2048
SSD_HEADDIM = 64
SSD_HEADS = 32
SSD_GROUPS = 4
SSD_HPG = 8
SSD_STATE = 128
SSD_CONV = 4
SSD_CHUNK = 128
SSD_GW = SSD_HPG * SSD_HEADDIM
SSD_CONV_DIM = SSD_D_INNER + 2 * SSD_GROUPS * SSD_STATE
NSA_HEADS = 16
NSA_KV = 4
NSA_REP = 4
NSA_DH = 64
NSA_BLOCK = 64
NSA_TOP = 16
NSA_WINDOW = 512
NSA_QD = NSA_HEADS * NSA_DH
NSA_KVD = NSA_KV * NSA_DH
PAGE_SIZE = 128
N_EXPERTS = 16
N_EGROUPS = 4
EPG = 4
D_FF = 512
ALPHA = (2.0 * DEPTH) ** 0.25
LN_EPS = 1e-5
NEG = -1e30
LANES = 128
VMEM_LIMIT = 56 * 1024 * 1024
PROJ_TM = 1024


def _cparams(sem):
    return pltpu.CompilerParams(dimension_semantics=sem, vmem_limit_bytes=VMEM_LIMIT)


def _ln(v, g, b):
    mu = jnp.mean(v, axis=-1, keepdims=True)
    d = v - mu
    var = jnp.mean(d * d, axis=-1, keepdims=True)
    return d * lax.rsqrt(var + LN_EPS) * g + b


def _dot(a, b):
    return jnp.dot(a, b, preferred_element_type=f32)


def _mxu(a, b, precise):
    if precise:
        return jnp.dot(a.astype(f32), b.astype(f32), precision=HIGHEST, preferred_element_type=f32)
    return jnp.dot(a.astype(bf16), b.astype(bf16), preferred_element_type=f32)


def _mxu_nt(a, b, precise):
    if precise:
        return _dot_nt(a.astype(f32), b.astype(f32), precision=HIGHEST)
    return _dot_nt(a.astype(bf16), b.astype(bf16))


def _dot_nt(a, b, precision=None):
    return lax.dot_general(a, b, (((1,), (1,)), ((), ())), precision=precision, preferred_element_type=f32)


def _dot_tn(a, b, precision=None):
    return lax.dot_general(a, b, (((0,), (0,)), ((), ())), precision=precision, preferred_element_type=f32)


def _is_f32(ref):
    return ref.dtype == jnp.float32


def _mm_body(x_ref, w_ref, *o_refs, act, scale):
    acc = _mxu(x_ref[...], w_ref[...], _is_f32(w_ref))
    if scale != 1.0:
        acc = acc * scale
    if act == "sigmoid":
        acc = jax.nn.sigmoid(acc)
    for o_ref in o_refs:
        o_ref[...] = acc.astype(o_ref.dtype)


def _mm(x, w, *, tm, tn, out_dtypes, name, act=None, scale=1.0):
    m, k = x.shape
    n = w.shape[1]
    tm, tn = min(tm, m), min(tn, n)
    assert m % tm == 0 and n % tn == 0
    outs = pl.pallas_call(
        functools.partial(_mm_body, act=act, scale=scale),
        name=f"{name}_r{m}",
        grid=(m // tm, n // tn),
        in_specs=[pl.BlockSpec((tm, k), lambda i, j: (i, 0)), pl.BlockSpec((k, tn), lambda i, j: (0, j))],
        out_specs=[pl.BlockSpec((tm, tn), lambda i, j: (i, j)) for _ in out_dtypes],
        out_shape=[jax.ShapeDtypeStruct((m, n), dt) for dt in out_dtypes],
        compiler_params=_cparams(("parallel", "arbitrary")),
    )(x, w)
    return outs


def _mm_ln_body(x_ref, w_ref, res_ref, g_ref, b_ref, o_ref):
    acc = _mxu(x_ref[...], w_ref[...], _is_f32(w_ref))
    o_ref[...] = _ln(ALPHA * res_ref[...] + acc, g_ref[...], b_ref[...])


def _mm_ln(x, w, res, g, b, *, tm, name):
    m, k = x.shape
    n = w.shape[1]
    tm = min(tm, m)
    assert m % tm == 0
    return pl.pallas_call(
        _mm_ln_body,
        name=f"{name}_r{m}",
        grid=(m // tm,),
        in_specs=[pl.BlockSpec((tm, k), lambda i: (i, 0)), pl.BlockSpec((k, n), lambda i: (0, 0)),
                  pl.BlockSpec((tm, n), lambda i: (i, 0)), pl.BlockSpec((1, n), lambda i: (0, 0)),
                  pl.BlockSpec((1, n), lambda i: (0, 0))],
        out_specs=pl.BlockSpec((tm, n), lambda i: (i, 0)),
        out_shape=jax.ShapeDtypeStruct((m, n), f32),
        compiler_params=_cparams(("parallel",)),
    )(x, w, res, g.reshape(1, n), b.reshape(1, n))


def _router_body(h_ref, rwt_ref, rb_ref, gate_ref, best_ref):
    logits = _mxu_nt(rwt_ref[...], h_ref[...], _is_f32(rwt_ref))
    score = jax.nn.sigmoid(logits)
    sel = score + rb_ref[...]
    gsum = []
    for g in range(N_EGROUPS):
        a, b, c, d = (sel[EPG * g + i:EPG * g + i + 1] for i in range(EPG))
        hi1, lo1, hi2, lo2 = jnp.maximum(a, b), jnp.minimum(a, b), jnp.maximum(c, d), jnp.minimum(c, d)
        gsum.append(jnp.maximum(hi1, hi2) + jnp.maximum(jnp.minimum(hi1, hi2), jnp.maximum(lo1, lo2)))
    best = jnp.zeros_like(gsum[0], dtype=i32)
    top = gsum[0]
    for g in range(1, N_EGROUPS):
        upd = gsum[g] > top
        best = jnp.where(upd, g, best)
        top = jnp.where(upd, gsum[g], top)
    selg = sel[0:EPG]
    scg = score[0:EPG]
    for g in range(1, N_EGROUPS):
        selg = jnp.where(best == g, sel[EPG * g:EPG * (g + 1)], selg)
        scg = jnp.where(best == g, score[EPG * g:EPG * (g + 1)], scg)
    rows = [selg[i:i + 1] for i in range(EPG)]
    chosen = []
    for i in range(EPG):
        rank = jnp.zeros_like(best)
        for j in range(EPG):
            if j == i:
                continue
            ahead = (rows[j] > rows[i]) | ((rows[j] == rows[i]) if j < i else False)
            rank = rank + ahead.astype(i32)
        chosen.append(rank < 2)
    wsum = sum(jnp.where(chosen[i], scg[i:i + 1], 0.0) for i in range(EPG))
    gates = [jnp.where(chosen[i], scg[i:i + 1] / wsum, 0.0) for i in range(EPG)]
    out_rows = []
    for g in range(N_EGROUPS):
        for i in range(EPG):
            out_rows.append(jnp.where(best == g, gates[i], 0.0))
    gate_ref[...] = jnp.concatenate(out_rows, axis=0).T
    best_ref[...] = best


def _router(h, rwt, rb, *, tm):
    m, d = h.shape
    tm = min(tm, m)
    return pl.pallas_call(
        _router_body,
        name=f"router_r{m}",
        grid=(m // tm,),
        in_specs=[pl.BlockSpec((tm, d), lambda i: (i, 0)), pl.BlockSpec((N_EXPERTS, d), lambda i: (0, 0)),
                  pl.BlockSpec((N_EXPERTS, 1), lambda i: (0, 0))],
        out_specs=[pl.BlockSpec((tm, N_EXPERTS), lambda i: (i, 0)), pl.BlockSpec((1, tm), lambda i: (0, i))],
        out_shape=[jax.ShapeDtypeStruct((m, N_EXPERTS), f32), jax.ShapeDtypeStruct((1, m), i32)],
        compiler_params=_cparams(("parallel",)),
    )(h, rwt, rb)


def _moe_body(h_ref, gate_ref, wg_ref, wu_ref, wd_ref, g2_ref, b2_ref, o_ref, acc_ref):
    e = pl.program_id(1)

    @pl.when(e == 0)
    def _():
        acc_ref[...] = jnp.zeros_like(acc_ref)

    precise = _is_f32(wg_ref)
    h = h_ref[...]
    hid = jax.nn.silu(_mxu(h, wg_ref[0, 0], precise)) * _mxu(h, wu_ref[0, 0], precise)
    out = _mxu(hid, wd_ref[0, 0], precise)
    gate = gate_ref[...]
    lane = lax.broadcasted_iota(i32, gate.shape, 1)
    gcol = jnp.sum(jnp.where(lane == e, gate, 0.0), axis=1, keepdims=True)
    acc_ref[...] += gcol * out

    @pl.when(e == N_EXPERTS - 1)
    def _():
        o_ref[...] = _ln(ALPHA * h_ref[...] + acc_ref[...], g2_ref[...], b2_ref[...])


def _moe_ln(h, gate, wg, wu, wd, g2, b2, *, tm, layer):
    m, d = h.shape
    tm = min(tm, m)
    return pl.pallas_call(
        _moe_body,
        name=f"moe_ln_r{m}",
        grid=(m // tm, N_EXPERTS),
        in_specs=[pl.BlockSpec((tm, d), lambda i, e: (i, 0)), pl.BlockSpec((tm, N_EXPERTS), lambda i, e: (i, 0)),
                  pl.BlockSpec((1, 1, d, D_FF), lambda i, e: (layer, e, 0, 0)),
                  pl.BlockSpec((1, 1, d, D_FF), lambda i, e: (layer, e, 0, 0)),
                  pl.BlockSpec((1, 1, D_FF, d), lambda i, e: (layer, e, 0, 0)),
                  pl.BlockSpec((1, d), lambda i, e: (0, 0)), pl.BlockSpec((1, d), lambda i, e: (0, 0))],
        out_specs=pl.BlockSpec((tm, d), lambda i, e: (i, 0)),
        out_shape=jax.ShapeDtypeStruct((m, d), f32),
        scratch_shapes=[pltpu.VMEM((tm, d), f32)],
        compiler_params=_cparams(("parallel", "arbitrary")),
    )(h, gate, wg, wu, wd, g2.reshape(1, d), b2.reshape(1, d))


def _ple_body(h_ref, p_ref, wg_ref, wp_ref, o_ref):
    h = h_ref[...]
    precise = _is_f32(wg_ref)
    gate = jax.nn.sigmoid(_mxu(h, wg_ref[...], precise))
    o_ref[...] = h + gate * _mxu(p_ref[...], wp_ref[...], precise)


def _ple(h, p, wg, wp, *, tm):
    m, d = h.shape
    tm = min(tm, m)
    return pl.pallas_call(
        _ple_body,
        name=f"ple_r{m}",
        grid=(m // tm,),
        in_specs=[pl.BlockSpec((tm, d), lambda i: (i, 0)), pl.BlockSpec((tm, PLE_DIM), lambda i: (i, 0)),
                  pl.BlockSpec((d, d), lambda i: (0, 0)), pl.BlockSpec((PLE_DIM, d), lambda i: (0, 0))],
        out_specs=pl.BlockSpec((tm, d), lambda i: (i, 0)),
        out_shape=jax.ShapeDtypeStruct((m, d), f32),
        compiler_params=_cparams(("parallel",)),
    )(h, p, wg, wp)


MOE_TM = 1024
MOE_CH = 288
MOE_ALIGN = 16
MOE_ROWS = MOE_TM + N_EGROUPS * MOE_ALIGN + MOE_CH


def _moe_grouped_body(h_ref, gate_ref, best_ref, wg_ref, wu_ref, wd_ref, g2_ref, b2_ref, o_ref,
                      p_ref, xs_ref, ys_ref, gs_ref, meta_ref):
    e = pl.program_id(1)
    tm, rows, ng = MOE_TM, MOE_ROWS, N_EGROUPS

    @pl.when(e == 0)
    def _():
        best = best_ref[...]
        ind = (best == lax.broadcasted_iota(i32, (8, tm), 0)).astype(f32)
        prefix = ind
        lane = lax.broadcasted_iota(i32, (8, tm), 1)
        shift = 1
        while shift < tm:
            prefix = prefix + jnp.where(lane >= shift, pltpu.roll(prefix, shift, axis=1), 0.0)
            shift *= 2
        cnt = prefix[:, tm - 1:tm]
        cnt_al = jnp.floor((cnt + (MOE_ALIGN - 1)) * (1.0 / MOE_ALIGN)) * MOE_ALIGN
        offs = [jnp.zeros((1, 1), f32)]
        for g in range(1, ng):
            offs.append(offs[-1] + cnt_al[g - 1:g])
        off_col = jnp.concatenate(offs + [jnp.zeros((8 - ng, 1), f32)], axis=0)
        pos = jnp.sum(ind * (off_col + prefix - 1.0), axis=0, keepdims=True).astype(i32)
        perm = (lax.broadcasted_iota(i32, (rows, tm), 0) == pos).astype(bf16)
        p_ref[...] = perm
        xs_ref[...] = _dot(perm, h_ref[...].astype(bf16)).astype(bf16)
        parts = _dot(perm, jnp.concatenate(_split3(gate_ref[...]), axis=1))
        gs_ref[...] = (parts[:, :N_EXPERTS] + parts[:, N_EXPERTS:2 * N_EXPERTS]) + parts[:, 2 * N_EXPERTS:]
        ys_ref[...] = jnp.zeros_like(ys_ref)
        for g in range(ng):
            meta_ref[g] = offs[g][0, 0].astype(i32)
            meta_ref[ng + g] = cnt[g:g + 1][0, 0].astype(i32)

    g = e // EPG
    start = meta_ref[g]
    n_ch = (meta_ref[ng + g] + (MOE_CH - 1)) // MOE_CH

    def chunk(c, carry):
        r0 = pl.multiple_of(start + c * MOE_CH, MOE_ALIGN)
        x = xs_ref[pl.ds(r0, MOE_CH), :]
        hid = jax.nn.silu(_dot(x, wg_ref[0, 0])) * _dot(x, wu_ref[0, 0])
        y = _dot(hid.astype(bf16), wd_ref[0, 0])
        gs = gs_ref[pl.ds(r0, MOE_CH), :]
        lane = lax.broadcasted_iota(i32, gs.shape, 1)
        gcol = jnp.sum(jnp.where(lane == e, gs, 0.0), axis=1, keepdims=True)
        ys_ref[pl.ds(r0, MOE_CH), :] += gcol * y
        return carry

    lax.fori_loop(0, n_ch, chunk, 0)

    @pl.when(e == N_EXPERTS - 1)
    def _():
        ys = ys_ref[...]
        hi = ys.astype(bf16)
        lo = (ys - hi.astype(f32)).astype(bf16)
        perm = p_ref[...]
        out = _dot_tn(perm, hi) + _dot_tn(perm, lo)
        o_ref[...] = _ln(ALPHA * h_ref[...] + out, g2_ref[...], b2_ref[...])


def _moe_grouped_ln(h, gate, best, wg, wu, wd, g2, b2, *, layer):
    m, d = h.shape
    tm = MOE_TM
    assert m % tm == 0
    return pl.pallas_call(
        _moe_grouped_body,
        name=f"moe_grouped_ln_r{m}",
        grid=(m // tm, N_EXPERTS),
        in_specs=[pl.BlockSpec((tm, d), lambda i, e: (i, 0)), pl.BlockSpec((tm, N_EXPERTS), lambda i, e: (i, 0)),
                  pl.BlockSpec((1, tm), lambda i, e: (0, i)),
                  pl.BlockSpec((1, 1, d, D_FF), lambda i, e: (layer, e, 0, 0)),
                  pl.BlockSpec((1, 1, d, D_FF), lambda i, e: (layer, e, 0, 0)),
                  pl.BlockSpec((1, 1, D_FF, d), lambda i, e: (layer, e, 0, 0)),
                  pl.BlockSpec((1, d), lambda i, e: (0, 0)), pl.BlockSpec((1, d), lambda i, e: (0, 0))],
        out_specs=pl.BlockSpec((tm, d), lambda i, e: (i, 0)),
        out_shape=jax.ShapeDtypeStruct((m, d), f32),
        scratch_shapes=[pltpu.VMEM((MOE_ROWS, tm), bf16), pltpu.VMEM((MOE_ROWS, d), bf16),
                        pltpu.VMEM((MOE_ROWS, d), f32), pltpu.VMEM((MOE_ROWS, N_EXPERTS), f32),
                        pltpu.SMEM((2 * N_EGROUPS,), i32)],
        compiler_params=_cparams(("arbitrary", "arbitrary")),
    )(h, gate, best, wg, wu, wd, g2.reshape(1, d), b2.reshape(1, d))


def _layer_tail(h1, p, tw, *, tm):
    gate, best = _router(h1, tw["rwt"], tw["rb"], tm=tm)
    if h1.shape[0] % MOE_TM == 0 and tw["wg"].dtype == bf16:
        h2 = _moe_grouped_ln(h1, gate, best, tw["wg"], tw["wu"], tw["wd"], tw["ln2_g"], tw["ln2_b"], layer=tw["layer"])
    else:
        h2 = _moe_ln(h1, gate, tw["wg"], tw["wu"], tw["wd"], tw["ln2_g"], tw["ln2_b"], tm=tm, layer=tw["layer"])
    return _ple(h2, p, tw["ple_gate"], tw["ple_proj"], tm=tm)


def _ssd_conv(x_ref, xp_ref, w_ref, b_ref):
    q = x_ref.shape[0]
    xp_ref[8:8 + q, :] = x_ref[...]
    w = w_ref[...]
    acc = b_ref[...] + xp_ref[5:5 + q, :] * w[0:1]
    for k in range(1, SSD_CONV):
        acc = acc + xp_ref[5 + k:5 + k + q, :] * w[k:k + 1]
    xp_ref[0:8, :] = xp_ref[q:q + 8, :]
    return jax.nn.silu(acc)


def _head_expand(width):
    r = lax.broadcasted_iota(i32, (SSD_HPG, SSD_HPG * width), 0)
    c = lax.broadcasted_iota(i32, (SSD_HPG, SSD_HPG * width), 1)
    return (c // width == r).astype(bf16)


def _split3(v):
    hi = v.astype(bf16)
    r1 = v - hi.astype(f32)
    mid = r1.astype(bf16)
    return hi, mid, (r1 - mid.astype(f32)).astype(bf16)


def _dot_exact01(v, onehot, left=False):
    parts = _split3(v)
    prods = [_dot(onehot, p) if left else _dot(p, onehot) for p in parts]
    return (prods[0] + prods[1]) + prods[2]


def _ssd_body(xs_ref, bm_ref, cm_ref, z_ref, dt_ref, dtt_ref, wx_ref, wb_ref, wc_ref, bx_ref, bb_ref, bc_ref,
              dtb_ref, dtbt_ref, alog_ref, alogt_ref, dsk_ref, ng_ref, y_ref, st_ref,
              stt_ref, xpx_ref, xpb_ref, xpc_ref):
    c = pl.program_id(2)
    q = SSD_CHUNK

    @pl.when(c == 0)
    def _():
        stt_ref[...] = jnp.zeros_like(stt_ref)
        xpx_ref[0:8, :] = jnp.zeros((8, xpx_ref.shape[1]), f32)
        xpb_ref[0:8, :] = jnp.zeros((8, xpb_ref.shape[1]), f32)
        xpc_ref[0:8, :] = jnp.zeros((8, xpc_ref.shape[1]), f32)

    xs = _ssd_conv(xs_ref, xpx_ref, wx_ref, bx_ref)
    bm = _ssd_conv(bm_ref, xpb_ref, wb_ref, bb_ref)
    cm = _ssd_conv(cm_ref, xpc_ref, wc_ref, bc_ref)
    dt = jax.nn.softplus(dt_ref[0] + dtb_ref[0])
    dtt = jax.nn.softplus(dtt_ref[0] + dtbt_ref[0])
    dta = dt * (-jnp.exp(alog_ref[0]))
    dtat = dtt * (-jnp.exp(alogt_ref[0]))
    row = lax.broadcasted_iota(i32, (q, q), 0)
    col = lax.broadcasted_iota(i32, (q, q), 1)
    causal = row >= col
    cum = _dot_exact01(dta, causal.astype(bf16), left=True)
    cumt = _dot_exact01(dtat, (row <= col).astype(bf16))
    both = _dot_exact01(jnp.concatenate([dt, cum], axis=0), _head_expand(SSD_HEADDIM))
    dtx, cumx = both[:q], both[q:]
    cum128 = _dot_exact01(cum, _head_expand(q))
    cum_last = cumx[q - 1:q, :]
    xdt = xs * dtx
    xdt_b = xdt.astype(bf16)
    bm_b = bm.astype(bf16)
    cm_b = cm.astype(bf16)
    cb = _dot_nt(cm_b, bm_b)
    head = lax.broadcasted_iota(i32, (q, SSD_GW), 1) // SSD_HEADDIM
    y = jnp.zeros((q, SSD_GW), f32)
    for r in range(SSD_HPG):
        seg = cum128[:, r * q:(r + 1) * q] - cumt[r:r + 1, :]
        decay = jnp.where(causal, jnp.exp(jnp.where(causal, seg, 0.0)), 0.0)
        yr = _dot((cb * decay).astype(bf16), xdt_b)
        y = jnp.where(head == r, yr, y)
    stt = stt_ref[...]
    y = y + _dot(cm_b, stt.astype(bf16)) * jnp.exp(cumx) + dsk_ref[0] * xs
    to_end = jnp.exp(cum_last - cumx)
    stt_new = stt * jnp.exp(cum_last) + _dot_tn(bm_b, (xdt * to_end).astype(bf16))
    stt_ref[...] = stt_new
    yz = y * jax.nn.silu(z_ref[...])
    yn = yz * lax.rsqrt(jnp.mean(yz * yz, axis=-1, keepdims=True) + 1e-5) * ng_ref[...]
    y_ref[...] = yn.astype(y_ref.dtype)

    @pl.when(c == pl.num_programs(2) - 1)
    def _():
        st_ref[0] = stt_new.T.reshape(SSD_HPG, SSD_HEADDIM, SSD_STATE)


def _ssd_prompt_scan(xbc, z, dt_raw, sw, bsz, t):
    m = bsz * t
    nc = t // SSD_CHUNK
    q, gw, n, g_, hpg = SSD_CHUNK, SSD_GW, SSD_STATE, SSD_GROUPS, SSD_HPG
    dt_g = dt_raw.reshape(m, g_, hpg).transpose(1, 0, 2)
    dtt_g = dt_g.transpose(0, 2, 1)
    nxb = SSD_D_INNER // n
    row = lambda b, g, c: b * nc + c
    in_specs = [
        pl.BlockSpec((q, gw), lambda b, g, c: (row(b, g, c), g)),
        pl.BlockSpec((q, n), lambda b, g, c: (row(b, g, c), nxb + g)),
        pl.BlockSpec((q, n), lambda b, g, c: (row(b, g, c), nxb + g_ + g)),
        pl.BlockSpec((q, gw), lambda b, g, c: (row(b, g, c), g)),
        pl.BlockSpec((1, q, hpg), lambda b, g, c: (g, row(b, g, c), 0)),
        pl.BlockSpec((1, hpg, q), lambda b, g, c: (g, 0, row(b, g, c))),
        pl.BlockSpec((SSD_CONV, gw), lambda b, g, c: (0, g)),
        pl.BlockSpec((SSD_CONV, n), lambda b, g, c: (0, nxb + g)),
        pl.BlockSpec((SSD_CONV, n), lambda b, g, c: (0, nxb + g_ + g)),
        pl.BlockSpec((1, gw), lambda b, g, c: (0, g)),
        pl.BlockSpec((1, n), lambda b, g, c: (0, nxb + g)),
        pl.BlockSpec((1, n), lambda b, g, c: (0, nxb + g_ + g)),
        pl.BlockSpec((1, 1, hpg), lambda b, g, c: (g, 0, 0)),
        pl.BlockSpec((1, hpg, 1), lambda b, g, c: (g, 0, 0)),
        pl.BlockSpec((1, 1, hpg), lambda b, g, c: (g, 0, 0)),
        pl.BlockSpec((1, hpg, 1), lambda b, g, c: (g, 0, 0)),
        pl.BlockSpec((1, 1, gw), lambda b, g, c: (g, 0, 0)),
        pl.BlockSpec((1, gw), lambda b, g, c: (0, g)),
    ]
    y, st = pl.pallas_call(
        _ssd_body,
        name="ssd_scan",
        grid=(bsz, g_, nc),
        in_specs=in_specs,
        out_specs=[pl.BlockSpec((q, gw), lambda b, g, c: (row(b, g, c), g)),
                   pl.BlockSpec((1, hpg, SSD_HEADDIM, n), lambda b, g, c: (b, g, 0, 0))],
        out_shape=[jax.ShapeDtypeStruct((m, SSD_D_INNER), bf16),
                   jax.ShapeDtypeStruct((bsz, SSD_HEADS, SSD_HEADDIM, n), f32)],
        scratch_shapes=[pltpu.VMEM((n, gw), f32), pltpu.VMEM((q + 8, gw), f32),
                        pltpu.VMEM((q + 8, n), f32), pltpu.VMEM((q + 8, n), f32)],
        compiler_params=_cparams(("arbitrary", "arbitrary", "arbitrary")),
    )(xbc, xbc, xbc, z, dt_g, dtt_g, sw["conv_w"], sw["conv_w"], sw["conv_w"], sw["conv_b"], sw["conv_b"], sw["conv_b"],
      sw["dtb"], sw["dtbt"], sw["alog"], sw["alogt"], sw["dskx"], sw["norm_g"])
    return y, st


def _ssd_weights(w_in, conv_w, conv_b, dt_bias, a_log, d_skip, norm_g, w_out, wdt):
    g_, hpg = SSD_GROUPS, SSD_HPG
    return dict(
        w_z=w_in[:, :SSD_D_INNER].astype(wdt),
        w_xbc=w_in[:, SSD_D_INNER:SSD_D_INNER + SSD_CONV_DIM].astype(wdt),
        w_dt=w_in[:, SSD_D_INNER + SSD_CONV_DIM:].astype(wdt),
        conv_w=conv_w, conv_b=conv_b.reshape(1, SSD_CONV_DIM),
        dtb=dt_bias.reshape(g_, 1, hpg), dtbt=dt_bias.reshape(g_, hpg, 1),
        alog=a_log.reshape(g_, 1, hpg), alogt=a_log.reshape(g_, hpg, 1),
        dskx=jnp.repeat(d_skip, SSD_HEADDIM).reshape(g_, 1, SSD_GW),
        dsk=d_skip, dt_bias=dt_bias, a_log=a_log,
        norm_g=norm_g.reshape(1, SSD_D_INNER), w_out=w_out.astype(wdt))


def _ssd_prompt(x2d, sw, bsz, t, ln_g, ln_b):
    (z,) = _mm(x2d, sw["w_z"], tm=PROJ_TM, tn=1024, out_dtypes=(f32,), name="ssd_in_z")
    (xbc,) = _mm(x2d, sw["w_xbc"], tm=PROJ_TM, tn=1024, out_dtypes=(f32,), name="ssd_in_xbc")
    (dt_raw,) = _mm(x2d, sw["w_dt"], tm=PROJ_TM, tn=SSD_HEADS, out_dtypes=(f32,), name="ssd_in_dt")
    y, st = _ssd_prompt_scan(xbc, z, dt_raw, sw, bsz, t)
    h1 = _mm_ln(y, sw["w_out"], x2d, ln_g, ln_b, tm=PROJ_TM, name="ssd_out_ln")
    new_conv = xbc.reshape(bsz, t, SSD_CONV_DIM)[:, t - (SSD_CONV - 1):]
    return h1, new_conv, st


def _pad_rows(x, rows=8):
    return jnp.concatenate([x, jnp.zeros((rows - x.shape[0], x.shape[1]), x.dtype)], axis=0)


def _ssd_step_body(z_ref, xbc_ref, dt_ref, cs_ref, st_ref, cw_ref, cb_ref, dtb_ref, alog_ref, dsk_ref, ng_ref,
                   y_ref, nc_ref, ns_ref):
    n, gw, hd = SSD_STATE, SSD_GW, SSD_HEADDIM
    xbc = xbc_ref[0]
    cs = cs_ref[0]
    w = cw_ref[...]
    conv = cb_ref[...] + cs[0:1] * w[0:1]
    for k in range(1, SSD_CONV - 1):
        conv = conv + cs[k:k + 1] * w[k:k + 1]
    conv = jax.nn.silu(conv + xbc * w[SSD_CONV - 1:SSD_CONV])
    nc_ref[0] = jnp.concatenate([cs[1:], xbc], axis=0)
    xs = conv[:, :SSD_D_INNER]
    b_g = _pad_rows(jnp.concatenate(
        [conv[:, SSD_D_INNER + g * n:SSD_D_INNER + (g + 1) * n] for g in range(SSD_GROUPS)], axis=0))
    c_g = _pad_rows(jnp.concatenate(
        [conv[:, SSD_D_INNER + (SSD_GROUPS + g) * n:SSD_D_INNER + (SSD_GROUPS + g + 1) * n] for g in range(SSD_GROUPS)],
        axis=0))
    dt = jax.nn.softplus(dt_ref[0] + dtb_ref[...])
    decay = jnp.exp(dt * (-jnp.exp(alog_ref[...])))
    er = lax.broadcasted_iota(i32, (SSD_HEADS, SSD_D_INNER), 0)
    ec = lax.broadcasted_iota(i32, (SSD_HEADS, SSD_D_INNER), 1)
    per_head = _pad_rows(jnp.concatenate([dt, decay, dsk_ref[...]], axis=0))
    hx = jnp.dot(per_head, (ec // hd == er).astype(f32), precision=HIGHEST, preferred_element_type=f32)
    dtx, decx, dskx = hx[0:1], hx[1:2], hx[2:3]
    xdt = xs * dtx
    gr = lax.broadcasted_iota(i32, (8, SSD_D_INNER), 0)
    gc = lax.broadcasted_iota(i32, (8, SSD_D_INNER), 1)
    gmask = (gc // gw == gr).astype(f32)
    row0 = (gr == 0).astype(f32)
    st = st_ref[0].reshape(SSD_D_INNER, n)
    upd = _dot_tn(gmask * xdt, b_g, precision=HIGHEST)
    dec_full = _dot_tn(row0 * decx, jnp.ones((8, n), f32), precision=HIGHEST)
    ns_ref[0] = (st * dec_full + upd).reshape(SSD_HEADS, hd, n)
    cst = _dot_nt(c_g, st, precision=HIGHEST)
    y_state = jnp.sum(gmask * cst, axis=0, keepdims=True)
    cbx = jnp.sum(gmask * jnp.sum(c_g * b_g, axis=1, keepdims=True), axis=0, keepdims=True)
    y = cbx * xdt + decx * y_state + dskx * xs
    yz = y * jax.nn.silu(z_ref[0])
    parts = []
    for g in range(SSD_GROUPS):
        seg = yz[:, g * gw:(g + 1) * gw]
        parts.append(seg * lax.rsqrt(jnp.mean(seg * seg, axis=-1, keepdims=True) + 1e-5))
    y_ref[0] = jnp.concatenate(parts, axis=1) * ng_ref[...]


def _ssd_step(x2d, conv_state, ssm_state, sw, ln_g, ln_b):
    bsz = x2d.shape[0]
    (z,) = _mm(x2d, sw["w_z"], tm=bsz, tn=1024, out_dtypes=(f32,), name="ssd_in_z")
    (xbc,) = _mm(x2d, sw["w_xbc"], tm=bsz, tn=1024, out_dtypes=(f32,), name="ssd_in_xbc")
    (dt_raw,) = _mm(x2d, sw["w_dt"], tm=bsz, tn=SSD_HEADS, out_dtypes=(f32,), name="ssd_in_dt")
    rowspec = lambda width: pl.BlockSpec((1, 1, width), lambda b: (b, 0, 0))
    full = lambda r, c: pl.BlockSpec((r, c), lambda b: (0, 0))
    y, new_conv, new_state = pl.pallas_call(
        _ssd_step_body,
        name="ssd_step",
        grid=(bsz,),
        in_specs=[rowspec(SSD_D_INNER), rowspec(SSD_CONV_DIM), rowspec(SSD_HEADS),
                  pl.BlockSpec((1, SSD_CONV - 1, SSD_CONV_DIM), lambda b: (b, 0, 0)),
                  pl.BlockSpec((1, SSD_HEADS, SSD_HEADDIM, SSD_STATE), lambda b: (b, 0, 0, 0)),
                  full(SSD_CONV, SSD_CONV_DIM), full(1, SSD_CONV_DIM), full(1, SSD_HEADS), full(1, SSD_HEADS),
                  full(1, SSD_HEADS), full(1, SSD_D_INNER)],
        out_specs=[rowspec(SSD_D_INNER),
                   pl.BlockSpec((1, SSD_CONV - 1, SSD_CONV_DIM), lambda b: (b, 0, 0)),
                   pl.BlockSpec((1, SSD_HEADS, SSD_HEADDIM, SSD_STATE), lambda b: (b, 0, 0, 0))],
        out_shape=[jax.ShapeDtypeStruct((bsz, 1, SSD_D_INNER), f32),
                   jax.ShapeDtypeStruct((bsz, SSD_CONV - 1, SSD_CONV_DIM), f32),
                   jax.ShapeDtypeStruct((bsz, SSD_HEADS, SSD_HEADDIM, SSD_STATE), f32)],
        compiler_params=_cparams(("parallel",)),
    )(z.reshape(bsz, 1, -1), xbc.reshape(bsz, 1, -1), dt_raw.reshape(bsz, 1, -1), conv_state, ssm_state,
      sw["conv_w"], sw["conv_b"], sw["dt_bias"].reshape(1, -1), sw["a_log"].reshape(1, -1), sw["dsk"].reshape(1, -1),
      sw["norm_g"])
    h1 = _mm_ln(y.reshape(bsz, SSD_D_INNER), sw["w_out"], x2d, ln_g, ln_b, tm=bsz, name="ssd_out_ln")
    return h1, new_conv, new_state


def _slope(head):
    return 2.0 ** (-8.0 * (head + 1) / NSA_HEADS)


def _masked_softmax(s, mask):
    s = jnp.where(mask, s, NEG)
    m = jnp.max(s, axis=-1, keepdims=True)
    e = jnp.exp(s - m) * mask.astype(f32)
    return e / jnp.maximum(jnp.sum(e, axis=-1, keepdims=True), 1e-30)


def _alibi_bias(ok, dist, kv_head):
    distf = dist.astype(f32)
    return jnp.concatenate(
        [jnp.where(ok, (-_slope(kv_head * NSA_REP + r)) * distf, NEG) for r in range(NSA_REP)], axis=0)


def _topk_rank(imp):
    nb = imp.shape[1]
    lane = lax.broadcasted_iota(i32, imp.shape, 1)
    rank = jnp.zeros(imp.shape, i32)
    for j in range(nb):
        cj = imp[:, j:j + 1]
        rank = rank + ((cj > imp) | ((cj == imp) & (lane > j))).astype(i32)
    return rank


def _topk_mask_cols(it, n_sel):
    nb = it.shape[0]
    sub = lax.broadcasted_iota(i32, it.shape, 0)
    rank = jnp.zeros(it.shape, i32)
    for j in range(nb):
        rj = it[j:j + 1, :]
        rank = rank + ((rj > it) | ((rj == it) & (sub > j))).astype(i32)
    return (rank < n_sel).astype(f32)


def _pos_weights(pw_ref):
    w = jax.nn.softmax(pw_ref[...], axis=0)
    r = lax.broadcasted_iota(i32, (2 * NSA_KV, 2 * NSA_KVD), 0)
    c = lax.broadcasted_iota(i32, (2 * NSA_KV, 2 * NSA_KVD), 1)
    return jnp.dot(w, (c // NSA_DH == r).astype(f32), precision=HIGHEST, preferred_element_type=f32)


def _compress_body(x_ref, pw_ref, o_ref):
    wx = _pos_weights(pw_ref)
    x = x_ref[...]
    nb = x.shape[0] // NSA_BLOCK
    o_ref[0] = jnp.sum(x.reshape(nb, NSA_BLOCK, 2 * NSA_KVD) * wx[None], axis=1)


def _nsa_compress(kv4, pw, bsz, t):
    nb = t // NSA_BLOCK
    return pl.pallas_call(
        _compress_body,
        name="nsa_compress",
        grid=(bsz,),
        in_specs=[pl.BlockSpec((t, 2 * NSA_KVD), lambda b: (b, 0)),
                  pl.BlockSpec((NSA_BLOCK, 2 * NSA_KV), lambda b: (0, 0))],
        out_specs=pl.BlockSpec((1, nb, 2 * NSA_KVD), lambda b: (b, 0, 0)),
        out_shape=jax.ShapeDtypeStruct((bsz, nb, 2 * NSA_KVD), f32),
        compiler_params=_cparams(("parallel",)),
    )(kv4, pw)


NSA_TQ = 128
NSA_CK = 512


def _nsa_prompt_body(q_ref, g_ref, kcvc_ref, kvs_ref, kvw_ref, o_ref, *, t_len):
    i = pl.program_id(1)
    tq, ck, rep = NSA_TQ, NSA_CK, NSA_REP
    ck = min(ck, t_len)
    nb = t_len // NSA_BLOCK
    n_sel = min(NSA_TOP, nb)
    wlen = min(NSA_WINDOW + tq, t_len)
    t0 = i * tq
    t_col = t0 + lax.broadcasted_iota(i32, (tq, 1), 0)
    lane128 = lax.broadcasted_iota(i32, (1, LANES), 1)
    gates = g_ref[...]
    kcvc = kcvc_ref[0]
    t_row = t0 + lax.broadcasted_iota(i32, (1, tq), 1)
    blk = lax.broadcasted_iota(i32, (nb, 1), 0)
    dist_c = t_row - ((blk + 1) * NSA_BLOCK - 1)
    ok_c = dist_c >= 0
    ok_c4 = jnp.concatenate([ok_c] * rep, axis=1).astype(f32)
    cur = t_row // NSA_BLOCK
    n_chunks = (t0 + tq + ck - 1) // ck
    w_start = jnp.clip(t0 - NSA_WINDOW, 0, t_len - wlen)
    w_start = pl.multiple_of(w_start, tq)
    dist_w = t_col - (w_start + lax.broadcasted_iota(i32, (1, wlen), 1))
    mask_w = (dist_w >= 0) & (dist_w < NSA_WINDOW)

    for j in range(NSA_KV // 2):
        cols = slice(j * LANES, (j + 1) * LANES)
        vcols = slice(NSA_KVD + j * LANES, NSA_KVD + (j + 1) * LANES)
        q_tiles = [q_ref[:, (j * rep + r) * LANES:(j * rep + r + 1) * LANES] for r in range(rep)]
        q_rows = jnp.concatenate(q_tiles, axis=0)
        kc_b = kcvc[:, cols].astype(bf16)
        vc_b = kcvc[:, vcols].astype(bf16)
        out_tiles = [jnp.zeros((tq, LANES), f32) for _ in range(rep)]
        for h in range(2):
            k = 2 * j + h
            half = (lane128 // NSA_DH) == h
            q_pad = jnp.where(half, q_rows, jnp.zeros_like(q_rows))
            bias_c = jnp.concatenate([jnp.where(ok_c, (-_slope(k * rep + r)) * dist_c.astype(f32), NEG)
                                      for r in range(rep)], axis=1)
            s_c = _dot_nt(kc_b, q_pad) + bias_c
            e_c = jnp.exp(s_c - jnp.max(s_c, axis=0, keepdims=True)) * ok_c4
            p_c = e_c / jnp.maximum(jnp.sum(e_c, axis=0, keepdims=True), 1e-30)
            o_c = _dot_tn(p_c.astype(bf16), vc_b)
            imp = sum(p_c[:, r * tq:(r + 1) * tq] for r in range(rep))
            imp = jnp.where((blk == cur) | (blk == 0), 1e4, imp)
            imp = jnp.where(blk > cur, -1.0, imp)
            sel_b = _topk_mask_cols(imp, n_sel).T.astype(bf16)

            def sel_step(c, carry, q_pad=q_pad, k=k, sel_b=sel_b, cols=cols, vcols=vcols):
                m, l, acc = carry
                k0 = pl.multiple_of(c * ck, ck)
                ks = kvs_ref[pl.ds(k0, ck), cols]
                vs = kvs_ref[pl.ds(k0, ck), vcols]
                kpos = k0 + lax.broadcasted_iota(i32, (1, ck), 1)
                brow = lax.broadcasted_iota(i32, (nb, ck), 0)
                expand = ((k0 + lax.broadcasted_iota(i32, (nb, ck), 1)) // NSA_BLOCK == brow).astype(bf16)
                selx = _dot(sel_b, expand)
                dist = t_col - kpos
                ok = (selx > 0.5) & (dist >= 0)
                s = _dot_nt(q_pad, ks) + _alibi_bias(ok, dist, k)
                m_new = jnp.maximum(m, jnp.max(s, axis=-1, keepdims=True))
                a = jnp.exp(m - m_new)
                p = jnp.exp(s - m_new)
                l = a * l + jnp.sum(p, axis=-1, keepdims=True)
                acc = a * acc + _dot(p.astype(bf16), vs)
                return m_new, l, acc

            init = (jnp.full((rep * tq, 1), NEG, f32), jnp.zeros((rep * tq, 1), f32), jnp.zeros((rep * tq, LANES), f32))
            _, l_s, acc_s = lax.fori_loop(0, n_chunks, sel_step, init)
            o_s = acc_s / jnp.maximum(l_s, 1e-30)
            kw = kvw_ref[pl.ds(w_start, wlen), cols]
            vw = kvw_ref[pl.ds(w_start, wlen), vcols]
            s_w = _dot_nt(q_pad, kw) + _alibi_bias(mask_w, dist_w, k)
            e_w = jnp.exp(s_w - jnp.max(s_w, axis=-1, keepdims=True))
            o_w = _dot(e_w.astype(bf16), vw) / jnp.sum(e_w, axis=-1, keepdims=True)
            for r in range(rep):
                gc = (k * rep + r) * 3
                rows = slice(r * tq, (r + 1) * tq)
                o_r = gates[:, gc:gc + 1] * o_c[rows] + gates[:, gc + 1:gc + 2] * o_s[rows] + gates[:, gc + 2:gc + 3] * o_w[rows]
                out_tiles[r] = jnp.where(half, o_r, out_tiles[r])
        for r in range(rep):
            o_ref[:, (j * rep + r) * LANES:(j * rep + r + 1) * LANES] = out_tiles[r].astype(o_ref.dtype)


def _nsa_prompt_attn(q, gates, kcvc, kv4_b, kvw_b, bsz, t):
    nq = t // NSA_TQ
    nb = t // NSA_BLOCK
    return pl.pallas_call(
        functools.partial(_nsa_prompt_body, t_len=t),
        name="nsa_attn",
        grid=(bsz, nq),
        in_specs=[pl.BlockSpec((NSA_TQ, NSA_QD), lambda b, i: (b * nq + i, 0)),
                  pl.BlockSpec((NSA_TQ, LANES), lambda b, i: (b * nq + i, 0)),
                  pl.BlockSpec((1, nb, 2 * NSA_KVD), lambda b, i: (b, 0, 0)),
                  pl.BlockSpec((t, 2 * NSA_KVD), lambda b, i: (b, 1)),
                  pl.BlockSpec((t, 2 * NSA_KVD), lambda b, i: (b, 0))],
        out_specs=pl.BlockSpec((NSA_TQ, NSA_QD), lambda b, i: (b * nq + i, 0)),
        out_shape=jax.ShapeDtypeStruct((bsz * t, NSA_QD), bf16),
        compiler_params=_cparams(("parallel", "arbitrary")),
    )(q, gates, kcvc, kv4_b, kvw_b)


def _pair_layout_cols(w):
    lead = w.shape[:-1]
    w = w.reshape(*lead, NSA_KV // 2, 2, NSA_REP, NSA_DH)
    return jnp.swapaxes(w, -3, -2).reshape(*lead, NSA_QD)


def _nsa_weights(w_in, pos_w, w_out, wdt, pair):
    lay = _pair_layout_cols if pair else (lambda w: w)
    kv0 = NSA_QD
    g0 = NSA_QD + 6 * NSA_KVD
    wg = jnp.zeros((D_MODEL, LANES), f32).at[:, :3 * NSA_HEADS].set(w_in[:, g0:])
    return dict(
        w_q=lay(w_in[:, :NSA_QD]).astype(wdt),
        w_kv4=w_in[:, kv0:kv0 + 4 * NSA_KVD].astype(wdt),
        w_kvw=w_in[:, kv0 + 4 * NSA_KVD:g0].astype(wdt),
        w_g=wg.astype(wdt),
        pw=pos_w.transpose(1, 0, 2).reshape(NSA_BLOCK, 2 * NSA_KV),
        pwt=pos_w.transpose(0, 2, 1).reshape(2 * NSA_KV, NSA_BLOCK),
        w_kvt=w_in[:, kv0:g0].T.astype(wdt),
        w_out=lay(w_out.T).T.astype(wdt))


def _nsa_prompt(x2d, nw, bsz, t, ln_g, ln_b):
    (q,) = _mm(x2d, nw["w_q"], tm=PROJ_TM, tn=1024, out_dtypes=(bf16,), scale=NSA_DH ** -0.5, name="nsa_in_q")
    kv4, kv4_b = _mm(x2d, nw["w_kv4"], tm=PROJ_TM, tn=1024, out_dtypes=(f32, bf16), name="nsa_in_kv4")
    kvw, kvw_b = _mm(x2d, nw["w_kvw"], tm=PROJ_TM, tn=512, out_dtypes=(f32, bf16), name="nsa_in_kvw")
    (gates,) = _mm(x2d, nw["w_g"], tm=PROJ_TM, tn=LANES, out_dtypes=(f32,), act="sigmoid", name="nsa_in_gates")
    kcvc = _nsa_compress(kv4, nw["pw"], bsz, t)
    o = _nsa_prompt_attn(q, gates, kcvc, kv4_b, kvw_b, bsz, t)
    h1 = _mm_ln(o, nw["w_out"], x2d, ln_g, ln_b, tm=PROJ_TM, name="nsa_out_ln")
    w_keep = min(NSA_WINDOW, t)
    new_kv = kv4.reshape(bsz, t, 4, NSA_KV, NSA_DH)
    new_win = kvw.reshape(bsz, t, 2, NSA_KV, NSA_DH)[:, t - w_keep:]
    return h1, new_kv, new_win


PAGES_PER_STEP = 16


def _lane_weights(pwt_ref):
    w = jax.nn.softmax(pwt_ref[...], axis=1)
    return jnp.concatenate([w] * (PAGE_SIZE // NSA_BLOCK), axis=1)


def _page_compress_body(pt_ref, *refs):
    del pt_ref
    page_refs, pwt_ref, o_ref = refs[:PAGES_PER_STEP], refs[PAGES_PER_STEP], refs[PAGES_PER_STEP + 1]
    s = pl.program_id(1)
    bpp = PAGE_SIZE // NSA_BLOCK
    nbp = o_ref.shape[-1]

    @pl.when(s == 0)
    def _():
        o_ref[...] = jnp.zeros_like(o_ref)

    w = _lane_weights(pwt_ref)
    tok_blk = lax.broadcasted_iota(i32, (1, PAGE_SIZE), 1) // NSA_BLOCK
    out_lane = lax.broadcasted_iota(i32, (1, nbp), 1)
    for c in range(2):
        for k in range(NSA_KV):
            wk = w[c * NSA_KV + k:c * NSA_KV + k + 1]
            upd = jnp.zeros((NSA_DH, nbp), f32)
            for i, p_ref in enumerate(page_refs):
                xw = p_ref[0, c, k] * wk
                first = (s * PAGES_PER_STEP + i) * bpp
                for h in range(bpp):
                    r = jnp.sum(jnp.where(tok_blk == h, xw, 0.0), axis=1, keepdims=True)
                    upd = upd + jnp.where(out_lane == first + h, r, 0.0)
            o_ref[0, c, k] += upd


def _page_compress(pool_t, page_table, pwt):
    bsz, n_pages = page_table.shape
    nbp = n_pages * (PAGE_SIZE // NSA_BLOCK)
    steps = n_pages // PAGES_PER_STEP
    assert n_pages % PAGES_PER_STEP == 0
    page_specs = [pl.BlockSpec((1, 2, NSA_KV, NSA_DH, PAGE_SIZE),
                               lambda b, s, pt, i=i: (pt[b, s * PAGES_PER_STEP + i], 0, 0, 0, 0))
                  for i in range(PAGES_PER_STEP)]
    return pl.pallas_call(
        _page_compress_body,
        name="nsa_page_compress",
        grid_spec=pltpu.PrefetchScalarGridSpec(
            num_scalar_prefetch=1, grid=(bsz, steps),
            in_specs=page_specs + [pl.BlockSpec((2 * NSA_KV, NSA_BLOCK), lambda b, s, pt: (0, 0))],
            out_specs=pl.BlockSpec((1, 2, NSA_KV, NSA_DH, nbp), lambda b, s, pt: (b, 0, 0, 0, 0))),
        out_shape=jax.ShapeDtypeStruct((bsz, 2, NSA_KV, NSA_DH, nbp), f32),
        compiler_params=_cparams(("parallel", "arbitrary")),
    )(page_table, *([pool_t] * PAGES_PER_STEP), pwt)


def _mm_nt_body(wt_ref, x_ref, o_ref):
    o_ref[...] = _mxu_nt(wt_ref[...], x_ref[...], _is_f32(wt_ref))


def _mm_nt(wt, x, *, tn, name):
    n, k = wt.shape
    m = x.shape[0]
    return pl.pallas_call(
        _mm_nt_body, name=f"{name}_r{m}", grid=(n // tn,),
        in_specs=[pl.BlockSpec((tn, k), lambda j: (j, 0)), pl.BlockSpec((m, k), lambda j: (0, 0))],
        out_specs=pl.BlockSpec((tn, m), lambda j: (j, 0)),
        out_shape=jax.ShapeDtypeStruct((n, m), f32),
        compiler_params=_cparams(("parallel",)),
    )(wt, x)


def _step_slopes(kv_head):
    r = lax.broadcasted_iota(i32, (8, 1), 0)
    return jnp.exp2(-0.5 * (kv_head * NSA_REP + r + 1).astype(f32))


def _gate_col(gates, kv_head, branch):
    cols = [(kv_head * NSA_REP + r) * 3 + branch for r in range(NSA_REP)]
    return jnp.concatenate([gates[:, c:c + 1] for c in cols], axis=0)


def _own_col(x, b):
    lane = lax.broadcasted_iota(i32, x.shape, x.ndim - 1)
    return jnp.sum(jnp.where(lane == b, x, 0.0), axis=-1, keepdims=True)


def _new_key_tile(col):
    lane = lax.broadcasted_iota(i32, (col.shape[0], LANES), 1)
    return jnp.where(lane == 0, col, 0.0)


def _nsa_step_cw_body(q_ref, g_ref, cmp_ref, kvt_ref, pwt_ref, win_ref, o_ref, idx_ref, *, past_len):
    b = pl.program_id(0)
    rep, kv = NSA_REP, NSA_KV
    nbp = past_len // NSA_BLOCK
    nbt = nbp + 8
    w_buf = win_ref.shape[-1]
    w0 = jax.nn.softmax(pwt_ref[...], axis=1)[:, 0:1]
    lane_c = lax.broadcasted_iota(i32, (1, nbp + LANES), 1)
    dist_c = past_len - ((lane_c + 1) * NSA_BLOCK - 1)
    lane_w = lax.broadcasted_iota(i32, (1, w_buf + LANES), 1)
    dist_w = jnp.where(lane_w <= w_buf, w_buf - lane_w, -1)
    mask_w = (dist_w >= 0) & (dist_w < NSA_WINDOW)
    blk = lax.broadcasted_iota(i32, (1, nbt), 1)
    cur = past_len // NSA_BLOCK
    gates = g_ref[0]
    for k in range(kv):
        q8 = _pad_rows(q_ref[0, k])
        slope = _step_slopes(k)
        new = [_own_col(kvt_ref[s, k], b) for s in (0, 1, 4, 5)]
        k_c = jnp.concatenate([cmp_ref[0, 0, k], _new_key_tile(new[0] * w0[k:k + 1])], axis=1)
        v_c = jnp.concatenate([cmp_ref[0, 1, k], _new_key_tile(new[1] * w0[kv + k:kv + k + 1])], axis=1)
        s_c = _mxu(q8, k_c, True) - slope * dist_c.astype(f32)
        p_c = _masked_softmax(s_c, jnp.broadcast_to(dist_c >= 0, s_c.shape))
        o_c = _mxu_nt(p_c, v_c, True)
        imp = jnp.sum(p_c[0:rep], axis=0, keepdims=True)[:, :nbt]
        imp = jnp.where((blk == cur) | (blk == 0), 1e4, imp)
        imp = jnp.where(blk > cur, -2.0, imp)
        rank = _topk_rank(imp)
        pick = lax.broadcasted_iota(i32, (NSA_TOP, nbt), 0) == rank
        idx_ref[0, k] = jnp.sum(jnp.where(pick, blk.astype(f32), 0.0), axis=1, keepdims=True).astype(i32)
        k_w = jnp.concatenate([win_ref[0, 0, k], _new_key_tile(new[2])], axis=1)
        v_w = jnp.concatenate([win_ref[0, 1, k], _new_key_tile(new[3])], axis=1)
        s_w = _mxu(q8, k_w, True) - slope * dist_w.astype(f32)
        p_w = _masked_softmax(s_w, jnp.broadcast_to(mask_w, s_w.shape))
        o_w = _mxu_nt(p_w, v_w, True)
        o_ref[0, k] = _gate_col(gates, k, 0) * o_c[0:rep] + _gate_col(gates, k, 2) * o_w[0:rep]


def _nsa_step_sel_body(idx_ref, pt_ref, *refs, past_len):
    del pt_ref
    n_in = 2 * NSA_TOP
    tile_refs = refs[:n_in]
    q_ref, g_ref, kvt_ref, part_ref, o_ref = refs[n_in:]
    b, k = pl.program_id(0), pl.program_id(1)
    nbp = past_len // NSA_BLOCK
    rep, lb, kv = NSA_REP, NSA_BLOCK, NSA_KV
    bpp = PAGE_SIZE // lb
    new_k = _new_key_tile(_own_col(kvt_ref[0, 0], b))
    new_v = _new_key_tile(_own_col(kvt_ref[1, 0], b))
    lane = lax.broadcasted_iota(i32, (1, NSA_TOP * PAGE_SIZE), 1)
    tok = lane % PAGE_SIZE
    spos = tok % lb
    valid = lane < 0
    ks, vs = [], []
    for i in range(NSA_TOP):
        n = idx_ref[(b * kv + k) * NSA_TOP + i]
        is_new = n >= nbp
        ks.append(jnp.where(is_new, new_k, tile_refs[2 * i][0, 0, 0]))
        vs.append(jnp.where(is_new, new_v, tile_refs[2 * i + 1][0, 0, 0]))
        mine = lane // PAGE_SIZE == i
        spos = spos + jnp.where(mine, n * lb, 0)
        half = jnp.where(is_new, 0, n % bpp)
        last = jnp.where(is_new, 0, lb - 1)
        valid = valid | (mine & (tok // lb == half) & (tok % lb <= last))
    k_all = jnp.concatenate(ks, axis=1)
    v_all = jnp.concatenate(vs, axis=1)
    dist = past_len - spos
    q8 = _pad_rows(q_ref[0, 0])
    s = _mxu(q8, k_all, True) - _step_slopes(k) * dist.astype(f32)
    p = _masked_softmax(s, jnp.broadcast_to(valid & (dist >= 0), s.shape))
    o_s = _mxu_nt(p, v_all, True)
    g = g_ref[0, 0]
    g_sel = jnp.concatenate([g[:, 3 * r + 1:3 * r + 2] for r in range(rep)], axis=0)
    o_ref[0, 0] = part_ref[0, 0] + g_sel * o_s[0:rep]


def _nsa_step(x2d, pool, page_table, win_cache, nw, ln_g, ln_b):
    bsz = x2d.shape[0]
    n_pages = page_table.shape[1]
    past_len = n_pages * PAGE_SIZE
    w_buf = win_cache.shape[1]
    rep, kv, dh = NSA_REP, NSA_KV, NSA_DH
    bpp = PAGE_SIZE // NSA_BLOCK
    nbp = past_len // NSA_BLOCK
    (q,) = _mm(x2d, nw["w_q"], tm=bsz, tn=1024, out_dtypes=(f32,), scale=NSA_DH ** -0.5, name="nsa_in_q")
    (kv4,) = _mm(x2d, nw["w_kv4"], tm=bsz, tn=1024, out_dtypes=(f32,), name="nsa_in_kv4")
    (kvw,) = _mm(x2d, nw["w_kvw"], tm=bsz, tn=512, out_dtypes=(f32,), name="nsa_in_kvw")
    (gates,) = _mm(x2d, nw["w_g"], tm=bsz, tn=LANES, out_dtypes=(f32,), act="sigmoid", name="nsa_in_gates")
    kvt = _mm_nt(nw["w_kvt"], x2d, tn=512, name="nsa_in_kvt").reshape(6, kv, dh, bsz)
    pool_t = jnp.transpose(pool, (0, 2, 3, 4, 1))
    win_t = jnp.transpose(win_cache, (0, 2, 3, 4, 1))
    cmp = _page_compress(pool_t, page_table, nw["pwt"])
    q4 = q.reshape(bsz, kv, rep, dh)
    slab = pl.BlockSpec((1, kv, rep, dh), lambda b: (b, 0, 0, 0))
    whole = lambda shape: pl.BlockSpec(shape, lambda b: (0,) * len(shape))
    part, idx = pl.pallas_call(
        functools.partial(_nsa_step_cw_body, past_len=past_len),
        name="nsa_step_cw",
        grid=(bsz,),
        in_specs=[slab, pl.BlockSpec((1, 1, LANES), lambda b: (b, 0, 0)),
                  pl.BlockSpec((1, 2, kv, dh, nbp), lambda b: (b, 0, 0, 0, 0)), whole((6, kv, dh, bsz)),
                  whole((2 * kv, NSA_BLOCK)), pl.BlockSpec((1, 2, kv, dh, w_buf), lambda b: (b, 0, 0, 0, 0))],
        out_specs=[slab, pl.BlockSpec((1, kv, NSA_TOP, 1), lambda b: (b, 0, 0, 0))],
        out_shape=[jax.ShapeDtypeStruct((bsz, kv, rep, dh), f32),
                   jax.ShapeDtypeStruct((bsz, kv, NSA_TOP, 1), i32)],
        compiler_params=_cparams(("parallel",)),
    )(q4, gates.reshape(bsz, 1, -1), cmp, kvt, nw["pwt"], win_t)

    def tile_spec(i, slot):
        def imap(b, k, idx_r, pt_r):
            n = jnp.minimum(idx_r[(b * kv + k) * NSA_TOP + i], nbp - 1)
            return (pt_r[b * n_pages + n // bpp], slot, k, 0, 0)
        return pl.BlockSpec((1, 1, 1, dh, PAGE_SIZE), imap)

    tile_specs = []
    for i in range(NSA_TOP):
        tile_specs += [tile_spec(i, 2), tile_spec(i, 3)]
    head = pl.BlockSpec((1, 1, rep, dh), lambda b, k, *_: (b, k, 0, 0))
    o = pl.pallas_call(
        functools.partial(_nsa_step_sel_body, past_len=past_len),
        name="nsa_step_sel",
        grid_spec=pltpu.PrefetchScalarGridSpec(
            num_scalar_prefetch=2, grid=(bsz, kv),
            in_specs=tile_specs + [
                head, pl.BlockSpec((1, 1, 1, 3 * rep), lambda b, k, *_: (b, k, 0, 0)),
                pl.BlockSpec((2, 1, dh, bsz), lambda b, k, *_: (1, k, 0, 0)), head],
            out_specs=head),
        out_shape=jax.ShapeDtypeStruct((bsz, kv, rep, dh), f32),
        compiler_params=_cparams(("arbitrary", "arbitrary")),
    )(idx.reshape(-1), page_table.reshape(-1), *([pool_t] * (2 * NSA_TOP)),
      q4, gates[:, :3 * NSA_HEADS].reshape(bsz, kv, 1, 3 * rep), kvt, part)
    h1 = _mm_ln(o.reshape(bsz, NSA_QD), nw["w_out"], x2d, ln_g, ln_b, tm=bsz, name="nsa_out_ln")
    new_kv = kv4.reshape(bsz, 1, 4, kv, dh)
    new_win = jnp.concatenate([win_cache[:, 1:], kvw.reshape(bsz, 1, 2, kv, dh)], axis=1)
    return h1, new_kv, new_win


def kernel(x_prompt, x_sample, state_ssm, state_conv, cache_kv, cache_win, page_table, p_prompt, p_sample,
           ssd_w_in, ssd_conv_w, ssd_conv_b, ssd_dt_bias, ssd_a_log, ssd_d, ssd_norm_g, ssd_w_out,
           nsa_w_in, nsa_pos_w, nsa_w_out, ln1_g, ln1_b, ln2_g, ln2_b, router_w, router_bias,
           moe_w_gate, moe_w_up, moe_w_down, ple_proj, ple_gate):
    bp, t, d = x_prompt.shape
    bs = x_sample.shape[0]
    xp = x_prompt.reshape(bp * t, d)
    xs = x_sample.reshape(bs, d)
    rb = router_bias.reshape(N_EXPERTS, 1)
    ssm_p, conv_p, kv_p, win_p, ssm_s, conv_s, kv_s, win_s = [], [], [], [], [], [], [], []
    for i in range(DEPTH):
        j = i // 2
        if i % 2 == 0:
            ssd = (ssd_w_in[j], ssd_conv_w[j], ssd_conv_b[j], ssd_dt_bias[j], ssd_a_log[j], ssd_d[j], ssd_norm_g[j],
                   ssd_w_out[j])
            h1p, c_new, s_new = _ssd_prompt(xp, _ssd_weights(*ssd, bf16), bp, t, ln1_g[i], ln1_b[i])
            conv_p.append(c_new)
            ssm_p.append(s_new)
            h1s, c_new, s_new = _ssd_step(xs, state_conv[j], state_ssm[j], _ssd_weights(*ssd, f32), ln1_g[i], ln1_b[i])
            conv_s.append(c_new)
            ssm_s.append(s_new)
        else:
            nsa = (nsa_w_in[j], nsa_pos_w[j], nsa_w_out[j])
            h1p, r_new, w_new = _nsa_prompt(xp, _nsa_weights(*nsa, bf16, True), bp, t, ln1_g[i], ln1_b[i])
            kv_p.append(r_new)
            win_p.append(w_new)
            h1s, r_new, w_new = _nsa_step(xs, cache_kv[j], page_table, cache_win[j], _nsa_weights(*nsa, f32, False),
                                          ln1_g[i], ln1_b[i])
            kv_s.append(r_new)
            win_s.append(w_new)

        def tail_weights(wdt, i=i):
            return dict(rwt=router_w.T.astype(wdt), rb=rb, layer=i, wg=moe_w_gate.astype(wdt), wu=moe_w_up.astype(wdt),
                        wd=moe_w_down.astype(wdt), ln2_g=ln2_g[i], ln2_b=ln2_b[i],
                        ple_gate=ple_gate[i].astype(wdt), ple_proj=ple_proj[i].astype(wdt))

        xp = _layer_tail(h1p, p_prompt[i].reshape(bp * t, PLE_DIM), tail_weights(bf16), tm=PROJ_TM)
        xs = _layer_tail(h1s, p_sample[i].reshape(bs, PLE_DIM), tail_weights(f32), tm=bs)
    return (xp.reshape(bp, t, d), xs.reshape(bs, 1, d), jnp.stack(ssm_p), jnp.stack(conv_p), jnp.stack(kv_p),
            jnp.stack(win_p), jnp.stack(ssm_s), jnp.stack(conv_s), jnp.stack(kv_s), jnp.stack(win_s))
```
